```python
import math
import jax, jax.numpy as jnp
from jax import lax
import numpy as np

D_MODEL = 1024
BATCH = 8
SEQ = 4096
DEPTH = 1

N_HEADS = 8
HEAD_DIM = D_MODEL // (2 * N_HEADS)
V_DIM = 2 * HEAD_DIM
ATTN_WIDTH = N_HEADS * V_DIM
QK_WIDTH = N_HEADS * 2 * HEAD_DIM
Q_BLOCK = 128
CONV_WIDTH = D_MODEL
CONV_K = 31
N_EXPERTS = 256
TOP_K = 8
N_GROUPS = 8
TOPK_GROUPS = 4
D_EXPERT = 256
D_SHARED = 256
ROUTED_SCALE = 2.5
MOE_BLOCK = 128
LN_EPS = 1e-5
RMS_EPS = 1e-5
DN_ALPHA = (2.0 * DEPTH) ** 0.25
DN_BETA = (8.0 * DEPTH) ** -0.25

IN_SIZES = [QK_WIDTH, QK_WIDTH, ATTN_WIDTH, CONV_WIDTH, CONV_WIDTH, D_MODEL, D_MODEL]
IN_COLS = sum(IN_SIZES)
IN_SPLITS = [sum(IN_SIZES[:i + 1]) for i in range(len(IN_SIZES) - 1)]
V_COL_START = 2 * QK_WIDTH
V_COL_END = 2 * QK_WIDTH + ATTN_WIDTH

kernel_name = "hybrid_diffattn_conformer_moe_block"


def lambda_init_for(layer_idx_1based):
    return 0.8 - 0.6 * math.exp(-0.3 * (layer_idx_1based - 1))


def layer_norm(x, g, b):
    xf = x.astype(jnp.float32)
    mu = xf.mean(-1, keepdims=True)
    var = jnp.square(xf - mu).mean(-1, keepdims=True)
    y = (xf - mu) * lax.rsqrt(var + LN_EPS) * g.astype(jnp.float32) + b.astype(jnp.float32)
    return y.astype(x.dtype)


def rms_norm(x, g):
    xf = x.astype(jnp.float32)
    y = xf * lax.rsqrt(jnp.square(xf).mean(-1, keepdims=True) + RMS_EPS) * g.astype(jnp.float32)
    return y


def ada_modulation(c, w_ada, b_ada):
    mod = jax.nn.silu(c) @ w_ada + b_ada
    return jnp.split(mod[:, None, :], 6, axis=-1)


def diff_attention(q, k, v, lam):
    S = q.shape[1]
    qf = q.astype(jnp.float32) * (HEAD_DIM ** -0.5)
    kf = k.astype(jnp.float32)
    vf = v.astype(jnp.float32)
    outs = []
    for i in range(S // Q_BLOCK):
        kv_len = (i + 1) * Q_BLOCK
        q_blk = qf[:, i * Q_BLOCK:kv_len]
        k_blk = kf[:, :kv_len]
        v_blk = vf[:, :kv_len]
        s = jnp.einsum('bqhmd,bkhmd->bhmqk', q_blk, k_blk)
        qpos = i * Q_BLOCK + jnp.arange(Q_BLOCK)
        kpos = jnp.arange(kv_len)
        causal = kpos[None, :] <= qpos[:, None]
        s = jnp.where(causal, s, -jnp.inf)
        p = jax.nn.softmax(s, axis=-1)
        a = p[:, :, 0] - lam * p[:, :, 1]
        outs.append(jnp.einsum('bhqk,bkhv->bqhv', a, v_blk))
    return jnp.concatenate(outs, axis=1)


def conformer_conv(u_a, u_b, w_dw, b_dw, ln_g, ln_b, w_pw, b_pw):
    u = u_a * jax.nn.sigmoid(u_b)
    y = lax.conv_general_dilated(
        u, w_dw[:, None, :].astype(u.dtype), window_strides=(1,),
        padding=[(CONV_K - 1, 0)], dimension_numbers=('NWC', 'WIO', 'NWC'),
        feature_group_count=CONV_WIDTH) + b_dw
    y = jax.nn.silu(layer_norm(y, ln_g, ln_b))
    return y @ w_pw + b_pw


def moe_ffn(h, w_router, router_bias, w1, w3, w2, ws1, ws3, ws2):
    B, S, D = h.shape
    T = B * S
    xf = h.reshape(T, D)
    scores = jax.nn.sigmoid(xf.astype(jnp.float32) @ w_router.astype(jnp.float32))
    choice = scores + router_bias.astype(jnp.float32)
    grp_score = lax.top_k(choice.reshape(T, N_GROUPS, N_EXPERTS // N_GROUPS), 2)[0].sum(-1)
    _, top_grp = lax.top_k(grp_score, TOPK_GROUPS)
    grp_mask = jnp.any(top_grp[..., None] == jnp.arange(N_GROUPS)[None, None, :], axis=1)
    exp_mask = jnp.repeat(grp_mask, N_EXPERTS // N_GROUPS, axis=-1)
    _, idx = lax.top_k(jnp.where(exp_mask, choice, -jnp.inf), TOP_K)
    gate = jnp.take_along_axis(scores, idx, axis=-1)
    gate = gate / gate.sum(-1, keepdims=True) * ROUTED_SCALE

    A = T * TOP_K
    e_flat = idx.reshape(A).astype(jnp.int32)
    tok_flat = jnp.arange(A, dtype=jnp.int32) // TOP_K
    g_flat = gate.reshape(A)
    order = jnp.argsort(e_flat)
    e_sorted = e_flat[order]
    tok_sorted = tok_flat[order]
    g_sorted = g_flat[order]
    counts = jnp.bincount(e_flat, length=N_EXPERTS).astype(jnp.int32)
    starts = jnp.cumsum(counts) - counts
    padded = (counts + MOE_BLOCK - 1) // MOE_BLOCK * MOE_BLOCK
    pad_ends = jnp.cumsum(padded)
    pad_starts = pad_ends - padded
    dest = pad_starts[e_sorted] + (jnp.arange(A, dtype=jnp.int32) - starts[e_sorted])
    P = A + N_EXPERTS * MOE_BLOCK
    n_blocks = P // MOE_BLOCK
    buf_tok = jnp.zeros((P,), jnp.int32).at[dest].set(tok_sorted).reshape(n_blocks, MOE_BLOCK)
    buf_gate = jnp.zeros((P,), jnp.float32).at[dest].set(g_sorted).reshape(n_blocks, MOE_BLOCK)
    blk_expert = jnp.minimum(
        jnp.searchsorted(pad_ends, jnp.arange(n_blocks, dtype=jnp.int32) * MOE_BLOCK, side='right'),
        N_EXPERTS - 1).astype(jnp.int32)

    def step(acc, inp):
        e, toks, g = inp
        xb = xf[toks]
        hb = jax.nn.silu(xb @ w1[e]) * (xb @ w3[e])
        yb = (hb @ w2[e]).astype(jnp.float32) * g[:, None]
        return acc.at[toks].add(yb), None

    routed, _ = lax.scan(step, jnp.zeros((T, D), jnp.float32), (blk_expert, buf_tok, buf_gate))
    shared = (jax.nn.silu(xf @ ws1) * (xf @ ws3)) @ ws2
    return (routed.astype(h.dtype) + shared).reshape(B, S, D)


def setup_inputs(seed: int = 0) -> dict:
    key = jax.random.key(seed)
    ks = jax.random.split(key, 32)
    L, D = DEPTH, D_MODEL
    f32 = jnp.float32
    nrm = lambda k, shape, s: jax.random.normal(k, shape, f32) * s
    col_scale = jnp.ones((IN_COLS,), f32).at[V_COL_START:V_COL_END].set(DN_BETA)
    return {
        "x": nrm(ks[0], (BATCH, SEQ, D), 1.0),
        "c": nrm(ks[1], (BATCH, D), 1.0),
        "ada_w": nrm(ks[2], (L, D, 6 * D), D ** -0.5),
        "ada_b": nrm(ks[3], (L, 6 * D), 0.02),
        "w_in": nrm(ks[4], (L, D, IN_COLS), D ** -0.5) * col_scale,
        "b_in": nrm(ks[5], (L, IN_COLS), 0.02),
        "lambda_q1": nrm(ks[6], (L, HEAD_DIM), 0.1),
        "lambda_k1": nrm(ks[7], (L, HEAD_DIM), 0.1),
        "lambda_q2": nrm(ks[8], (L, HEAD_DIM), 0.1),
        "lambda_k2": nrm(ks[9], (L, HEAD_DIM), 0.1),
        "subln_g": 1.0 + nrm(ks[10], (L, V_DIM), 0.02),
        "w_attn_o": nrm(ks[11], (L, ATTN_WIDTH, D), ATTN_WIDTH ** -0.5),
        "conv_w": nrm(ks[12], (L, CONV_K, CONV_WIDTH), CONV_K ** -0.5),
        "conv_b": nrm(ks[13], (L, CONV_WIDTH), 0.02),
        "conv_ln_g": 1.0 + nrm(ks[14], (L, CONV_WIDTH), 0.02),
        "conv_ln_b": nrm(ks[15], (L, CONV_WIDTH), 0.02),
        "conv_pw_w": nrm(ks[16], (L, CONV_WIDTH, D), CONV_WIDTH ** -0.5),
        "conv_pw_b": nrm(ks[17], (L, D), 0.02),
        "w_out": nrm(ks[18], (L, D, D), D ** -0.5) * DN_BETA,
        "ln1_g": 1.0 + nrm(ks[19], (L, D), 0.02),
        "ln1_b": nrm(ks[20], (L, D), 0.02),
        "router_w": nrm(ks[21], (L, D, N_EXPERTS), D ** -0.5),
        "router_bias": nrm(ks[22], (L, N_EXPERTS), 0.01),
        "w1": nrm(ks[23], (L, N_EXPERTS, D, D_EXPERT), D ** -0.5),
        "w3": nrm(ks[24], (L, N_EXPERTS, D, D_EXPERT), D ** -0.5),
        "w2": nrm(ks[25], (L, N_EXPERTS, D_EXPERT, D), D_EXPERT ** -0.5) * DN_BETA,
        "ws1": nrm(ks[26], (L, D, D_SHARED), D ** -0.5),
        "ws3": nrm(ks[27], (L, D, D_SHARED), D ** -0.5),
        "ws2": nrm(ks[28], (L, D_SHARED, D), D_SHARED ** -0.5) * DN_BETA,
        "ln2_g": 1.0 + nrm(ks[29], (L, D), 0.02),
        "ln2_b": nrm(ks[30], (L, D), 0.02),
    }


def reference(x, c, ada_w, ada_b, w_in, b_in, lambda_q1, lambda_k1, lambda_q2, lambda_k2,
              subln_g, w_attn_o, conv_w, conv_b, conv_ln_g, conv_ln_b, conv_pw_w, conv_pw_b,
              w_out, ln1_g, ln1_b, router_w, router_bias, w1, w3, w2, ws1, ws3, ws2, ln2_g, ln2_b):
    B, S, _ = x.shape
    for l in range(DEPTH):
        sh1, sc1, g1, sh2, sc2, g2 = ada_modulation(c, ada_w[l], ada_b[l])
        h = x * (1.0 + sc1) + sh1
        proj = h @ w_in[l] + b_in[l]
        q, k, v, glu_a, glu_b, gate_a, gate_b = jnp.split(proj, IN_SPLITS, axis=-1)
        q = q.reshape(B, S, N_HEADS, 2, HEAD_DIM)
        k = k.reshape(B, S, N_HEADS, 2, HEAD_DIM)
        v = v.reshape(B, S, N_HEADS, V_DIM)
        lam_init = lambda_init_for(l + 1)
        lam = (jnp.exp(jnp.sum(lambda_q1[l].astype(jnp.float32) * lambda_k1[l].astype(jnp.float32)))
               - jnp.exp(jnp.sum(lambda_q2[l].astype(jnp.float32) * lambda_k2[l].astype(jnp.float32)))
               + lam_init)
        o = diff_attention(q, k, v, lam)
        o = rms_norm(o, subln_g[l]) * (1.0 - lam_init)
        y_attn = o.reshape(B, S, ATTN_WIDTH).astype(x.dtype) @ w_attn_o[l]
        y_conv = conformer_conv(glu_a, glu_b, conv_w[l], conv_b[l], conv_ln_g[l], conv_ln_b[l],
                                conv_pw_w[l], conv_pw_b[l])
        m = jax.nn.sigmoid(gate_a) * y_attn + jax.nn.sigmoid(gate_b) * y_conv
        y = m @ w_out[l]
        x = layer_norm(DN_ALPHA * x + g1 * y, ln1_g[l], ln1_b[l])
        h2 = x * (1.0 + sc2) + sh2
        y2 = moe_ffn(h2, router_w[l], router_bias[l], w1[l], w3[l], w2[l], ws1[l], ws3[l], ws2[l])
        x = layer_norm(DN_ALPHA * x + g2 * y2, ln2_g[l], ln2_b[l])
    return x
```

```python
import math
from functools import partial

import jax
import jax.numpy as jnp
from jax import lax
from jax.experimental import pallas as pl
from jax.experimental.pallas import tpu as pltpu

D_MODEL = 1024
N_HEADS = 8
HEAD_DIM = 64
V_DIM = 128
CONV_K = 31
N_EXPERTS = 256
TOP_K = 8
N_GROUPS = 8
GROUP_SIZE = N_EXPERTS // N_GROUPS
TOPK_GROUPS = 4
D_EXPERT = 256
ROUTED_SCALE = 2.5
LN_EPS = 1e-5
RMS_EPS = 1e-5
DEPTH = 1
DN_ALPHA = (2.0 * DEPTH) ** 0.25
LAMBDA_INIT = 0.8 - 0.6 * math.exp(-0.3 * 0)

LANES = 128
VMEM_LIMIT_BYTES = 56 * 1024 * 1024

PROJ_TM = 512
ATTN_TQ = 256
MIX_TM = 256
CONV_HALO = 32
ROUTE_TM = 512
EXPERT_BLK = 256
FINAL_TM = 256

F32 = jnp.float32
BF16 = jnp.bfloat16
NEG_INF = float("-inf")


def _cparams(*sem):
    return pltpu.CompilerParams(dimension_semantics=sem, vmem_limit_bytes=VMEM_LIMIT_BYTES)


def _sigmoid(x):
    return 1.0 / (1.0 + jnp.exp(-x))


def _silu(x):
    return x * _sigmoid(x)


def _layer_norm(r, g, b):
    mu = jnp.mean(r, axis=-1, keepdims=True)
    d = r - mu
    var = jnp.mean(d * d, axis=-1, keepdims=True)
    return d * lax.rsqrt(var + LN_EPS) * g + b


def _ada_kernel(c_ref, w_ref, b_ref, o_ref):
    a = _silu(c_ref[...])
    o_ref[...] = jnp.dot(a, w_ref[...], preferred_element_type=F32,
                         precision=lax.Precision.HIGHEST) + b_ref[...]


def _ada_modulation(c, w, b):
    B, D = c.shape
    n = w.shape[1] // D
    return pl.pallas_call(
        _ada_kernel,
        out_shape=jax.ShapeDtypeStruct((B, n * D), F32),
        grid=(n,),
        in_specs=[pl.BlockSpec((B, D), lambda j: (0, 0)),
                  pl.BlockSpec((D, D), lambda j: (0, j)),
                  pl.BlockSpec((1, D), lambda j: (0, j))],
        out_specs=pl.BlockSpec((B, D), lambda j: (0, j)),
        compiler_params=_cparams("arbitrary"),
        name="ada_modulation",
    )(c, w, b.reshape(1, n * D))


def _inproj_kernel(x_ref, sc_ref, sh_ref, w_ref, b_ref,
                   q_ref, k_ref, v_ref, u_ref, ga_ref, gb_ref):
    D = D_MODEL
    h = (x_ref[...] * (1.0 + sc_ref[0]) + sh_ref[0]).astype(BF16)

    def proj(j):
        return (jnp.dot(h, w_ref[:, j * D:(j + 1) * D], preferred_element_type=F32)
                + b_ref[:, j * D:(j + 1) * D])

    q_ref[...] = (proj(0) * (HEAD_DIM ** -0.5)).astype(BF16)
    k_ref[...] = proj(1).astype(BF16)
    v_ref[...] = proj(2).astype(BF16)
    u_ref[...] = (proj(3) * _sigmoid(proj(4))).astype(BF16)
    ga_ref[...] = _sigmoid(proj(5)).astype(BF16)
    gb_ref[...] = _sigmoid(proj(6)).astype(BF16)


def _in_projection(x2, sc1, sh1, w_in, b_in, seq):
    T, D = x2.shape
    tm = PROJ_TM
    tiles_per_seq = seq // tm
    ncols = w_in.shape[1]
    tok = pl.BlockSpec((tm, D), lambda i: (i, 0))
    mod = pl.BlockSpec((1, 1, D), lambda i: (i // tiles_per_seq, 0, 0))
    out = jax.ShapeDtypeStruct((T, D), BF16)
    return pl.pallas_call(
        _inproj_kernel,
        out_shape=[out] * 6,
        grid=(T // tm,),
        in_specs=[tok, mod, mod,
                  pl.BlockSpec((D, ncols), lambda i: (0, 0)),
                  pl.BlockSpec((1, ncols), lambda i: (0, 0))],
        out_specs=[tok] * 6,
        compiler_params=_cparams("arbitrary"),
        name="in_projection",
    )(x2, sc1, sh1, w_in, b_in)


def _attn_kernel(lq1_ref, lk1_ref, lq2_ref, lk2_ref, g_ref, q_ref, k_ref, v_ref, o_ref):
    tq = ATTN_TQ
    qi = pl.program_id(2)
    lam = (jnp.exp(jnp.sum(lq1_ref[...] * lk1_ref[...], axis=-1, keepdims=True))
           - jnp.exp(jnp.sum(lq2_ref[...] * lk2_ref[...], axis=-1, keepdims=True))
           + LAMBDA_INIT)

    q = q_ref[...]
    lane = lax.broadcasted_iota(jnp.int32, q.shape, 1)
    zero = jnp.zeros_like(q)
    q2 = jnp.concatenate([jnp.where(lane < HEAD_DIM, q, zero),
                          jnp.where(lane >= HEAD_DIM, q, zero)], axis=0)

    def tile(j, carry, masked):
        m, l, acc = carry
        start = pl.multiple_of(j * tq, tq)
        kt = k_ref[pl.ds(start, tq), :]
        vt = v_ref[pl.ds(start, tq), :]
        s = lax.dot_general(q2, kt, (((1,), (1,)), ((), ())),
                            preferred_element_type=F32)
        if masked:
            row = lax.broadcasted_iota(jnp.int32, s.shape, 0)
            row = jnp.where(row >= tq, row - tq, row)
            col = lax.broadcasted_iota(jnp.int32, s.shape, 1)
            s = jnp.where(col <= row, s, NEG_INF)
        m_new = jnp.maximum(m, jnp.max(s, axis=-1, keepdims=True))
        alpha = jnp.exp(m - m_new)
        p = jnp.exp(s - m_new)
        l = alpha * l + jnp.sum(p, axis=-1, keepdims=True)
        acc = alpha * acc + jnp.dot(p.astype(BF16), vt, preferred_element_type=F32)
        return m_new, l, acc

    init = (jnp.full((2 * tq, 1), NEG_INF, F32),
            jnp.zeros((2 * tq, 1), F32),
            jnp.zeros((2 * tq, V_DIM), F32))
    carry = lax.fori_loop(0, qi, lambda j, c: tile(j, c, False), init)
    m, l, acc = tile(qi, carry, True)

    o = acc / l
    o = o[:tq] - lam * o[tq:]
    o = o * lax.rsqrt(jnp.mean(o * o, axis=-1, keepdims=True) + RMS_EPS) * g_ref[...]
    o_ref[...] = (o * (1.0 - LAMBDA_INIT)).astype(BF16)


def _diff_attention(q, k, v, lq1, lk1, lq2, lk2, subln_g, batch, seq):
    T, D = q.shape
    tq = ATTN_TQ
    nq = seq // tq
    lam_spec = pl.BlockSpec((1, HEAD_DIM), lambda b, h, i: (0, 0))
    q_spec = pl.BlockSpec((tq, V_DIM), lambda b, h, i: (b * nq + i, h))
    kv_spec = pl.BlockSpec((seq, V_DIM), lambda b, h, i: (b, h))
    return pl.pallas_call(
        _attn_kernel,
        out_shape=jax.ShapeDtypeStruct((T, D), BF16),
        grid=(batch, N_HEADS, nq),
        in_specs=[lam_spec, lam_spec, lam_spec, lam_spec,
                  pl.BlockSpec((1, V_DIM), lambda b, h, i: (0, 0)),
                  q_spec, kv_spec, kv_spec],
        out_specs=q_spec,
        compiler_params=_cparams("arbitrary", "arbitrary", "arbitrary"),
        name="diff_attention",
    )(lq1, lk1, lq2, lk2, subln_g, q, k, v)


def _mixer_kernel(x_ref, o_ref, u_ref, up_ref, ga_ref, gb_ref, g1_ref, sc2_ref, sh2_ref,
                  wao_ref, cw_ref, cb_ref, clg_ref, clb_ref, wpw_ref, bpw_ref, wout_ref,
                  ln1g_ref, ln1b_ref, wrh_ref, wrl_ref,
                  x1_ref, h2_ref, lg_ref, ubuf_ref, *, tiles_per_seq):
    tm = MIX_TM
    first = (pl.program_id(0) % tiles_per_seq) == 0

    prev = up_ref[...].astype(F32)
    ubuf_ref[0:CONV_HALO, :] = jnp.where(first, jnp.zeros_like(prev), prev)
    ubuf_ref[CONV_HALO:, :] = u_ref[...].astype(F32)
    base = CONV_HALO - (CONV_K - 1)
    acc = jnp.zeros((tm, D_MODEL), F32)
    for j in range(CONV_K):
        acc = acc + cw_ref[j:j + 1, :] * ubuf_ref[base + j:base + j + tm, :]
    yc = _silu(_layer_norm(acc + cb_ref[...], clg_ref[...], clb_ref[...]))
    y_conv = jnp.dot(yc.astype(BF16), wpw_ref[...], preferred_element_type=F32) + bpw_ref[...]

    y_attn = jnp.dot(o_ref[...], wao_ref[...], preferred_element_type=F32)
    mix = ga_ref[...].astype(F32) * y_attn + gb_ref[...].astype(F32) * y_conv
    y = jnp.dot(mix.astype(BF16), wout_ref[...], preferred_element_type=F32)
    x1 = _layer_norm(DN_ALPHA * x_ref[...] + g1_ref[0] * y, ln1g_ref[...], ln1b_ref[...])
    x1_ref[...] = x1

    h2 = x1 * (1.0 + sc2_ref[0]) + sh2_ref[0]
    h2_hi = h2.astype(BF16)
    h2_ref[...] = h2_hi
    h2_lo = (h2 - h2_hi.astype(F32)).astype(BF16)
    lg_ref[...] = (jnp.dot(h2_hi, wrh_ref[...], preferred_element_type=F32)
                   + (jnp.dot(h2_hi, wrl_ref[...], preferred_element_type=F32)
                      + jnp.dot(h2_lo, wrh_ref[...], preferred_element_type=F32)))


def _mixer_epilogue(x2, o, u, ga, gb, g1, sc2, sh2, wao, cw, cb, clg, clb, wpw, bpw, wout,
                    ln1g, ln1b, wr_hi, wr_lo, seq):
    T, D = x2.shape
    tm = MIX_TM
    tiles_per_seq = seq // tm
    halo_per_tile = tm // CONV_HALO
    tok = pl.BlockSpec((tm, D), lambda i: (i, 0))
    prev = pl.BlockSpec((CONV_HALO, D), lambda i: (jnp.maximum(i * halo_per_tile - 1, 0), 0))
    mod = pl.BlockSpec((1, 1, D), lambda i: (i // tiles_per_seq, 0, 0))
    row = pl.BlockSpec((1, D), lambda i: (0, 0))
    mat = pl.BlockSpec((D, D), lambda i: (0, 0))
    rw = pl.BlockSpec((D, N_EXPERTS), lambda i: (0, 0))
    return pl.pallas_call(
        partial(_mixer_kernel, tiles_per_seq=tiles_per_seq),
        out_shape=[jax.ShapeDtypeStruct((T, D), F32),
                   jax.ShapeDtypeStruct((T, D), BF16),
                   jax.ShapeDtypeStruct((T, N_EXPERTS), F32)],
        grid=(T // tm,),
        in_specs=[tok, tok, tok, prev, tok, tok, mod, mod, mod,
                  mat, pl.BlockSpec((CONV_K, D), lambda i: (0, 0)), row, row, row, mat, row, mat,
                  row, row, rw, rw],
        out_specs=[tok, tok, pl.BlockSpec((tm, N_EXPERTS), lambda i: (i, 0))],
        scratch_shapes=[pltpu.VMEM((CONV_HALO + tm, D), F32)],
        compiler_params=_cparams("arbitrary"),
        name="mixer_epilogue",
    )(x2, o, u, u, ga, gb, g1, sc2, sh2, wao, cw, cb, clg, clb, wpw, bpw, wout,
      ln1g, ln1b, wr_hi, wr_lo)


def _first_argmax(v, lane):
    m = jnp.max(v, axis=-1, keepdims=True)
    i = jnp.min(jnp.where(v == m, lane, N_EXPERTS), axis=-1, keepdims=True)
    return m, i


def _route_kernel(lg_ref, rb_ref, idx_ref, gate_ref, cnt_ref):
    tm = ROUTE_TM
    scores = _sigmoid(lg_ref[...])
    choice = scores + rb_ref[...]
    lane = lax.broadcasted_iota(jnp.int32, (tm, N_EXPERTS), 1)
    grp = lane // GROUP_SIZE

    gs = []
    for g in range(N_GROUPS):
        cg = jnp.where(grp == g, choice, NEG_INF)
        m1, i1 = _first_argmax(cg, lane)
        m2 = jnp.max(jnp.where(lane == i1, NEG_INF, cg), axis=-1, keepdims=True)
        gs.append(m1 + m2)

    allowed = jnp.zeros((tm, N_EXPERTS), jnp.bool_)
    for g in range(N_GROUPS):
        ahead = jnp.zeros((tm, 1), jnp.int32)
        for o in range(N_GROUPS):
            if o == g:
                continue
            beats = (gs[o] >= gs[g]) if o < g else (gs[o] > gs[g])
            ahead = ahead + beats.astype(jnp.int32)
        allowed = allowed | ((grp == g) & (ahead < TOPK_GROUPS))

    masked = jnp.where(allowed, choice, NEG_INF)
    out_lane = lax.broadcasted_iota(jnp.int32, (tm, LANES), 1)
    idx_out = jnp.zeros((tm, LANES), jnp.int32)
    gate_out = jnp.zeros((tm, LANES), F32)
    sel = jnp.zeros((tm, N_EXPERTS), F32)
    total = jnp.zeros((tm, 1), F32)
    for k in range(TOP_K):
        _, i = _first_argmax(masked, lane)
        hit = lane == i
        sc = jnp.sum(jnp.where(hit, scores, 0.0), axis=-1, keepdims=True)
        masked = jnp.where(hit, NEG_INF, masked)
        sel = sel + hit.astype(F32)
        total = total + sc
        idx_out = jnp.where(out_lane == k, i, idx_out)
        gate_out = jnp.where(out_lane == k, sc, gate_out)
    idx_ref[...] = idx_out
    gate_ref[...] = gate_out / total * ROUTED_SCALE

    @pl.when(pl.program_id(0) == 0)
    def _():
        cnt_ref[...] = jnp.zeros_like(cnt_ref)
    cnt_ref[...] += jnp.sum(sel, axis=0, keepdims=True)


def _route(logits, router_bias):
    T = logits.shape[0]
    tm = ROUTE_TM
    return pl.pallas_call(
        _route_kernel,
        out_shape=[jax.ShapeDtypeStruct((T, LANES), jnp.int32),
                   jax.ShapeDtypeStruct((T, LANES), F32),
                   jax.ShapeDtypeStruct((1, N_EXPERTS), F32)],
        grid=(T // tm,),
        in_specs=[pl.BlockSpec((tm, N_EXPERTS), lambda i: (i, 0)),
                  pl.BlockSpec((1, N_EXPERTS), lambda i: (0, 0))],
        out_specs=[pl.BlockSpec((tm, LANES), lambda i: (i, 0)),
                   pl.BlockSpec((tm, LANES), lambda i: (i, 0)),
                   pl.BlockSpec((1, N_EXPERTS), lambda i: (0, 0))],
        compiler_params=_cparams("arbitrary"),
        name="route_topk",
    )(logits, router_bias.reshape(1, N_EXPERTS))


def _slot_kernel(idx_ref, start_ref, pos_ref, carry_ref):
    tm = ROUTE_TM

    @pl.when(pl.program_id(0) == 0)
    def _():
        carry_ref[...] = jnp.zeros_like(carry_ref)

    idx = idx_ref[...]
    lane = lax.broadcasted_iota(jnp.int32, (tm, N_EXPERTS), 1)
    hits = [lane == idx[:, k:k + 1] for k in range(TOP_K)]
    sel = jnp.zeros((tm, N_EXPERTS), F32)
    for h in hits:
        sel = sel + h.astype(F32)
    r = lax.broadcasted_iota(jnp.int32, (tm, tm), 0)
    c = lax.broadcasted_iota(jnp.int32, (tm, tm), 1)
    earlier = (c < r).astype(BF16)
    rank = jnp.dot(earlier, sel.astype(BF16), preferred_element_type=F32)
    slot = rank + carry_ref[...] + start_ref[...]
    out_lane = lax.broadcasted_iota(jnp.int32, (tm, LANES), 1)
    pos = jnp.zeros((tm, LANES), jnp.int32)
    for k, h in enumerate(hits):
        pk = jnp.sum(jnp.where(h, slot, 0.0), axis=-1, keepdims=True).astype(jnp.int32)
        pos = jnp.where(out_lane == k, pk, pos)
    pos_ref[...] = pos
    carry_ref[...] += jnp.sum(sel, axis=0, keepdims=True)


def _assign_slots(idx, pad_start):
    T = idx.shape[0]
    tm = ROUTE_TM
    return pl.pallas_call(
        _slot_kernel,
        out_shape=jax.ShapeDtypeStruct((T, LANES), jnp.int32),
        grid=(T // tm,),
        in_specs=[pl.BlockSpec((tm, LANES), lambda i: (i, 0)),
                  pl.BlockSpec((1, N_EXPERTS), lambda i: (0, 0))],
        out_specs=pl.BlockSpec((tm, LANES), lambda i: (i, 0)),
        scratch_shapes=[pltpu.VMEM((1, N_EXPERTS), F32)],
        compiler_params=_cparams("arbitrary"),
        name="assign_slots",
    )(idx, pad_start)


def _expert_kernel(be_ref, nu_ref, xs_ref, w1_ref, w3_ref, w2_ref, ys_ref):
    @pl.when(pl.program_id(0) < nu_ref[0])
    def _():
        xs = xs_ref[...]
        a = jnp.dot(xs, w1_ref[0].astype(BF16), preferred_element_type=F32)
        b = jnp.dot(xs, w3_ref[0].astype(BF16), preferred_element_type=F32)
        hb = (_silu(a) * b).astype(BF16)
        ys_ref[...] = jnp.dot(hb, w2_ref[0].astype(BF16),
                              preferred_element_type=F32).astype(BF16)


def _routed_experts(xs, blk_expert, n_used, w1, w3, w2):
    P, D = xs.shape
    blk = EXPERT_BLK
    F = w1.shape[-1]
    grid_spec = pltpu.PrefetchScalarGridSpec(
        num_scalar_prefetch=2,
        grid=(P // blk,),
        in_specs=[pl.BlockSpec((blk, D), lambda b, be, nu: (jnp.minimum(b, nu[0] - 1), 0)),
                  pl.BlockSpec((1, D, F), lambda b, be, nu: (be[b], 0, 0)),
                  pl.BlockSpec((1, D, F), lambda b, be, nu: (be[b], 0, 0)),
                  pl.BlockSpec((1, F, D), lambda b, be, nu: (be[b], 0, 0))],
        out_specs=pl.BlockSpec((blk, D), lambda b, be, nu: (b, 0)),
    )
    return pl.pallas_call(
        _expert_kernel,
        out_shape=jax.ShapeDtypeStruct((P, D), BF16),
        grid_spec=grid_spec,
        compiler_params=_cparams("arbitrary"),
        name="routed_experts",
    )(blk_expert, n_used, xs, w1, w3, w2)


def _final_kernel(x1_ref, h2_ref, yg_ref, gate_ref, g2_ref, ws1_ref, ws3_ref, ws2_ref,
                  lng_ref, lnb_ref, o_ref):
    D = D_MODEL
    gate = gate_ref[...]
    routed = jnp.zeros(x1_ref.shape, F32)
    for k in range(TOP_K):
        routed = routed + yg_ref[:, k * D:(k + 1) * D].astype(F32) * gate[:, k:k + 1]
    h2 = h2_ref[...]
    a = jnp.dot(h2, ws1_ref[...], preferred_element_type=F32)
    b = jnp.dot(h2, ws3_ref[...], preferred_element_type=F32)
    shared = jnp.dot((_silu(a) * b).astype(BF16), ws2_ref[...], preferred_element_type=F32)
    y2 = routed + shared
    o_ref[...] = _layer_norm(DN_ALPHA * x1_ref[...] + g2_ref[0] * y2, lng_ref[...], lnb_ref[...])


def _final(x1, h2, yg, gate, g2, ws1, ws3, ws2, ln2g, ln2b, seq):
    T, D = x1.shape
    tm = FINAL_TM
    tiles_per_seq = seq // tm
    F = ws1.shape[-1]
    tok = pl.BlockSpec((tm, D), lambda i: (i, 0))
    row = pl.BlockSpec((1, D), lambda i: (0, 0))
    return pl.pallas_call(
        _final_kernel,
        out_shape=jax.ShapeDtypeStruct((T, D), F32),
        grid=(T // tm,),
        in_specs=[tok, tok,
                  pl.BlockSpec((tm, TOP_K * D), lambda i: (i, 0)),
                  pl.BlockSpec((tm, LANES), lambda i: (i, 0)),
                  pl.BlockSpec((1, 1, D), lambda i: (i // tiles_per_seq, 0, 0)),
                  pl.BlockSpec((D, F), lambda i: (0, 0)),
                  pl.BlockSpec((D, F), lambda i: (0, 0)),
                  pl.BlockSpec((F, D), lambda i: (0, 0)),
                  row, row],
        out_specs=tok,
        compiler_params=_cparams("arbitrary"),
        name="moe_combine_final",
    )(x1, h2, yg, gate, g2, ws1, ws3, ws2, ln2g, ln2b)


def kernel(x, c, ada_w, ada_b, w_in, b_in, lambda_q1, lambda_k1, lambda_q2, lambda_k2, subln_g, w_attn_o, conv_w, conv_b, conv_ln_g, conv_ln_b, conv_pw_w, conv_pw_b, w_out, ln1_g, ln1_b, router_w, router_bias, w1, w3, w2, ws1, ws3, ws2, ln2_g, ln2_b):
    B, S, D = x.shape
    T = B * S
    l = 0
    x2 = x.reshape(T, D)
    row = lambda a: a.reshape(1, -1)

    mod = _ada_modulation(c, ada_w[l], ada_b[l])
    sh1, sc1, g1, sh2, sc2, g2 = [m.reshape(B, 1, D) for m in jnp.split(mod, 6, axis=-1)]

    q, k, v, u, ga, gb = _in_projection(x2, sc1, sh1, w_in[l].astype(BF16), row(b_in[l]), S)
    o = _diff_attention(q, k, v, row(lambda_q1[l]), row(lambda_k1[l]), row(lambda_q2[l]),
                        row(lambda_k2[l]), row(subln_g[l]), B, S)

    wr = router_w[l]
    wr_hi = wr.astype(BF16)
    wr_lo = (wr - wr_hi.astype(F32)).astype(BF16)
    x1, h2, logits = _mixer_epilogue(
        x2, o, u, ga, gb, g1, sc2, sh2, w_attn_o[l].astype(BF16), conv_w[l], row(conv_b[l]),
        row(conv_ln_g[l]), row(conv_ln_b[l]), conv_pw_w[l].astype(BF16), row(conv_pw_b[l]),
        w_out[l].astype(BF16), row(ln1_g[l]), row(ln1_b[l]), wr_hi, wr_lo, S)

    idx, gate, counts = _route(logits, router_bias[l])

    blk = EXPERT_BLK
    cnt = counts.reshape(N_EXPERTS).astype(jnp.int32)
    padded = (cnt + blk - 1) // blk * blk
    pad_end = jnp.cumsum(padded)
    pad_start = pad_end - padded
    P = T * TOP_K + N_EXPERTS * blk
    n_blocks = P // blk
    blk_expert = jnp.minimum(
        jnp.searchsorted(pad_end, jnp.arange(n_blocks, dtype=jnp.int32) * blk, side='right'),
        N_EXPERTS - 1).astype(jnp.int32)
    n_used = (pad_end[-1:] // blk).astype(jnp.int32)

    pos = _assign_slots(idx, pad_start.astype(F32).reshape(1, N_EXPERTS))[:, :TOP_K]

    pos_flat = pos.reshape(T * TOP_K)
    tok_of_slot = jnp.zeros((P,), jnp.int32).at[pos_flat].set(
        jnp.arange(T * TOP_K, dtype=jnp.int32) // TOP_K)
    xs = jnp.take(h2, tok_of_slot, axis=0)
    ys = _routed_experts(xs, blk_expert, n_used, w1[l], w3[l], w2[l])
    yg = jnp.take(ys, pos_flat, axis=0).reshape(T, TOP_K * D)

    out = _final(x1, h2, yg, gate, g2, ws1[l].astype(BF16), ws3[l].astype(BF16),
                 ws2[l].astype(BF16), row(ln2_g[l]), row(ln2_b[l]), S)
    return out.reshape(B, S, D)
```

```python
import math
from functools import partial

import jax
import jax.numpy as jnp
from jax import lax
from jax.experimental import pallas as pl
from jax.experimental.pallas import tpu as pltpu

D_MODEL = 1024
N_HEADS = 8
HEAD_DIM = 64
V_DIM = 128
CONV_K = 31
N_EXPERTS = 256
TOP_K = 8
N_GROUPS = 8
GROUP_SIZE = N_EXPERTS // N_GROUPS
TOPK_GROUPS = 4
D_EXPERT = 256
ROUTED_SCALE = 2.5
LN_EPS = 1e-5
RMS_EPS = 1e-5
DEPTH = 1
DN_ALPHA = (2.0 * DEPTH) ** 0.25
LAMBDA_INIT = 0.8 - 0.6 * math.exp(-0.3 * 0)
Q_SCALE = HEAD_DIM ** -0.5 * math.log2(math.e)

LANES = 128
VMEM_LIMIT_BYTES = 56 * 1024 * 1024

PROJ_TM = 512
ATTN_TQ = 512
MIX_TM = 256
CONV_HALO = 32
ROUTE_TM = 512
EXPERT_BLK = 256
FINAL_TM = 256

F32 = jnp.float32
BF16 = jnp.bfloat16
NEG_INF = float("-inf")


def _cparams(*sem):
    return pltpu.CompilerParams(dimension_semantics=sem, vmem_limit_bytes=VMEM_LIMIT_BYTES)


def _sigmoid(x):
    return 1.0 / (1.0 + jnp.exp(-x))


def _silu(x):
    return x * _sigmoid(x)


def _layer_norm(r, g, b):
    mu = jnp.mean(r, axis=-1, keepdims=True)
    d = r - mu
    var = jnp.mean(d * d, axis=-1, keepdims=True)
    return d * lax.rsqrt(var + LN_EPS) * g + b


def _ada_kernel(c_ref, w_ref, b_ref, o_ref):
    a = _silu(c_ref[...])
    o_ref[...] = jnp.dot(a, w_ref[...], preferred_element_type=F32,
                         precision=lax.Precision.HIGHEST) + b_ref[...]


def _ada_modulation(c, w, b):
    B, D = c.shape
    n = w.shape[1] // D
    return pl.pallas_call(
        _ada_kernel,
        out_shape=jax.ShapeDtypeStruct((B, n * D), F32),
        grid=(n,),
        in_specs=[pl.BlockSpec((B, D), lambda j: (0, 0)),
                  pl.BlockSpec((D, D), lambda j: (0, j)),
                  pl.BlockSpec((1, D), lambda j: (0, j))],
        out_specs=pl.BlockSpec((B, D), lambda j: (0, j)),
        compiler_params=_cparams("arbitrary"),
        name="ada_modulation",
    )(c, w, b.reshape(1, n * D))


def _inproj_kernel(x_ref, sc_ref, sh_ref, w_ref, b_ref,
                   q_ref, k_ref, v_ref, u_ref, ga_ref, gb_ref):
    D = D_MODEL
    h = (x_ref[...] * (1.0 + sc_ref[0]) + sh_ref[0]).astype(BF16)

    def proj(j):
        return (jnp.dot(h, w_ref[:, j * D:(j + 1) * D], preferred_element_type=F32)
                + b_ref[:, j * D:(j + 1) * D])

    q_ref[...] = (proj(0) * Q_SCALE).astype(BF16)
    k_ref[...] = proj(1).astype(BF16)
    v_ref[...] = proj(2).astype(BF16)
    u_ref[...] = (proj(3) * _sigmoid(proj(4))).astype(BF16)
    ga_ref[...] = _sigmoid(proj(5)).astype(BF16)
    gb_ref[...] = _sigmoid(proj(6)).astype(BF16)


def _in_projection(x2, sc1, sh1, w_in, b_in, seq):
    T, D = x2.shape
    tm = PROJ_TM
    tiles_per_seq = seq // tm
    ncols = w_in.shape[1]
    tok = pl.BlockSpec((tm, D), lambda i: (i, 0))
    mod = pl.BlockSpec((1, 1, D), lambda i: (i // tiles_per_seq, 0, 0))
    out = jax.ShapeDtypeStruct((T, D), BF16)
    return pl.pallas_call(
        _inproj_kernel,
        out_shape=[out] * 6,
        grid=(T // tm,),
        in_specs=[tok, mod, mod,
                  pl.BlockSpec((D, ncols), lambda i: (0, 0)),
                  pl.BlockSpec((1, ncols), lambda i: (0, 0))],
        out_specs=[tok] * 6,
        compiler_params=_cparams("arbitrary"),
        name="in_projection",
    )(x2, sc1, sh1, w_in, b_in)


def _attn_kernel(lq1_ref, lk1_ref, lq2_ref, lk2_ref, g_ref, q_ref, k_ref, v_ref, o_ref,
                 m_ref, l_ref, acc_ref):
    tq = ATTN_TQ
    qi = pl.program_id(2)
    lam = (jnp.exp(jnp.sum(lq1_ref[...] * lk1_ref[...], axis=-1, keepdims=True))
           - jnp.exp(jnp.sum(lq2_ref[...] * lk2_ref[...], axis=-1, keepdims=True))
           + LAMBDA_INIT)

    q = q_ref[...]
    lane = lax.broadcasted_iota(jnp.int32, q.shape, 1)
    zero = jnp.zeros_like(q)
    q2 = jnp.concatenate([jnp.where(lane < HEAD_DIM, q, zero),
                          jnp.where(lane >= HEAD_DIM, q, zero)], axis=0)

    def scores(j):
        start = pl.multiple_of(j * tq, tq)
        return lax.dot_general(k_ref[pl.ds(start, tq), :], q2, (((1,), (1,)), ((), ())),
                               preferred_element_type=F32)

    def update(j, s, masked):
        if masked:
            key = lax.broadcasted_iota(jnp.int32, s.shape, 0)
            qry = lax.broadcasted_iota(jnp.int32, s.shape, 1)
            qry = jnp.where(qry >= tq, qry - tq, qry)
            s = jnp.where(key <= qry, s, NEG_INF)
        m_prev = m_ref[...]
        m_new = jnp.maximum(m_prev, jnp.max(s, axis=0, keepdims=True))
        alpha = jnp.exp2(m_prev - m_new)
        p = jnp.exp2(s - m_new)
        l_ref[...] = alpha * l_ref[...] + jnp.sum(p, axis=0, keepdims=True)
        start = pl.multiple_of(j * tq, tq)
        pv = lax.dot_general(v_ref[pl.ds(start, tq), :], p.astype(BF16),
                             (((0,), (0,)), ((), ())), preferred_element_type=F32)
        acc_ref[...] = alpha * acc_ref[...] + pv
        m_ref[...] = m_new

    m_ref[...] = jnp.full(m_ref.shape, NEG_INF, F32)
    l_ref[...] = jnp.zeros(l_ref.shape, F32)
    acc_ref[...] = jnp.zeros(acc_ref.shape, F32)

    def body(j, s):
        s_next = scores(j + 1)
        update(j, s, False)
        return s_next

    s = lax.fori_loop(0, qi, body, scores(0))
    update(qi, s, True)

    o = acc_ref[...] / l_ref[...]
    o = o[:, :tq] - lam * o[:, tq:]
    o = o * lax.rsqrt(jnp.mean(o * o, axis=0, keepdims=True) + RMS_EPS) * g_ref[...]
    o_ref[...] = (o * (1.0 - LAMBDA_INIT)).T.astype(BF16)


def _diff_attention(q, k, v, lq1, lk1, lq2, lk2, subln_g, batch, seq):
    T, D = q.shape
    tq = ATTN_TQ
    nq = seq // tq
    lam_spec = pl.BlockSpec((1, HEAD_DIM), lambda b, h, i: (0, 0))
    q_spec = pl.BlockSpec((tq, V_DIM), lambda b, h, i: (b * nq + i, h))
    kv_spec = pl.BlockSpec((seq, V_DIM), lambda b, h, i: (b, h))
    return pl.pallas_call(
        _attn_kernel,
        out_shape=jax.ShapeDtypeStruct((T, D), BF16),
        grid=(batch, N_HEADS, nq),
        in_specs=[lam_spec, lam_spec, lam_spec, lam_spec,
                  pl.BlockSpec((V_DIM, 1), lambda b, h, i: (0, 0)),
                  q_spec, kv_spec, kv_spec],
        out_specs=q_spec,
        scratch_shapes=[pltpu.VMEM((1, 2 * tq), F32), pltpu.VMEM((1, 2 * tq), F32),
                        pltpu.VMEM((V_DIM, 2 * tq), F32)],
        compiler_params=_cparams("arbitrary", "arbitrary", "arbitrary"),
        name="diff_attention",
    )(lq1, lk1, lq2, lk2, subln_g.reshape(V_DIM, 1), q, k, v)


def _mixer_kernel(x_ref, o_ref, u_ref, up_ref, ga_ref, gb_ref, g1_ref, sc2_ref, sh2_ref,
                  wao_ref, cw_ref, cb_ref, clg_ref, clb_ref, wpw_ref, bpw_ref, wout_ref,
                  ln1g_ref, ln1b_ref, wrh_ref, wrl_ref,
                  x1_ref, h2_ref, lg_ref, ubuf_ref, *, tiles_per_seq):
    tm = MIX_TM
    first = (pl.program_id(0) % tiles_per_seq) == 0

    prev = up_ref[...].astype(F32)
    ubuf_ref[0:CONV_HALO, :] = jnp.where(first, jnp.zeros_like(prev), prev)
    ubuf_ref[CONV_HALO:, :] = u_ref[...].astype(F32)
    base = CONV_HALO - (CONV_K - 1)
    acc = jnp.zeros((tm, D_MODEL), F32)
    for j in range(CONV_K):
        acc = acc + cw_ref[j:j + 1, :] * ubuf_ref[base + j:base + j + tm, :]
    yc = _silu(_layer_norm(acc + cb_ref[...], clg_ref[...], clb_ref[...]))
    y_conv = jnp.dot(yc.astype(BF16), wpw_ref[...], preferred_element_type=F32) + bpw_ref[...]

    y_attn = jnp.dot(o_ref[...], wao_ref[...], preferred_element_type=F32)
    mix = ga_ref[...].astype(F32) * y_attn + gb_ref[...].astype(F32) * y_conv
    y = jnp.dot(mix.astype(BF16), wout_ref[...], preferred_element_type=F32)
    x1 = _layer_norm(DN_ALPHA * x_ref[...] + g1_ref[0] * y, ln1g_ref[...], ln1b_ref[...])
    x1_ref[...] = x1

    h2 = x1 * (1.0 + sc2_ref[0]) + sh2_ref[0]
    h2_hi = h2.astype(BF16)
    h2_ref[...] = h2_hi
    h2_lo = (h2 - h2_hi.astype(F32)).astype(BF16)
    lg_ref[...] = (jnp.dot(h2_hi, wrh_ref[...], preferred_element_type=F32)
                   + (jnp.dot(h2_hi, wrl_ref[...], preferred_element_type=F32)
                      + jnp.dot(h2_lo, wrh_ref[...], preferred_element_type=F32)))


def _mixer_epilogue(x2, o, u, ga, gb, g1, sc2, sh2, wao, cw, cb, clg, clb, wpw, bpw, wout,
                    ln1g, ln1b, wr_hi, wr_lo, seq):
    T, D = x2.shape
    tm = MIX_TM
    tiles_per_seq = seq // tm
    halo_per_tile = tm // CONV_HALO
    tok = pl.BlockSpec((tm, D), lambda i: (i, 0))
    prev = pl.BlockSpec((CONV_HALO, D), lambda i: (jnp.maximum(i * halo_per_tile - 1, 0), 0))
    mod = pl.BlockSpec((1, 1, D), lambda i: (i // tiles_per_seq, 0, 0))
    row = pl.BlockSpec((1, D), lambda i: (0, 0))
    mat = pl.BlockSpec((D, D), lambda i: (0, 0))
    rw = pl.BlockSpec((D, N_EXPERTS), lambda i: (0, 0))
    return pl.pallas_call(
        partial(_mixer_kernel, tiles_per_seq=tiles_per_seq),
        out_shape=[jax.ShapeDtypeStruct((T, D), F32),
                   jax.ShapeDtypeStruct((T, D), BF16),
                   jax.ShapeDtypeStruct((T, N_EXPERTS), F32)],
        grid=(T // tm,),
        in_specs=[tok, tok, tok, prev, tok, tok, mod, mod, mod,
                  mat, pl.BlockSpec((CONV_K, D), lambda i: (0, 0)), row, row, row, mat, row, mat,
                  row, row, rw, rw],
        out_specs=[tok, tok, pl.BlockSpec((tm, N_EXPERTS), lambda i: (i, 0))],
        scratch_shapes=[pltpu.VMEM((CONV_HALO + tm, D), F32)],
        compiler_params=_cparams("arbitrary"),
        name="mixer_epilogue",
    )(x2, o, u, u, ga, gb, g1, sc2, sh2, wao, cw, cb, clg, clb, wpw, bpw, wout,
      ln1g, ln1b, wr_hi, wr_lo)


def _first_argmax(v, lane):
    m = jnp.max(v, axis=-1, keepdims=True)
    i = jnp.min(jnp.where(v == m, lane, N_EXPERTS), axis=-1, keepdims=True)
    return m, i


def _route_kernel(lg_ref, rb_ref, idx_ref, gate_ref, cnt_ref):
    tm = ROUTE_TM
    scores = _sigmoid(lg_ref[...])
    choice = scores + rb_ref[...]
    lane = lax.broadcasted_iota(jnp.int32, (tm, N_EXPERTS), 1)
    grp = lane // GROUP_SIZE

    gs = []
    for g in range(N_GROUPS):
        cg = jnp.where(grp == g, choice, NEG_INF)
        m1, i1 = _first_argmax(cg, lane)
        m2 = jnp.max(jnp.where(lane == i1, NEG_INF, cg), axis=-1, keepdims=True)
        gs.append(m1 + m2)

    allowed = jnp.zeros((tm, N_EXPERTS), jnp.bool_)
    for g in range(N_GROUPS):
        ahead = jnp.zeros((tm, 1), jnp.int32)
        for o in range(N_GROUPS):
            if o == g:
                continue
            beats = (gs[o] >= gs[g]) if o < g else (gs[o] > gs[g])
            ahead = ahead + beats.astype(jnp.int32)
        allowed = allowed | ((grp == g) & (ahead < TOPK_GROUPS))

    masked = jnp.where(allowed, choice, NEG_INF)
    out_lane = lax.broadcasted_iota(jnp.int32, (tm, LANES), 1)
    idx_out = jnp.zeros((tm, LANES), jnp.int32)
    gate_out = jnp.zeros((tm, LANES), F32)
    sel = jnp.zeros((tm, N_EXPERTS), F32)
    total = jnp.zeros((tm, 1), F32)
    for k in range(TOP_K):
        _, i = _first_argmax(masked, lane)
        hit = lane == i
        sc = jnp.sum(jnp.where(hit, scores, 0.0), axis=-1, keepdims=True)
        masked = jnp.where(hit, NEG_INF, masked)
        sel = sel + hit.astype(F32)
        total = total + sc
        idx_out = jnp.where(out_lane == k, i, idx_out)
        gate_out = jnp.where(out_lane == k, sc, gate_out)
    idx_ref[...] = idx_out
    gate_ref[...] = gate_out / total * ROUTED_SCALE

    @pl.when(pl.program_id(0) == 0)
    def _():
        cnt_ref[...] = jnp.zeros_like(cnt_ref)
    cnt_ref[...] += jnp.sum(sel, axis=0, keepdims=True)


def _route(logits, router_bias):
    T = logits.shape[0]
    tm = ROUTE_TM
    return pl.pallas_call(
        _route_kernel,
        out_shape=[jax.ShapeDtypeStruct((T, LANES), jnp.int32),
                   jax.ShapeDtypeStruct((T, LANES), F32),
                   jax.ShapeDtypeStruct((1, N_EXPERTS), F32)],
        grid=(T // tm,),
        in_specs=[pl.BlockSpec((tm, N_EXPERTS), lambda i: (i, 0)),
                  pl.BlockSpec((1, N_EXPERTS), lambda i: (0, 0))],
        out_specs=[pl.BlockSpec((tm, LANES), lambda i: (i, 0)),
                   pl.BlockSpec((tm, LANES), lambda i: (i, 0)),
                   pl.BlockSpec((1, N_EXPERTS), lambda i: (0, 0))],
        compiler_params=_cparams("arbitrary"),
        name="route_topk",
    )(logits, router_bias.reshape(1, N_EXPERTS))


def _slot_kernel(idx_ref, start_ref, pos_ref, carry_ref):
    tm = ROUTE_TM

    @pl.when(pl.program_id(0) == 0)
    def _():
        carry_ref[...] = jnp.zeros_like(carry_ref)

    idx = idx_ref[...]
    lane = lax.broadcasted_iota(jnp.int32, (tm, N_EXPERTS), 1)
    hits = [lane == idx[:, k:k + 1] for k in range(TOP_K)]
    sel = jnp.zeros((tm, N_EXPERTS), F32)
    for h in hits:
        sel = sel + h.astype(F32)
    r = lax.broadcasted_iota(jnp.int32, (tm, tm), 0)
    c = lax.broadcasted_iota(jnp.int32, (tm, tm), 1)
    earlier = (c < r).astype(BF16)
    rank = jnp.dot(earlier, sel.astype(BF16), preferred_element_type=F32)
    slot = rank + carry_ref[...] + start_ref[...]
    out_lane = lax.broadcasted_iota(jnp.int32, (tm, LANES), 1)
    pos = jnp.zeros((tm, LANES), jnp.int32)
    for k, h in enumerate(hits):
        pk = jnp.sum(jnp.where(h, slot, 0.0), axis=-1, keepdims=True).astype(jnp.int32)
        pos = jnp.where(out_lane == k, pk, pos)
    pos_ref[...] = pos
    carry_ref[...] += jnp.sum(sel, axis=0, keepdims=True)


def _assign_slots(idx, pad_start):
    T = idx.shape[0]
    tm = ROUTE_TM
    return pl.pallas_call(
        _slot_kernel,
        out_shape=jax.ShapeDtypeStruct((T, LANES), jnp.int32),
        grid=(T // tm,),
        in_specs=[pl.BlockSpec((tm, LANES), lambda i: (i, 0)),
                  pl.BlockSpec((1, N_EXPERTS), lambda i: (0, 0))],
        out_specs=pl.BlockSpec((tm, LANES), lambda i: (i, 0)),
        scratch_shapes=[pltpu.VMEM((1, N_EXPERTS), F32)],
        compiler_params=_cparams("arbitrary"),
        name="assign_slots",
    )(idx, pad_start)


def _expert_kernel(be_ref, nu_ref, xs_ref, w1_ref, w3_ref, w2_ref, ys_ref):
    @pl.when(pl.program_id(0) < nu_ref[0])
    def _():
        xs = xs_ref[...]
        a = jnp.dot(xs, w1_ref[0].astype(BF16), preferred_element_type=F32)
        b = jnp.dot(xs, w3_ref[0].astype(BF16), preferred_element_type=F32)
        hb = (_silu(a) * b).astype(BF16)
        ys_ref[...] = jnp.dot(hb, w2_ref[0].astype(BF16),
                              preferred_element_type=F32).astype(BF16)


def _routed_experts(xs, blk_expert, n_used, w1, w3, w2):
    P, D = xs.shape
    blk = EXPERT_BLK
    F = w1.shape[-1]
    grid_spec = pltpu.PrefetchScalarGridSpec(
        num_scalar_prefetch=2,
        grid=(P // blk,),
        in_specs=[pl.BlockSpec((blk, D), lambda b, be, nu: (jnp.minimum(b, nu[0] - 1), 0)),
                  pl.BlockSpec((1, D, F), lambda b, be, nu: (be[b], 0, 0)),
                  pl.BlockSpec((1, D, F), lambda b, be, nu: (be[b], 0, 0)),
                  pl.BlockSpec((1, F, D), lambda b, be, nu: (be[b], 0, 0))],
        out_specs=pl.BlockSpec((blk, D), lambda b, be, nu: (b, 0)),
    )
    return pl.pallas_call(
        _expert_kernel,
        out_shape=jax.ShapeDtypeStruct((P, D), BF16),
        grid_spec=grid_spec,
        compiler_params=_cparams("arbitrary"),
        name="routed_experts",
    )(blk_expert, n_used, xs, w1, w3, w2)


def _final_kernel(x1_ref, h2_ref, yg_ref, gate_ref, g2_ref, ws1_ref, ws3_ref, ws2_ref,
                  lng_ref, lnb_ref, o_ref):
    D = D_MODEL
    gate = gate_ref[...]
    routed = jnp.zeros(x1_ref.shape, F32)
    for k in range(TOP_K):
        routed = routed + yg_ref[:, k * D:(k + 1) * D].astype(F32) * gate[:, k:k + 1]
    h2 = h2_ref[...]
    a = jnp.dot(h2, ws1_ref[...], preferred_element_type=F32)
    b = jnp.dot(h2, ws3_ref[...], preferred_element_type=F32)
    shared = jnp.dot((_silu(a) * b).astype(BF16), ws2_ref[...], preferred_element_type=F32)
    y2 = routed + shared
    o_ref[...] = _layer_norm(DN_ALPHA * x1_ref[...] + g2_ref[0] * y2, lng_ref[...], lnb_ref[...])


def _final(x1, h2, yg, gate, g2, ws1, ws3, ws2, ln2g, ln2b, seq):
    T, D = x1.shape
    tm = FINAL_TM
    tiles_per_seq = seq // tm
    F = ws1.shape[-1]
    tok = pl.BlockSpec((tm, D), lambda i: (i, 0))
    row = pl.BlockSpec((1, D), lambda i: (0, 0))
    return pl.pallas_call(
        _final_kernel,
        out_shape=jax.ShapeDtypeStruct((T, D), F32),
        grid=(T // tm,),
        in_specs=[tok, tok,
                  pl.BlockSpec((tm, TOP_K * D), lambda i: (i, 0)),
                  pl.BlockSpec((tm, LANES), lambda i: (i, 0)),
                  pl.BlockSpec((1, 1, D), lambda i: (i // tiles_per_seq, 0, 0)),
                  pl.BlockSpec((D, F), lambda i: (0, 0)),
                  pl.BlockSpec((D, F), lambda i: (0, 0)),
                  pl.BlockSpec((F, D), lambda i: (0, 0)),
                  row, row],
        out_specs=tok,
        compiler_params=_cparams("arbitrary"),
        name="moe_combine_final",
    )(x1, h2, yg, gate, g2, ws1, ws3, ws2, ln2g, ln2b)


def kernel(x, c, ada_w, ada_b, w_in, b_in, lambda_q1, lambda_k1, lambda_q2, lambda_k2, subln_g, w_attn_o, conv_w, conv_b, conv_ln_g, conv_ln_b, conv_pw_w, conv_pw_b, w_out, ln1_g, ln1_b, router_w, router_bias, w1, w3, w2, ws1, ws3, ws2, ln2_g, ln2_b):
    B, S, D = x.shape
    T = B * S
    l = 0
    x2 = x.reshape(T, D)
    row = lambda a: a.reshape(1, -1)

    mod = _ada_modulation(c, ada_w[l], ada_b[l])
    sh1, sc1, g1, sh2, sc2, g2 = [m.reshape(B, 1, D) for m in jnp.split(mod, 6, axis=-1)]

    q, k, v, u, ga, gb = _in_projection(x2, sc1, sh1, w_in[l].astype(BF16), row(b_in[l]), S)
    o = _diff_attention(q, k, v, row(lambda_q1[l]), row(lambda_k1[l]), row(lambda_q2[l]),
                        row(lambda_k2[l]), row(subln_g[l]), B, S)

    wr = router_w[l]
    wr_hi = wr.astype(BF16)
    wr_lo = (wr - wr_hi.astype(F32)).astype(BF16)
    x1, h2, logits = _mixer_epilogue(
        x2, o, u, ga, gb, g1, sc2, sh2, w_attn_o[l].astype(BF16), conv_w[l], row(conv_b[l]),
        row(conv_ln_g[l]), row(conv_ln_b[l]), conv_pw_w[l].astype(BF16), row(conv_pw_b[l]),
        w_out[l].astype(BF16), row(ln1_g[l]), row(ln1_b[l]), wr_hi, wr_lo, S)

    idx, gate, counts = _route(logits, router_bias[l])

    blk = EXPERT_BLK
    cnt = counts.reshape(N_EXPERTS).astype(jnp.int32)
    padded = (cnt + blk - 1) // blk * blk
    pad_end = jnp.cumsum(padded)
    pad_start = pad_end - padded
    P = T * TOP_K + N_EXPERTS * blk
    n_blocks = P // blk
    blk_expert = jnp.minimum(
        jnp.searchsorted(pad_end, jnp.arange(n_blocks, dtype=jnp.int32) * blk, side='right'),
        N_EXPERTS - 1).astype(jnp.int32)
    n_used = (pad_end[-1:] // blk).astype(jnp.int32)

    pos = _assign_slots(idx, pad_start.astype(F32).reshape(1, N_EXPERTS))[:, :TOP_K]

    pos_flat = pos.reshape(T * TOP_K)
    tok_of_slot = jnp.zeros((P,), jnp.int32).at[pos_flat].set(
        jnp.arange(T * TOP_K, dtype=jnp.int32) // TOP_K)
    xs = jnp.take(h2, tok_of_slot, axis=0)
    ys = _routed_experts(xs, blk_expert, n_used, w1[l], w3[l], w2[l])
    yg = jnp.take(ys, pos_flat, axis=0).reshape(T, TOP_K * D)

    out = _final(x1, h2, yg, gate, g2, ws1[l].astype(BF16), ws3[l].astype(BF16),
                 ws2[l].astype(BF16), row(ln2_g[l]), row(ln2_b[l]), S)
    return out.reshape(B, S, D)
```

```python
import math
from functools import partial

import jax
import jax.numpy as jnp
from jax import lax
from jax.experimental import pallas as pl
from jax.experimental.pallas import tpu as pltpu
from jax.experimental.pallas import tpu_sc as plsc

D_MODEL = 1024
N_HEADS = 8
HEAD_DIM = 64
V_DIM = 128
CONV_K = 31
N_EXPERTS = 256
TOP_K = 8
N_GROUPS = 8
GROUP_SIZE = N_EXPERTS // N_GROUPS
TOPK_GROUPS = 4
D_EXPERT = 256
ROUTED_SCALE = 2.5
LN_EPS = 1e-5
RMS_EPS = 1e-5
DEPTH = 1
DN_ALPHA = (2.0 * DEPTH) ** 0.25
LAMBDA_INIT = 0.8 - 0.6 * math.exp(-0.3 * 0)
Q_SCALE = HEAD_DIM ** -0.5 * math.log2(math.e)

LANES = 128
VMEM_LIMIT_BYTES = 56 * 1024 * 1024

PROJ_TM = 512
ATTN_TQ = 512
MIX_TM = 256
CONV_HALO = 32
ROUTE_TM = 512
EXPERT_BLK = 256
FINAL_TM = 256
SC_WINDOW = 128

F32 = jnp.float32
BF16 = jnp.bfloat16
U32 = jnp.uint32
NEG_INF = float("-inf")


def _cparams(*sem):
    return pltpu.CompilerParams(dimension_semantics=sem, vmem_limit_bytes=VMEM_LIMIT_BYTES)


def _sigmoid(x):
    return 1.0 / (1.0 + jnp.exp(-x))


def _silu(x):
    return x * _sigmoid(x)


def _pack_halves(x):
    half = x.shape[-1] // 2
    bits = lambda v: lax.bitcast_convert_type(v.astype(BF16).astype(F32), U32)
    return (bits(x[:, :half]) >> 16) | (bits(x[:, half:]) & jnp.uint32(0xFFFF0000))


def _unpack_halves(w):
    lo = lax.bitcast_convert_type(w << 16, F32)
    hi = lax.bitcast_convert_type(w & jnp.uint32(0xFFFF0000), F32)
    return lo, hi


def _layer_norm(r, g, b):
    mu = jnp.mean(r, axis=-1, keepdims=True)
    d = r - mu
    var = jnp.mean(d * d, axis=-1, keepdims=True)
    return d * lax.rsqrt(var + LN_EPS) * g + b


def _ada_kernel(c_ref, w_ref, b_ref, o_ref):
    a = _silu(c_ref[...])
    o_ref[...] = jnp.dot(a, w_ref[...], preferred_element_type=F32,
                         precision=lax.Precision.HIGHEST) + b_ref[...]


def _ada_modulation(c, w, b):
    B, D = c.shape
    n = w.shape[1] // D
    return pl.pallas_call(
        _ada_kernel,
        out_shape=jax.ShapeDtypeStruct((B, n * D), F32),
        grid=(n,),
        in_specs=[pl.BlockSpec((B, D), lambda j: (0, 0)),
                  pl.BlockSpec((D, D), lambda j: (0, j)),
                  pl.BlockSpec((1, D), lambda j: (0, j))],
        out_specs=pl.BlockSpec((B, D), lambda j: (0, j)),
        compiler_params=_cparams("arbitrary"),
        name="ada_modulation",
    )(c, w, b.reshape(1, n * D))


def _inproj_kernel(x_ref, sc_ref, sh_ref, w_ref, b_ref,
                   q_ref, k_ref, v_ref, u_ref, ga_ref, gb_ref):
    D = D_MODEL
    h = (x_ref[...] * (1.0 + sc_ref[0]) + sh_ref[0]).astype(BF16)

    def proj(j):
        return (jnp.dot(h, w_ref[:, j * D:(j + 1) * D], preferred_element_type=F32)
                + b_ref[:, j * D:(j + 1) * D])

    q_ref[...] = (proj(0) * Q_SCALE).astype(BF16)
    k_ref[...] = proj(1).astype(BF16)
    v_ref[...] = proj(2).astype(BF16)
    u_ref[...] = (proj(3) * _sigmoid(proj(4))).astype(BF16)
    ga_ref[...] = _sigmoid(proj(5)).astype(BF16)
    gb_ref[...] = _sigmoid(proj(6)).astype(BF16)


def _in_projection(x2, sc1, sh1, w_in, b_in, seq):
    T, D = x2.shape
    tm = PROJ_TM
    tiles_per_seq = seq // tm
    ncols = w_in.shape[1]
    tok = pl.BlockSpec((tm, D), lambda i: (i, 0))
    mod = pl.BlockSpec((1, 1, D), lambda i: (i // tiles_per_seq, 0, 0))
    out = jax.ShapeDtypeStruct((T, D), BF16)
    return pl.pallas_call(
        _inproj_kernel,
        out_shape=[out] * 6,
        grid=(T // tm,),
        in_specs=[tok, mod, mod,
                  pl.BlockSpec((D, ncols), lambda i: (0, 0)),
                  pl.BlockSpec((1, ncols), lambda i: (0, 0))],
        out_specs=[tok] * 6,
        compiler_params=_cparams("arbitrary"),
        name="in_projection",
    )(x2, sc1, sh1, w_in, b_in)


def _attn_kernel(lq1_ref, lk1_ref, lq2_ref, lk2_ref, g_ref, q_ref, k_ref, v_ref, o_ref,
                 m_ref, l_ref, acc_ref):
    tq = ATTN_TQ
    qi = pl.program_id(2)
    lam = (jnp.exp(jnp.sum(lq1_ref[...] * lk1_ref[...], axis=-1, keepdims=True))
           - jnp.exp(jnp.sum(lq2_ref[...] * lk2_ref[...], axis=-1, keepdims=True))
           + LAMBDA_INIT)

    q = q_ref[...]
    lane = lax.broadcasted_iota(jnp.int32, q.shape, 1)
    zero = jnp.zeros_like(q)
    q2 = jnp.concatenate([jnp.where(lane < HEAD_DIM, q, zero),
                          jnp.where(lane >= HEAD_DIM, q, zero)], axis=0)

    def scores(j):
        start = pl.multiple_of(j * tq, tq)
        return lax.dot_general(k_ref[pl.ds(start, tq), :], q2, (((1,), (1,)), ((), ())),
                               preferred_element_type=F32)

    def update(j, s, masked):
        if masked:
            key = lax.broadcasted_iota(jnp.int32, s.shape, 0)
            qry = lax.broadcasted_iota(jnp.int32, s.shape, 1)
            qry = jnp.where(qry >= tq, qry - tq, qry)
            s = jnp.where(key <= qry, s, NEG_INF)
        m_prev = m_ref[...]
        m_new = jnp.maximum(m_prev, jnp.max(s, axis=0, keepdims=True))
        alpha = jnp.exp2(m_prev - m_new)
        p = jnp.exp2(s - m_new)
        l_ref[...] = alpha * l_ref[...] + jnp.sum(p, axis=0, keepdims=True)
        start = pl.multiple_of(j * tq, tq)
        pv = lax.dot_general(v_ref[pl.ds(start, tq), :], p.astype(BF16),
                             (((0,), (0,)), ((), ())), preferred_element_type=F32)
        acc_ref[...] = alpha * acc_ref[...] + pv
        m_ref[...] = m_new

    m_ref[...] = jnp.full(m_ref.shape, NEG_INF, F32)
    l_ref[...] = jnp.zeros(l_ref.shape, F32)
    acc_ref[...] = jnp.zeros(acc_ref.shape, F32)

    def body(j, s):
        s_next = scores(j + 1)
        update(j, s, False)
        return s_next

    s = lax.fori_loop(0, qi, body, scores(0))
    update(qi, s, True)

    o = acc_ref[...] / l_ref[...]
    o = o[:, :tq] - lam * o[:, tq:]
    o = o * lax.rsqrt(jnp.mean(o * o, axis=0, keepdims=True) + RMS_EPS) * g_ref[...]
    o_ref[...] = (o * (1.0 - LAMBDA_INIT)).T.astype(BF16)


def _diff_attention(q, k, v, lq1, lk1, lq2, lk2, subln_g, batch, seq):
    T, D = q.shape
    tq = ATTN_TQ
    nq = seq // tq
    lam_spec = pl.BlockSpec((1, HEAD_DIM), lambda b, h, i: (0, 0))
    q_spec = pl.BlockSpec((tq, V_DIM), lambda b, h, i: (b * nq + i, h))
    kv_spec = pl.BlockSpec((seq, V_DIM), lambda b, h, i: (b, h))
    return pl.pallas_call(
        _attn_kernel,
        out_shape=jax.ShapeDtypeStruct((T, D), BF16),
        grid=(batch, N_HEADS, nq),
        in_specs=[lam_spec, lam_spec, lam_spec, lam_spec,
                  pl.BlockSpec((V_DIM, 1), lambda b, h, i: (0, 0)),
                  q_spec, kv_spec, kv_spec],
        out_specs=q_spec,
        scratch_shapes=[pltpu.VMEM((1, 2 * tq), F32), pltpu.VMEM((1, 2 * tq), F32),
                        pltpu.VMEM((V_DIM, 2 * tq), F32)],
        compiler_params=_cparams("arbitrary", "arbitrary", "arbitrary"),
        name="diff_attention",
    )(lq1, lk1, lq2, lk2, subln_g.reshape(V_DIM, 1), q, k, v)


def _mixer_kernel(x_ref, o_ref, u_ref, up_ref, ga_ref, gb_ref, g1_ref, sc2_ref, sh2_ref,
                  wao_ref, cw_ref, cb_ref, clg_ref, clb_ref, wpw_ref, bpw_ref, wout_ref,
                  ln1g_ref, ln1b_ref, wrh_ref, wrl_ref,
                  x1_ref, h2_ref, lg_ref, ubuf_ref, *, tiles_per_seq):
    tm = MIX_TM
    first = (pl.program_id(0) % tiles_per_seq) == 0

    prev = up_ref[...].astype(F32)
    ubuf_ref[0:CONV_HALO, :] = jnp.where(first, jnp.zeros_like(prev), prev)
    ubuf_ref[CONV_HALO:, :] = u_ref[...].astype(F32)
    base = CONV_HALO - (CONV_K - 1)
    acc = jnp.zeros((tm, D_MODEL), F32)
    for j in range(CONV_K):
        acc = acc + cw_ref[j:j + 1, :] * ubuf_ref[base + j:base + j + tm, :]
    yc = _silu(_layer_norm(acc + cb_ref[...], clg_ref[...], clb_ref[...]))
    y_conv = jnp.dot(yc.astype(BF16), wpw_ref[...], preferred_element_type=F32) + bpw_ref[...]

    y_attn = jnp.dot(o_ref[...], wao_ref[...], preferred_element_type=F32)
    mix = ga_ref[...].astype(F32) * y_attn + gb_ref[...].astype(F32) * y_conv
    y = jnp.dot(mix.astype(BF16), wout_ref[...], preferred_element_type=F32)
    x1 = _layer_norm(DN_ALPHA * x_ref[...] + g1_ref[0] * y, ln1g_ref[...], ln1b_ref[...])
    x1_ref[...] = x1

    h2 = x1 * (1.0 + sc2_ref[0]) + sh2_ref[0]
    h2_hi = h2.astype(BF16)
    h2_ref[...] = _pack_halves(h2)
    h2_lo = (h2 - h2_hi.astype(F32)).astype(BF16)
    lg_ref[...] = (jnp.dot(h2_hi, wrh_ref[...], preferred_element_type=F32)
                   + (jnp.dot(h2_hi, wrl_ref[...], preferred_element_type=F32)
                      + jnp.dot(h2_lo, wrh_ref[...], preferred_element_type=F32)))


def _mixer_epilogue(x2, o, u, ga, gb, g1, sc2, sh2, wao, cw, cb, clg, clb, wpw, bpw, wout,
                    ln1g, ln1b, wr_hi, wr_lo, seq):
    T, D = x2.shape
    tm = MIX_TM
    tiles_per_seq = seq // tm
    halo_per_tile = tm // CONV_HALO
    tok = pl.BlockSpec((tm, D), lambda i: (i, 0))
    prev = pl.BlockSpec((CONV_HALO, D), lambda i: (jnp.maximum(i * halo_per_tile - 1, 0), 0))
    mod = pl.BlockSpec((1, 1, D), lambda i: (i // tiles_per_seq, 0, 0))
    row = pl.BlockSpec((1, D), lambda i: (0, 0))
    mat = pl.BlockSpec((D, D), lambda i: (0, 0))
    rw = pl.BlockSpec((D, N_EXPERTS), lambda i: (0, 0))
    return pl.pallas_call(
        partial(_mixer_kernel, tiles_per_seq=tiles_per_seq),
        out_shape=[jax.ShapeDtypeStruct((T, D), F32),
                   jax.ShapeDtypeStruct((T, D // 2), U32),
                   jax.ShapeDtypeStruct((T, N_EXPERTS), F32)],
        grid=(T // tm,),
        in_specs=[tok, tok, tok, prev, tok, tok, mod, mod, mod,
                  mat, pl.BlockSpec((CONV_K, D), lambda i: (0, 0)), row, row, row, mat, row, mat,
                  row, row, rw, rw],
        out_specs=[tok, pl.BlockSpec((tm, D // 2), lambda i: (i, 0)),
                   pl.BlockSpec((tm, N_EXPERTS), lambda i: (i, 0))],
        scratch_shapes=[pltpu.VMEM((CONV_HALO + tm, D), F32)],
        compiler_params=_cparams("arbitrary"),
        name="mixer_epilogue",
    )(x2, o, u, u, ga, gb, g1, sc2, sh2, wao, cw, cb, clg, clb, wpw, bpw, wout,
      ln1g, ln1b, wr_hi, wr_lo)


def _first_argmax(v, lane):
    m = jnp.max(v, axis=-1, keepdims=True)
    i = jnp.min(jnp.where(v == m, lane, N_EXPERTS), axis=-1, keepdims=True)
    return m, i


def _route_kernel(lg_ref, rb_ref, idx_ref, gate_ref, cnt_ref):
    tm = ROUTE_TM
    scores = _sigmoid(lg_ref[...])
    choice = scores + rb_ref[...]
    lane = lax.broadcasted_iota(jnp.int32, (tm, N_EXPERTS), 1)
    grp = lane // GROUP_SIZE

    gs = []
    for g in range(N_GROUPS):
        cg = jnp.where(grp == g, choice, NEG_INF)
        m1, i1 = _first_argmax(cg, lane)
        m2 = jnp.max(jnp.where(lane == i1, NEG_INF, cg), axis=-1, keepdims=True)
        gs.append(m1 + m2)

    allowed = jnp.zeros((tm, N_EXPERTS), jnp.bool_)
    for g in range(N_GROUPS):
        ahead = jnp.zeros((tm, 1), jnp.int32)
        for o in range(N_GROUPS):
            if o == g:
                continue
            beats = (gs[o] >= gs[g]) if o < g else (gs[o] > gs[g])
            ahead = ahead + beats.astype(jnp.int32)
        allowed = allowed | ((grp == g) & (ahead < TOPK_GROUPS))

    masked = jnp.where(allowed, choice, NEG_INF)
    out_lane = lax.broadcasted_iota(jnp.int32, (tm, LANES), 1)
    idx_out = jnp.zeros((tm, LANES), jnp.int32)
    gate_out = jnp.zeros((tm, LANES), F32)
    sel = jnp.zeros((tm, N_EXPERTS), F32)
    total = jnp.zeros((tm, 1), F32)
    for k in range(TOP_K):
        _, i = _first_argmax(masked, lane)
        hit = lane == i
        sc = jnp.sum(jnp.where(hit, scores, 0.0), axis=-1, keepdims=True)
        masked = jnp.where(hit, NEG_INF, masked)
        sel = sel + hit.astype(F32)
        total = total + sc
        idx_out = jnp.where(out_lane == k, i, idx_out)
        gate_out = jnp.where(out_lane == k, sc, gate_out)
    idx_ref[...] = idx_out
    gate_ref[...] = gate_out / total * ROUTED_SCALE

    @pl.when(pl.program_id(0) == 0)
    def _():
        cnt_ref[...] = jnp.zeros_like(cnt_ref)
    cnt_ref[...] += jnp.sum(sel, axis=0, keepdims=True)


def _route(logits, router_bias):
    T = logits.shape[0]
    tm = ROUTE_TM
    return pl.pallas_call(
        _route_kernel,
        out_shape=[jax.ShapeDtypeStruct((T, LANES), jnp.int32),
                   jax.ShapeDtypeStruct((T, LANES), F32),
                   jax.ShapeDtypeStruct((1, N_EXPERTS), F32)],
        grid=(T // tm,),
        in_specs=[pl.BlockSpec((tm, N_EXPERTS), lambda i: (i, 0)),
                  pl.BlockSpec((1, N_EXPERTS), lambda i: (0, 0))],
        out_specs=[pl.BlockSpec((tm, LANES), lambda i: (i, 0)),
                   pl.BlockSpec((tm, LANES), lambda i: (i, 0)),
                   pl.BlockSpec((1, N_EXPERTS), lambda i: (0, 0))],
        compiler_params=_cparams("arbitrary"),
        name="route_topk",
    )(logits, router_bias.reshape(1, N_EXPERTS))


def _slot_kernel(idx_ref, start_ref, pos_ref, carry_ref):
    tm = ROUTE_TM

    @pl.when(pl.program_id(0) == 0)
    def _():
        carry_ref[...] = jnp.zeros_like(carry_ref)

    idx = idx_ref[...]
    lane = lax.broadcasted_iota(jnp.int32, (tm, N_EXPERTS), 1)
    hits = [lane == idx[:, k:k + 1] for k in range(TOP_K)]
    sel = jnp.zeros((tm, N_EXPERTS), F32)
    for h in hits:
        sel = sel + h.astype(F32)
    r = lax.broadcasted_iota(jnp.int32, (tm, tm), 0)
    c = lax.broadcasted_iota(jnp.int32, (tm, tm), 1)
    earlier = (c < r).astype(BF16)
    rank = jnp.dot(earlier, sel.astype(BF16), preferred_element_type=F32)
    slot = rank + carry_ref[...] + start_ref[...]
    out_lane = lax.broadcasted_iota(jnp.int32, (tm, LANES), 1)
    pos = jnp.zeros((tm, LANES), jnp.int32)
    for k, h in enumerate(hits):
        pk = jnp.sum(jnp.where(h, slot, 0.0), axis=-1, keepdims=True).astype(jnp.int32)
        pos = jnp.where(out_lane == k, pk, pos)
    pos_ref[...] = pos
    carry_ref[...] += jnp.sum(sel, axis=0, keepdims=True)


def _assign_slots(idx, pad_start):
    T = idx.shape[0]
    tm = ROUTE_TM
    return pl.pallas_call(
        _slot_kernel,
        out_shape=jax.ShapeDtypeStruct((T, LANES), jnp.int32),
        grid=(T // tm,),
        in_specs=[pl.BlockSpec((tm, LANES), lambda i: (i, 0)),
                  pl.BlockSpec((1, N_EXPERTS), lambda i: (0, 0))],
        out_specs=pl.BlockSpec((tm, LANES), lambda i: (i, 0)),
        scratch_shapes=[pltpu.VMEM((1, N_EXPERTS), F32)],
        compiler_params=_cparams("arbitrary"),
        name="assign_slots",
    )(idx, pad_start)


def _dot_halves(lo, hi, w_ref):
    half = lo.shape[-1]
    return (jnp.dot(lo.astype(BF16), w_ref[:half, :].astype(BF16), preferred_element_type=F32)
            + jnp.dot(hi.astype(BF16), w_ref[half:, :].astype(BF16), preferred_element_type=F32))


def _expert_kernel(be_ref, nu_ref, xs_ref, w1_ref, w3_ref, w2_ref, ys_ref):
    @pl.when(pl.program_id(0) < nu_ref[0])
    def _():
        lo, hi = _unpack_halves(xs_ref[...])
        a = _dot_halves(lo, hi, w1_ref.at[0])
        b = _dot_halves(lo, hi, w3_ref.at[0])
        hb = (_silu(a) * b).astype(BF16)
        ys_ref[...] = _pack_halves(
            jnp.dot(hb, w2_ref[0].astype(BF16), preferred_element_type=F32))

    @pl.when(pl.program_id(0) >= nu_ref[0])
    def _():
        ys_ref[...] = jnp.zeros_like(ys_ref)


def _routed_experts(xs, blk_expert, n_used, w1, w3, w2):
    P, Dh = xs.shape
    blk = EXPERT_BLK
    D, F = w1.shape[-2:]
    grid_spec = pltpu.PrefetchScalarGridSpec(
        num_scalar_prefetch=2,
        grid=(P // blk,),
        in_specs=[pl.BlockSpec((blk, Dh), lambda b, be, nu: (jnp.minimum(b, nu[0] - 1), 0)),
                  pl.BlockSpec((1, D, F), lambda b, be, nu: (be[b], 0, 0)),
                  pl.BlockSpec((1, D, F), lambda b, be, nu: (be[b], 0, 0)),
                  pl.BlockSpec((1, F, D), lambda b, be, nu: (be[b], 0, 0))],
        out_specs=pl.BlockSpec((blk, Dh), lambda b, be, nu: (b, 0)),
    )
    return pl.pallas_call(
        _expert_kernel,
        out_shape=jax.ShapeDtypeStruct((P, Dh), U32),
        grid_spec=grid_spec,
        compiler_params=_cparams("arbitrary"),
        name="routed_experts",
    )(blk_expert, n_used, xs, w1, w3, w2)


def _sc_mesh():
    return plsc.VectorSubcoreMesh(core_axis_name="c", subcore_axis_name="s")


def _sc_dispatch(rows, slots_t, n_slots):
    T, W = rows.shape
    K = slots_t.shape[0]
    win = SC_WINDOW

    @pl.kernel(out_type=jax.ShapeDtypeStruct((n_slots, W), rows.dtype), mesh=_sc_mesh(),
               scratch_types=[])
    def dispatch(x_hbm, i_hbm, o_hbm):
        def body(x_vmem, i_vmem):
            for k in range(K):
                pltpu.sync_copy(x_vmem, o_hbm.at[i_vmem.at[k]])

        pltpu.emit_pipeline(
            body,
            grid=(T // win,),
            in_specs=[pl.BlockSpec((win, W), lambda i: (i, 0)),
                      pl.BlockSpec((K, win), lambda i: (0, i))],
            out_specs=[],
            core_axis_name=("c", "s"),
            dimension_semantics=(pltpu.PARALLEL,),
        )(x_hbm, i_hbm)

    return dispatch(rows, slots_t)


def _sc_gather(table, idx):
    W = table.shape[1]
    N = idx.shape[1]
    win = SC_WINDOW

    @pl.kernel(out_type=jax.ShapeDtypeStruct((N, W), table.dtype), mesh=_sc_mesh(),
               scratch_types=[])
    def gather(x_hbm, i_hbm, o_hbm):
        def body(i_vmem, o_vmem):
            pltpu.sync_copy(x_hbm.at[i_vmem.at[0]], o_vmem)

        pltpu.emit_pipeline(
            body,
            grid=(N // win,),
            in_specs=[pl.BlockSpec((1, win), lambda i: (0, i))],
            out_specs=[pl.BlockSpec((win, W), lambda i: (i, 0))],
            core_axis_name=("c", "s"),
            dimension_semantics=(pltpu.PARALLEL,),
        )(i_hbm, o_hbm)

    return gather(table, idx)


def _final_kernel(x1_ref, h2_ref, yg_ref, gate_ref, g2_ref, ws1_ref, ws3_ref, ws2_ref,
                  lng_ref, lnb_ref, o_ref):
    Dh = D_MODEL // 2
    gate = gate_ref[...]
    routed_lo = jnp.zeros((x1_ref.shape[0], Dh), F32)
    routed_hi = jnp.zeros((x1_ref.shape[0], Dh), F32)
    for k in range(TOP_K):
        lo, hi = _unpack_halves(yg_ref[:, k * Dh:(k + 1) * Dh])
        routed_lo = routed_lo + lo * gate[:, k:k + 1]
        routed_hi = routed_hi + hi * gate[:, k:k + 1]
    lo, hi = _unpack_halves(h2_ref[...])
    a = _dot_halves(lo, hi, ws1_ref)
    b = _dot_halves(lo, hi, ws3_ref)
    shared = jnp.dot((_silu(a) * b).astype(BF16), ws2_ref[...], preferred_element_type=F32)
    y2 = jnp.concatenate([routed_lo, routed_hi], axis=-1) + shared
    o_ref[...] = _layer_norm(DN_ALPHA * x1_ref[...] + g2_ref[0] * y2, lng_ref[...], lnb_ref[...])


def _final(x1, h2, yg, gate, g2, ws1, ws3, ws2, ln2g, ln2b, seq):
    T, D = x1.shape
    tm = FINAL_TM
    tiles_per_seq = seq // tm
    F = ws1.shape[-1]
    tok = pl.BlockSpec((tm, D), lambda i: (i, 0))
    row = pl.BlockSpec((1, D), lambda i: (0, 0))
    return pl.pallas_call(
        _final_kernel,
        out_shape=jax.ShapeDtypeStruct((T, D), F32),
        grid=(T // tm,),
        in_specs=[tok, pl.BlockSpec((tm, D // 2), lambda i: (i, 0)),
                  pl.BlockSpec((tm, TOP_K * (D // 2)), lambda i: (i, 0)),
                  pl.BlockSpec((tm, LANES), lambda i: (i, 0)),
                  pl.BlockSpec((1, 1, D), lambda i: (i // tiles_per_seq, 0, 0)),
                  pl.BlockSpec((D, F), lambda i: (0, 0)),
                  pl.BlockSpec((D, F), lambda i: (0, 0)),
                  pl.BlockSpec((F, D), lambda i: (0, 0)),
                  row, row],
        out_specs=tok,
        compiler_params=_cparams("arbitrary"),
        name="moe_combine_final",
    )(x1, h2, yg, gate, g2, ws1, ws3, ws2, ln2g, ln2b)


def kernel(x, c, ada_w, ada_b, w_in, b_in, lambda_q1, lambda_k1, lambda_q2, lambda_k2, subln_g, w_attn_o, conv_w, conv_b, conv_ln_g, conv_ln_b, conv_pw_w, conv_pw_b, w_out, ln1_g, ln1_b, router_w, router_bias, w1, w3, w2, ws1, ws3, ws2, ln2_g, ln2_b):
    B, S, D = x.shape
    T = B * S
    l = 0
    x2 = x.reshape(T, D)
    row = lambda a: a.reshape(1, -1)

    mod = _ada_modulation(c, ada_w[l], ada_b[l])
    sh1, sc1, g1, sh2, sc2, g2 = [m.reshape(B, 1, D) for m in jnp.split(mod, 6, axis=-1)]

    q, k, v, u, ga, gb = _in_projection(x2, sc1, sh1, w_in[l].astype(BF16), row(b_in[l]), S)
    o = _diff_attention(q, k, v, row(lambda_q1[l]), row(lambda_k1[l]), row(lambda_q2[l]),
                        row(lambda_k2[l]), row(subln_g[l]), B, S)

    wr = router_w[l]
    wr_hi = wr.astype(BF16)
    wr_lo = (wr - wr_hi.astype(F32)).astype(BF16)
    x1, h2, logits = _mixer_epilogue(
        x2, o, u, ga, gb, g1, sc2, sh2, w_attn_o[l].astype(BF16), conv_w[l], row(conv_b[l]),
        row(conv_ln_g[l]), row(conv_ln_b[l]), conv_pw_w[l].astype(BF16), row(conv_pw_b[l]),
        w_out[l].astype(BF16), row(ln1_g[l]), row(ln1_b[l]), wr_hi, wr_lo, S)

    idx, gate, counts = _route(logits, router_bias[l])

    blk = EXPERT_BLK
    cnt = counts.reshape(N_EXPERTS).astype(jnp.int32)
    padded = (cnt + blk - 1) // blk * blk
    pad_end = jnp.cumsum(padded)
    pad_start = pad_end - padded
    P = T * TOP_K + N_EXPERTS * blk
    n_blocks = P // blk
    blk_expert = jnp.minimum(
        jnp.searchsorted(pad_end, jnp.arange(n_blocks, dtype=jnp.int32) * blk, side='right'),
        N_EXPERTS - 1).astype(jnp.int32)
    n_used = (pad_end[-1:] // blk).astype(jnp.int32)

    pos = _assign_slots(idx, pad_start.astype(F32).reshape(1, N_EXPERTS))[:, :TOP_K]

    Dq = D // 4
    half = jnp.arange(2, dtype=jnp.int32)
    slots_t = (2 * pos.T[:, :, None] + half).reshape(TOP_K, 2 * T)
    slots_flat = (2 * pos.reshape(T * TOP_K, 1) + half).reshape(1, 2 * T * TOP_K)
    xs = _sc_dispatch(h2.reshape(2 * T, Dq), slots_t, 2 * P).reshape(P, 2 * Dq)
    ys = _routed_experts(xs, blk_expert, n_used, w1[l], w3[l], w2[l])
    yg = _sc_gather(ys.reshape(2 * P, Dq), slots_flat).reshape(T, TOP_K * 2 * Dq)

    out = _final(x1, h2, yg, gate, g2, ws1[l].astype(BF16), ws3[l].astype(BF16),
                 ws2[l].astype(BF16), row(ln2_g[l]), row(ln2_b[l]), S)
    return out.reshape(B, S, D)
```

```python
import math
from functools import partial

import jax
import jax.numpy as jnp
from jax import lax
from jax.experimental import pallas as pl
from jax.experimental.pallas import tpu as pltpu
from jax.experimental.pallas import tpu_sc as plsc

D_MODEL = 1024
N_HEADS = 8
HEAD_DIM = 64
V_DIM = 128
CONV_K = 31
N_EXPERTS = 256
TOP_K = 8
N_GROUPS = 8
GROUP_SIZE = N_EXPERTS // N_GROUPS
TOPK_GROUPS = 4
D_EXPERT = 256
ROUTED_SCALE = 2.5
LN_EPS = 1e-5
RMS_EPS = 1e-5
DEPTH = 1
DN_ALPHA = (2.0 * DEPTH) ** 0.25
LAMBDA_INIT = 0.8 - 0.6 * math.exp(-0.3 * 0)
Q_SCALE = HEAD_DIM ** -0.5 * math.log2(math.e)

LANES = 128
VMEM_LIMIT_BYTES = 56 * 1024 * 1024

PROJ_TM = 512
ATTN_TQ = 512
MIX_TM = 256
CONV_HALO = 32
ROUTE_TM = 512
EXPERT_BLK = 256
FINAL_TM = 256
SC_WINDOW = 128

F32 = jnp.float32
BF16 = jnp.bfloat16
U32 = jnp.uint32
NEG_INF = float("-inf")


def _cparams(*sem):
    return pltpu.CompilerParams(dimension_semantics=sem, vmem_limit_bytes=VMEM_LIMIT_BYTES)


def _sigmoid(x):
    return 1.0 / (1.0 + jnp.exp(-x))


def _silu(x):
    return x * _sigmoid(x)


def _pack_halves(x):
    half = x.shape[-1] // 2
    bits = lambda v: lax.bitcast_convert_type(v.astype(BF16).astype(F32), U32)
    return (bits(x[:, :half]) >> 16) | (bits(x[:, half:]) & jnp.uint32(0xFFFF0000))


def _unpack_halves(w):
    lo = lax.bitcast_convert_type(w << 16, F32)
    hi = lax.bitcast_convert_type(w & jnp.uint32(0xFFFF0000), F32)
    return lo, hi


def _store_packed(ref, x):
    w = _pack_halves(x)
    q = w.shape[-1] // 2
    ref[0] = w[:, :q]
    ref[1] = w[:, q:]


def _load_packed(ref):
    return _unpack_halves(jnp.concatenate([ref[0], ref[1]], axis=-1))


def _layer_norm(r, g, b):
    mu = jnp.mean(r, axis=-1, keepdims=True)
    d = r - mu
    var = jnp.mean(d * d, axis=-1, keepdims=True)
    return d * lax.rsqrt(var + LN_EPS) * g + b


def _ada_kernel(c_ref, w_ref, b_ref, o_ref):
    a = _silu(c_ref[...])
    o_ref[...] = jnp.dot(a, w_ref[...], preferred_element_type=F32,
                         precision=lax.Precision.HIGHEST) + b_ref[...]


def _ada_modulation(c, w, b):
    B, D = c.shape
    n = w.shape[1] // D
    return pl.pallas_call(
        _ada_kernel,
        out_shape=jax.ShapeDtypeStruct((B, n * D), F32),
        grid=(n,),
        in_specs=[pl.BlockSpec((B, D), lambda j: (0, 0)),
                  pl.BlockSpec((D, D), lambda j: (0, j)),
                  pl.BlockSpec((1, D), lambda j: (0, j))],
        out_specs=pl.BlockSpec((B, D), lambda j: (0, j)),
        compiler_params=_cparams("arbitrary"),
        name="ada_modulation",
    )(c, w, b.reshape(1, n * D))


def _inproj_kernel(x_ref, sc_ref, sh_ref, w_ref, b_ref,
                   q_ref, k_ref, v_ref, u_ref, ga_ref, gb_ref):
    D = D_MODEL
    h = (x_ref[...] * (1.0 + sc_ref[0]) + sh_ref[0]).astype(BF16)

    def proj(j):
        return (jnp.dot(h, w_ref[:, j * D:(j + 1) * D], preferred_element_type=F32)
                + b_ref[:, j * D:(j + 1) * D])

    q_ref[...] = (proj(0) * Q_SCALE).astype(BF16)
    k_ref[...] = proj(1).astype(BF16)
    v_ref[...] = proj(2).astype(BF16)
    u_ref[...] = (proj(3) * _sigmoid(proj(4))).astype(BF16)
    ga_ref[...] = _sigmoid(proj(5)).astype(BF16)
    gb_ref[...] = _sigmoid(proj(6)).astype(BF16)


def _in_projection(x2, sc1, sh1, w_in, b_in, seq):
    T, D = x2.shape
    tm = PROJ_TM
    tiles_per_seq = seq // tm
    ncols = w_in.shape[1]
    tok = pl.BlockSpec((tm, D), lambda i: (i, 0))
    mod = pl.BlockSpec((1, 1, D), lambda i: (i // tiles_per_seq, 0, 0))
    out = jax.ShapeDtypeStruct((T, D), BF16)
    return pl.pallas_call(
        _inproj_kernel,
        out_shape=[out] * 6,
        grid=(T // tm,),
        in_specs=[tok, mod, mod,
                  pl.BlockSpec((D, ncols), lambda i: (0, 0)),
                  pl.BlockSpec((1, ncols), lambda i: (0, 0))],
        out_specs=[tok] * 6,
        compiler_params=_cparams("arbitrary"),
        name="in_projection",
    )(x2, sc1, sh1, w_in, b_in)


def _attn_kernel(lq1_ref, lk1_ref, lq2_ref, lk2_ref, g_ref, q_ref, k_ref, v_ref, o_ref,
                 m_ref, l_ref, acc_ref):
    tq = ATTN_TQ
    qi = pl.program_id(2)
    lam = (jnp.exp(jnp.sum(lq1_ref[...] * lk1_ref[...], axis=-1, keepdims=True))
           - jnp.exp(jnp.sum(lq2_ref[...] * lk2_ref[...], axis=-1, keepdims=True))
           + LAMBDA_INIT)

    q = q_ref[...]
    lane = lax.broadcasted_iota(jnp.int32, q.shape, 1)
    zero = jnp.zeros_like(q)
    q2 = jnp.concatenate([jnp.where(lane < HEAD_DIM, q, zero),
                          jnp.where(lane >= HEAD_DIM, q, zero)], axis=0)

    def scores(j):
        start = pl.multiple_of(j * tq, tq)
        return lax.dot_general(k_ref[pl.ds(start, tq), :], q2, (((1,), (1,)), ((), ())),
                               preferred_element_type=F32)

    def update(j, s, masked):
        if masked:
            key = lax.broadcasted_iota(jnp.int32, s.shape, 0)
            qry = lax.broadcasted_iota(jnp.int32, s.shape, 1)
            qry = jnp.where(qry >= tq, qry - tq, qry)
            s = jnp.where(key <= qry, s, NEG_INF)
        m_prev = m_ref[...]
        m_new = jnp.maximum(m_prev, jnp.max(s, axis=0, keepdims=True))
        alpha = jnp.exp2(m_prev - m_new)
        p = jnp.exp2(s - m_new)
        l_ref[...] = alpha * l_ref[...] + jnp.sum(p, axis=0, keepdims=True)
        start = pl.multiple_of(j * tq, tq)
        pv = lax.dot_general(v_ref[pl.ds(start, tq), :], p.astype(BF16),
                             (((0,), (0,)), ((), ())), preferred_element_type=F32)
        acc_ref[...] = alpha * acc_ref[...] + pv
        m_ref[...] = m_new

    m_ref[...] = jnp.full(m_ref.shape, NEG_INF, F32)
    l_ref[...] = jnp.zeros(l_ref.shape, F32)
    acc_ref[...] = jnp.zeros(acc_ref.shape, F32)

    def body(j, s):
        s_next = scores(j + 1)
        update(j, s, False)
        return s_next

    s = lax.fori_loop(0, qi, body, scores(0))
    update(qi, s, True)

    o = acc_ref[...] / l_ref[...]
    o = o[:, :tq] - lam * o[:, tq:]
    o = o * lax.rsqrt(jnp.mean(o * o, axis=0, keepdims=True) + RMS_EPS) * g_ref[...]
    o_ref[...] = (o * (1.0 - LAMBDA_INIT)).T.astype(BF16)


def _diff_attention(q, k, v, lq1, lk1, lq2, lk2, subln_g, batch, seq):
    T, D = q.shape
    tq = ATTN_TQ
    nq = seq // tq
    lam_spec = pl.BlockSpec((1, HEAD_DIM), lambda b, h, i: (0, 0))
    q_spec = pl.BlockSpec((tq, V_DIM), lambda b, h, i: (b * nq + i, h))
    kv_spec = pl.BlockSpec((seq, V_DIM), lambda b, h, i: (b, h))
    return pl.pallas_call(
        _attn_kernel,
        out_shape=jax.ShapeDtypeStruct((T, D), BF16),
        grid=(batch, N_HEADS, nq),
        in_specs=[lam_spec, lam_spec, lam_spec, lam_spec,
                  pl.BlockSpec((V_DIM, 1), lambda b, h, i: (0, 0)),
                  q_spec, kv_spec, kv_spec],
        out_specs=q_spec,
        scratch_shapes=[pltpu.VMEM((1, 2 * tq), F32), pltpu.VMEM((1, 2 * tq), F32),
                        pltpu.VMEM((V_DIM, 2 * tq), F32)],
        compiler_params=_cparams("arbitrary", "arbitrary", "arbitrary"),
        name="diff_attention",
    )(lq1, lk1, lq2, lk2, subln_g.reshape(V_DIM, 1), q, k, v)


def _mixer_kernel(x_ref, o_ref, u_ref, up_ref, ga_ref, gb_ref, g1_ref, sc2_ref, sh2_ref,
                  wao_ref, cw_ref, cb_ref, clg_ref, clb_ref, wpw_ref, bpw_ref, wout_ref,
                  ln1g_ref, ln1b_ref, wrh_ref, wrl_ref,
                  x1_ref, h2_ref, lg_ref, ubuf_ref, *, tiles_per_seq):
    tm = MIX_TM
    first = (pl.program_id(0) % tiles_per_seq) == 0

    prev = up_ref[...].astype(F32)
    ubuf_ref[0:CONV_HALO, :] = jnp.where(first, jnp.zeros_like(prev), prev)
    ubuf_ref[CONV_HALO:, :] = u_ref[...].astype(F32)
    base = CONV_HALO - (CONV_K - 1)
    acc = jnp.zeros((tm, D_MODEL), F32)
    for j in range(CONV_K):
        acc = acc + cw_ref[j:j + 1, :] * ubuf_ref[base + j:base + j + tm, :]
    yc = _silu(_layer_norm(acc + cb_ref[...], clg_ref[...], clb_ref[...]))
    y_conv = jnp.dot(yc.astype(BF16), wpw_ref[...], preferred_element_type=F32) + bpw_ref[...]

    y_attn = jnp.dot(o_ref[...], wao_ref[...], preferred_element_type=F32)
    mix = ga_ref[...].astype(F32) * y_attn + gb_ref[...].astype(F32) * y_conv
    y = jnp.dot(mix.astype(BF16), wout_ref[...], preferred_element_type=F32)
    x1 = _layer_norm(DN_ALPHA * x_ref[...] + g1_ref[0] * y, ln1g_ref[...], ln1b_ref[...])
    x1_ref[...] = x1

    h2 = x1 * (1.0 + sc2_ref[0]) + sh2_ref[0]
    h2_hi = h2.astype(BF16)
    _store_packed(h2_ref, h2)
    h2_lo = (h2 - h2_hi.astype(F32)).astype(BF16)
    lg_ref[...] = (jnp.dot(h2_hi, wrh_ref[...], preferred_element_type=F32)
                   + (jnp.dot(h2_hi, wrl_ref[...], preferred_element_type=F32)
                      + jnp.dot(h2_lo, wrh_ref[...], preferred_element_type=F32)))


def _mixer_epilogue(x2, o, u, ga, gb, g1, sc2, sh2, wao, cw, cb, clg, clb, wpw, bpw, wout,
                    ln1g, ln1b, wr_hi, wr_lo, seq):
    T, D = x2.shape
    tm = MIX_TM
    tiles_per_seq = seq // tm
    halo_per_tile = tm // CONV_HALO
    tok = pl.BlockSpec((tm, D), lambda i: (i, 0))
    prev = pl.BlockSpec((CONV_HALO, D), lambda i: (jnp.maximum(i * halo_per_tile - 1, 0), 0))
    mod = pl.BlockSpec((1, 1, D), lambda i: (i // tiles_per_seq, 0, 0))
    row = pl.BlockSpec((1, D), lambda i: (0, 0))
    mat = pl.BlockSpec((D, D), lambda i: (0, 0))
    rw = pl.BlockSpec((D, N_EXPERTS), lambda i: (0, 0))
    return pl.pallas_call(
        partial(_mixer_kernel, tiles_per_seq=tiles_per_seq),
        out_shape=[jax.ShapeDtypeStruct((T, D), F32),
                   jax.ShapeDtypeStruct((2, T, D // 4), U32),
                   jax.ShapeDtypeStruct((T, N_EXPERTS), F32)],
        grid=(T // tm,),
        in_specs=[tok, tok, tok, prev, tok, tok, mod, mod, mod,
                  mat, pl.BlockSpec((CONV_K, D), lambda i: (0, 0)), row, row, row, mat, row, mat,
                  row, row, rw, rw],
        out_specs=[tok, pl.BlockSpec((2, tm, D // 4), lambda i: (0, i, 0)),
                   pl.BlockSpec((tm, N_EXPERTS), lambda i: (i, 0))],
        scratch_shapes=[pltpu.VMEM((CONV_HALO + tm, D), F32)],
        compiler_params=_cparams("arbitrary"),
        name="mixer_epilogue",
    )(x2, o, u, u, ga, gb, g1, sc2, sh2, wao, cw, cb, clg, clb, wpw, bpw, wout,
      ln1g, ln1b, wr_hi, wr_lo)


def _first_argmax(v, lane):
    m = jnp.max(v, axis=-1, keepdims=True)
    i = jnp.min(jnp.where(v == m, lane, N_EXPERTS), axis=-1, keepdims=True)
    return m, i


def _route_kernel(lg_ref, rb_ref, idx_ref, gate_ref, cnt_ref):
    tm = ROUTE_TM
    scores = _sigmoid(lg_ref[...])
    choice = scores + rb_ref[...]
    lane = lax.broadcasted_iota(jnp.int32, (tm, N_EXPERTS), 1)
    grp = lane // GROUP_SIZE

    gs = []
    for g in range(N_GROUPS):
        cg = jnp.where(grp == g, choice, NEG_INF)
        m1, i1 = _first_argmax(cg, lane)
        m2 = jnp.max(jnp.where(lane == i1, NEG_INF, cg), axis=-1, keepdims=True)
        gs.append(m1 + m2)

    allowed = jnp.zeros((tm, N_EXPERTS), jnp.bool_)
    for g in range(N_GROUPS):
        ahead = jnp.zeros((tm, 1), jnp.int32)
        for o in range(N_GROUPS):
            if o == g:
                continue
            beats = (gs[o] >= gs[g]) if o < g else (gs[o] > gs[g])
            ahead = ahead + beats.astype(jnp.int32)
        allowed = allowed | ((grp == g) & (ahead < TOPK_GROUPS))

    masked = jnp.where(allowed, choice, NEG_INF)
    out_lane = lax.broadcasted_iota(jnp.int32, (tm, LANES), 1)
    idx_out = jnp.zeros((tm, LANES), jnp.int32)
    gate_out = jnp.zeros((tm, LANES), F32)
    sel = jnp.zeros((tm, N_EXPERTS), F32)
    total = jnp.zeros((tm, 1), F32)
    for k in range(TOP_K):
        _, i = _first_argmax(masked, lane)
        hit = lane == i
        sc = jnp.sum(jnp.where(hit, scores, 0.0), axis=-1, keepdims=True)
        masked = jnp.where(hit, NEG_INF, masked)
        sel = sel + hit.astype(F32)
        total = total + sc
        idx_out = jnp.where(out_lane == k, i, idx_out)
        gate_out = jnp.where(out_lane == k, sc, gate_out)
    idx_ref[...] = idx_out
    gate_ref[...] = gate_out / total * ROUTED_SCALE

    @pl.when(pl.program_id(0) == 0)
    def _():
        cnt_ref[...] = jnp.zeros_like(cnt_ref)
    cnt_ref[...] += jnp.sum(sel, axis=0, keepdims=True)


def _route(logits, router_bias):
    T = logits.shape[0]
    tm = ROUTE_TM
    return pl.pallas_call(
        _route_kernel,
        out_shape=[jax.ShapeDtypeStruct((T, LANES), jnp.int32),
                   jax.ShapeDtypeStruct((T, LANES), F32),
                   jax.ShapeDtypeStruct((1, N_EXPERTS), F32)],
        grid=(T // tm,),
        in_specs=[pl.BlockSpec((tm, N_EXPERTS), lambda i: (i, 0)),
                  pl.BlockSpec((1, N_EXPERTS), lambda i: (0, 0))],
        out_specs=[pl.BlockSpec((tm, LANES), lambda i: (i, 0)),
                   pl.BlockSpec((tm, LANES), lambda i: (i, 0)),
                   pl.BlockSpec((1, N_EXPERTS), lambda i: (0, 0))],
        compiler_params=_cparams("arbitrary"),
        name="route_topk",
    )(logits, router_bias.reshape(1, N_EXPERTS))


def _slot_kernel(idx_ref, start_ref, pos_ref, carry_ref):
    tm = ROUTE_TM

    @pl.when(pl.program_id(0) == 0)
    def _():
        carry_ref[...] = jnp.zeros_like(carry_ref)

    idx = idx_ref[...]
    lane = lax.broadcasted_iota(jnp.int32, (tm, N_EXPERTS), 1)
    hits = [lane == idx[:, k:k + 1] for k in range(TOP_K)]
    sel = jnp.zeros((tm, N_EXPERTS), F32)
    for h in hits:
        sel = sel + h.astype(F32)
    r = lax.broadcasted_iota(jnp.int32, (tm, tm), 0)
    c = lax.broadcasted_iota(jnp.int32, (tm, tm), 1)
    earlier = (c < r).astype(BF16)
    rank = jnp.dot(earlier, sel.astype(BF16), preferred_element_type=F32)
    slot = rank + carry_ref[...] + start_ref[...]
    out_lane = lax.broadcasted_iota(jnp.int32, (tm, LANES), 1)
    pos = jnp.zeros((tm, LANES), jnp.int32)
    for k, h in enumerate(hits):
        pk = jnp.sum(jnp.where(h, slot, 0.0), axis=-1, keepdims=True).astype(jnp.int32)
        pos = jnp.where(out_lane == k, pk, pos)
    pos_ref[...] = pos
    carry_ref[...] += jnp.sum(sel, axis=0, keepdims=True)


def _assign_slots(idx, pad_start):
    T = idx.shape[0]
    tm = ROUTE_TM
    return pl.pallas_call(
        _slot_kernel,
        out_shape=jax.ShapeDtypeStruct((T, LANES), jnp.int32),
        grid=(T // tm,),
        in_specs=[pl.BlockSpec((tm, LANES), lambda i: (i, 0)),
                  pl.BlockSpec((1, N_EXPERTS), lambda i: (0, 0))],
        out_specs=pl.BlockSpec((tm, LANES), lambda i: (i, 0)),
        scratch_shapes=[pltpu.VMEM((1, N_EXPERTS), F32)],
        compiler_params=_cparams("arbitrary"),
        name="assign_slots",
    )(idx, pad_start)


def _dot_halves(lo, hi, w_ref):
    half = lo.shape[-1]
    return (jnp.dot(lo.astype(BF16), w_ref[:half, :].astype(BF16), preferred_element_type=F32)
            + jnp.dot(hi.astype(BF16), w_ref[half:, :].astype(BF16), preferred_element_type=F32))


def _expert_kernel(be_ref, nu_ref, xs_ref, w1_ref, w3_ref, w2_ref, ys_ref):
    @pl.when(pl.program_id(0) < nu_ref[0])
    def _():
        lo, hi = _load_packed(xs_ref)
        a = _dot_halves(lo, hi, w1_ref.at[0])
        b = _dot_halves(lo, hi, w3_ref.at[0])
        hb = (_silu(a) * b).astype(BF16)
        _store_packed(ys_ref, jnp.dot(hb, w2_ref[0].astype(BF16), preferred_element_type=F32))

    @pl.when(pl.program_id(0) >= nu_ref[0])
    def _():
        ys_ref[...] = jnp.zeros_like(ys_ref)


def _routed_experts(xs, blk_expert, n_used, w1, w3, w2):
    _, P, Dq = xs.shape
    blk = EXPERT_BLK
    D, F = w1.shape[-2:]
    grid_spec = pltpu.PrefetchScalarGridSpec(
        num_scalar_prefetch=2,
        grid=(P // blk,),
        in_specs=[pl.BlockSpec((2, blk, Dq), lambda b, be, nu: (0, jnp.minimum(b, nu[0] - 1), 0)),
                  pl.BlockSpec((1, D, F), lambda b, be, nu: (be[b], 0, 0)),
                  pl.BlockSpec((1, D, F), lambda b, be, nu: (be[b], 0, 0)),
                  pl.BlockSpec((1, F, D), lambda b, be, nu: (be[b], 0, 0))],
        out_specs=pl.BlockSpec((2, blk, Dq), lambda b, be, nu: (0, b, 0)),
    )
    return pl.pallas_call(
        _expert_kernel,
        out_shape=jax.ShapeDtypeStruct((2, P, Dq), U32),
        grid_spec=grid_spec,
        compiler_params=_cparams("arbitrary"),
        name="routed_experts",
    )(blk_expert, n_used, xs, w1, w3, w2)


def _sc_mesh():
    return plsc.VectorSubcoreMesh(core_axis_name="c", subcore_axis_name="s")


def _sc_dispatch(rows, slots_t, n_slots):
    T, W = rows.shape
    K = slots_t.shape[0]
    win = SC_WINDOW

    @pl.kernel(out_type=jax.ShapeDtypeStruct((n_slots, W), rows.dtype), mesh=_sc_mesh(),
               scratch_types=[])
    def dispatch(x_hbm, i_hbm, o_hbm):
        def body(x_vmem, i_vmem):
            for k in range(K):
                pltpu.sync_copy(x_vmem, o_hbm.at[i_vmem.at[k]])

        pltpu.emit_pipeline(
            body,
            grid=(T // win,),
            in_specs=[pl.BlockSpec((win, W), lambda i: (i, 0)),
                      pl.BlockSpec((K, win), lambda i: (0, i))],
            out_specs=[],
            core_axis_name=("c", "s"),
            dimension_semantics=(pltpu.PARALLEL,),
        )(x_hbm, i_hbm)

    return dispatch(rows, slots_t)


def _sc_gather(table, idx):
    W = table.shape[1]
    N = idx.shape[1]
    win = SC_WINDOW

    @pl.kernel(out_type=jax.ShapeDtypeStruct((N, W), table.dtype), mesh=_sc_mesh(),
               scratch_types=[])
    def gather(x_hbm, i_hbm, o_hbm):
        def body(i_vmem, o_vmem):
            pltpu.sync_copy(x_hbm.at[i_vmem.at[0]], o_vmem)

        pltpu.emit_pipeline(
            body,
            grid=(N // win,),
            in_specs=[pl.BlockSpec((1, win), lambda i: (0, i))],
            out_specs=[pl.BlockSpec((win, W), lambda i: (i, 0))],
            core_axis_name=("c", "s"),
            dimension_semantics=(pltpu.PARALLEL,),
        )(i_hbm, o_hbm)

    return gather(table, idx)


def _final_kernel(x1_ref, h2_ref, yg_ref, gate_ref, g2_ref, ws1_ref, ws3_ref, ws2_ref,
                  lng_ref, lnb_ref, o_ref):
    Dh = D_MODEL // 2
    gate = gate_ref[...]
    routed_lo = jnp.zeros((x1_ref.shape[0], Dh), F32)
    routed_hi = jnp.zeros((x1_ref.shape[0], Dh), F32)
    for k in range(TOP_K):
        lo, hi = _load_packed(yg_ref.at[k])
        routed_lo = routed_lo + lo * gate[:, k:k + 1]
        routed_hi = routed_hi + hi * gate[:, k:k + 1]
    lo, hi = _load_packed(h2_ref)
    a = _dot_halves(lo, hi, ws1_ref)
    b = _dot_halves(lo, hi, ws3_ref)
    shared = jnp.dot((_silu(a) * b).astype(BF16), ws2_ref[...], preferred_element_type=F32)
    y2 = jnp.concatenate([routed_lo, routed_hi], axis=-1) + shared
    o_ref[...] = _layer_norm(DN_ALPHA * x1_ref[...] + g2_ref[0] * y2, lng_ref[...], lnb_ref[...])


def _final(x1, h2, yg, gate, g2, ws1, ws3, ws2, ln2g, ln2b, seq):
    T, D = x1.shape
    tm = FINAL_TM
    tiles_per_seq = seq // tm
    F = ws1.shape[-1]
    tok = pl.BlockSpec((tm, D), lambda i: (i, 0))
    row = pl.BlockSpec((1, D), lambda i: (0, 0))
    return pl.pallas_call(
        _final_kernel,
        out_shape=jax.ShapeDtypeStruct((T, D), F32),
        grid=(T // tm,),
        in_specs=[tok, pl.BlockSpec((2, tm, D // 4), lambda i: (0, i, 0)),
                  pl.BlockSpec((TOP_K, 2, tm, D // 4), lambda i: (0, 0, i, 0)),
                  pl.BlockSpec((tm, LANES), lambda i: (i, 0)),
                  pl.BlockSpec((1, 1, D), lambda i: (i // tiles_per_seq, 0, 0)),
                  pl.BlockSpec((D, F), lambda i: (0, 0)),
                  pl.BlockSpec((D, F), lambda i: (0, 0)),
                  pl.BlockSpec((F, D), lambda i: (0, 0)),
                  row, row],
        out_specs=tok,
        compiler_params=_cparams("arbitrary"),
        name="moe_combine_final",
    )(x1, h2, yg, gate, g2, ws1, ws3, ws2, ln2g, ln2b)


def kernel(x, c, ada_w, ada_b, w_in, b_in, lambda_q1, lambda_k1, lambda_q2, lambda_k2, subln_g, w_attn_o, conv_w, conv_b, conv_ln_g, conv_ln_b, conv_pw_w, conv_pw_b, w_out, ln1_g, ln1_b, router_w, router_bias, w1, w3, w2, ws1, ws3, ws2, ln2_g, ln2_b):
    B, S, D = x.shape
    T = B * S
    l = 0
    x2 = x.reshape(T, D)
    row = lambda a: a.reshape(1, -1)

    mod = _ada_modulation(c, ada_w[l], ada_b[l])
    sh1, sc1, g1, sh2, sc2, g2 = [m.reshape(B, 1, D) for m in jnp.split(mod, 6, axis=-1)]

    q, k, v, u, ga, gb = _in_projection(x2, sc1, sh1, w_in[l].astype(BF16), row(b_in[l]), S)
    o = _diff_attention(q, k, v, row(lambda_q1[l]), row(lambda_k1[l]), row(lambda_q2[l]),
                        row(lambda_k2[l]), row(subln_g[l]), B, S)

    wr = router_w[l]
    wr_hi = wr.astype(BF16)
    wr_lo = (wr - wr_hi.astype(F32)).astype(BF16)
    x1, h2, logits = _mixer_epilogue(
        x2, o, u, ga, gb, g1, sc2, sh2, w_attn_o[l].astype(BF16), conv_w[l], row(conv_b[l]),
        row(conv_ln_g[l]), row(conv_ln_b[l]), conv_pw_w[l].astype(BF16), row(conv_pw_b[l]),
        w_out[l].astype(BF16), row(ln1_g[l]), row(ln1_b[l]), wr_hi, wr_lo, S)

    idx, gate, counts = _route(logits, router_bias[l])

    blk = EXPERT_BLK
    cnt = counts.reshape(N_EXPERTS).astype(jnp.int32)
    padded = (cnt + blk - 1) // blk * blk
    pad_end = jnp.cumsum(padded)
    pad_start = pad_end - padded
    P = T * TOP_K + N_EXPERTS * blk
    n_blocks = P // blk
    blk_first_row = jnp.arange(n_blocks, dtype=jnp.int32) * blk
    blk_expert = jnp.minimum(
        jnp.sum((pad_end[None, :] <= blk_first_row[:, None]).astype(jnp.int32), axis=1),
        N_EXPERTS - 1)
    n_used = (pad_end[-1:] // blk).astype(jnp.int32)

    pos = _assign_slots(idx, pad_start.astype(F32).reshape(1, N_EXPERTS))[:, :TOP_K]

    Dq = D // 4
    half_base = (jnp.arange(2, dtype=jnp.int32) * P)[None, :, None]
    slots = (pos.T[:, None, :] + half_base)
    xs = _sc_dispatch(h2.reshape(2 * T, Dq), slots.reshape(TOP_K, 2 * T), 2 * P)
    ys = _routed_experts(xs.reshape(2, P, Dq), blk_expert, n_used, w1[l], w3[l], w2[l])
    yg = _sc_gather(ys.reshape(2 * P, Dq), slots.reshape(1, TOP_K * 2 * T))
    yg = yg.reshape(TOP_K, 2, T, Dq)

    out = _final(x1, h2, yg, gate, g2, ws1[l].astype(BF16), ws3[l].astype(BF16),
                 ws2[l].astype(BF16), row(ln2_g[l]), row(ln2_b[l]), S)
    return out.reshape(B, S, D)
```

```python
import math
from functools import partial

import jax
import jax.numpy as jnp
from jax import lax
from jax.experimental import pallas as pl
from jax.experimental.pallas import tpu as pltpu
from jax.experimental.pallas import tpu_sc as plsc

D_MODEL = 1024
N_HEADS = 8
HEAD_DIM = 64
V_DIM = 128
CONV_K = 31
N_EXPERTS = 256
TOP_K = 8
N_GROUPS = 8
GROUP_SIZE = N_EXPERTS // N_GROUPS
TOPK_GROUPS = 4
D_EXPERT = 256
ROUTED_SCALE = 2.5
LN_EPS = 1e-5
RMS_EPS = 1e-5
DEPTH = 1
DN_ALPHA = (2.0 * DEPTH) ** 0.25
LAMBDA_INIT = 0.8 - 0.6 * math.exp(-0.3 * 0)
Q_SCALE = HEAD_DIM ** -0.5 * math.log2(math.e)

LANES = 128
VMEM_LIMIT_BYTES = 56 * 1024 * 1024

PROJ_TM = 512
ATTN_TQ = 512
ATTN_QG = 256
MIX_TM = 256
CONV_HALO = 32
ROUTE_TM = 512
EXPERT_BLK = 512
FINAL_TM = 256
SC_WINDOW = 128

F32 = jnp.float32
BF16 = jnp.bfloat16
U32 = jnp.uint32
NEG_INF = float("-inf")


def _cparams(*sem):
    return pltpu.CompilerParams(dimension_semantics=sem, vmem_limit_bytes=VMEM_LIMIT_BYTES)


def _sigmoid(x):
    return 1.0 / (1.0 + jnp.exp(-x))


def _silu(x):
    return x * _sigmoid(x)


def _pack_halves(x):
    half = x.shape[-1] // 2
    bits = lambda v: lax.bitcast_convert_type(v.astype(BF16).astype(F32), U32)
    return (bits(x[:, :half]) >> 16) | (bits(x[:, half:]) & jnp.uint32(0xFFFF0000))


def _unpack_halves(w):
    lo = lax.bitcast_convert_type(w << 16, F32)
    hi = lax.bitcast_convert_type(w & jnp.uint32(0xFFFF0000), F32)
    return lo, hi


def _store_packed(ref, x):
    w = _pack_halves(x)
    q = w.shape[-1] // 2
    ref[0] = w[:, :q]
    ref[1] = w[:, q:]


def _load_packed(ref):
    return _unpack_halves(jnp.concatenate([ref[0], ref[1]], axis=-1))


def _layer_norm(r, g, b):
    mu = jnp.mean(r, axis=-1, keepdims=True)
    d = r - mu
    var = jnp.mean(d * d, axis=-1, keepdims=True)
    return d * lax.rsqrt(var + LN_EPS) * g + b


def _ada_kernel(c_ref, w_ref, b_ref, o_ref):
    a = _silu(c_ref[...])
    o_ref[...] = jnp.dot(a, w_ref[...], preferred_element_type=F32,
                         precision=lax.Precision.HIGHEST) + b_ref[...]


def _ada_modulation(c, w, b):
    B, D = c.shape
    n = w.shape[1] // D
    return pl.pallas_call(
        _ada_kernel,
        out_shape=jax.ShapeDtypeStruct((B, n * D), F32),
        grid=(n,),
        in_specs=[pl.BlockSpec((B, D), lambda j: (0, 0)),
                  pl.BlockSpec((D, D), lambda j: (0, j)),
                  pl.BlockSpec((1, D), lambda j: (0, j))],
        out_specs=pl.BlockSpec((B, D), lambda j: (0, j)),
        compiler_params=_cparams("arbitrary"),
        name="ada_modulation",
    )(c, w, b.reshape(1, n * D))


def _inproj_kernel(x_ref, sc_ref, sh_ref, w_ref, b_ref,
                   q_ref, k_ref, v_ref, u_ref, ga_ref, gb_ref):
    D = D_MODEL
    h = (x_ref[...] * (1.0 + sc_ref[0]) + sh_ref[0]).astype(BF16)

    def proj(j):
        return (jnp.dot(h, w_ref[:, j * D:(j + 1) * D], preferred_element_type=F32)
                + b_ref[:, j * D:(j + 1) * D])

    q_ref[...] = (proj(0) * Q_SCALE).astype(BF16)
    k_ref[...] = proj(1).astype(BF16)
    v_ref[...] = proj(2).astype(BF16)
    u_ref[...] = (proj(3) * _sigmoid(proj(4))).astype(BF16)
    ga_ref[...] = _sigmoid(proj(5)).astype(BF16)
    gb_ref[...] = _sigmoid(proj(6)).astype(BF16)


def _in_projection(x2, sc1, sh1, w_in, b_in, seq):
    T, D = x2.shape
    tm = PROJ_TM
    tiles_per_seq = seq // tm
    ncols = w_in.shape[1]
    tok = pl.BlockSpec((tm, D), lambda i: (i, 0))
    mod = pl.BlockSpec((1, 1, D), lambda i: (i // tiles_per_seq, 0, 0))
    out = jax.ShapeDtypeStruct((T, D), BF16)
    return pl.pallas_call(
        _inproj_kernel,
        out_shape=[out] * 6,
        grid=(T // tm,),
        in_specs=[tok, mod, mod,
                  pl.BlockSpec((D, ncols), lambda i: (0, 0)),
                  pl.BlockSpec((1, ncols), lambda i: (0, 0))],
        out_specs=[tok] * 6,
        compiler_params=_cparams("arbitrary"),
        name="in_projection",
    )(x2, sc1, sh1, w_in, b_in)


def _attn_kernel(lq1_ref, lk1_ref, lq2_ref, lk2_ref, g_ref, q_ref, k_ref, vt_ref, o_ref,
                 q2_ref, *scratch):
    tq = ATTN_TQ
    qg = ATTN_QG
    ng = 2 * tq // qg
    groups = [scratch[i * ng:(i + 1) * ng] for i in range(7)]
    s_slots = groups[0:2]
    p_refs, m_refs, l_refs, a_refs, acc_refs = groups[2:]
    qi = pl.program_id(2)
    lam = (jnp.exp(jnp.sum(lq1_ref[...] * lk1_ref[...], axis=-1, keepdims=True))
           - jnp.exp(jnp.sum(lq2_ref[...] * lk2_ref[...], axis=-1, keepdims=True))
           + LAMBDA_INIT)

    q = q_ref[...]
    lane = lax.broadcasted_iota(jnp.int32, q.shape, 1)
    zero = jnp.zeros_like(q)
    q2_ref[:tq, :] = jnp.where(lane < HEAD_DIM, q, zero)
    q2_ref[tq:, :] = jnp.where(lane >= HEAD_DIM, q, zero)

    def scores(j, slot, g):
        start = pl.multiple_of(j * tq, tq)
        s_slots[slot][g][...] = lax.dot_general(
            k_ref[pl.ds(start, tq), :], q2_ref[g * qg:(g + 1) * qg, :],
            (((1,), (1,)), ((), ())), preferred_element_type=F32)

    def update(j, slot, g, masked):
        start = pl.multiple_of(j * tq, tq)
        s_ref = s_slots[slot][g]
        for c in range(qg // LANES):
            cs = slice(c * LANES, (c + 1) * LANES)

            def load():
                s = s_ref[:, cs]
                if masked:
                    key = lax.broadcasted_iota(jnp.int32, s.shape, 0)
                    qry = (lax.broadcasted_iota(jnp.int32, s.shape, 1)
                           + (g * qg + c * LANES) % tq)
                    s = jnp.where(key <= qry, s, NEG_INF)
                return s

            m_prev = m_refs[g][:, cs]
            m_new = jnp.maximum(m_prev, jnp.max(load(), axis=0, keepdims=True))
            alpha = jnp.exp2(m_prev - m_new)
            p = jnp.exp2(load() - m_new)
            l_refs[g][:, cs] = alpha * l_refs[g][:, cs] + jnp.sum(p, axis=0, keepdims=True)
            p_refs[g][:, cs] = p.astype(BF16)
            m_refs[g][:, cs] = m_new
            a_refs[g][:, cs] = alpha
        pv = jnp.dot(vt_ref[:, pl.ds(start, tq)], p_refs[g][...], preferred_element_type=F32)
        acc_refs[g][...] = a_refs[g][...] * acc_refs[g][...] + pv

    def step(j, slot, masked=False, prefetch=True):
        for g in range(ng):
            if prefetch:
                scores(j + 1, 1 - slot, g)
            update(j, slot, g, masked)

    for g in range(ng):
        m_refs[g][...] = jnp.full(m_refs[g].shape, NEG_INF, F32)
        l_refs[g][...] = jnp.zeros(l_refs[g].shape, F32)
        acc_refs[g][...] = jnp.zeros(acc_refs[g].shape, F32)
        scores(0, 0, g)

    def pair(jj, carry):
        step(2 * jj, 0)
        step(2 * jj + 1, 1)
        return carry

    lax.fori_loop(0, qi // 2, pair, 0)

    @pl.when(qi % 2 == 1)
    def _():
        step(qi - 1, 0)
        step(qi, 1, masked=True, prefetch=False)

    @pl.when(qi % 2 == 0)
    def _():
        step(qi, 0, masked=True, prefetch=False)

    o = (jnp.concatenate([r[...] for r in acc_refs], axis=-1)
         / jnp.concatenate([r[...] for r in l_refs], axis=-1))
    o = o[:, :tq] - lam * o[:, tq:]
    o = o * lax.rsqrt(jnp.mean(o * o, axis=0, keepdims=True) + RMS_EPS) * g_ref[...]
    o_ref[...] = (o * (1.0 - LAMBDA_INIT)).T.astype(BF16)


def _diff_attention(q, k, vt, lq1, lk1, lq2, lk2, subln_g, batch, seq):
    T, D = q.shape
    tq = ATTN_TQ
    nq = seq // tq
    lam_spec = pl.BlockSpec((1, HEAD_DIM), lambda b, h, i: (0, 0))
    q_spec = pl.BlockSpec((tq, V_DIM), lambda b, h, i: (b * nq + i, h))
    kv_spec = pl.BlockSpec((seq, V_DIM), lambda b, h, i: (b, h))
    qg = ATTN_QG
    per_group = lambda shape, dtype: [pltpu.VMEM(shape, dtype) for _ in range(2 * tq // qg)]
    return pl.pallas_call(
        _attn_kernel,
        out_shape=jax.ShapeDtypeStruct((T, D), BF16),
        grid=(batch, N_HEADS, nq),
        in_specs=[lam_spec, lam_spec, lam_spec, lam_spec,
                  pl.BlockSpec((V_DIM, 1), lambda b, h, i: (0, 0)),
                  q_spec, kv_spec, pl.BlockSpec((V_DIM, seq), lambda b, h, i: (h, b))],
        out_specs=q_spec,
        scratch_shapes=(
            [pltpu.VMEM((2 * tq, V_DIM), BF16)]
            + per_group((tq, qg), F32)
            + per_group((tq, qg), F32)
            + per_group((tq, qg), BF16)
            + per_group((1, qg), F32)
            + per_group((1, qg), F32)
            + per_group((1, qg), F32)
            + per_group((V_DIM, qg), F32)),
        compiler_params=_cparams("arbitrary", "arbitrary", "arbitrary"),
        name="diff_attention",
    )(lq1, lk1, lq2, lk2, subln_g.reshape(V_DIM, 1), q, k, vt)


def _mixer_kernel(x_ref, o_ref, u_ref, up_ref, ga_ref, gb_ref, g1_ref, sc2_ref, sh2_ref,
                  wao_ref, cw_ref, cb_ref, clg_ref, clb_ref, wpw_ref, bpw_ref, wout_ref,
                  ln1g_ref, ln1b_ref, wrh_ref, wrl_ref,
                  x1_ref, h2_ref, lg_ref, ubuf_ref, *, tiles_per_seq):
    tm = MIX_TM
    first = (pl.program_id(0) % tiles_per_seq) == 0

    prev = up_ref[...].astype(F32)
    ubuf_ref[0:CONV_HALO, :] = jnp.where(first, jnp.zeros_like(prev), prev)
    ubuf_ref[CONV_HALO:, :] = u_ref[...].astype(F32)
    base = CONV_HALO - (CONV_K - 1)
    acc = jnp.zeros((tm, D_MODEL), F32)
    for j in range(CONV_K):
        acc = acc + cw_ref[j:j + 1, :] * ubuf_ref[base + j:base + j + tm, :]
    yc = _silu(_layer_norm(acc + cb_ref[...], clg_ref[...], clb_ref[...]))
    y_conv = jnp.dot(yc.astype(BF16), wpw_ref[...], preferred_element_type=F32) + bpw_ref[...]

    y_attn = jnp.dot(o_ref[...], wao_ref[...], preferred_element_type=F32)
    mix = ga_ref[...].astype(F32) * y_attn + gb_ref[...].astype(F32) * y_conv
    y = jnp.dot(mix.astype(BF16), wout_ref[...], preferred_element_type=F32)
    x1 = _layer_norm(DN_ALPHA * x_ref[...] + g1_ref[0] * y, ln1g_ref[...], ln1b_ref[...])
    x1_ref[...] = x1

    h2 = x1 * (1.0 + sc2_ref[0]) + sh2_ref[0]
    h2_hi = h2.astype(BF16)
    _store_packed(h2_ref, h2)
    h2_lo = (h2 - h2_hi.astype(F32)).astype(BF16)
    lg_ref[...] = (jnp.dot(h2_hi, wrh_ref[...], preferred_element_type=F32)
                   + (jnp.dot(h2_hi, wrl_ref[...], preferred_element_type=F32)
                      + jnp.dot(h2_lo, wrh_ref[...], preferred_element_type=F32)))


def _mixer_epilogue(x2, o, u, ga, gb, g1, sc2, sh2, wao, cw, cb, clg, clb, wpw, bpw, wout,
                    ln1g, ln1b, wr_hi, wr_lo, seq):
    T, D = x2.shape
    tm = MIX_TM
    tiles_per_seq = seq // tm
    halo_per_tile = tm // CONV_HALO
    tok = pl.BlockSpec((tm, D), lambda i: (i, 0))
    prev = pl.BlockSpec((CONV_HALO, D), lambda i: (jnp.maximum(i * halo_per_tile - 1, 0), 0))
    mod = pl.BlockSpec((1, 1, D), lambda i: (i // tiles_per_seq, 0, 0))
    row = pl.BlockSpec((1, D), lambda i: (0, 0))
    mat = pl.BlockSpec((D, D), lambda i: (0, 0))
    rw = pl.BlockSpec((D, N_EXPERTS), lambda i: (0, 0))
    return pl.pallas_call(
        partial(_mixer_kernel, tiles_per_seq=tiles_per_seq),
        out_shape=[jax.ShapeDtypeStruct((T, D), F32),
                   jax.ShapeDtypeStruct((2, T, D // 4), U32),
                   jax.ShapeDtypeStruct((T, N_EXPERTS), F32)],
        grid=(T // tm,),
        in_specs=[tok, tok, tok, prev, tok, tok, mod, mod, mod,
                  mat, pl.BlockSpec((CONV_K, D), lambda i: (0, 0)), row, row, row, mat, row, mat,
                  row, row, rw, rw],
        out_specs=[tok, pl.BlockSpec((2, tm, D // 4), lambda i: (0, i, 0)),
                   pl.BlockSpec((tm, N_EXPERTS), lambda i: (i, 0))],
        scratch_shapes=[pltpu.VMEM((CONV_HALO + tm, D), F32)],
        compiler_params=_cparams("arbitrary"),
        name="mixer_epilogue",
    )(x2, o, u, u, ga, gb, g1, sc2, sh2, wao, cw, cb, clg, clb, wpw, bpw, wout,
      ln1g, ln1b, wr_hi, wr_lo)


def _first_argmax(v, lane):
    m = jnp.max(v, axis=-1, keepdims=True)
    i = jnp.min(jnp.where(v == m, lane, N_EXPERTS), axis=-1, keepdims=True)
    return m, i


def _route_kernel(lg_ref, rb_ref, idx_ref, gate_ref, cnt_ref):
    tm = ROUTE_TM
    scores = _sigmoid(lg_ref[...])
    choice = scores + rb_ref[...]
    lane = lax.broadcasted_iota(jnp.int32, (tm, N_EXPERTS), 1)
    grp = lane // GROUP_SIZE

    gs = []
    for g in range(N_GROUPS):
        cg = jnp.where(grp == g, choice, NEG_INF)
        m1, i1 = _first_argmax(cg, lane)
        m2 = jnp.max(jnp.where(lane == i1, NEG_INF, cg), axis=-1, keepdims=True)
        gs.append(m1 + m2)

    allowed = jnp.zeros((tm, N_EXPERTS), jnp.bool_)
    for g in range(N_GROUPS):
        ahead = jnp.zeros((tm, 1), jnp.int32)
        for o in range(N_GROUPS):
            if o == g:
                continue
            beats = (gs[o] >= gs[g]) if o < g else (gs[o] > gs[g])
            ahead = ahead + beats.astype(jnp.int32)
        allowed = allowed | ((grp == g) & (ahead < TOPK_GROUPS))

    masked = jnp.where(allowed, choice, NEG_INF)
    out_lane = lax.broadcasted_iota(jnp.int32, (tm, LANES), 1)
    idx_out = jnp.zeros((tm, LANES), jnp.int32)
    gate_out = jnp.zeros((tm, LANES), F32)
    sel = jnp.zeros((tm, N_EXPERTS), F32)
    total = jnp.zeros((tm, 1), F32)
    for k in range(TOP_K):
        _, i = _first_argmax(masked, lane)
        hit = lane == i
        sc = jnp.sum(jnp.where(hit, scores, 0.0), axis=-1, keepdims=True)
        masked = jnp.where(hit, NEG_INF, masked)
        sel = sel + hit.astype(F32)
        total = total + sc
        idx_out = jnp.where(out_lane == k, i, idx_out)
        gate_out = jnp.where(out_lane == k, sc, gate_out)
    idx_ref[...] = idx_out
    gate_ref[...] = gate_out / total * ROUTED_SCALE

    @pl.when(pl.program_id(0) == 0)
    def _():
        cnt_ref[...] = jnp.zeros_like(cnt_ref)
    cnt_ref[...] += jnp.sum(sel, axis=0, keepdims=True)


def _route(logits, router_bias):
    T = logits.shape[0]
    tm = ROUTE_TM
    return pl.pallas_call(
        _route_kernel,
        out_shape=[jax.ShapeDtypeStruct((T, LANES), jnp.int32),
                   jax.ShapeDtypeStruct((T, LANES), F32),
                   jax.ShapeDtypeStruct((1, N_EXPERTS), F32)],
        grid=(T // tm,),
        in_specs=[pl.BlockSpec((tm, N_EXPERTS), lambda i: (i, 0)),
                  pl.BlockSpec((1, N_EXPERTS), lambda i: (0, 0))],
        out_specs=[pl.BlockSpec((tm, LANES), lambda i: (i, 0)),
                   pl.BlockSpec((tm, LANES), lambda i: (i, 0)),
                   pl.BlockSpec((1, N_EXPERTS), lambda i: (0, 0))],
        compiler_params=_cparams("arbitrary"),
        name="route_topk",
    )(logits, router_bias.reshape(1, N_EXPERTS))


def _slot_kernel(idx_ref, start_ref, pos_ref, carry_ref):
    tm = ROUTE_TM

    @pl.when(pl.program_id(0) == 0)
    def _():
        carry_ref[...] = jnp.zeros_like(carry_ref)

    idx = idx_ref[...]
    lane = lax.broadcasted_iota(jnp.int32, (tm, N_EXPERTS), 1)
    hits = [lane == idx[:, k:k + 1] for k in range(TOP_K)]
    sel = jnp.zeros((tm, N_EXPERTS), F32)
    for h in hits:
        sel = sel + h.astype(F32)
    r = lax.broadcasted_iota(jnp.int32, (tm, tm), 0)
    c = lax.broadcasted_iota(jnp.int32, (tm, tm), 1)
    earlier = (c < r).astype(BF16)
    rank = jnp.dot(earlier, sel.astype(BF16), preferred_element_type=F32)
    slot = rank + carry_ref[...] + start_ref[...]
    out_lane = lax.broadcasted_iota(jnp.int32, (tm, LANES), 1)
    pos = jnp.zeros((tm, LANES), jnp.int32)
    for k, h in enumerate(hits):
        pk = jnp.sum(jnp.where(h, slot, 0.0), axis=-1, keepdims=True).astype(jnp.int32)
        pos = jnp.where(out_lane == k, pk, pos)
    pos_ref[...] = pos
    carry_ref[...] += jnp.sum(sel, axis=0, keepdims=True)


def _assign_slots(idx, pad_start):
    T = idx.shape[0]
    tm = ROUTE_TM
    return pl.pallas_call(
        _slot_kernel,
        out_shape=jax.ShapeDtypeStruct((T, LANES), jnp.int32),
        grid=(T // tm,),
        in_specs=[pl.BlockSpec((tm, LANES), lambda i: (i, 0)),
                  pl.BlockSpec((1, N_EXPERTS), lambda i: (0, 0))],
        out_specs=pl.BlockSpec((tm, LANES), lambda i: (i, 0)),
        scratch_shapes=[pltpu.VMEM((1, N_EXPERTS), F32)],
        compiler_params=_cparams("arbitrary"),
        name="assign_slots",
    )(idx, pad_start)


def _dot_halves(lo, hi, w_ref):
    half = lo.shape[-1]
    return (jnp.dot(lo.astype(BF16), w_ref[:half, :].astype(BF16), preferred_element_type=F32)
            + jnp.dot(hi.astype(BF16), w_ref[half:, :].astype(BF16), preferred_element_type=F32))


def _expert_kernel(be_ref, nu_ref, xs_ref, w1_ref, w3_ref, w2_ref, ys_ref):
    @pl.when(pl.program_id(0) < nu_ref[0])
    def _():
        lo, hi = _load_packed(xs_ref)
        a = _dot_halves(lo, hi, w1_ref.at[0])
        b = _dot_halves(lo, hi, w3_ref.at[0])
        hb = (_silu(a) * b).astype(BF16)
        _store_packed(ys_ref, jnp.dot(hb, w2_ref[0].astype(BF16), preferred_element_type=F32))

    @pl.when(pl.program_id(0) >= nu_ref[0])
    def _():
        ys_ref[...] = jnp.zeros_like(ys_ref)


def _routed_experts(xs, blk_expert, n_used, w1, w3, w2):
    _, P, Dq = xs.shape
    blk = EXPERT_BLK
    D, F = w1.shape[-2:]
    grid_spec = pltpu.PrefetchScalarGridSpec(
        num_scalar_prefetch=2,
        grid=(P // blk,),
        in_specs=[pl.BlockSpec((2, blk, Dq), lambda b, be, nu: (0, jnp.minimum(b, nu[0] - 1), 0)),
                  pl.BlockSpec((1, D, F), lambda b, be, nu: (be[b], 0, 0)),
                  pl.BlockSpec((1, D, F), lambda b, be, nu: (be[b], 0, 0)),
                  pl.BlockSpec((1, F, D), lambda b, be, nu: (be[b], 0, 0))],
        out_specs=pl.BlockSpec((2, blk, Dq), lambda b, be, nu: (0, b, 0)),
    )
    return pl.pallas_call(
        _expert_kernel,
        out_shape=jax.ShapeDtypeStruct((2, P, Dq), U32),
        grid_spec=grid_spec,
        compiler_params=_cparams("arbitrary"),
        name="routed_experts",
    )(blk_expert, n_used, xs, w1, w3, w2)


def _sc_mesh():
    return plsc.VectorSubcoreMesh(core_axis_name="c", subcore_axis_name="s")


def _sc_dispatch(rows, slots_t, n_slots):
    T, W = rows.shape
    K = slots_t.shape[0]
    win = SC_WINDOW

    @pl.kernel(out_type=jax.ShapeDtypeStruct((n_slots, W), rows.dtype), mesh=_sc_mesh(),
               scratch_types=[])
    def dispatch(x_hbm, i_hbm, o_hbm):
        def body(x_vmem, i_vmem):
            for k in range(K):
                pltpu.sync_copy(x_vmem, o_hbm.at[i_vmem.at[k]])

        pltpu.emit_pipeline(
            body,
            grid=(T // win,),
            in_specs=[pl.BlockSpec((win, W), lambda i: (i, 0)),
                      pl.BlockSpec((K, win), lambda i: (0, i))],
            out_specs=[],
            core_axis_name=("c", "s"),
            dimension_semantics=(pltpu.PARALLEL,),
        )(x_hbm, i_hbm)

    return dispatch(rows, slots_t)


def _sc_gather(table, idx):
    W = table.shape[1]
    N = idx.shape[1]
    win = SC_WINDOW

    @pl.kernel(out_type=jax.ShapeDtypeStruct((N, W), table.dtype), mesh=_sc_mesh(),
               scratch_types=[])
    def gather(x_hbm, i_hbm, o_hbm):
        def body(i_vmem, o_vmem):
            pltpu.sync_copy(x_hbm.at[i_vmem.at[0]], o_vmem)

        pltpu.emit_pipeline(
            body,
            grid=(N // win,),
            in_specs=[pl.BlockSpec((1, win), lambda i: (0, i))],
            out_specs=[pl.BlockSpec((win, W), lambda i: (i, 0))],
            core_axis_name=("c", "s"),
            dimension_semantics=(pltpu.PARALLEL,),
        )(i_hbm, o_hbm)

    return gather(table, idx)


def _final_kernel(x1_ref, h2_ref, yg_ref, gate_ref, g2_ref, ws1_ref, ws3_ref, ws2_ref,
                  lng_ref, lnb_ref, o_ref):
    Dh = D_MODEL // 2
    gate = gate_ref[...]
    routed_lo = jnp.zeros((x1_ref.shape[0], Dh), F32)
    routed_hi = jnp.zeros((x1_ref.shape[0], Dh), F32)
    for k in range(TOP_K):
        lo, hi = _load_packed(yg_ref.at[k])
        routed_lo = routed_lo + lo * gate[:, k:k + 1]
        routed_hi = routed_hi + hi * gate[:, k:k + 1]
    lo, hi = _load_packed(h2_ref)
    a = _dot_halves(lo, hi, ws1_ref)
    b = _dot_halves(lo, hi, ws3_ref)
    shared = jnp.dot((_silu(a) * b).astype(BF16), ws2_ref[...], preferred_element_type=F32)
    y2 = jnp.concatenate([routed_lo, routed_hi], axis=-1) + shared
    o_ref[...] = _layer_norm(DN_ALPHA * x1_ref[...] + g2_ref[0] * y2, lng_ref[...], lnb_ref[...])


def _final(x1, h2, yg, gate, g2, ws1, ws3, ws2, ln2g, ln2b, seq):
    T, D = x1.shape
    tm = FINAL_TM
    tiles_per_seq = seq // tm
    F = ws1.shape[-1]
    tok = pl.BlockSpec((tm, D), lambda i: (i, 0))
    row = pl.BlockSpec((1, D), lambda i: (0, 0))
    return pl.pallas_call(
        _final_kernel,
        out_shape=jax.ShapeDtypeStruct((T, D), F32),
        grid=(T // tm,),
        in_specs=[tok, pl.BlockSpec((2, tm, D // 4), lambda i: (0, i, 0)),
                  pl.BlockSpec((TOP_K, 2, tm, D // 4), lambda i: (0, 0, i, 0)),
                  pl.BlockSpec((tm, LANES), lambda i: (i, 0)),
                  pl.BlockSpec((1, 1, D), lambda i: (i // tiles_per_seq, 0, 0)),
                  pl.BlockSpec((D, F), lambda i: (0, 0)),
                  pl.BlockSpec((D, F), lambda i: (0, 0)),
                  pl.BlockSpec((F, D), lambda i: (0, 0)),
                  row, row],
        out_specs=tok,
        compiler_params=_cparams("arbitrary"),
        name="moe_combine_final",
    )(x1, h2, yg, gate, g2, ws1, ws3, ws2, ln2g, ln2b)


def kernel(x, c, ada_w, ada_b, w_in, b_in, lambda_q1, lambda_k1, lambda_q2, lambda_k2, subln_g, w_attn_o, conv_w, conv_b, conv_ln_g, conv_ln_b, conv_pw_w, conv_pw_b, w_out, ln1_g, ln1_b, router_w, router_bias, w1, w3, w2, ws1, ws3, ws2, ln2_g, ln2_b):
    B, S, D = x.shape
    T = B * S
    l = 0
    x2 = x.reshape(T, D)
    row = lambda a: a.reshape(1, -1)

    mod = _ada_modulation(c, ada_w[l], ada_b[l])
    sh1, sc1, g1, sh2, sc2, g2 = [m.reshape(B, 1, D) for m in jnp.split(mod, 6, axis=-1)]

    q, k, v, u, ga, gb = _in_projection(x2, sc1, sh1, w_in[l].astype(BF16), row(b_in[l]), S)
    o = _diff_attention(q, k, v.T, row(lambda_q1[l]), row(lambda_k1[l]), row(lambda_q2[l]),
                        row(lambda_k2[l]), row(subln_g[l]), B, S)

    wr = router_w[l]
    wr_hi = wr.astype(BF16)
    wr_lo = (wr - wr_hi.astype(F32)).astype(BF16)
    x1, h2, logits = _mixer_epilogue(
        x2, o, u, ga, gb, g1, sc2, sh2, w_attn_o[l].astype(BF16), conv_w[l], row(conv_b[l]),
        row(conv_ln_g[l]), row(conv_ln_b[l]), conv_pw_w[l].astype(BF16), row(conv_pw_b[l]),
        w_out[l].astype(BF16), row(ln1_g[l]), row(ln1_b[l]), wr_hi, wr_lo, S)

    idx, gate, counts = _route(logits, router_bias[l])

    blk = EXPERT_BLK
    cnt = counts.reshape(N_EXPERTS).astype(jnp.int32)
    padded = (cnt + blk - 1) // blk * blk
    pad_end = jnp.cumsum(padded)
    pad_start = pad_end - padded
    P = T * TOP_K + N_EXPERTS * blk
    n_blocks = P // blk
    blk_first_row = jnp.arange(n_blocks, dtype=jnp.int32) * blk
    blk_expert = jnp.minimum(
        jnp.sum((pad_end[None, :] <= blk_first_row[:, None]).astype(jnp.int32), axis=1),
        N_EXPERTS - 1)
    n_used = (pad_end[-1:] // blk).astype(jnp.int32)

    pos = _assign_slots(idx, pad_start.astype(F32).reshape(1, N_EXPERTS))[:, :TOP_K]

    Dq = D // 4
    half_base = (jnp.arange(2, dtype=jnp.int32) * P)[None, :, None]
    slots = (pos.T[:, None, :] + half_base)
    xs = _sc_dispatch(h2.reshape(2 * T, Dq), slots.reshape(TOP_K, 2 * T), 2 * P)
    ys = _routed_experts(xs.reshape(2, P, Dq), blk_expert, n_used, w1[l], w3[l], w2[l])
    yg = _sc_gather(ys.reshape(2 * P, Dq), slots.reshape(1, TOP_K * 2 * T))
    yg = yg.reshape(TOP_K, 2, T, Dq)

    out = _final(x1, h2, yg, gate, g2, ws1[l].astype(BF16), ws3[l].astype(BF16),
                 ws2[l].astype(BF16), row(ln2_g[l]), row(ln2_b[l]), S)
    return out.reshape(B, S, D)
```

```python
import math
from functools import partial

import jax
import jax.numpy as jnp
from jax import lax
from jax.experimental import pallas as pl
from jax.experimental.pallas import tpu as pltpu
from jax.experimental.pallas import tpu_sc as plsc

D_MODEL = 1024
N_HEADS = 8
HEAD_DIM = 64
V_DIM = 128
CONV_K = 31
N_EXPERTS = 256
TOP_K = 8
N_GROUPS = 8
GROUP_SIZE = N_EXPERTS // N_GROUPS
TOPK_GROUPS = 4
D_EXPERT = 256
ROUTED_SCALE = 2.5
LN_EPS = 1e-5
RMS_EPS = 1e-5
DEPTH = 1
DN_ALPHA = (2.0 * DEPTH) ** 0.25
LAMBDA_INIT = 0.8 - 0.6 * math.exp(-0.3 * 0)
Q_SCALE = HEAD_DIM ** -0.5 * math.log2(math.e)

LANES = 128
SUBLANES = 8
VMEM_LIMIT_BYTES = 56 * 1024 * 1024

PROJ_TM = 512
ATTN_TQ = 512
ATTN_QG = 256
MIX_TM = 256
CONV_HALO = 32
CONV_ROWS = 64
ROUTE_TM = 512
EXPERT_BLK = 512
FINAL_TM = 256
SC_WINDOW = 128

F32 = jnp.float32
BF16 = jnp.bfloat16
U32 = jnp.uint32
NEG_INF = float("-inf")


def _cparams(*sem):
    return pltpu.CompilerParams(dimension_semantics=sem, vmem_limit_bytes=VMEM_LIMIT_BYTES)


def _sigmoid(x):
    return 1.0 / (1.0 + jnp.exp(-x))


def _silu(x):
    return x * _sigmoid(x)


def _pack_halves(x):
    half = x.shape[-1] // 2
    bits = lambda v: lax.bitcast_convert_type(v.astype(BF16).astype(F32), U32)
    return (bits(x[:, :half]) >> 16) | (bits(x[:, half:]) & jnp.uint32(0xFFFF0000))


def _unpack_halves(w):
    lo = lax.bitcast_convert_type(w << 16, F32)
    hi = lax.bitcast_convert_type(w & jnp.uint32(0xFFFF0000), F32)
    return lo, hi


def _store_packed(ref, x):
    w = _pack_halves(x)
    q = w.shape[-1] // 2
    ref[0] = w[:, :q]
    ref[1] = w[:, q:]


def _load_packed(ref):
    return _unpack_halves(jnp.concatenate([ref[0], ref[1]], axis=-1))


def _layer_norm(r, g, b):
    mu = jnp.mean(r, axis=-1, keepdims=True)
    d = r - mu
    var = jnp.mean(d * d, axis=-1, keepdims=True)
    return d * lax.rsqrt(var + LN_EPS) * g + b


def _ada_kernel(c_ref, w_ref, b_ref, o_ref):
    a = _silu(c_ref[...])
    o_ref[...] = jnp.dot(a, w_ref[...], preferred_element_type=F32,
                         precision=lax.Precision.HIGHEST) + b_ref[...]


def _ada_modulation(c, w, b):
    B, D = c.shape
    n = w.shape[1] // D
    return pl.pallas_call(
        _ada_kernel,
        out_shape=jax.ShapeDtypeStruct((B, n * D), F32),
        grid=(n,),
        in_specs=[pl.BlockSpec((B, D), lambda j: (0, 0)),
                  pl.BlockSpec((D, D), lambda j: (0, j)),
                  pl.BlockSpec((1, D), lambda j: (0, j))],
        out_specs=pl.BlockSpec((B, D), lambda j: (0, j)),
        compiler_params=_cparams("arbitrary"),
        name="ada_modulation",
    )(c, w, b.reshape(1, n * D))


def _inproj_kernel(x_ref, sc_ref, sh_ref, w_ref, b_ref,
                   q_ref, k_ref, v_ref, u_ref, ga_ref, gb_ref):
    D = D_MODEL
    h = (x_ref[...] * (1.0 + sc_ref[0]) + sh_ref[0]).astype(BF16)

    def proj(j):
        return (jnp.dot(h, w_ref[:, j * D:(j + 1) * D], preferred_element_type=F32)
                + b_ref[:, j * D:(j + 1) * D])

    q_ref[...] = (proj(0) * Q_SCALE).astype(BF16)
    k_ref[...] = proj(1).astype(BF16)
    v_ref[...] = proj(2).astype(BF16)
    u_ref[...] = (proj(3) * _sigmoid(proj(4))).astype(BF16)
    ga_ref[...] = _sigmoid(proj(5)).astype(BF16)
    gb_ref[...] = _sigmoid(proj(6)).astype(BF16)


def _in_projection(x2, sc1, sh1, w_in, b_in, seq):
    T, D = x2.shape
    tm = PROJ_TM
    tiles_per_seq = seq // tm
    ncols = w_in.shape[1]
    tok = pl.BlockSpec((tm, D), lambda i: (i, 0))
    mod = pl.BlockSpec((1, 1, D), lambda i: (i // tiles_per_seq, 0, 0))
    out = jax.ShapeDtypeStruct((T, D), BF16)
    return pl.pallas_call(
        _inproj_kernel,
        out_shape=[out] * 6,
        grid=(T // tm,),
        in_specs=[tok, mod, mod,
                  pl.BlockSpec((D, ncols), lambda i: (0, 0)),
                  pl.BlockSpec((1, ncols), lambda i: (0, 0))],
        out_specs=[tok] * 6,
        compiler_params=_cparams("arbitrary"),
        name="in_projection",
    )(x2, sc1, sh1, w_in, b_in)


def _attn_kernel(lq1_ref, lk1_ref, lq2_ref, lk2_ref, g_ref, q_ref, k_ref, vt_ref, o_ref,
                 q2_ref, *scratch):
    tq = ATTN_TQ
    qg = ATTN_QG
    ng = 2 * tq // qg
    groups = [scratch[i * ng:(i + 1) * ng] for i in range(7)]
    s_slots = groups[0:2]
    p_refs, m_refs, l_refs, a_refs, acc_refs = groups[2:]
    qi = pl.program_id(2)
    lam = (jnp.exp(jnp.sum(lq1_ref[...] * lk1_ref[...], axis=-1, keepdims=True))
           - jnp.exp(jnp.sum(lq2_ref[...] * lk2_ref[...], axis=-1, keepdims=True))
           + LAMBDA_INIT)

    q = q_ref[...]
    lane = lax.broadcasted_iota(jnp.int32, q.shape, 1)
    zero = jnp.zeros_like(q)
    q2_ref[:tq, :] = jnp.where(lane < HEAD_DIM, q, zero)
    q2_ref[tq:, :] = jnp.where(lane >= HEAD_DIM, q, zero)

    def scores(j, slot, g):
        start = pl.multiple_of(j * tq, tq)
        s_slots[slot][g][...] = lax.dot_general(
            k_ref[pl.ds(start, tq), :], q2_ref[g * qg:(g + 1) * qg, :],
            (((1,), (1,)), ((), ())), preferred_element_type=F32)

    def update(j, slot, g, masked):
        start = pl.multiple_of(j * tq, tq)
        s_ref = s_slots[slot][g]
        for c in range(qg // LANES):
            cs = slice(c * LANES, (c + 1) * LANES)

            def load():
                s = s_ref[:, cs]
                if masked:
                    key = lax.broadcasted_iota(jnp.int32, s.shape, 0)
                    qry = (lax.broadcasted_iota(jnp.int32, s.shape, 1)
                           + (g * qg + c * LANES) % tq)
                    s = jnp.where(key <= qry, s, NEG_INF)
                return s

            m_prev = m_refs[g][:, cs]
            m_new = jnp.maximum(m_prev, jnp.max(load(), axis=0, keepdims=True))
            alpha = jnp.exp2(m_prev - m_new)
            p = jnp.exp2(load() - m_new)
            l_refs[g][:, cs] = alpha * l_refs[g][:, cs] + jnp.sum(p, axis=0, keepdims=True)
            p_refs[g][:, cs] = p.astype(BF16)
            m_refs[g][:, cs] = m_new
            a_refs[g][:, cs] = alpha
        pv = jnp.dot(vt_ref[:, pl.ds(start, tq)], p_refs[g][...], preferred_element_type=F32)
        acc_refs[g][...] = a_refs[g][...] * acc_refs[g][...] + pv

    def step(j, slot, masked=False, prefetch=True):
        for g in range(ng):
            if prefetch:
                scores(j + 1, 1 - slot, g)
            update(j, slot, g, masked)

    for g in range(ng):
        m_refs[g][...] = jnp.full(m_refs[g].shape, NEG_INF, F32)
        l_refs[g][...] = jnp.zeros(l_refs[g].shape, F32)
        acc_refs[g][...] = jnp.zeros(acc_refs[g].shape, F32)
        scores(0, 0, g)

    def pair(jj, carry):
        step(2 * jj, 0)
        step(2 * jj + 1, 1)
        return carry

    lax.fori_loop(0, qi // 2, pair, 0)

    @pl.when(qi % 2 == 1)
    def _():
        step(qi - 1, 0)
        step(qi, 1, masked=True, prefetch=False)

    @pl.when(qi % 2 == 0)
    def _():
        step(qi, 0, masked=True, prefetch=False)

    o = (jnp.concatenate([r[...] for r in acc_refs], axis=-1)
         / jnp.concatenate([r[...] for r in l_refs], axis=-1))
    o = o[:, :tq] - lam * o[:, tq:]
    o = o * lax.rsqrt(jnp.mean(o * o, axis=0, keepdims=True) + RMS_EPS) * g_ref[...]
    o_ref[...] = (o * (1.0 - LAMBDA_INIT)).T.astype(BF16)


def _diff_attention(q, k, vt, lq1, lk1, lq2, lk2, subln_g, batch, seq):
    T, D = q.shape
    tq = ATTN_TQ
    nq = seq // tq
    lam_spec = pl.BlockSpec((1, HEAD_DIM), lambda b, h, i: (0, 0))
    q_spec = pl.BlockSpec((tq, V_DIM), lambda b, h, i: (b * nq + i, h))
    kv_spec = pl.BlockSpec((seq, V_DIM), lambda b, h, i: (b, h))
    qg = ATTN_QG
    per_group = lambda shape, dtype: [pltpu.VMEM(shape, dtype) for _ in range(2 * tq // qg)]
    return pl.pallas_call(
        _attn_kernel,
        out_shape=jax.ShapeDtypeStruct((T, D), BF16),
        grid=(batch, N_HEADS, nq),
        in_specs=[lam_spec, lam_spec, lam_spec, lam_spec,
                  pl.BlockSpec((V_DIM, 1), lambda b, h, i: (0, 0)),
                  q_spec, kv_spec, pl.BlockSpec((V_DIM, seq), lambda b, h, i: (h, b))],
        out_specs=q_spec,
        scratch_shapes=(
            [pltpu.VMEM((2 * tq, V_DIM), BF16)]
            + per_group((tq, qg), F32)
            + per_group((tq, qg), F32)
            + per_group((tq, qg), BF16)
            + per_group((1, qg), F32)
            + per_group((1, qg), F32)
            + per_group((1, qg), F32)
            + per_group((V_DIM, qg), F32)),
        compiler_params=_cparams("arbitrary", "arbitrary", "arbitrary"),
        name="diff_attention",
    )(lq1, lk1, lq2, lk2, subln_g.reshape(V_DIM, 1), q, k, vt)


def _mixer_kernel(x_ref, o_ref, u_ref, up_ref, ga_ref, gb_ref, g1_ref, sc2_ref, sh2_ref,
                  wao_ref, cw_ref, cb_ref, clg_ref, clb_ref, wpw_ref, bpw_ref, wout_ref,
                  ln1g_ref, ln1b_ref, wrh_ref, wrl_ref,
                  x1_ref, h2_ref, lg_ref, ubuf_ref, ush_ref, conv_ref, *, tiles_per_seq):
    tm = MIX_TM
    first = (pl.program_id(0) % tiles_per_seq) == 0

    prev = up_ref[...].astype(F32)
    ubuf_ref[0:CONV_HALO, :] = jnp.where(first, jnp.zeros_like(prev), prev)
    ubuf_ref[CONV_HALO:, :] = u_ref[...].astype(F32)
    base = CONV_HALO - (CONV_K - 1)
    n_shift = tm + CONV_HALO - SUBLANES
    for r in range(1, SUBLANES):
        ush_ref[r - 1] = ubuf_ref[r:r + n_shift, :]
    for c in range(D_MODEL // LANES):
        cl = slice(c * LANES, (c + 1) * LANES)
        for r0 in range(0, tm, CONV_ROWS):
            acc = jnp.zeros((CONV_ROWS, LANES), F32)
            for j in range(CONV_K):
                a, r = divmod(base + j, SUBLANES)
                src = ubuf_ref if r == 0 else ush_ref.at[r - 1]
                lo = SUBLANES * a + r0
                acc = acc + cw_ref[j:j + 1, cl] * src[lo:lo + CONV_ROWS, cl]
            conv_ref[r0:r0 + CONV_ROWS, cl] = acc
    yc = _silu(_layer_norm(conv_ref[...] + cb_ref[...], clg_ref[...], clb_ref[...]))
    y_conv = jnp.dot(yc.astype(BF16), wpw_ref[...], preferred_element_type=F32) + bpw_ref[...]

    y_attn = jnp.dot(o_ref[...], wao_ref[...], preferred_element_type=F32)
    mix = ga_ref[...].astype(F32) * y_attn + gb_ref[...].astype(F32) * y_conv
    y = jnp.dot(mix.astype(BF16), wout_ref[...], preferred_element_type=F32)
    x1 = _layer_norm(DN_ALPHA * x_ref[...] + g1_ref[0] * y, ln1g_ref[...], ln1b_ref[...])
    x1_ref[...] = x1

    h2 = x1 * (1.0 + sc2_ref[0]) + sh2_ref[0]
    h2_hi = h2.astype(BF16)
    _store_packed(h2_ref, h2)
    h2_lo = (h2 - h2_hi.astype(F32)).astype(BF16)
    nt = lambda w, a: lax.dot_general(w, a, (((1,), (1,)), ((), ())), preferred_element_type=F32)
    lg_ref[...] = (nt(wrh_ref[...], h2_hi)
                   + (nt(wrl_ref[...], h2_hi) + nt(wrh_ref[...], h2_lo)))


def _mixer_epilogue(x2, o, u, ga, gb, g1, sc2, sh2, wao, cw, cb, clg, clb, wpw, bpw, wout,
                    ln1g, ln1b, wr_hi, wr_lo, seq):
    T, D = x2.shape
    tm = MIX_TM
    tiles_per_seq = seq // tm
    halo_per_tile = tm // CONV_HALO
    tok = pl.BlockSpec((tm, D), lambda i: (i, 0))
    prev = pl.BlockSpec((CONV_HALO, D), lambda i: (jnp.maximum(i * halo_per_tile - 1, 0), 0))
    mod = pl.BlockSpec((1, 1, D), lambda i: (i // tiles_per_seq, 0, 0))
    row = pl.BlockSpec((1, D), lambda i: (0, 0))
    mat = pl.BlockSpec((D, D), lambda i: (0, 0))
    rw = pl.BlockSpec((N_EXPERTS, D), lambda i: (0, 0))
    return pl.pallas_call(
        partial(_mixer_kernel, tiles_per_seq=tiles_per_seq),
        out_shape=[jax.ShapeDtypeStruct((T, D), F32),
                   jax.ShapeDtypeStruct((2, T, D // 4), U32),
                   jax.ShapeDtypeStruct((N_EXPERTS, T), F32)],
        grid=(T // tm,),
        in_specs=[tok, tok, tok, prev, tok, tok, mod, mod, mod,
                  mat, pl.BlockSpec((CONV_K, D), lambda i: (0, 0)), row, row, row, mat, row, mat,
                  row, row, rw, rw],
        out_specs=[tok, pl.BlockSpec((2, tm, D // 4), lambda i: (0, i, 0)),
                   pl.BlockSpec((N_EXPERTS, tm), lambda i: (0, i))],
        scratch_shapes=[pltpu.VMEM((CONV_HALO + tm, D), F32),
                        pltpu.VMEM((SUBLANES - 1, tm + CONV_HALO - SUBLANES, D), F32),
                        pltpu.VMEM((tm, D), F32)],
        compiler_params=_cparams("arbitrary"),
        name="mixer_epilogue",
    )(x2, o, u, u, ga, gb, g1, sc2, sh2, wao, cw, cb, clg, clb, wpw, bpw, wout,
      ln1g, ln1b, wr_hi, wr_lo)


def _first_argmax(v, eid):
    m = jnp.max(v, axis=0, keepdims=True)
    i = jnp.min(jnp.where(v == m, eid, float(N_EXPERTS)), axis=0, keepdims=True)
    return m, i


def _route_kernel(lg_ref, rb_ref, idx_ref, gate_ref, cnt_ref):
    tm = ROUTE_TM
    scores = _sigmoid(lg_ref[...])
    choice = scores + rb_ref[...]
    eid = lax.broadcasted_iota(jnp.int32, (N_EXPERTS, tm), 0).astype(F32)
    rows = lambda a, g: a[g * GROUP_SIZE:(g + 1) * GROUP_SIZE, :]

    gs = []
    for g in range(N_GROUPS):
        cg = rows(choice, g)
        eg = (lax.broadcasted_iota(jnp.int32, cg.shape, 0) + g * GROUP_SIZE).astype(F32)
        m1, i1 = _first_argmax(cg, eg)
        m2 = jnp.max(jnp.where(eg == i1, NEG_INF, cg), axis=0, keepdims=True)
        gs.append(jnp.broadcast_to(m1 + m2, (SUBLANES, tm)))

    kept = []
    for g in range(N_GROUPS):
        ahead = jnp.zeros((SUBLANES, tm), F32)
        for o in range(N_GROUPS):
            if o == g:
                continue
            beats = (gs[o] >= gs[g]) if o < g else (gs[o] > gs[g])
            ahead = ahead + jnp.where(beats, 1.0, 0.0)
        cap = jnp.where(ahead < TOPK_GROUPS, float("inf"), NEG_INF)
        for r in range(g * GROUP_SIZE, (g + 1) * GROUP_SIZE, SUBLANES):
            kept.append(jnp.minimum(choice[r:r + SUBLANES, :], cap))
    masked = jnp.concatenate(kept, axis=0)

    sel = jnp.zeros((N_EXPERTS, tm), F32)
    total = jnp.zeros((1, tm), F32)
    picked = []
    for k in range(TOP_K):
        _, i = _first_argmax(masked, eid)
        hit = eid == i
        sc = jnp.sum(jnp.where(hit, scores, 0.0), axis=0, keepdims=True)
        masked = jnp.where(hit, NEG_INF, masked)
        sel = sel + hit.astype(F32)
        total = total + sc
        idx_ref[k:k + 1, :] = i.astype(jnp.int32)
        picked.append(sc)
    for k in range(TOP_K):
        gate_ref[k:k + 1, :] = picked[k] / total * ROUTED_SCALE

    @pl.when(pl.program_id(0) == 0)
    def _():
        cnt_ref[...] = jnp.zeros_like(cnt_ref)
    cnt_ref[...] += jnp.sum(sel, axis=1, keepdims=True)


def _route(logits_t, router_bias):
    T = logits_t.shape[1]
    tm = ROUTE_TM
    return pl.pallas_call(
        _route_kernel,
        out_shape=[jax.ShapeDtypeStruct((TOP_K, T), jnp.int32),
                   jax.ShapeDtypeStruct((TOP_K, T), F32),
                   jax.ShapeDtypeStruct((N_EXPERTS, 1), F32)],
        grid=(T // tm,),
        in_specs=[pl.BlockSpec((N_EXPERTS, tm), lambda i: (0, i)),
                  pl.BlockSpec((N_EXPERTS, 1), lambda i: (0, 0))],
        out_specs=[pl.BlockSpec((TOP_K, tm), lambda i: (0, i)),
                   pl.BlockSpec((TOP_K, tm), lambda i: (0, i)),
                   pl.BlockSpec((N_EXPERTS, 1), lambda i: (0, 0))],
        compiler_params=_cparams("arbitrary"),
        name="route_topk",
    )(logits_t, router_bias.reshape(N_EXPERTS, 1))


def _slot_kernel(idx_ref, start_ref, pos_ref, carry_ref):
    tm = ROUTE_TM

    @pl.when(pl.program_id(0) == 0)
    def _():
        carry_ref[...] = jnp.zeros_like(carry_ref)

    idx = idx_ref[...]
    eid = lax.broadcasted_iota(jnp.int32, (N_EXPERTS, tm), 0)
    hits = [eid == idx[k:k + 1, :] for k in range(TOP_K)]
    sel = jnp.zeros((N_EXPERTS, tm), F32)
    for h in hits:
        sel = sel + h.astype(F32)
    r = lax.broadcasted_iota(jnp.int32, (tm, tm), 0)
    c = lax.broadcasted_iota(jnp.int32, (tm, tm), 1)
    earlier = (r < c).astype(BF16)
    rank = jnp.dot(sel.astype(BF16), earlier, preferred_element_type=F32)
    slot = rank + carry_ref[...] + start_ref[...]
    for k, h in enumerate(hits):
        pos_ref[k:k + 1, :] = jnp.sum(jnp.where(h, slot, 0.0), axis=0,
                                      keepdims=True).astype(jnp.int32)
    carry_ref[...] += jnp.sum(sel, axis=1, keepdims=True)


def _assign_slots(idx_t, pad_start):
    T = idx_t.shape[1]
    tm = ROUTE_TM
    return pl.pallas_call(
        _slot_kernel,
        out_shape=jax.ShapeDtypeStruct((TOP_K, T), jnp.int32),
        grid=(T // tm,),
        in_specs=[pl.BlockSpec((TOP_K, tm), lambda i: (0, i)),
                  pl.BlockSpec((N_EXPERTS, 1), lambda i: (0, 0))],
        out_specs=pl.BlockSpec((TOP_K, tm), lambda i: (0, i)),
        scratch_shapes=[pltpu.VMEM((N_EXPERTS, 1), F32)],
        compiler_params=_cparams("arbitrary"),
        name="assign_slots",
    )(idx_t, pad_start)


def _dot_halves(lo, hi, w_ref):
    half = lo.shape[-1]
    return (jnp.dot(lo.astype(BF16), w_ref[:half, :].astype(BF16), preferred_element_type=F32)
            + jnp.dot(hi.astype(BF16), w_ref[half:, :].astype(BF16), preferred_element_type=F32))


def _expert_kernel(be_ref, nu_ref, xs_ref, w1_ref, w3_ref, w2_ref, ys_ref):
    @pl.when(pl.program_id(0) < nu_ref[0])
    def _():
        lo, hi = _load_packed(xs_ref)
        a = _dot_halves(lo, hi, w1_ref.at[0])
        b = _dot_halves(lo, hi, w3_ref.at[0])
        hb = (_silu(a) * b).astype(BF16)
        _store_packed(ys_ref, jnp.dot(hb, w2_ref[0].astype(BF16), preferred_element_type=F32))

    @pl.when(pl.program_id(0) >= nu_ref[0])
    def _():
        ys_ref[...] = jnp.zeros_like(ys_ref)


def _routed_experts(xs, blk_expert, n_used, w1, w3, w2):
    _, P, Dq = xs.shape
    blk = EXPERT_BLK
    D, F = w1.shape[-2:]
    grid_spec = pltpu.PrefetchScalarGridSpec(
        num_scalar_prefetch=2,
        grid=(P // blk,),
        in_specs=[pl.BlockSpec((2, blk, Dq), lambda b, be, nu: (0, jnp.minimum(b, nu[0] - 1), 0)),
                  pl.BlockSpec((1, D, F), lambda b, be, nu: (be[b], 0, 0)),
                  pl.BlockSpec((1, D, F), lambda b, be, nu: (be[b], 0, 0)),
                  pl.BlockSpec((1, F, D), lambda b, be, nu: (be[b], 0, 0))],
        out_specs=pl.BlockSpec((2, blk, Dq), lambda b, be, nu: (0, b, 0)),
    )
    return pl.pallas_call(
        _expert_kernel,
        out_shape=jax.ShapeDtypeStruct((2, P, Dq), U32),
        grid_spec=grid_spec,
        compiler_params=_cparams("arbitrary"),
        name="routed_experts",
    )(blk_expert, n_used, xs, w1, w3, w2)


def _sc_mesh():
    return plsc.VectorSubcoreMesh(core_axis_name="c", subcore_axis_name="s")


def _sc_dispatch(rows, slots_t, n_slots):
    T, W = rows.shape
    K = slots_t.shape[0]
    win = SC_WINDOW

    @pl.kernel(out_type=jax.ShapeDtypeStruct((n_slots, W), rows.dtype), mesh=_sc_mesh(),
               scratch_types=[])
    def dispatch(x_hbm, i_hbm, o_hbm):
        def body(x_vmem, i_vmem):
            for k in range(K):
                pltpu.sync_copy(x_vmem, o_hbm.at[i_vmem.at[k]])

        pltpu.emit_pipeline(
            body,
            grid=(T // win,),
            in_specs=[pl.BlockSpec((win, W), lambda i: (i, 0)),
                      pl.BlockSpec((K, win), lambda i: (0, i))],
            out_specs=[],
            core_axis_name=("c", "s"),
            dimension_semantics=(pltpu.PARALLEL,),
        )(x_hbm, i_hbm)

    return dispatch(rows, slots_t)


def _sc_gather(table, idx):
    W = table.shape[1]
    N = idx.shape[1]
    win = SC_WINDOW

    @pl.kernel(out_type=jax.ShapeDtypeStruct((N, W), table.dtype), mesh=_sc_mesh(),
               scratch_types=[])
    def gather(x_hbm, i_hbm, o_hbm):
        def body(i_vmem, o_vmem):
            pltpu.sync_copy(x_hbm.at[i_vmem.at[0]], o_vmem)

        pltpu.emit_pipeline(
            body,
            grid=(N // win,),
            in_specs=[pl.BlockSpec((1, win), lambda i: (0, i))],
            out_specs=[pl.BlockSpec((win, W), lambda i: (i, 0))],
            core_axis_name=("c", "s"),
            dimension_semantics=(pltpu.PARALLEL,),
        )(i_hbm, o_hbm)

    return gather(table, idx)


def _final_kernel(x1_ref, h2_ref, yg_ref, gate_ref, g2_ref, ws1_ref, ws3_ref, ws2_ref,
                  lng_ref, lnb_ref, o_ref):
    Dh = D_MODEL // 2
    gate = gate_ref[...]
    routed_lo = jnp.zeros((x1_ref.shape[0], Dh), F32)
    routed_hi = jnp.zeros((x1_ref.shape[0], Dh), F32)
    for k in range(TOP_K):
        lo, hi = _load_packed(yg_ref.at[k])
        routed_lo = routed_lo + lo * gate[:, k:k + 1]
        routed_hi = routed_hi + hi * gate[:, k:k + 1]
    lo, hi = _load_packed(h2_ref)
    a = _dot_halves(lo, hi, ws1_ref)
    b = _dot_halves(lo, hi, ws3_ref)
    shared = jnp.dot((_silu(a) * b).astype(BF16), ws2_ref[...], preferred_element_type=F32)
    y2 = jnp.concatenate([routed_lo, routed_hi], axis=-1) + shared
    o_ref[...] = _layer_norm(DN_ALPHA * x1_ref[...] + g2_ref[0] * y2, lng_ref[...], lnb_ref[...])


def _final(x1, h2, yg, gate, g2, ws1, ws3, ws2, ln2g, ln2b, seq):
    T, D = x1.shape
    tm = FINAL_TM
    tiles_per_seq = seq // tm
    F = ws1.shape[-1]
    tok = pl.BlockSpec((tm, D), lambda i: (i, 0))
    row = pl.BlockSpec((1, D), lambda i: (0, 0))
    return pl.pallas_call(
        _final_kernel,
        out_shape=jax.ShapeDtypeStruct((T, D), F32),
        grid=(T // tm,),
        in_specs=[tok, pl.BlockSpec((2, tm, D // 4), lambda i: (0, i, 0)),
                  pl.BlockSpec((TOP_K, 2, tm, D // 4), lambda i: (0, 0, i, 0)),
                  pl.BlockSpec((tm, TOP_K), lambda i: (i, 0)),
                  pl.BlockSpec((1, 1, D), lambda i: (i // tiles_per_seq, 0, 0)),
                  pl.BlockSpec((D, F), lambda i: (0, 0)),
                  pl.BlockSpec((D, F), lambda i: (0, 0)),
                  pl.BlockSpec((F, D), lambda i: (0, 0)),
                  row, row],
        out_specs=tok,
        compiler_params=_cparams("arbitrary"),
        name="moe_combine_final",
    )(x1, h2, yg, gate, g2, ws1, ws3, ws2, ln2g, ln2b)


def kernel(x, c, ada_w, ada_b, w_in, b_in, lambda_q1, lambda_k1, lambda_q2, lambda_k2, subln_g, w_attn_o, conv_w, conv_b, conv_ln_g, conv_ln_b, conv_pw_w, conv_pw_b, w_out, ln1_g, ln1_b, router_w, router_bias, w1, w3, w2, ws1, ws3, ws2, ln2_g, ln2_b):
    B, S, D = x.shape
    T = B * S
    l = 0
    x2 = x.reshape(T, D)
    row = lambda a: a.reshape(1, -1)

    mod = _ada_modulation(c, ada_w[l], ada_b[l])
    sh1, sc1, g1, sh2, sc2, g2 = [m.reshape(B, 1, D) for m in jnp.split(mod, 6, axis=-1)]

    q, k, v, u, ga, gb = _in_projection(x2, sc1, sh1, w_in[l].astype(BF16), row(b_in[l]), S)
    o = _diff_attention(q, k, v.T, row(lambda_q1[l]), row(lambda_k1[l]), row(lambda_q2[l]),
                        row(lambda_k2[l]), row(subln_g[l]), B, S)

    wr = router_w[l].T
    wr_hi = wr.astype(BF16)
    wr_lo = (wr - wr_hi.astype(F32)).astype(BF16)
    x1, h2, logits = _mixer_epilogue(
        x2, o, u, ga, gb, g1, sc2, sh2, w_attn_o[l].astype(BF16), conv_w[l], row(conv_b[l]),
        row(conv_ln_g[l]), row(conv_ln_b[l]), conv_pw_w[l].astype(BF16), row(conv_pw_b[l]),
        w_out[l].astype(BF16), row(ln1_g[l]), row(ln1_b[l]), wr_hi, wr_lo, S)

    idx, gate, counts = _route(logits, router_bias[l])

    blk = EXPERT_BLK
    cnt = counts.reshape(N_EXPERTS).astype(jnp.int32)
    padded = (cnt + blk - 1) // blk * blk
    pad_end = jnp.cumsum(padded)
    pad_start = pad_end - padded
    P = T * TOP_K + N_EXPERTS * blk
    n_blocks = P // blk
    blk_first_row = jnp.arange(n_blocks, dtype=jnp.int32) * blk
    blk_expert = jnp.minimum(
        jnp.sum((pad_end[None, :] <= blk_first_row[:, None]).astype(jnp.int32), axis=1),
        N_EXPERTS - 1)
    n_used = (pad_end[-1:] // blk).astype(jnp.int32)

    pos = _assign_slots(idx, pad_start.astype(F32).reshape(N_EXPERTS, 1))

    Dq = D // 4
    half_base = (jnp.arange(2, dtype=jnp.int32) * P)[None, :, None]
    slots = pos[:, None, :] + half_base
    xs = _sc_dispatch(h2.reshape(2 * T, Dq), slots.reshape(TOP_K, 2 * T), 2 * P)
    ys = _routed_experts(xs.reshape(2, P, Dq), blk_expert, n_used, w1[l], w3[l], w2[l])
    yg = _sc_gather(ys.reshape(2 * P, Dq), slots.reshape(1, TOP_K * 2 * T))
    yg = yg.reshape(TOP_K, 2, T, Dq)

    out = _final(x1, h2, yg, gate.T, g2, ws1[l].astype(BF16), ws3[l].astype(BF16),
                 ws2[l].astype(BF16), row(ln2_g[l]), row(ln2_b[l]), S)
    return out.reshape(B, S, D)
```

```python
import math
from functools import partial

import jax
import jax.numpy as jnp
from jax import lax
from jax.experimental import pallas as pl
from jax.experimental.pallas import tpu as pltpu
from jax.experimental.pallas import tpu_sc as plsc

D_MODEL = 1024
N_HEADS = 8
HEAD_DIM = 64
V_DIM = 128
CONV_K = 31
N_EXPERTS = 256
TOP_K = 8
N_GROUPS = 8
GROUP_SIZE = N_EXPERTS // N_GROUPS
TOPK_GROUPS = 4
D_EXPERT = 256
ROUTED_SCALE = 2.5
LN_EPS = 1e-5
RMS_EPS = 1e-5
DEPTH = 1
DN_ALPHA = (2.0 * DEPTH) ** 0.25
LAMBDA_INIT = 0.8 - 0.6 * math.exp(-0.3 * 0)
Q_SCALE = HEAD_DIM ** -0.5 * math.log2(math.e)

LANES = 128
SUBLANES = 8
VMEM_LIMIT_BYTES = 56 * 1024 * 1024

PROJ_TM = 512
ATTN_TQ = 512
ATTN_QG = 256
ONES_ROWS = 16
MIX_TM = 256
CONV_HALO = 32
CONV_ROWS = 64
ROUTE_TM = 512
EXPERT_BLK = 512
FINAL_TM = 256
SC_WINDOW = 128

F32 = jnp.float32
BF16 = jnp.bfloat16
U32 = jnp.uint32
NEG_INF = float("-inf")


def _cparams(*sem):
    return pltpu.CompilerParams(dimension_semantics=sem, vmem_limit_bytes=VMEM_LIMIT_BYTES)


def _sigmoid(x):
    return 1.0 / (1.0 + jnp.exp(-x))


def _silu(x):
    return x * _sigmoid(x)


def _pack_halves(x):
    half = x.shape[-1] // 2
    bits = lambda v: lax.bitcast_convert_type(v.astype(BF16).astype(F32), U32)
    return (bits(x[:, :half]) >> 16) | (bits(x[:, half:]) & jnp.uint32(0xFFFF0000))


def _unpack_halves(w):
    lo = lax.bitcast_convert_type(w << 16, F32)
    hi = lax.bitcast_convert_type(w & jnp.uint32(0xFFFF0000), F32)
    return lo, hi


def _store_packed(ref, x):
    w = _pack_halves(x)
    q = w.shape[-1] // 2
    ref[0] = w[:, :q]
    ref[1] = w[:, q:]


def _load_packed(ref):
    return _unpack_halves(jnp.concatenate([ref[0], ref[1]], axis=-1))


def _layer_norm(r, g, b):
    mu = jnp.mean(r, axis=-1, keepdims=True)
    d = r - mu
    var = jnp.mean(d * d, axis=-1, keepdims=True)
    return d * lax.rsqrt(var + LN_EPS) * g + b


def _ada_kernel(c_ref, w_ref, b_ref, o_ref):
    a = _silu(c_ref[...])
    o_ref[...] = jnp.dot(a, w_ref[...], preferred_element_type=F32,
                         precision=lax.Precision.HIGHEST) + b_ref[...]


def _ada_modulation(c, w, b):
    B, D = c.shape
    n = w.shape[1] // D
    return pl.pallas_call(
        _ada_kernel,
        out_shape=jax.ShapeDtypeStruct((B, n * D), F32),
        grid=(n,),
        in_specs=[pl.BlockSpec((B, D), lambda j: (0, 0)),
                  pl.BlockSpec((D, D), lambda j: (0, j)),
                  pl.BlockSpec((1, D), lambda j: (0, j))],
        out_specs=pl.BlockSpec((B, D), lambda j: (0, j)),
        compiler_params=_cparams("arbitrary"),
        name="ada_modulation",
    )(c, w, b.reshape(1, n * D))


def _inproj_kernel(x_ref, sc_ref, sh_ref, w_ref, b_ref,
                   q_ref, k_ref, v_ref, u_ref, ga_ref, gb_ref):
    D = D_MODEL
    h = (x_ref[...] * (1.0 + sc_ref[0]) + sh_ref[0]).astype(BF16)

    def proj(j):
        return (jnp.dot(h, w_ref[:, j * D:(j + 1) * D], preferred_element_type=F32)
                + b_ref[:, j * D:(j + 1) * D])

    q_ref[...] = (proj(0) * Q_SCALE).astype(BF16)
    k_ref[...] = proj(1).astype(BF16)
    v_ref[...] = proj(2).astype(BF16)
    u_ref[...] = (proj(3) * _sigmoid(proj(4))).astype(BF16)
    ga_ref[...] = _sigmoid(proj(5)).astype(BF16)
    gb_ref[...] = _sigmoid(proj(6)).astype(BF16)


def _in_projection(x2, sc1, sh1, w_in, b_in, seq):
    T, D = x2.shape
    tm = PROJ_TM
    tiles_per_seq = seq // tm
    ncols = w_in.shape[1]
    tok = pl.BlockSpec((tm, D), lambda i: (i, 0))
    mod = pl.BlockSpec((1, 1, D), lambda i: (i // tiles_per_seq, 0, 0))
    out = jax.ShapeDtypeStruct((T, D), BF16)
    return pl.pallas_call(
        _inproj_kernel,
        out_shape=[out] * 6,
        grid=(T // tm,),
        in_specs=[tok, mod, mod,
                  pl.BlockSpec((D, ncols), lambda i: (0, 0)),
                  pl.BlockSpec((1, ncols), lambda i: (0, 0))],
        out_specs=[tok] * 6,
        compiler_params=_cparams("arbitrary"),
        name="in_projection",
    )(x2, sc1, sh1, w_in, b_in)


def _attn_kernel(lq1_ref, lk1_ref, lq2_ref, lk2_ref, g_ref, qt_ref, k_ref, vt_ref, o_ref,
                 q2_ref, *scratch):
    tq = ATTN_TQ
    qg = ATTN_QG
    ng = 2 * tq // qg
    groups = [scratch[i * ng:(i + 1) * ng] for i in range(7)]
    s_slots = groups[0:2]
    p_refs, m_refs, l_refs, a_refs, acc_refs = groups[2:]
    qi = pl.program_id(2)
    lam = (jnp.exp(jnp.sum(lq1_ref[...] * lk1_ref[...], axis=-1, keepdims=True))
           - jnp.exp(jnp.sum(lq2_ref[...] * lk2_ref[...], axis=-1, keepdims=True))
           + LAMBDA_INIT)

    qt = qt_ref[...]
    feat = lax.broadcasted_iota(jnp.int32, qt.shape, 0)
    zero = jnp.zeros_like(qt)
    q2_ref[:, :tq] = jnp.where(feat < HEAD_DIM, qt, zero)
    q2_ref[:, tq:] = jnp.where(feat >= HEAD_DIM, qt, zero)

    def scores(j, slot, g):
        start = pl.multiple_of(j * tq, tq)
        s_slots[slot][g][...] = jnp.dot(
            k_ref[pl.ds(start, tq), :], q2_ref[:, g * qg:(g + 1) * qg],
            preferred_element_type=F32)

    def update(j, slot, g, masked):
        start = pl.multiple_of(j * tq, tq)
        s_ref = s_slots[slot][g]
        for c in range(qg // LANES):
            cs = slice(c * LANES, (c + 1) * LANES)

            def load():
                s = s_ref[:, cs]
                if masked:
                    key = lax.broadcasted_iota(jnp.int32, s.shape, 0)
                    qry = (lax.broadcasted_iota(jnp.int32, s.shape, 1)
                           + (g * qg + c * LANES) % tq)
                    s = jnp.where(key <= qry, s, NEG_INF)
                return s

            m_prev = m_refs[g][:, cs]
            m_new = jnp.maximum(m_prev, jnp.max(load(), axis=0, keepdims=True))
            p_refs[g][:, cs] = jnp.exp2((load() - m_new).astype(BF16))
            m_refs[g][:, cs] = m_new
            a_refs[g][:, cs] = jnp.exp2(m_prev - m_new)
        pv = jnp.dot(vt_ref[:, pl.ds(start, tq)], p_refs[g][...], preferred_element_type=F32)
        alpha = a_refs[g][...]
        acc_refs[g][...] = alpha * acc_refs[g][...] + pv[:V_DIM, :]
        l_refs[g][...] = alpha * l_refs[g][...] + pv[V_DIM:V_DIM + 1, :]

    def step(j, slot, masked=False, prefetch=True):
        for g in range(ng):
            if prefetch:
                scores(j + 1, 1 - slot, g)
            update(j, slot, g, masked)

    for g in range(ng):
        m_refs[g][...] = jnp.full(m_refs[g].shape, NEG_INF, F32)
        l_refs[g][...] = jnp.zeros(l_refs[g].shape, F32)
        acc_refs[g][...] = jnp.zeros(acc_refs[g].shape, F32)
        scores(0, 0, g)

    def pair(jj, carry):
        step(2 * jj, 0)
        step(2 * jj + 1, 1)
        return carry

    lax.fori_loop(0, qi // 2, pair, 0)

    @pl.when(qi % 2 == 1)
    def _():
        step(qi - 1, 0)
        step(qi, 1, masked=True, prefetch=False)

    @pl.when(qi % 2 == 0)
    def _():
        step(qi, 0, masked=True, prefetch=False)

    o = (jnp.concatenate([r[...] for r in acc_refs], axis=-1)
         / jnp.concatenate([r[...] for r in l_refs], axis=-1))
    o = o[:, :tq] - lam * o[:, tq:]
    o = o * lax.rsqrt(jnp.mean(o * o, axis=0, keepdims=True) + RMS_EPS) * g_ref[...]
    o_ref[...] = (o * (1.0 - LAMBDA_INIT)).T.astype(BF16)


def _diff_attention(qt, k, vt, lq1, lk1, lq2, lk2, subln_g, batch, seq):
    T, D = k.shape
    tq = ATTN_TQ
    nq = seq // tq
    lam_spec = pl.BlockSpec((1, HEAD_DIM), lambda b, h, i: (0, 0))
    q_spec = pl.BlockSpec((tq, V_DIM), lambda b, h, i: (b * nq + i, h))
    kv_spec = pl.BlockSpec((seq, V_DIM), lambda b, h, i: (b, h))
    qg = ATTN_QG
    per_group = lambda shape, dtype: [pltpu.VMEM(shape, dtype) for _ in range(2 * tq // qg)]
    return pl.pallas_call(
        _attn_kernel,
        out_shape=jax.ShapeDtypeStruct((T, D), BF16),
        grid=(batch, N_HEADS, nq),
        in_specs=[lam_spec, lam_spec, lam_spec, lam_spec,
                  pl.BlockSpec((V_DIM, 1), lambda b, h, i: (0, 0)),
                  pl.BlockSpec((V_DIM, tq), lambda b, h, i: (h, b * nq + i)), kv_spec,
                  pl.BlockSpec((V_DIM + ONES_ROWS, seq), lambda b, h, i: (h, b))],
        out_specs=q_spec,
        scratch_shapes=(
            [pltpu.VMEM((V_DIM, 2 * tq), BF16)]
            + per_group((tq, qg), F32)
            + per_group((tq, qg), F32)
            + per_group((tq, qg), BF16)
            + per_group((1, qg), F32)
            + per_group((1, qg), F32)
            + per_group((1, qg), F32)
            + per_group((V_DIM, qg), F32)),
        compiler_params=_cparams("arbitrary", "arbitrary", "arbitrary"),
        name="diff_attention",
    )(lq1, lk1, lq2, lk2, subln_g.reshape(V_DIM, 1), qt, k, vt)


def _mixer_kernel(x_ref, o_ref, u_ref, up_ref, ga_ref, gb_ref, g1_ref, sc2_ref, sh2_ref,
                  wao_ref, cw_ref, cb_ref, clg_ref, clb_ref, wpw_ref, bpw_ref, wout_ref,
                  ln1g_ref, ln1b_ref, wrh_ref, wrl_ref,
                  x1_ref, h2_ref, lg_ref, ubuf_ref, ush_ref, conv_ref, *, tiles_per_seq):
    tm = MIX_TM
    first = (pl.program_id(0) % tiles_per_seq) == 0

    prev = up_ref[...].astype(F32)
    ubuf_ref[0:CONV_HALO, :] = jnp.where(first, jnp.zeros_like(prev), prev)
    ubuf_ref[CONV_HALO:, :] = u_ref[...].astype(F32)
    base = CONV_HALO - (CONV_K - 1)
    n_shift = tm + CONV_HALO - SUBLANES
    for r in range(1, SUBLANES):
        ush_ref[r - 1] = ubuf_ref[r:r + n_shift, :]
    for c in range(D_MODEL // LANES):
        cl = slice(c * LANES, (c + 1) * LANES)
        for r0 in range(0, tm, CONV_ROWS):
            acc = jnp.zeros((CONV_ROWS, LANES), F32)
            for j in range(CONV_K):
                a, r = divmod(base + j, SUBLANES)
                src = ubuf_ref if r == 0 else ush_ref.at[r - 1]
                lo = SUBLANES * a + r0
                acc = acc + cw_ref[j:j + 1, cl] * src[lo:lo + CONV_ROWS, cl]
            conv_ref[r0:r0 + CONV_ROWS, cl] = acc
    yc = _silu(_layer_norm(conv_ref[...] + cb_ref[...], clg_ref[...], clb_ref[...]))
    y_conv = jnp.dot(yc.astype(BF16), wpw_ref[...], preferred_element_type=F32) + bpw_ref[...]

    y_attn = jnp.dot(o_ref[...], wao_ref[...], preferred_element_type=F32)
    mix = ga_ref[...].astype(F32) * y_attn + gb_ref[...].astype(F32) * y_conv
    y = jnp.dot(mix.astype(BF16), wout_ref[...], preferred_element_type=F32)
    x1 = _layer_norm(DN_ALPHA * x_ref[...] + g1_ref[0] * y, ln1g_ref[...], ln1b_ref[...])
    x1_ref[...] = x1

    h2 = x1 * (1.0 + sc2_ref[0]) + sh2_ref[0]
    h2_hi = h2.astype(BF16)
    _store_packed(h2_ref, h2)
    h2_lo = (h2 - h2_hi.astype(F32)).astype(BF16)
    nt = lambda w, a: lax.dot_general(w, a, (((1,), (1,)), ((), ())), preferred_element_type=F32)
    lg_ref[...] = (nt(wrh_ref[...], h2_hi)
                   + (nt(wrl_ref[...], h2_hi) + nt(wrh_ref[...], h2_lo)))


def _mixer_epilogue(x2, o, u, ga, gb, g1, sc2, sh2, wao, cw, cb, clg, clb, wpw, bpw, wout,
                    ln1g, ln1b, wr_hi, wr_lo, seq):
    T, D = x2.shape
    tm = MIX_TM
    tiles_per_seq = seq // tm
    halo_per_tile = tm // CONV_HALO
    tok = pl.BlockSpec((tm, D), lambda i: (i, 0))
    prev = pl.BlockSpec((CONV_HALO, D), lambda i: (jnp.maximum(i * halo_per_tile - 1, 0), 0))
    mod = pl.BlockSpec((1, 1, D), lambda i: (i // tiles_per_seq, 0, 0))
    row = pl.BlockSpec((1, D), lambda i: (0, 0))
    mat = pl.BlockSpec((D, D), lambda i: (0, 0))
    rw = pl.BlockSpec((N_EXPERTS, D), lambda i: (0, 0))
    return pl.pallas_call(
        partial(_mixer_kernel, tiles_per_seq=tiles_per_seq),
        out_shape=[jax.ShapeDtypeStruct((T, D), F32),
                   jax.ShapeDtypeStruct((2, T, D // 4), U32),
                   jax.ShapeDtypeStruct((N_EXPERTS, T), F32)],
        grid=(T // tm,),
        in_specs=[tok, tok, tok, prev, tok, tok, mod, mod, mod,
                  mat, pl.BlockSpec((CONV_K, D), lambda i: (0, 0)), row, row, row, mat, row, mat,
                  row, row, rw, rw],
        out_specs=[tok, pl.BlockSpec((2, tm, D // 4), lambda i: (0, i, 0)),
                   pl.BlockSpec((N_EXPERTS, tm), lambda i: (0, i))],
        scratch_shapes=[pltpu.VMEM((CONV_HALO + tm, D), F32),
                        pltpu.VMEM((SUBLANES - 1, tm + CONV_HALO - SUBLANES, D), F32),
                        pltpu.VMEM((tm, D), F32)],
        compiler_params=_cparams("arbitrary"),
        name="mixer_epilogue",
    )(x2, o, u, u, ga, gb, g1, sc2, sh2, wao, cw, cb, clg, clb, wpw, bpw, wout,
      ln1g, ln1b, wr_hi, wr_lo)


def _first_argmax(v, eid):
    m = jnp.max(v, axis=0, keepdims=True)
    i = jnp.min(jnp.where(v == m, eid, float(N_EXPERTS)), axis=0, keepdims=True)
    return m, i


def _route_kernel(lg_ref, rb_ref, idx_ref, gate_ref, cnt_ref):
    tm = ROUTE_TM
    scores = _sigmoid(lg_ref[...])
    choice = scores + rb_ref[...]
    eid = lax.broadcasted_iota(jnp.int32, (N_EXPERTS, tm), 0).astype(F32)
    rows = lambda a, g: a[g * GROUP_SIZE:(g + 1) * GROUP_SIZE, :]

    gs = []
    for g in range(N_GROUPS):
        cg = rows(choice, g)
        eg = (lax.broadcasted_iota(jnp.int32, cg.shape, 0) + g * GROUP_SIZE).astype(F32)
        m1, i1 = _first_argmax(cg, eg)
        m2 = jnp.max(jnp.where(eg == i1, NEG_INF, cg), axis=0, keepdims=True)
        gs.append(jnp.broadcast_to(m1 + m2, (SUBLANES, tm)))

    kept = []
    for g in range(N_GROUPS):
        ahead = jnp.zeros((SUBLANES, tm), F32)
        for o in range(N_GROUPS):
            if o == g:
                continue
            beats = (gs[o] >= gs[g]) if o < g else (gs[o] > gs[g])
            ahead = ahead + jnp.where(beats, 1.0, 0.0)
        cap = jnp.where(ahead < TOPK_GROUPS, float("inf"), NEG_INF)
        for r in range(g * GROUP_SIZE, (g + 1) * GROUP_SIZE, SUBLANES):
            kept.append(jnp.minimum(choice[r:r + SUBLANES, :], cap))
    masked = jnp.concatenate(kept, axis=0)

    sel = jnp.zeros((N_EXPERTS, tm), F32)
    total = jnp.zeros((1, tm), F32)
    picked = []
    for k in range(TOP_K):
        _, i = _first_argmax(masked, eid)
        hit = eid == i
        sc = jnp.sum(jnp.where(hit, scores, 0.0), axis=0, keepdims=True)
        masked = jnp.where(hit, NEG_INF, masked)
        sel = sel + hit.astype(F32)
        total = total + sc
        idx_ref[k:k + 1, :] = i.astype(jnp.int32)
        picked.append(sc)
    for k in range(TOP_K):
        gate_ref[k:k + 1, :] = picked[k] / total * ROUTED_SCALE

    @pl.when(pl.program_id(0) == 0)
    def _():
        cnt_ref[...] = jnp.zeros_like(cnt_ref)
    cnt_ref[...] += jnp.sum(sel, axis=1, keepdims=True)


def _route(logits_t, router_bias):
    T = logits_t.shape[1]
    tm = ROUTE_TM
    return pl.pallas_call(
        _route_kernel,
        out_shape=[jax.ShapeDtypeStruct((TOP_K, T), jnp.int32),
                   jax.ShapeDtypeStruct((TOP_K, T), F32),
                   jax.ShapeDtypeStruct((N_EXPERTS, 1), F32)],
        grid=(T // tm,),
        in_specs=[pl.BlockSpec((N_EXPERTS, tm), lambda i: (0, i)),
                  pl.BlockSpec((N_EXPERTS, 1), lambda i: (0, 0))],
        out_specs=[pl.BlockSpec((TOP_K, tm), lambda i: (0, i)),
                   pl.BlockSpec((TOP_K, tm), lambda i: (0, i)),
                   pl.BlockSpec((N_EXPERTS, 1), lambda i: (0, 0))],
        compiler_params=_cparams("arbitrary"),
        name="route_topk",
    )(logits_t, router_bias.reshape(N_EXPERTS, 1))


def _slot_kernel(idx_ref, start_ref, pos_ref, carry_ref):
    tm = ROUTE_TM

    @pl.when(pl.program_id(0) == 0)
    def _():
        carry_ref[...] = jnp.zeros_like(carry_ref)

    idx = idx_ref[...]
    eid = lax.broadcasted_iota(jnp.int32, (N_EXPERTS, tm), 0)
    hits = [eid == idx[k:k + 1, :] for k in range(TOP_K)]
    sel = jnp.zeros((N_EXPERTS, tm), F32)
    for h in hits:
        sel = sel + h.astype(F32)
    r = lax.broadcasted_iota(jnp.int32, (tm, tm), 0)
    c = lax.broadcasted_iota(jnp.int32, (tm, tm), 1)
    earlier = (r < c).astype(BF16)
    rank = jnp.dot(sel.astype(BF16), earlier, preferred_element_type=F32)
    slot = rank + carry_ref[...] + start_ref[...]
    for k, h in enumerate(hits):
        pos_ref[k:k + 1, :] = jnp.sum(jnp.where(h, slot, 0.0), axis=0,
                                      keepdims=True).astype(jnp.int32)
    carry_ref[...] += jnp.sum(sel, axis=1, keepdims=True)


def _assign_slots(idx_t, pad_start):
    T = idx_t.shape[1]
    tm = ROUTE_TM
    return pl.pallas_call(
        _slot_kernel,
        out_shape=jax.ShapeDtypeStruct((TOP_K, T), jnp.int32),
        grid=(T // tm,),
        in_specs=[pl.BlockSpec((TOP_K, tm), lambda i: (0, i)),
                  pl.BlockSpec((N_EXPERTS, 1), lambda i: (0, 0))],
        out_specs=pl.BlockSpec((TOP_K, tm), lambda i: (0, i)),
        scratch_shapes=[pltpu.VMEM((N_EXPERTS, 1), F32)],
        compiler_params=_cparams("arbitrary"),
        name="assign_slots",
    )(idx_t, pad_start)


def _dot_halves(lo, hi, w_ref):
    half = lo.shape[-1]
    return (jnp.dot(lo.astype(BF16), w_ref[:half, :].astype(BF16), preferred_element_type=F32)
            + jnp.dot(hi.astype(BF16), w_ref[half:, :].astype(BF16), preferred_element_type=F32))


def _expert_kernel(be_ref, nu_ref, xs_ref, w1_ref, w3_ref, w2_ref, ys_ref):
    @pl.when(pl.program_id(0) < nu_ref[0])
    def _():
        lo, hi = _load_packed(xs_ref)
        half = lo.shape[-1]
        mm = lambda w: (jnp.dot(lo, w[0, :half, :], preferred_element_type=F32)
                        + jnp.dot(hi, w[0, half:, :], preferred_element_type=F32))
        hb = _silu(mm(w1_ref)) * mm(w3_ref)
        _store_packed(ys_ref, jnp.dot(hb, w2_ref[0], preferred_element_type=F32))

    @pl.when(pl.program_id(0) >= nu_ref[0])
    def _():
        ys_ref[...] = jnp.zeros_like(ys_ref)


def _routed_experts(xs, blk_expert, n_used, w1, w3, w2):
    _, P, Dq = xs.shape
    blk = EXPERT_BLK
    D, F = w1.shape[-2:]
    grid_spec = pltpu.PrefetchScalarGridSpec(
        num_scalar_prefetch=2,
        grid=(P // blk,),
        in_specs=[pl.BlockSpec((2, blk, Dq), lambda b, be, nu: (0, jnp.minimum(b, nu[0] - 1), 0)),
                  pl.BlockSpec((1, D, F), lambda b, be, nu: (be[b], 0, 0)),
                  pl.BlockSpec((1, D, F), lambda b, be, nu: (be[b], 0, 0)),
                  pl.BlockSpec((1, F, D), lambda b, be, nu: (be[b], 0, 0))],
        out_specs=pl.BlockSpec((2, blk, Dq), lambda b, be, nu: (0, b, 0)),
    )
    return pl.pallas_call(
        _expert_kernel,
        out_shape=jax.ShapeDtypeStruct((2, P, Dq), U32),
        grid_spec=grid_spec,
        compiler_params=_cparams("arbitrary"),
        name="routed_experts",
    )(blk_expert, n_used, xs, w1, w3, w2)


def _sc_mesh():
    return plsc.VectorSubcoreMesh(core_axis_name="c", subcore_axis_name="s")


def _sc_dispatch(rows, slots_t, n_slots):
    T, W = rows.shape
    K = slots_t.shape[0]
    win = SC_WINDOW

    @pl.kernel(out_type=jax.ShapeDtypeStruct((n_slots, W), rows.dtype), mesh=_sc_mesh(),
               scratch_types=[])
    def dispatch(x_hbm, i_hbm, o_hbm):
        def body(x_vmem, i_vmem):
            for k in range(K):
                pltpu.sync_copy(x_vmem, o_hbm.at[i_vmem.at[k]])

        pltpu.emit_pipeline(
            body,
            grid=(T // win,),
            in_specs=[pl.BlockSpec((win, W), lambda i: (i, 0)),
                      pl.BlockSpec((K, win), lambda i: (0, i))],
            out_specs=[],
            core_axis_name=("c", "s"),
            dimension_semantics=(pltpu.PARALLEL,),
        )(x_hbm, i_hbm)

    return dispatch(rows, slots_t)


def _sc_gather(table, idx):
    W = table.shape[1]
    N = idx.shape[1]
    win = SC_WINDOW

    @pl.kernel(out_type=jax.ShapeDtypeStruct((N, W), table.dtype), mesh=_sc_mesh(),
               scratch_types=[])
    def gather(x_hbm, i_hbm, o_hbm):
        def body(i_vmem, o_vmem):
            pltpu.sync_copy(x_hbm.at[i_vmem.at[0]], o_vmem)

        pltpu.emit_pipeline(
            body,
            grid=(N // win,),
            in_specs=[pl.BlockSpec((1, win), lambda i: (0, i))],
            out_specs=[pl.BlockSpec((win, W), lambda i: (i, 0))],
            core_axis_name=("c", "s"),
            dimension_semantics=(pltpu.PARALLEL,),
        )(i_hbm, o_hbm)

    return gather(table, idx)


def _final_kernel(x1_ref, h2_ref, yg_ref, gate_ref, g2_ref, ws1_ref, ws3_ref, ws2_ref,
                  lng_ref, lnb_ref, o_ref):
    Dh = D_MODEL // 2
    gate = gate_ref[...]
    routed_lo = jnp.zeros((x1_ref.shape[0], Dh), F32)
    routed_hi = jnp.zeros((x1_ref.shape[0], Dh), F32)
    for k in range(TOP_K):
        lo, hi = _load_packed(yg_ref.at[k])
        routed_lo = routed_lo + lo * gate[:, k:k + 1]
        routed_hi = routed_hi + hi * gate[:, k:k + 1]
    lo, hi = _load_packed(h2_ref)
    a = _dot_halves(lo, hi, ws1_ref)
    b = _dot_halves(lo, hi, ws3_ref)
    shared = jnp.dot((_silu(a) * b).astype(BF16), ws2_ref[...], preferred_element_type=F32)
    y2 = jnp.concatenate([routed_lo, routed_hi], axis=-1) + shared
    o_ref[...] = _layer_norm(DN_ALPHA * x1_ref[...] + g2_ref[0] * y2, lng_ref[...], lnb_ref[...])


def _final(x1, h2, yg, gate, g2, ws1, ws3, ws2, ln2g, ln2b, seq):
    T, D = x1.shape
    tm = FINAL_TM
    tiles_per_seq = seq // tm
    F = ws1.shape[-1]
    tok = pl.BlockSpec((tm, D), lambda i: (i, 0))
    row = pl.BlockSpec((1, D), lambda i: (0, 0))
    return pl.pallas_call(
        _final_kernel,
        out_shape=jax.ShapeDtypeStruct((T, D), F32),
        grid=(T // tm,),
        in_specs=[tok, pl.BlockSpec((2, tm, D // 4), lambda i: (0, i, 0)),
                  pl.BlockSpec((TOP_K, 2, tm, D // 4), lambda i: (0, 0, i, 0)),
                  pl.BlockSpec((tm, TOP_K), lambda i: (i, 0)),
                  pl.BlockSpec((1, 1, D), lambda i: (i // tiles_per_seq, 0, 0)),
                  pl.BlockSpec((D, F), lambda i: (0, 0)),
                  pl.BlockSpec((D, F), lambda i: (0, 0)),
                  pl.BlockSpec((F, D), lambda i: (0, 0)),
                  row, row],
        out_specs=tok,
        compiler_params=_cparams("arbitrary"),
        name="moe_combine_final",
    )(x1, h2, yg, gate, g2, ws1, ws3, ws2, ln2g, ln2b)


def kernel(x, c, ada_w, ada_b, w_in, b_in, lambda_q1, lambda_k1, lambda_q2, lambda_k2, subln_g, w_attn_o, conv_w, conv_b, conv_ln_g, conv_ln_b, conv_pw_w, conv_pw_b, w_out, ln1_g, ln1_b, router_w, router_bias, w1, w3, w2, ws1, ws3, ws2, ln2_g, ln2_b):
    B, S, D = x.shape
    T = B * S
    l = 0
    x2 = x.reshape(T, D)
    row = lambda a: a.reshape(1, -1)

    mod = _ada_modulation(c, ada_w[l], ada_b[l])
    sh1, sc1, g1, sh2, sc2, g2 = [m.reshape(B, 1, D) for m in jnp.split(mod, 6, axis=-1)]

    q, k, v, u, ga, gb = _in_projection(x2, sc1, sh1, w_in[l].astype(BF16), row(b_in[l]), S)
    vt = jnp.concatenate([v.T.reshape(N_HEADS, V_DIM, T),
                          jnp.ones((N_HEADS, ONES_ROWS, T), BF16)], axis=1)
    vt = vt.reshape(N_HEADS * (V_DIM + ONES_ROWS), T)
    o = _diff_attention(q.T, k, vt, row(lambda_q1[l]), row(lambda_k1[l]), row(lambda_q2[l]),
                        row(lambda_k2[l]), row(subln_g[l]), B, S)

    wr = router_w[l].T
    wr_hi = wr.astype(BF16)
    wr_lo = (wr - wr_hi.astype(F32)).astype(BF16)
    x1, h2, logits = _mixer_epilogue(
        x2, o, u, ga, gb, g1, sc2, sh2, w_attn_o[l].astype(BF16), conv_w[l], row(conv_b[l]),
        row(conv_ln_g[l]), row(conv_ln_b[l]), conv_pw_w[l].astype(BF16), row(conv_pw_b[l]),
        w_out[l].astype(BF16), row(ln1_g[l]), row(ln1_b[l]), wr_hi, wr_lo, S)

    idx, gate, counts = _route(logits, router_bias[l])

    blk = EXPERT_BLK
    cnt = counts.reshape(N_EXPERTS).astype(jnp.int32)
    padded = (cnt + blk - 1) // blk * blk
    pad_end = jnp.cumsum(padded)
    pad_start = pad_end - padded
    P = T * TOP_K + N_EXPERTS * blk
    n_blocks = P // blk
    blk_first_row = jnp.arange(n_blocks, dtype=jnp.int32) * blk
    blk_expert = jnp.minimum(
        jnp.sum((pad_end[None, :] <= blk_first_row[:, None]).astype(jnp.int32), axis=1),
        N_EXPERTS - 1)
    n_used = (pad_end[-1:] // blk).astype(jnp.int32)

    pos = _assign_slots(idx, pad_start.astype(F32).reshape(N_EXPERTS, 1))

    Dq = D // 4
    half_base = (jnp.arange(2, dtype=jnp.int32) * P)[None, :, None]
    slots = pos[:, None, :] + half_base
    xs = _sc_dispatch(h2.reshape(2 * T, Dq), slots.reshape(TOP_K, 2 * T), 2 * P)
    ys = _routed_experts(xs.reshape(2, P, Dq), blk_expert, n_used, w1[l], w3[l], w2[l])
    yg = _sc_gather(ys.reshape(2 * P, Dq), slots.reshape(1, TOP_K * 2 * T))
    yg = yg.reshape(TOP_K, 2, T, Dq)

    out = _final(x1, h2, yg, gate.T, g2, ws1[l].astype(BF16), ws3[l].astype(BF16),
                 ws2[l].astype(BF16), row(ln2_g[l]), row(ln2_b[l]), S)
    return out.reshape(B, S, D)
```

```python
import math
from functools import partial

import jax
import jax.numpy as jnp
from jax import lax
from jax.experimental import pallas as pl
from jax.experimental.pallas import tpu as pltpu
from jax.experimental.pallas import tpu_sc as plsc

D_MODEL = 1024
N_HEADS = 8
HEAD_DIM = 64
V_DIM = 128
CONV_K = 31
N_EXPERTS = 256
TOP_K = 8
N_GROUPS = 8
GROUP_SIZE = N_EXPERTS // N_GROUPS
TOPK_GROUPS = 4
D_EXPERT = 256
ROUTED_SCALE = 2.5
LN_EPS = 1e-5
RMS_EPS = 1e-5
DEPTH = 1
DN_ALPHA = (2.0 * DEPTH) ** 0.25
LAMBDA_INIT = 0.8 - 0.6 * math.exp(-0.3 * 0)
Q_SCALE = HEAD_DIM ** -0.5 * math.log2(math.e)

LANES = 128
SUBLANES = 8
VMEM_LIMIT_BYTES = 56 * 1024 * 1024

PROJ_TM = 512
ATTN_TQ = 512
ATTN_QG = 256
ONES_ROWS = 16
MIX_TM = 256
CONV_HALO = 32
CONV_ROWS = 64
ROUTE_TM = 512
EXPERT_BLK = 512
FINAL_TM = 256
SC_WINDOW = 128

F32 = jnp.float32
BF16 = jnp.bfloat16
U32 = jnp.uint32
NEG_INF = float("-inf")


def _cparams(*sem):
    return pltpu.CompilerParams(dimension_semantics=sem, vmem_limit_bytes=VMEM_LIMIT_BYTES)


def _sigmoid(x):
    return 1.0 / (1.0 + jnp.exp(-x))


def _silu(x):
    return x * _sigmoid(x)


def _pack_halves(x):
    half = x.shape[-1] // 2
    bits = lambda v: lax.bitcast_convert_type(v.astype(BF16).astype(F32), U32)
    return (bits(x[:, :half]) >> 16) | (bits(x[:, half:]) & jnp.uint32(0xFFFF0000))


def _unpack_halves(w):
    lo = lax.bitcast_convert_type(w << 16, F32)
    hi = lax.bitcast_convert_type(w & jnp.uint32(0xFFFF0000), F32)
    return lo, hi


def _store_packed(ref, x):
    w = _pack_halves(x)
    q = w.shape[-1] // 2
    ref[0] = w[:, :q]
    ref[1] = w[:, q:]


def _load_packed(ref):
    return _unpack_halves(jnp.concatenate([ref[0], ref[1]], axis=-1))


def _layer_norm(r, g, b):
    mu = jnp.mean(r, axis=-1, keepdims=True)
    d = r - mu
    var = jnp.mean(d * d, axis=-1, keepdims=True)
    return d * lax.rsqrt(var + LN_EPS) * g + b


def _ada_kernel(c_ref, w_ref, b_ref, o_ref):
    a = _silu(c_ref[...])
    o_ref[...] = jnp.dot(a, w_ref[...], preferred_element_type=F32,
                         precision=lax.Precision.HIGHEST) + b_ref[...]


def _ada_modulation(c, w, b):
    B, D = c.shape
    n = w.shape[1] // D
    return pl.pallas_call(
        _ada_kernel,
        out_shape=jax.ShapeDtypeStruct((B, n * D), F32),
        grid=(n,),
        in_specs=[pl.BlockSpec((B, D), lambda j: (0, 0)),
                  pl.BlockSpec((D, D), lambda j: (0, j)),
                  pl.BlockSpec((1, D), lambda j: (0, j))],
        out_specs=pl.BlockSpec((B, D), lambda j: (0, j)),
        compiler_params=_cparams("arbitrary"),
        name="ada_modulation",
    )(c, w, b.reshape(1, n * D))


def _inproj_kernel(x_ref, sc_ref, sh_ref, w_ref, b_ref,
                   q_ref, k_ref, v_ref, u_ref, ga_ref, gb_ref):
    D = D_MODEL
    h = (x_ref[...] * (1.0 + sc_ref[0]) + sh_ref[0]).astype(BF16)

    def proj(j):
        return (jnp.dot(h, w_ref[:, j * D:(j + 1) * D], preferred_element_type=F32)
                + b_ref[:, j * D:(j + 1) * D])

    q_ref[...] = (proj(0) * Q_SCALE).astype(BF16)
    k_ref[...] = proj(1).astype(BF16)
    v_ref[...] = proj(2).astype(BF16)
    u_ref[...] = (proj(3) * _sigmoid(proj(4))).astype(BF16)
    ga_ref[...] = _sigmoid(proj(5)).astype(BF16)
    gb_ref[...] = _sigmoid(proj(6)).astype(BF16)


def _in_projection(x2, sc1, sh1, w_in, b_in, seq):
    T, D = x2.shape
    tm = PROJ_TM
    tiles_per_seq = seq // tm
    ncols = w_in.shape[1]
    tok = pl.BlockSpec((tm, D), lambda i: (i, 0))
    mod = pl.BlockSpec((1, 1, D), lambda i: (i // tiles_per_seq, 0, 0))
    out = jax.ShapeDtypeStruct((T, D), BF16)
    return pl.pallas_call(
        _inproj_kernel,
        out_shape=[out] * 6,
        grid=(T // tm,),
        in_specs=[tok, mod, mod,
                  pl.BlockSpec((D, ncols), lambda i: (0, 0)),
                  pl.BlockSpec((1, ncols), lambda i: (0, 0))],
        out_specs=[tok] * 6,
        compiler_params=_cparams("arbitrary"),
        name="in_projection",
    )(x2, sc1, sh1, w_in, b_in)


def _attn_kernel(lq1_ref, lk1_ref, lq2_ref, lk2_ref, g_ref, qt_ref, k_ref, vt_ref, o_ref,
                 q2_ref, vx_ref, *scratch):
    tq = ATTN_TQ
    qg = ATTN_QG
    ng = 2 * tq // qg
    groups = [scratch[i * ng:(i + 1) * ng] for i in range(7)]
    s_slots = groups[0:2]
    p_refs, m_refs, l_refs, a_refs, acc_refs = groups[2:]
    qi = pl.program_id(2)

    @pl.when(qi == 0)
    def _():
        vx_ref[:V_DIM, :] = vt_ref[...]
        vx_ref[V_DIM:, :] = jnp.ones((ONES_ROWS, vx_ref.shape[1]), BF16)

    lam = (jnp.exp(jnp.sum(lq1_ref[...] * lk1_ref[...], axis=-1, keepdims=True))
           - jnp.exp(jnp.sum(lq2_ref[...] * lk2_ref[...], axis=-1, keepdims=True))
           + LAMBDA_INIT)

    qt = qt_ref[...]
    feat = lax.broadcasted_iota(jnp.int32, qt.shape, 0)
    zero = jnp.zeros_like(qt)
    q2_ref[:, :tq] = jnp.where(feat < HEAD_DIM, qt, zero)
    q2_ref[:, tq:] = jnp.where(feat >= HEAD_DIM, qt, zero)

    def scores(j, slot, g):
        start = pl.multiple_of(j * tq, tq)
        s_slots[slot][g][...] = jnp.dot(
            k_ref[pl.ds(start, tq), :], q2_ref[:, g * qg:(g + 1) * qg],
            preferred_element_type=F32)

    def update(j, slot, g, masked):
        start = pl.multiple_of(j * tq, tq)
        s_ref = s_slots[slot][g]
        for c in range(qg // LANES):
            cs = slice(c * LANES, (c + 1) * LANES)

            def load():
                s = s_ref[:, cs]
                if masked:
                    key = lax.broadcasted_iota(jnp.int32, s.shape, 0)
                    qry = (lax.broadcasted_iota(jnp.int32, s.shape, 1)
                           + (g * qg + c * LANES) % tq)
                    s = jnp.where(key <= qry, s, NEG_INF)
                return s

            m_prev = m_refs[g][:, cs]
            m_new = jnp.maximum(m_prev, jnp.max(load(), axis=0, keepdims=True))
            p_refs[g][:, cs] = jnp.exp2((load() - m_new).astype(BF16))
            m_refs[g][:, cs] = m_new
            a_refs[g][:, cs] = jnp.exp2(m_prev - m_new)
        pv = jnp.dot(vx_ref[:, pl.ds(start, tq)], p_refs[g][...], preferred_element_type=F32)
        alpha = a_refs[g][...]
        acc_refs[g][...] = alpha * acc_refs[g][...] + pv[:V_DIM, :]
        l_refs[g][...] = alpha * l_refs[g][...] + pv[V_DIM:V_DIM + 1, :]

    def step(j, slot, masked=False, prefetch=True):
        for g in range(ng):
            if prefetch:
                scores(j + 1, 1 - slot, g)
            update(j, slot, g, masked)

    for g in range(ng):
        m_refs[g][...] = jnp.full(m_refs[g].shape, NEG_INF, F32)
        l_refs[g][...] = jnp.zeros(l_refs[g].shape, F32)
        acc_refs[g][...] = jnp.zeros(acc_refs[g].shape, F32)
        scores(0, 0, g)

    def pair(jj, carry):
        step(2 * jj, 0)
        step(2 * jj + 1, 1)
        return carry

    lax.fori_loop(0, qi // 2, pair, 0)

    @pl.when(qi % 2 == 1)
    def _():
        step(qi - 1, 0)
        step(qi, 1, masked=True, prefetch=False)

    @pl.when(qi % 2 == 0)
    def _():
        step(qi, 0, masked=True, prefetch=False)

    o = (jnp.concatenate([r[...] for r in acc_refs], axis=-1)
         / jnp.concatenate([r[...] for r in l_refs], axis=-1))
    o = o[:, :tq] - lam * o[:, tq:]
    o = o * lax.rsqrt(jnp.mean(o * o, axis=0, keepdims=True) + RMS_EPS) * g_ref[...]
    o_ref[...] = (o * (1.0 - LAMBDA_INIT)).T.astype(BF16)


def _diff_attention(qt, k, vt, lq1, lk1, lq2, lk2, subln_g, batch, seq):
    T, D = k.shape
    tq = ATTN_TQ
    nq = seq // tq
    lam_spec = pl.BlockSpec((1, HEAD_DIM), lambda b, h, i: (0, 0))
    q_spec = pl.BlockSpec((tq, V_DIM), lambda b, h, i: (b * nq + i, h))
    kv_spec = pl.BlockSpec((seq, V_DIM), lambda b, h, i: (b, h))
    qg = ATTN_QG
    per_group = lambda shape, dtype: [pltpu.VMEM(shape, dtype) for _ in range(2 * tq // qg)]
    return pl.pallas_call(
        _attn_kernel,
        out_shape=jax.ShapeDtypeStruct((T, D), BF16),
        grid=(batch, N_HEADS, nq),
        in_specs=[lam_spec, lam_spec, lam_spec, lam_spec,
                  pl.BlockSpec((V_DIM, 1), lambda b, h, i: (0, 0)),
                  pl.BlockSpec((V_DIM, tq), lambda b, h, i: (h, b * nq + i)), kv_spec,
                  pl.BlockSpec((V_DIM, seq), lambda b, h, i: (h, b))],
        out_specs=q_spec,
        scratch_shapes=(
            [pltpu.VMEM((V_DIM, 2 * tq), BF16),
             pltpu.VMEM((V_DIM + ONES_ROWS, seq), BF16)]
            + per_group((tq, qg), F32)
            + per_group((tq, qg), F32)
            + per_group((tq, qg), BF16)
            + per_group((1, qg), F32)
            + per_group((1, qg), F32)
            + per_group((1, qg), F32)
            + per_group((V_DIM, qg), F32)),
        compiler_params=_cparams("arbitrary", "arbitrary", "arbitrary"),
        name="diff_attention",
    )(lq1, lk1, lq2, lk2, subln_g.reshape(V_DIM, 1), qt, k, vt)


def _mixer_kernel(x_ref, o_ref, u_ref, up_ref, ga_ref, gb_ref, g1_ref, sc2_ref, sh2_ref,
                  wao_ref, cw_ref, cb_ref, clg_ref, clb_ref, wpw_ref, bpw_ref, wout_ref,
                  ln1g_ref, ln1b_ref, wrh_ref, wrl_ref,
                  x1_ref, h2_ref, lg_ref, ubuf_ref, ush_ref, conv_ref, *, tiles_per_seq):
    tm = MIX_TM
    first = (pl.program_id(0) % tiles_per_seq) == 0

    prev = up_ref[...].astype(F32)
    ubuf_ref[0:CONV_HALO, :] = jnp.where(first, jnp.zeros_like(prev), prev)
    ubuf_ref[CONV_HALO:, :] = u_ref[...].astype(F32)
    base = CONV_HALO - (CONV_K - 1)
    n_shift = tm + CONV_HALO - SUBLANES
    for r in range(1, SUBLANES):
        ush_ref[r - 1] = ubuf_ref[r:r + n_shift, :]
    for c in range(D_MODEL // LANES):
        cl = slice(c * LANES, (c + 1) * LANES)
        for r0 in range(0, tm, CONV_ROWS):
            acc = jnp.zeros((CONV_ROWS, LANES), F32)
            for j in range(CONV_K):
                a, r = divmod(base + j, SUBLANES)
                src = ubuf_ref if r == 0 else ush_ref.at[r - 1]
                lo = SUBLANES * a + r0
                acc = acc + cw_ref[j:j + 1, cl] * src[lo:lo + CONV_ROWS, cl]
            conv_ref[r0:r0 + CONV_ROWS, cl] = acc
    yc = _silu(_layer_norm(conv_ref[...] + cb_ref[...], clg_ref[...], clb_ref[...]))
    y_conv = jnp.dot(yc.astype(BF16), wpw_ref[...], preferred_element_type=F32) + bpw_ref[...]

    y_attn = jnp.dot(o_ref[...], wao_ref[...], preferred_element_type=F32)
    mix = ga_ref[...].astype(F32) * y_attn + gb_ref[...].astype(F32) * y_conv
    y = jnp.dot(mix.astype(BF16), wout_ref[...], preferred_element_type=F32)
    x1 = _layer_norm(DN_ALPHA * x_ref[...] + g1_ref[0] * y, ln1g_ref[...], ln1b_ref[...])
    x1_ref[...] = x1

    h2 = x1 * (1.0 + sc2_ref[0]) + sh2_ref[0]
    h2_hi = h2.astype(BF16)
    _store_packed(h2_ref, h2)
    h2_lo = (h2 - h2_hi.astype(F32)).astype(BF16)
    nt = lambda w, a: lax.dot_general(w, a, (((1,), (1,)), ((), ())), preferred_element_type=F32)
    lg_ref[...] = (nt(wrh_ref[...], h2_hi)
                   + (nt(wrl_ref[...], h2_hi) + nt(wrh_ref[...], h2_lo)))


def _mixer_epilogue(x2, o, u, ga, gb, g1, sc2, sh2, wao, cw, cb, clg, clb, wpw, bpw, wout,
                    ln1g, ln1b, wr_hi, wr_lo, seq):
    T, D = x2.shape
    tm = MIX_TM
    tiles_per_seq = seq // tm
    halo_per_tile = tm // CONV_HALO
    tok = pl.BlockSpec((tm, D), lambda i: (i, 0))
    prev = pl.BlockSpec((CONV_HALO, D), lambda i: (jnp.maximum(i * halo_per_tile - 1, 0), 0))
    mod = pl.BlockSpec((1, 1, D), lambda i: (i // tiles_per_seq, 0, 0))
    row = pl.BlockSpec((1, D), lambda i: (0, 0))
    mat = pl.BlockSpec((D, D), lambda i: (0, 0))
    rw = pl.BlockSpec((N_EXPERTS, D), lambda i: (0, 0))
    return pl.pallas_call(
        partial(_mixer_kernel, tiles_per_seq=tiles_per_seq),
        out_shape=[jax.ShapeDtypeStruct((T, D), F32),
                   jax.ShapeDtypeStruct((2, T, D // 4), U32),
                   jax.ShapeDtypeStruct((N_EXPERTS, T), F32)],
        grid=(T // tm,),
        in_specs=[tok, tok, tok, prev, tok, tok, mod, mod, mod,
                  mat, pl.BlockSpec((CONV_K, D), lambda i: (0, 0)), row, row, row, mat, row, mat,
                  row, row, rw, rw],
        out_specs=[tok, pl.BlockSpec((2, tm, D // 4), lambda i: (0, i, 0)),
                   pl.BlockSpec((N_EXPERTS, tm), lambda i: (0, i))],
        scratch_shapes=[pltpu.VMEM((CONV_HALO + tm, D), F32),
                        pltpu.VMEM((SUBLANES - 1, tm + CONV_HALO - SUBLANES, D), F32),
                        pltpu.VMEM((tm, D), F32)],
        compiler_params=_cparams("arbitrary"),
        name="mixer_epilogue",
    )(x2, o, u, u, ga, gb, g1, sc2, sh2, wao, cw, cb, clg, clb, wpw, bpw, wout,
      ln1g, ln1b, wr_hi, wr_lo)


def _first_argmax(v, eid):
    m = jnp.max(v, axis=0, keepdims=True)
    i = jnp.min(jnp.where(v == m, eid, float(N_EXPERTS)), axis=0, keepdims=True)
    return m, i


def _route_kernel(lg_ref, rb_ref, idx_ref, gate_ref, cnt_ref):
    tm = ROUTE_TM
    scores = _sigmoid(lg_ref[...])
    choice = scores + rb_ref[...]
    eid = lax.broadcasted_iota(jnp.int32, (N_EXPERTS, tm), 0).astype(F32)
    rows = lambda a, g: a[g * GROUP_SIZE:(g + 1) * GROUP_SIZE, :]

    gs = []
    for g in range(N_GROUPS):
        cg = rows(choice, g)
        eg = (lax.broadcasted_iota(jnp.int32, cg.shape, 0) + g * GROUP_SIZE).astype(F32)
        m1, i1 = _first_argmax(cg, eg)
        m2 = jnp.max(jnp.where(eg == i1, NEG_INF, cg), axis=0, keepdims=True)
        gs.append(jnp.broadcast_to(m1 + m2, (SUBLANES, tm)))

    kept = []
    for g in range(N_GROUPS):
        ahead = jnp.zeros((SUBLANES, tm), F32)
        for o in range(N_GROUPS):
            if o == g:
                continue
            beats = (gs[o] >= gs[g]) if o < g else (gs[o] > gs[g])
            ahead = ahead + jnp.where(beats, 1.0, 0.0)
        cap = jnp.where(ahead < TOPK_GROUPS, float("inf"), NEG_INF)
        for r in range(g * GROUP_SIZE, (g + 1) * GROUP_SIZE, SUBLANES):
            kept.append(jnp.minimum(choice[r:r + SUBLANES, :], cap))
    masked = jnp.concatenate(kept, axis=0)

    sel = jnp.zeros((N_EXPERTS, tm), F32)
    total = jnp.zeros((1, tm), F32)
    picked = []
    for k in range(TOP_K):
        _, i = _first_argmax(masked, eid)
        hit = eid == i
        sc = jnp.sum(jnp.where(hit, scores, 0.0), axis=0, keepdims=True)
        masked = jnp.where(hit, NEG_INF, masked)
        sel = sel + hit.astype(F32)
        total = total + sc
        idx_ref[k:k + 1, :] = i.astype(jnp.int32)
        picked.append(sc)
    for k in range(TOP_K):
        gate_ref[k:k + 1, :] = picked[k] / total * ROUTED_SCALE

    @pl.when(pl.program_id(0) == 0)
    def _():
        cnt_ref[...] = jnp.zeros_like(cnt_ref)
    cnt_ref[...] += jnp.sum(sel, axis=1, keepdims=True)


def _route(logits_t, router_bias):
    T = logits_t.shape[1]
    tm = ROUTE_TM
    return pl.pallas_call(
        _route_kernel,
        out_shape=[jax.ShapeDtypeStruct((TOP_K, T), jnp.int32),
                   jax.ShapeDtypeStruct((TOP_K, T), F32),
                   jax.ShapeDtypeStruct((N_EXPERTS, 1), F32)],
        grid=(T // tm,),
        in_specs=[pl.BlockSpec((N_EXPERTS, tm), lambda i: (0, i)),
                  pl.BlockSpec((N_EXPERTS, 1), lambda i: (0, 0))],
        out_specs=[pl.BlockSpec((TOP_K, tm), lambda i: (0, i)),
                   pl.BlockSpec((TOP_K, tm), lambda i: (0, i)),
                   pl.BlockSpec((N_EXPERTS, 1), lambda i: (0, 0))],
        compiler_params=_cparams("arbitrary"),
        name="route_topk",
    )(logits_t, router_bias.reshape(N_EXPERTS, 1))


def _slot_kernel(idx_ref, start_ref, pos_ref, carry_ref):
    tm = ROUTE_TM

    @pl.when(pl.program_id(0) == 0)
    def _():
        carry_ref[...] = jnp.zeros_like(carry_ref)

    idx = idx_ref[...]
    eid = lax.broadcasted_iota(jnp.int32, (N_EXPERTS, tm), 0)
    hits = [eid == idx[k:k + 1, :] for k in range(TOP_K)]
    sel = jnp.zeros((N_EXPERTS, tm), F32)
    for h in hits:
        sel = sel + h.astype(F32)
    r = lax.broadcasted_iota(jnp.int32, (tm, tm), 0)
    c = lax.broadcasted_iota(jnp.int32, (tm, tm), 1)
    earlier = (r < c).astype(BF16)
    rank = jnp.dot(sel.astype(BF16), earlier, preferred_element_type=F32)
    slot = rank + carry_ref[...] + start_ref[...]
    for k, h in enumerate(hits):
        pos_ref[k:k + 1, :] = jnp.sum(jnp.where(h, slot, 0.0), axis=0,
                                      keepdims=True).astype(jnp.int32)
    carry_ref[...] += jnp.sum(sel, axis=1, keepdims=True)


def _assign_slots(idx_t, pad_start):
    T = idx_t.shape[1]
    tm = ROUTE_TM
    return pl.pallas_call(
        _slot_kernel,
        out_shape=jax.ShapeDtypeStruct((TOP_K, T), jnp.int32),
        grid=(T // tm,),
        in_specs=[pl.BlockSpec((TOP_K, tm), lambda i: (0, i)),
                  pl.BlockSpec((N_EXPERTS, 1), lambda i: (0, 0))],
        out_specs=pl.BlockSpec((TOP_K, tm), lambda i: (0, i)),
        scratch_shapes=[pltpu.VMEM((N_EXPERTS, 1), F32)],
        compiler_params=_cparams("arbitrary"),
        name="assign_slots",
    )(idx_t, pad_start)


def _dot_halves(lo, hi, w_ref):
    half = lo.shape[-1]
    return (jnp.dot(lo.astype(BF16), w_ref[:half, :].astype(BF16), preferred_element_type=F32)
            + jnp.dot(hi.astype(BF16), w_ref[half:, :].astype(BF16), preferred_element_type=F32))


def _expert_kernel(be_ref, nu_ref, first_ref, slot_ref, nxt_ref, xs_ref, w1_hbm, w3_hbm, w2_hbm,
                   ys_ref, w1_buf, w3_buf, w2_buf, sem):
    b = pl.program_id(0)

    def weight_copies(expert, slot):
        pairs = ((w1_hbm, w1_buf), (w3_hbm, w3_buf), (w2_hbm, w2_buf))
        return [pltpu.make_async_copy(hbm.at[expert], buf.at[slot], sem.at[slot, i])
                for i, (hbm, buf) in enumerate(pairs)]

    @pl.when(b < nu_ref[0])
    def _():
        slot = slot_ref[b]

        @pl.when(b == 0)
        def _():
            for c in weight_copies(be_ref[0], 0):
                c.start()

        @pl.when(first_ref[b] == 1)
        def _():
            for c in weight_copies(be_ref[b], slot):
                c.wait()

            @pl.when(nxt_ref[b] >= 0)
            def _():
                for c in weight_copies(nxt_ref[b], 1 - slot):
                    c.start()

        lo, hi = _load_packed(xs_ref)
        half = lo.shape[-1]
        mm = lambda w: (jnp.dot(lo, w[slot, :half, :], preferred_element_type=F32)
                        + jnp.dot(hi, w[slot, half:, :], preferred_element_type=F32))
        hb = _silu(mm(w1_buf)) * mm(w3_buf)
        _store_packed(ys_ref, jnp.dot(hb, w2_buf[slot], preferred_element_type=F32))

    @pl.when(b >= nu_ref[0])
    def _():
        ys_ref[...] = jnp.zeros_like(ys_ref)


def _routed_experts(xs, blk_expert, n_used, w1, w3, w2):
    _, P, Dq = xs.shape
    blk = EXPERT_BLK
    n_blocks = P // blk
    D, F = w1.shape[-2:]

    b_ids = jnp.arange(n_blocks, dtype=jnp.int32)
    used = b_ids < n_used[0]
    prev = jnp.concatenate([jnp.full((1,), -1, jnp.int32), blk_expert[:-1]])
    first = (used & (blk_expert != prev)).astype(jnp.int32)
    slot = (jnp.cumsum(first) - 1) % 2
    start_at = jnp.where(first == 1, b_ids, n_blocks)
    later = jnp.concatenate([start_at[1:], jnp.full((1,), n_blocks, jnp.int32)])
    next_start = lax.cummin(later, axis=0, reverse=True)
    nxt = jnp.where(next_start < n_blocks,
                    blk_expert[jnp.minimum(next_start, n_blocks - 1)], -1).astype(jnp.int32)

    row_blk = lambda b, be, nu, *_: (0, jnp.minimum(b, nu[0] - 1), 0)
    grid_spec = pltpu.PrefetchScalarGridSpec(
        num_scalar_prefetch=5,
        grid=(n_blocks,),
        in_specs=[pl.BlockSpec((2, blk, Dq), row_blk),
                  pl.BlockSpec(memory_space=pl.ANY),
                  pl.BlockSpec(memory_space=pl.ANY),
                  pl.BlockSpec(memory_space=pl.ANY)],
        out_specs=pl.BlockSpec((2, blk, Dq), lambda b, *_: (0, b, 0)),
        scratch_shapes=[pltpu.VMEM((2, D, F), F32), pltpu.VMEM((2, D, F), F32),
                        pltpu.VMEM((2, F, D), F32), pltpu.SemaphoreType.DMA((2, 3))],
    )
    return pl.pallas_call(
        _expert_kernel,
        out_shape=jax.ShapeDtypeStruct((2, P, Dq), U32),
        grid_spec=grid_spec,
        compiler_params=_cparams("arbitrary"),
        name="routed_experts",
    )(blk_expert, n_used, first, slot.astype(jnp.int32), nxt, xs, w1, w3, w2)


def _sc_mesh():
    return plsc.VectorSubcoreMesh(core_axis_name="c", subcore_axis_name="s")


def _sc_dispatch(rows, slots_t, n_slots):
    T, W = rows.shape
    K = slots_t.shape[0]
    win = SC_WINDOW

    @pl.kernel(out_type=jax.ShapeDtypeStruct((n_slots, W), rows.dtype), mesh=_sc_mesh(),
               scratch_types=[])
    def dispatch(x_hbm, i_hbm, o_hbm):
        def body(x_vmem, i_vmem):
            for k in range(K):
                pltpu.sync_copy(x_vmem, o_hbm.at[i_vmem.at[k]])

        pltpu.emit_pipeline(
            body,
            grid=(T // win,),
            in_specs=[pl.BlockSpec((win, W), lambda i: (i, 0)),
                      pl.BlockSpec((K, win), lambda i: (0, i))],
            out_specs=[],
            core_axis_name=("c", "s"),
            dimension_semantics=(pltpu.PARALLEL,),
        )(x_hbm, i_hbm)

    return dispatch(rows, slots_t)


def _sc_gather(table, idx):
    W = table.shape[1]
    N = idx.shape[1]
    win = SC_WINDOW

    @pl.kernel(out_type=jax.ShapeDtypeStruct((N, W), table.dtype), mesh=_sc_mesh(),
               scratch_types=[])
    def gather(x_hbm, i_hbm, o_hbm):
        def body(i_vmem, o_vmem):
            pltpu.sync_copy(x_hbm.at[i_vmem.at[0]], o_vmem)

        pltpu.emit_pipeline(
            body,
            grid=(N // win,),
            in_specs=[pl.BlockSpec((1, win), lambda i: (0, i))],
            out_specs=[pl.BlockSpec((win, W), lambda i: (i, 0))],
            core_axis_name=("c", "s"),
            dimension_semantics=(pltpu.PARALLEL,),
        )(i_hbm, o_hbm)

    return gather(table, idx)


def _final_kernel(x1_ref, h2_ref, yg_ref, gate_ref, g2_ref, ws1_ref, ws3_ref, ws2_ref,
                  lng_ref, lnb_ref, o_ref):
    Dh = D_MODEL // 2
    gate = gate_ref[...]
    routed_lo = jnp.zeros((x1_ref.shape[0], Dh), F32)
    routed_hi = jnp.zeros((x1_ref.shape[0], Dh), F32)
    for k in range(TOP_K):
        lo, hi = _load_packed(yg_ref.at[k])
        routed_lo = routed_lo + lo * gate[:, k:k + 1]
        routed_hi = routed_hi + hi * gate[:, k:k + 1]
    lo, hi = _load_packed(h2_ref)
    a = _dot_halves(lo, hi, ws1_ref)
    b = _dot_halves(lo, hi, ws3_ref)
    shared = jnp.dot((_silu(a) * b).astype(BF16), ws2_ref[...], preferred_element_type=F32)
    y2 = jnp.concatenate([routed_lo, routed_hi], axis=-1) + shared
    o_ref[...] = _layer_norm(DN_ALPHA * x1_ref[...] + g2_ref[0] * y2, lng_ref[...], lnb_ref[...])


def _final(x1, h2, yg, gate, g2, ws1, ws3, ws2, ln2g, ln2b, seq):
    T, D = x1.shape
    tm = FINAL_TM
    tiles_per_seq = seq // tm
    F = ws1.shape[-1]
    tok = pl.BlockSpec((tm, D), lambda i: (i, 0))
    row = pl.BlockSpec((1, D), lambda i: (0, 0))
    return pl.pallas_call(
        _final_kernel,
        out_shape=jax.ShapeDtypeStruct((T, D), F32),
        grid=(T // tm,),
        in_specs=[tok, pl.BlockSpec((2, tm, D // 4), lambda i: (0, i, 0)),
                  pl.BlockSpec((TOP_K, 2, tm, D // 4), lambda i: (0, 0, i, 0)),
                  pl.BlockSpec((tm, TOP_K), lambda i: (i, 0)),
                  pl.BlockSpec((1, 1, D), lambda i: (i // tiles_per_seq, 0, 0)),
                  pl.BlockSpec((D, F), lambda i: (0, 0)),
                  pl.BlockSpec((D, F), lambda i: (0, 0)),
                  pl.BlockSpec((F, D), lambda i: (0, 0)),
                  row, row],
        out_specs=tok,
        compiler_params=_cparams("arbitrary"),
        name="moe_combine_final",
    )(x1, h2, yg, gate, g2, ws1, ws3, ws2, ln2g, ln2b)


def kernel(x, c, ada_w, ada_b, w_in, b_in, lambda_q1, lambda_k1, lambda_q2, lambda_k2, subln_g, w_attn_o, conv_w, conv_b, conv_ln_g, conv_ln_b, conv_pw_w, conv_pw_b, w_out, ln1_g, ln1_b, router_w, router_bias, w1, w3, w2, ws1, ws3, ws2, ln2_g, ln2_b):
    B, S, D = x.shape
    T = B * S
    l = 0
    x2 = x.reshape(T, D)
    row = lambda a: a.reshape(1, -1)

    mod = _ada_modulation(c, ada_w[l], ada_b[l])
    sh1, sc1, g1, sh2, sc2, g2 = [m.reshape(B, 1, D) for m in jnp.split(mod, 6, axis=-1)]

    q, k, v, u, ga, gb = _in_projection(x2, sc1, sh1, w_in[l].astype(BF16), row(b_in[l]), S)
    o = _diff_attention(q.T, k, v.T, row(lambda_q1[l]), row(lambda_k1[l]), row(lambda_q2[l]),
                        row(lambda_k2[l]), row(subln_g[l]), B, S)

    wr = router_w[l].T
    wr_hi = wr.astype(BF16)
    wr_lo = (wr - wr_hi.astype(F32)).astype(BF16)
    x1, h2, logits = _mixer_epilogue(
        x2, o, u, ga, gb, g1, sc2, sh2, w_attn_o[l].astype(BF16), conv_w[l], row(conv_b[l]),
        row(conv_ln_g[l]), row(conv_ln_b[l]), conv_pw_w[l].astype(BF16), row(conv_pw_b[l]),
        w_out[l].astype(BF16), row(ln1_g[l]), row(ln1_b[l]), wr_hi, wr_lo, S)

    idx, gate, counts = _route(logits, router_bias[l])

    blk = EXPERT_BLK
    cnt = counts.reshape(N_EXPERTS).astype(jnp.int32)
    padded = (cnt + blk - 1) // blk * blk
    pad_end = jnp.cumsum(padded)
    pad_start = pad_end - padded
    P = T * TOP_K + N_EXPERTS * blk
    n_blocks = P // blk
    blk_first_row = jnp.arange(n_blocks, dtype=jnp.int32) * blk
    blk_expert = jnp.minimum(
        jnp.sum((pad_end[None, :] <= blk_first_row[:, None]).astype(jnp.int32), axis=1),
        N_EXPERTS - 1)
    n_used = (pad_end[-1:] // blk).astype(jnp.int32)

    pos = _assign_slots(idx, pad_start.astype(F32).reshape(N_EXPERTS, 1))

    Dq = D // 4
    half_base = (jnp.arange(2, dtype=jnp.int32) * P)[None, :, None]
    slots = pos[:, None, :] + half_base
    xs = _sc_dispatch(h2.reshape(2 * T, Dq), slots.reshape(TOP_K, 2 * T), 2 * P)
    ys = _routed_experts(xs.reshape(2, P, Dq), blk_expert, n_used, w1[l], w3[l], w2[l])
    yg = _sc_gather(ys.reshape(2 * P, Dq), slots.reshape(1, TOP_K * 2 * T))
    yg = yg.reshape(TOP_K, 2, T, Dq)

    out = _final(x1, h2, yg, gate.T, g2, ws1[l].astype(BF16), ws3[l].astype(BF16),
                 ws2[l].astype(BF16), row(ln2_g[l]), row(ln2_b[l]), S)
    return out.reshape(B, S, D)
```

```python
import math
from functools import partial

import jax
import jax.numpy as jnp
from jax import lax
from jax.experimental import pallas as pl
from jax.experimental.pallas import tpu as pltpu
from jax.experimental.pallas import tpu_sc as plsc

D_MODEL = 1024
N_HEADS = 8
HEAD_DIM = 64
V_DIM = 128
CONV_K = 31
N_EXPERTS = 256
TOP_K = 8
N_GROUPS = 8
GROUP_SIZE = N_EXPERTS // N_GROUPS
TOPK_GROUPS = 4
D_EXPERT = 256
ROUTED_SCALE = 2.5
LN_EPS = 1e-5
RMS_EPS = 1e-5
DEPTH = 1
DN_ALPHA = (2.0 * DEPTH) ** 0.25
LAMBDA_INIT = 0.8 - 0.6 * math.exp(-0.3 * 0)
Q_SCALE = HEAD_DIM ** -0.5 * math.log2(math.e)

LANES = 128
SUBLANES = 8
VMEM_LIMIT_BYTES = 56 * 1024 * 1024

PROJ_TM = 512
ATTN_TQ = 512
ATTN_QG = 256
ONES_ROWS = 16
MIX_TM = 256
CONV_HALO = 32
CONV_ROWS = 64
ROUTE_TM = 512
EXPERT_BLK = 512
FINAL_TM = 256
SC_WINDOW = 128

F32 = jnp.float32
BF16 = jnp.bfloat16
U32 = jnp.uint32
NEG_INF = float("-inf")


def _cparams(*sem):
    return pltpu.CompilerParams(dimension_semantics=sem, vmem_limit_bytes=VMEM_LIMIT_BYTES)


def _sigmoid(x):
    return 1.0 / (1.0 + jnp.exp(-x))


def _silu(x):
    return x * _sigmoid(x)


def _pack_halves(x):
    half = x.shape[-1] // 2
    bits = lambda v: lax.bitcast_convert_type(v.astype(BF16).astype(F32), U32)
    return (bits(x[:, :half]) >> 16) | (bits(x[:, half:]) & jnp.uint32(0xFFFF0000))


def _unpack_halves(w):
    lo = lax.bitcast_convert_type(w << 16, F32)
    hi = lax.bitcast_convert_type(w & jnp.uint32(0xFFFF0000), F32)
    return lo, hi


def _store_packed(ref, x):
    w = _pack_halves(x)
    q = w.shape[-1] // 2
    ref[0] = w[:, :q]
    ref[1] = w[:, q:]


def _load_packed(ref):
    return _unpack_halves(jnp.concatenate([ref[0], ref[1]], axis=-1))


def _layer_norm(r, g, b):
    mu = jnp.mean(r, axis=-1, keepdims=True)
    d = r - mu
    var = jnp.mean(d * d, axis=-1, keepdims=True)
    return d * lax.rsqrt(var + LN_EPS) * g + b


def _ada_kernel(c_ref, w_ref, b_ref, o_ref):
    a = _silu(c_ref[...])
    o_ref[...] = jnp.dot(a, w_ref[...], preferred_element_type=F32,
                         precision=lax.Precision.HIGHEST) + b_ref[...]


def _ada_modulation(c, w, b):
    B, D = c.shape
    n = w.shape[1] // D
    return pl.pallas_call(
        _ada_kernel,
        out_shape=jax.ShapeDtypeStruct((B, n * D), F32),
        grid=(n,),
        in_specs=[pl.BlockSpec((B, D), lambda j: (0, 0)),
                  pl.BlockSpec((D, D), lambda j: (0, j)),
                  pl.BlockSpec((1, D), lambda j: (0, j))],
        out_specs=pl.BlockSpec((B, D), lambda j: (0, j)),
        compiler_params=_cparams("arbitrary"),
        name="ada_modulation",
    )(c, w, b.reshape(1, n * D))


def _inproj_kernel(x_ref, sc_ref, sh_ref, w_ref, b_ref,
                   q_ref, k_ref, v_ref, u_ref, ga_ref, gb_ref):
    D = D_MODEL
    h = (x_ref[...] * (1.0 + sc_ref[0]) + sh_ref[0]).astype(BF16)

    def proj(j):
        return (jnp.dot(h, w_ref[:, j * D:(j + 1) * D], preferred_element_type=F32)
                + b_ref[:, j * D:(j + 1) * D])

    q_ref[...] = (proj(0) * Q_SCALE).astype(BF16)
    k_ref[...] = proj(1).astype(BF16)
    v_ref[...] = proj(2).astype(BF16)
    u_ref[...] = (proj(3) * _sigmoid(proj(4))).astype(BF16)
    ga_ref[...] = _sigmoid(proj(5)).astype(BF16)
    gb_ref[...] = _sigmoid(proj(6)).astype(BF16)


def _in_projection(x2, sc1, sh1, w_in, b_in, seq):
    T, D = x2.shape
    tm = PROJ_TM
    tiles_per_seq = seq // tm
    ncols = w_in.shape[1]
    tok = pl.BlockSpec((tm, D), lambda i: (i, 0))
    mod = pl.BlockSpec((1, 1, D), lambda i: (i // tiles_per_seq, 0, 0))
    out = jax.ShapeDtypeStruct((T, D), BF16)
    return pl.pallas_call(
        _inproj_kernel,
        out_shape=[out] * 6,
        grid=(T // tm,),
        in_specs=[tok, mod, mod,
                  pl.BlockSpec((D, ncols), lambda i: (0, 0)),
                  pl.BlockSpec((1, ncols), lambda i: (0, 0))],
        out_specs=[tok] * 6,
        compiler_params=_cparams("arbitrary"),
        name="in_projection",
    )(x2, sc1, sh1, w_in, b_in)


def _attn_tasks(nq):
    tasks = [(qi, j, int(j == qi)) for qi in range(nq) for j in range(qi + 1)]
    if len(tasks) % 2:
        raise ValueError("the two-slot task loop needs an even number of tile pairs")
    qis, js, lasts = zip(*(tasks + [(0, 0, 0)]))
    as_i32 = lambda v: jnp.asarray(v, jnp.int32)
    return as_i32(qis), as_i32(js), as_i32(lasts), len(tasks)


def _attn_kernel(tqi_ref, tj_ref, tlast_ref, lq1_ref, lk1_ref, lq2_ref, lk2_ref, g_ref,
                 qt_ref, k_ref, vt_ref, o_ref, q2_ref, vx_ref, *scratch, n_tasks):
    tq = ATTN_TQ
    qg = ATTN_QG
    ng = 2 * tq // qg
    groups = [scratch[i * ng:(i + 1) * ng] for i in range(9)]
    s_slots, x_slots = groups[0:2], groups[2:4]
    p_refs, m_refs, l_refs, a_refs, acc_refs = groups[4:]
    seq = k_ref.shape[0]

    vx_ref[:V_DIM, :] = vt_ref[...]
    vx_ref[V_DIM:, :] = jnp.ones((ONES_ROWS, seq), BF16)

    lam = (jnp.exp(jnp.sum(lq1_ref[...] * lk1_ref[...], axis=-1, keepdims=True))
           - jnp.exp(jnp.sum(lq2_ref[...] * lk2_ref[...], axis=-1, keepdims=True))
           + LAMBDA_INIT)

    for i in range(seq // tq):
        qt = qt_ref[:, i * tq:(i + 1) * tq]
        feat = lax.broadcasted_iota(jnp.int32, qt.shape, 0)
        zero = jnp.zeros_like(qt)
        q2_ref[:, 2 * tq * i:2 * tq * i + tq] = jnp.where(feat < HEAD_DIM, qt, zero)
        q2_ref[:, 2 * tq * i + tq:2 * tq * (i + 1)] = jnp.where(feat >= HEAD_DIM, qt, zero)

    def scores(qi, j, slot, g):
        start = pl.multiple_of(j * tq, tq)
        col = pl.multiple_of(qi * (2 * tq) + g * qg, qg)
        s = jnp.dot(k_ref[pl.ds(start, tq), :], q2_ref[:, pl.ds(col, qg)],
                    preferred_element_type=F32)
        s_slots[slot][g][...] = s
        x_slots[slot][g][...] = jnp.max(s, axis=0, keepdims=True)

    def update(j, slot, g, masked):
        start = pl.multiple_of(j * tq, tq)
        s_ref = s_slots[slot][g]
        for c in range(qg // LANES):
            cs = slice(c * LANES, (c + 1) * LANES)

            def load():
                s = s_ref[:, cs]
                if masked:
                    key = lax.broadcasted_iota(jnp.int32, s.shape, 0)
                    qry = (lax.broadcasted_iota(jnp.int32, s.shape, 1)
                           + (g * qg + c * LANES) % tq)
                    s = jnp.where(key <= qry, s, NEG_INF)
                return s

            m_prev = m_refs[g][:, cs]
            tile_max = (jnp.max(load(), axis=0, keepdims=True) if masked
                        else x_slots[slot][g][:, cs])
            m_new = jnp.maximum(m_prev, tile_max)
            p_refs[g][:, cs] = jnp.exp2((load() - m_new).astype(BF16))
            m_refs[g][:, cs] = m_new
            a_refs[g][:, cs] = jnp.exp2(m_prev - m_new)
        pv = jnp.dot(vx_ref[:, pl.ds(start, tq)], p_refs[g][...], preferred_element_type=F32)
        alpha = a_refs[g][...]
        acc_refs[g][...] = alpha * acc_refs[g][...] + pv[:V_DIM, :]
        l_refs[g][...] = alpha * l_refs[g][...] + pv[V_DIM:V_DIM + 1, :]

    def reset():
        for g in range(ng):
            m_refs[g][...] = jnp.full(m_refs[g].shape, NEG_INF, F32)
            l_refs[g][...] = jnp.zeros(l_refs[g].shape, F32)
            acc_refs[g][...] = jnp.zeros(acc_refs[g].shape, F32)

    def finish(qi):
        o = (jnp.concatenate([r[...] for r in acc_refs], axis=-1)
             / jnp.concatenate([r[...] for r in l_refs], axis=-1))
        o = o[:, :tq] - lam * o[:, tq:]
        o = o * lax.rsqrt(jnp.mean(o * o, axis=0, keepdims=True) + RMS_EPS) * g_ref[...]
        row = pl.multiple_of(qi * tq, tq)
        o_ref[pl.ds(row, tq), :] = (o * (1.0 - LAMBDA_INIT)).T.astype(BF16)

    def task(t, slot):
        qi, j = tqi_ref[t], tj_ref[t]
        nqi, nj = tqi_ref[t + 1], tj_ref[t + 1]

        def run(masked):
            for g in range(ng):
                scores(nqi, nj, 1 - slot, g)
                update(j, slot, g, masked)

        @pl.when(tlast_ref[t] == 0)
        def _():
            run(False)

        @pl.when(tlast_ref[t] == 1)
        def _():
            run(True)
            finish(qi)
            reset()

    reset()
    for g in range(ng):
        scores(0, 0, 0, g)

    def pair(i, carry):
        task(2 * i, 0)
        task(2 * i + 1, 1)
        return carry

    lax.fori_loop(0, n_tasks // 2, pair, 0)


def _diff_attention(qt, k, vt, lq1, lk1, lq2, lk2, subln_g, batch, seq):
    T, D = k.shape
    tq = ATTN_TQ
    nq = seq // tq
    tqi, tj, tlast, n_tasks = _attn_tasks(nq)
    lam_spec = pl.BlockSpec((1, HEAD_DIM), lambda b, h, *_: (0, 0))
    rows_spec = pl.BlockSpec((seq, V_DIM), lambda b, h, *_: (b, h))
    cols_spec = pl.BlockSpec((V_DIM, seq), lambda b, h, *_: (h, b))
    qg = ATTN_QG
    per_group = lambda shape, dtype: [pltpu.VMEM(shape, dtype) for _ in range(2 * tq // qg)]
    grid_spec = pltpu.PrefetchScalarGridSpec(
        num_scalar_prefetch=3,
        grid=(batch, N_HEADS),
        in_specs=[lam_spec, lam_spec, lam_spec, lam_spec,
                  pl.BlockSpec((V_DIM, 1), lambda b, h, *_: (0, 0)),
                  cols_spec, rows_spec, cols_spec],
        out_specs=rows_spec,
        scratch_shapes=(
            [pltpu.VMEM((V_DIM, 2 * seq), BF16),
             pltpu.VMEM((V_DIM + ONES_ROWS, seq), BF16)]
            + per_group((tq, qg), F32)
            + per_group((tq, qg), F32)
            + per_group((1, qg), F32)
            + per_group((1, qg), F32)
            + per_group((tq, qg), BF16)
            + per_group((1, qg), F32)
            + per_group((1, qg), F32)
            + per_group((1, qg), F32)
            + per_group((V_DIM, qg), F32)),
    )
    return pl.pallas_call(
        partial(_attn_kernel, n_tasks=n_tasks),
        out_shape=jax.ShapeDtypeStruct((T, D), BF16),
        grid_spec=grid_spec,
        compiler_params=_cparams("arbitrary", "arbitrary"),
        name="diff_attention",
    )(tqi, tj, tlast, lq1, lk1, lq2, lk2, subln_g.reshape(V_DIM, 1), qt, k, vt)


def _mixer_kernel(x_ref, o_ref, u_ref, up_ref, ga_ref, gb_ref, g1_ref, sc2_ref, sh2_ref,
                  wao_ref, cw_ref, cb_ref, clg_ref, clb_ref, wpw_ref, bpw_ref, wout_ref,
                  ln1g_ref, ln1b_ref, wrh_ref, wrl_ref,
                  x1_ref, h2_ref, lg_ref, ubuf_ref, ush_ref, conv_ref, *, tiles_per_seq):
    tm = MIX_TM
    first = (pl.program_id(0) % tiles_per_seq) == 0

    prev = up_ref[...].astype(F32)
    ubuf_ref[0:CONV_HALO, :] = jnp.where(first, jnp.zeros_like(prev), prev)
    ubuf_ref[CONV_HALO:, :] = u_ref[...].astype(F32)
    base = CONV_HALO - (CONV_K - 1)
    n_shift = tm + CONV_HALO - SUBLANES
    for r in range(1, SUBLANES):
        ush_ref[r - 1] = ubuf_ref[r:r + n_shift, :]
    for c in range(D_MODEL // LANES):
        cl = slice(c * LANES, (c + 1) * LANES)
        for r0 in range(0, tm, CONV_ROWS):
            acc = jnp.zeros((CONV_ROWS, LANES), F32)
            for j in range(CONV_K):
                a, r = divmod(base + j, SUBLANES)
                src = ubuf_ref if r == 0 else ush_ref.at[r - 1]
                lo = SUBLANES * a + r0
                acc = acc + cw_ref[j:j + 1, cl] * src[lo:lo + CONV_ROWS, cl]
            conv_ref[r0:r0 + CONV_ROWS, cl] = acc
    yc = _silu(_layer_norm(conv_ref[...] + cb_ref[...], clg_ref[...], clb_ref[...]))
    y_conv = jnp.dot(yc.astype(BF16), wpw_ref[...], preferred_element_type=F32) + bpw_ref[...]

    y_attn = jnp.dot(o_ref[...], wao_ref[...], preferred_element_type=F32)
    mix = ga_ref[...].astype(F32) * y_attn + gb_ref[...].astype(F32) * y_conv
    y = jnp.dot(mix.astype(BF16), wout_ref[...], preferred_element_type=F32)
    x1 = _layer_norm(DN_ALPHA * x_ref[...] + g1_ref[0] * y, ln1g_ref[...], ln1b_ref[...])
    x1_ref[...] = x1

    h2 = x1 * (1.0 + sc2_ref[0]) + sh2_ref[0]
    h2_hi = h2.astype(BF16)
    _store_packed(h2_ref, h2)
    h2_lo = (h2 - h2_hi.astype(F32)).astype(BF16)
    nt = lambda w, a: lax.dot_general(w, a, (((1,), (1,)), ((), ())), preferred_element_type=F32)
    lg_ref[...] = (nt(wrh_ref[...], h2_hi)
                   + (nt(wrl_ref[...], h2_hi) + nt(wrh_ref[...], h2_lo)))


def _mixer_epilogue(x2, o, u, ga, gb, g1, sc2, sh2, wao, cw, cb, clg, clb, wpw, bpw, wout,
                    ln1g, ln1b, wr_hi, wr_lo, seq):
    T, D = x2.shape
    tm = MIX_TM
    tiles_per_seq = seq // tm
    halo_per_tile = tm // CONV_HALO
    tok = pl.BlockSpec((tm, D), lambda i: (i, 0))
    prev = pl.BlockSpec((CONV_HALO, D), lambda i: (jnp.maximum(i * halo_per_tile - 1, 0), 0))
    mod = pl.BlockSpec((1, 1, D), lambda i: (i // tiles_per_seq, 0, 0))
    row = pl.BlockSpec((1, D), lambda i: (0, 0))
    mat = pl.BlockSpec((D, D), lambda i: (0, 0))
    rw = pl.BlockSpec((N_EXPERTS, D), lambda i: (0, 0))
    return pl.pallas_call(
        partial(_mixer_kernel, tiles_per_seq=tiles_per_seq),
        out_shape=[jax.ShapeDtypeStruct((T, D), F32),
                   jax.ShapeDtypeStruct((2, T, D // 4), U32),
                   jax.ShapeDtypeStruct((N_EXPERTS, T), F32)],
        grid=(T // tm,),
        in_specs=[tok, tok, tok, prev, tok, tok, mod, mod, mod,
                  mat, pl.BlockSpec((CONV_K, D), lambda i: (0, 0)), row, row, row, mat, row, mat,
                  row, row, rw, rw],
        out_specs=[tok, pl.BlockSpec((2, tm, D // 4), lambda i: (0, i, 0)),
                   pl.BlockSpec((N_EXPERTS, tm), lambda i: (0, i))],
        scratch_shapes=[pltpu.VMEM((CONV_HALO + tm, D), F32),
                        pltpu.VMEM((SUBLANES - 1, tm + CONV_HALO - SUBLANES, D), F32),
                        pltpu.VMEM((tm, D), F32)],
        compiler_params=_cparams("arbitrary"),
        name="mixer_epilogue",
    )(x2, o, u, u, ga, gb, g1, sc2, sh2, wao, cw, cb, clg, clb, wpw, bpw, wout,
      ln1g, ln1b, wr_hi, wr_lo)


def _first_argmax(v, eid):
    m = jnp.max(v, axis=0, keepdims=True)
    i = jnp.min(jnp.where(v == m, eid, float(N_EXPERTS)), axis=0, keepdims=True)
    return m, i


def _route_kernel(lg_ref, rb_ref, idx_ref, gate_ref, cnt_ref):
    tm = ROUTE_TM
    scores = _sigmoid(lg_ref[...])
    choice = scores + rb_ref[...]
    eid = lax.broadcasted_iota(jnp.int32, (N_EXPERTS, tm), 0).astype(F32)
    rows = lambda a, g: a[g * GROUP_SIZE:(g + 1) * GROUP_SIZE, :]

    gs = []
    for g in range(N_GROUPS):
        cg = rows(choice, g)
        eg = (lax.broadcasted_iota(jnp.int32, cg.shape, 0) + g * GROUP_SIZE).astype(F32)
        m1, i1 = _first_argmax(cg, eg)
        m2 = jnp.max(jnp.where(eg == i1, NEG_INF, cg), axis=0, keepdims=True)
        gs.append(jnp.broadcast_to(m1 + m2, (SUBLANES, tm)))

    kept = []
    for g in range(N_GROUPS):
        ahead = jnp.zeros((SUBLANES, tm), F32)
        for o in range(N_GROUPS):
            if o == g:
                continue
            beats = (gs[o] >= gs[g]) if o < g else (gs[o] > gs[g])
            ahead = ahead + jnp.where(beats, 1.0, 0.0)
        cap = jnp.where(ahead < TOPK_GROUPS, float("inf"), NEG_INF)
        for r in range(g * GROUP_SIZE, (g + 1) * GROUP_SIZE, SUBLANES):
            kept.append(jnp.minimum(choice[r:r + SUBLANES, :], cap))
    masked = jnp.concatenate(kept, axis=0)

    sel = jnp.zeros((N_EXPERTS, tm), F32)
    total = jnp.zeros((1, tm), F32)
    picked = []
    for k in range(TOP_K):
        _, i = _first_argmax(masked, eid)
        hit = eid == i
        sc = jnp.sum(jnp.where(hit, scores, 0.0), axis=0, keepdims=True)
        masked = jnp.where(hit, NEG_INF, masked)
        sel = sel + hit.astype(F32)
        total = total + sc
        idx_ref[k:k + 1, :] = i.astype(jnp.int32)
        picked.append(sc)
    for k in range(TOP_K):
        gate_ref[k:k + 1, :] = picked[k] / total * ROUTED_SCALE

    @pl.when(pl.program_id(0) == 0)
    def _():
        cnt_ref[...] = jnp.zeros_like(cnt_ref)
    cnt_ref[...] += jnp.sum(sel, axis=1, keepdims=True)


def _route(logits_t, router_bias):
    T = logits_t.shape[1]
    tm = ROUTE_TM
    return pl.pallas_call(
        _route_kernel,
        out_shape=[jax.ShapeDtypeStruct((TOP_K, T), jnp.int32),
                   jax.ShapeDtypeStruct((TOP_K, T), F32),
                   jax.ShapeDtypeStruct((N_EXPERTS, 1), F32)],
        grid=(T // tm,),
        in_specs=[pl.BlockSpec((N_EXPERTS, tm), lambda i: (0, i)),
                  pl.BlockSpec((N_EXPERTS, 1), lambda i: (0, 0))],
        out_specs=[pl.BlockSpec((TOP_K, tm), lambda i: (0, i)),
                   pl.BlockSpec((TOP_K, tm), lambda i: (0, i)),
                   pl.BlockSpec((N_EXPERTS, 1), lambda i: (0, 0))],
        compiler_params=_cparams("arbitrary"),
        name="route_topk",
    )(logits_t, router_bias.reshape(N_EXPERTS, 1))


def _slot_kernel(idx_ref, start_ref, pos_ref, carry_ref):
    tm = ROUTE_TM

    @pl.when(pl.program_id(0) == 0)
    def _():
        carry_ref[...] = jnp.zeros_like(carry_ref)

    idx = idx_ref[...]
    eid = lax.broadcasted_iota(jnp.int32, (N_EXPERTS, tm), 0)
    hits = [eid == idx[k:k + 1, :] for k in range(TOP_K)]
    sel = jnp.zeros((N_EXPERTS, tm), F32)
    for h in hits:
        sel = sel + h.astype(F32)
    r = lax.broadcasted_iota(jnp.int32, (tm, tm), 0)
    c = lax.broadcasted_iota(jnp.int32, (tm, tm), 1)
    earlier = (r < c).astype(BF16)
    rank = jnp.dot(sel.astype(BF16), earlier, preferred_element_type=F32)
    slot = rank + carry_ref[...] + start_ref[...]
    for k, h in enumerate(hits):
        pos_ref[k:k + 1, :] = jnp.sum(jnp.where(h, slot, 0.0), axis=0,
                                      keepdims=True).astype(jnp.int32)
    carry_ref[...] += jnp.sum(sel, axis=1, keepdims=True)


def _assign_slots(idx_t, pad_start):
    T = idx_t.shape[1]
    tm = ROUTE_TM
    return pl.pallas_call(
        _slot_kernel,
        out_shape=jax.ShapeDtypeStruct((TOP_K, T), jnp.int32),
        grid=(T // tm,),
        in_specs=[pl.BlockSpec((TOP_K, tm), lambda i: (0, i)),
                  pl.BlockSpec((N_EXPERTS, 1), lambda i: (0, 0))],
        out_specs=pl.BlockSpec((TOP_K, tm), lambda i: (0, i)),
        scratch_shapes=[pltpu.VMEM((N_EXPERTS, 1), F32)],
        compiler_params=_cparams("arbitrary"),
        name="assign_slots",
    )(idx_t, pad_start)


def _dot_halves(lo, hi, w_ref):
    half = lo.shape[-1]
    return (jnp.dot(lo.astype(BF16), w_ref[:half, :].astype(BF16), preferred_element_type=F32)
            + jnp.dot(hi.astype(BF16), w_ref[half:, :].astype(BF16), preferred_element_type=F32))


def _expert_kernel(be_ref, nu_ref, first_ref, slot_ref, nxt_ref, xs_ref, w1_hbm, w3_hbm, w2_hbm,
                   ys_ref, w1_buf, w3_buf, w2_buf, sem):
    b = pl.program_id(0)

    def weight_copies(expert, slot):
        pairs = ((w1_hbm, w1_buf), (w3_hbm, w3_buf), (w2_hbm, w2_buf))
        return [pltpu.make_async_copy(hbm.at[expert], buf.at[slot], sem.at[slot, i])
                for i, (hbm, buf) in enumerate(pairs)]

    @pl.when(b < nu_ref[0])
    def _():
        slot = slot_ref[b]

        @pl.when(b == 0)
        def _():
            for c in weight_copies(be_ref[0], 0):
                c.start()

        @pl.when(first_ref[b] == 1)
        def _():
            for c in weight_copies(be_ref[b], slot):
                c.wait()

            @pl.when(nxt_ref[b] >= 0)
            def _():
                for c in weight_copies(nxt_ref[b], 1 - slot):
                    c.start()

        lo, hi = _load_packed(xs_ref)
        half = lo.shape[-1]
        mm = lambda w: (jnp.dot(lo, w[slot, :half, :], preferred_element_type=F32)
                        + jnp.dot(hi, w[slot, half:, :], preferred_element_type=F32))
        hb = _silu(mm(w1_buf)) * mm(w3_buf)
        _store_packed(ys_ref, jnp.dot(hb, w2_buf[slot], preferred_element_type=F32))

    @pl.when(b >= nu_ref[0])
    def _():
        ys_ref[...] = jnp.zeros_like(ys_ref)


def _routed_experts(xs, blk_expert, n_used, w1, w3, w2):
    _, P, Dq = xs.shape
    blk = EXPERT_BLK
    n_blocks = P // blk
    D, F = w1.shape[-2:]

    b_ids = jnp.arange(n_blocks, dtype=jnp.int32)
    used = b_ids < n_used[0]
    prev = jnp.concatenate([jnp.full((1,), -1, jnp.int32), blk_expert[:-1]])
    first = (used & (blk_expert != prev)).astype(jnp.int32)
    slot = (jnp.cumsum(first) - 1) % 2
    start_at = jnp.where(first == 1, b_ids, n_blocks)
    later = jnp.concatenate([start_at[1:], jnp.full((1,), n_blocks, jnp.int32)])
    next_start = lax.cummin(later, axis=0, reverse=True)
    nxt = jnp.where(next_start < n_blocks,
                    blk_expert[jnp.minimum(next_start, n_blocks - 1)], -1).astype(jnp.int32)

    row_blk = lambda b, be, nu, *_: (0, jnp.minimum(b, nu[0] - 1), 0)
    grid_spec = pltpu.PrefetchScalarGridSpec(
        num_scalar_prefetch=5,
        grid=(n_blocks,),
        in_specs=[pl.BlockSpec((2, blk, Dq), row_blk),
                  pl.BlockSpec(memory_space=pl.ANY),
                  pl.BlockSpec(memory_space=pl.ANY),
                  pl.BlockSpec(memory_space=pl.ANY)],
        out_specs=pl.BlockSpec((2, blk, Dq), lambda b, *_: (0, b, 0)),
        scratch_shapes=[pltpu.VMEM((2, D, F), F32), pltpu.VMEM((2, D, F), F32),
                        pltpu.VMEM((2, F, D), F32), pltpu.SemaphoreType.DMA((2, 3))],
    )
    return pl.pallas_call(
        _expert_kernel,
        out_shape=jax.ShapeDtypeStruct((2, P, Dq), U32),
        grid_spec=grid_spec,
        compiler_params=_cparams("arbitrary"),
        name="routed_experts",
    )(blk_expert, n_used, first, slot.astype(jnp.int32), nxt, xs, w1, w3, w2)


def _sc_mesh():
    return plsc.VectorSubcoreMesh(core_axis_name="c", subcore_axis_name="s")


def _sc_dispatch(rows, slots_t, n_slots):
    T, W = rows.shape
    K = slots_t.shape[0]
    win = SC_WINDOW

    @pl.kernel(out_type=jax.ShapeDtypeStruct((n_slots, W), rows.dtype), mesh=_sc_mesh(),
               scratch_types=[])
    def dispatch(x_hbm, i_hbm, o_hbm):
        def body(x_vmem, i_vmem):
            for k in range(K):
                pltpu.sync_copy(x_vmem, o_hbm.at[i_vmem.at[k]])

        pltpu.emit_pipeline(
            body,
            grid=(T // win,),
            in_specs=[pl.BlockSpec((win, W), lambda i: (i, 0)),
                      pl.BlockSpec((K, win), lambda i: (0, i))],
            out_specs=[],
            core_axis_name=("c", "s"),
            dimension_semantics=(pltpu.PARALLEL,),
        )(x_hbm, i_hbm)

    return dispatch(rows, slots_t)


def _sc_gather(table, idx):
    W = table.shape[1]
    N = idx.shape[1]
    win = SC_WINDOW

    @pl.kernel(out_type=jax.ShapeDtypeStruct((N, W), table.dtype), mesh=_sc_mesh(),
               scratch_types=[])
    def gather(x_hbm, i_hbm, o_hbm):
        def body(i_vmem, o_vmem):
            pltpu.sync_copy(x_hbm.at[i_vmem.at[0]], o_vmem)

        pltpu.emit_pipeline(
            body,
            grid=(N // win,),
            in_specs=[pl.BlockSpec((1, win), lambda i: (0, i))],
            out_specs=[pl.BlockSpec((win, W), lambda i: (i, 0))],
            core_axis_name=("c", "s"),
            dimension_semantics=(pltpu.PARALLEL,),
        )(i_hbm, o_hbm)

    return gather(table, idx)


def _final_kernel(x1_ref, h2_ref, yg_ref, gate_ref, g2_ref, ws1_ref, ws3_ref, ws2_ref,
                  lng_ref, lnb_ref, o_ref):
    Dh = D_MODEL // 2
    gate = gate_ref[...]
    routed_lo = jnp.zeros((x1_ref.shape[0], Dh), F32)
    routed_hi = jnp.zeros((x1_ref.shape[0], Dh), F32)
    for k in range(TOP_K):
        lo, hi = _load_packed(yg_ref.at[k])
        routed_lo = routed_lo + lo * gate[:, k:k + 1]
        routed_hi = routed_hi + hi * gate[:, k:k + 1]
    lo, hi = _load_packed(h2_ref)
    a = _dot_halves(lo, hi, ws1_ref)
    b = _dot_halves(lo, hi, ws3_ref)
    shared = jnp.dot((_silu(a) * b).astype(BF16), ws2_ref[...], preferred_element_type=F32)
    y2 = jnp.concatenate([routed_lo, routed_hi], axis=-1) + shared
    o_ref[...] = _layer_norm(DN_ALPHA * x1_ref[...] + g2_ref[0] * y2, lng_ref[...], lnb_ref[...])


def _final(x1, h2, yg, gate, g2, ws1, ws3, ws2, ln2g, ln2b, seq):
    T, D = x1.shape
    tm = FINAL_TM
    tiles_per_seq = seq // tm
    F = ws1.shape[-1]
    tok = pl.BlockSpec((tm, D), lambda i: (i, 0))
    row = pl.BlockSpec((1, D), lambda i: (0, 0))
    return pl.pallas_call(
        _final_kernel,
        out_shape=jax.ShapeDtypeStruct((T, D), F32),
        grid=(T // tm,),
        in_specs=[tok, pl.BlockSpec((2, tm, D // 4), lambda i: (0, i, 0)),
                  pl.BlockSpec((TOP_K, 2, tm, D // 4), lambda i: (0, 0, i, 0)),
                  pl.BlockSpec((tm, TOP_K), lambda i: (i, 0)),
                  pl.BlockSpec((1, 1, D), lambda i: (i // tiles_per_seq, 0, 0)),
                  pl.BlockSpec((D, F), lambda i: (0, 0)),
                  pl.BlockSpec((D, F), lambda i: (0, 0)),
                  pl.BlockSpec((F, D), lambda i: (0, 0)),
                  row, row],
        out_specs=tok,
        compiler_params=_cparams("arbitrary"),
        name="moe_combine_final",
    )(x1, h2, yg, gate, g2, ws1, ws3, ws2, ln2g, ln2b)


def kernel(x, c, ada_w, ada_b, w_in, b_in, lambda_q1, lambda_k1, lambda_q2, lambda_k2, subln_g, w_attn_o, conv_w, conv_b, conv_ln_g, conv_ln_b, conv_pw_w, conv_pw_b, w_out, ln1_g, ln1_b, router_w, router_bias, w1, w3, w2, ws1, ws3, ws2, ln2_g, ln2_b):
    B, S, D = x.shape
    T = B * S
    l = 0
    x2 = x.reshape(T, D)
    row = lambda a: a.reshape(1, -1)

    mod = _ada_modulation(c, ada_w[l], ada_b[l])
    sh1, sc1, g1, sh2, sc2, g2 = [m.reshape(B, 1, D) for m in jnp.split(mod, 6, axis=-1)]

    q, k, v, u, ga, gb = _in_projection(x2, sc1, sh1, w_in[l].astype(BF16), row(b_in[l]), S)
    o = _diff_attention(q.T, k, v.T, row(lambda_q1[l]), row(lambda_k1[l]), row(lambda_q2[l]),
                        row(lambda_k2[l]), row(subln_g[l]), B, S)

    wr = router_w[l].T
    wr_hi = wr.astype(BF16)
    wr_lo = (wr - wr_hi.astype(F32)).astype(BF16)
    x1, h2, logits = _mixer_epilogue(
        x2, o, u, ga, gb, g1, sc2, sh2, w_attn_o[l].astype(BF16), conv_w[l], row(conv_b[l]),
        row(conv_ln_g[l]), row(conv_ln_b[l]), conv_pw_w[l].astype(BF16), row(conv_pw_b[l]),
        w_out[l].astype(BF16), row(ln1_g[l]), row(ln1_b[l]), wr_hi, wr_lo, S)

    idx, gate, counts = _route(logits, router_bias[l])

    blk = EXPERT_BLK
    cnt = counts.reshape(N_EXPERTS).astype(jnp.int32)
    padded = (cnt + blk - 1) // blk * blk
    pad_end = jnp.cumsum(padded)
    pad_start = pad_end - padded
    P = T * TOP_K + N_EXPERTS * blk
    n_blocks = P // blk
    blk_first_row = jnp.arange(n_blocks, dtype=jnp.int32) * blk
    blk_expert = jnp.minimum(
        jnp.sum((pad_end[None, :] <= blk_first_row[:, None]).astype(jnp.int32), axis=1),
        N_EXPERTS - 1)
    n_used = (pad_end[-1:] // blk).astype(jnp.int32)

    pos = _assign_slots(idx, pad_start.astype(F32).reshape(N_EXPERTS, 1))

    Dq = D // 4
    half_base = (jnp.arange(2, dtype=jnp.int32) * P)[None, :, None]
    slots = pos[:, None, :] + half_base
    xs = _sc_dispatch(h2.reshape(2 * T, Dq), slots.reshape(TOP_K, 2 * T), 2 * P)
    ys = _routed_experts(xs.reshape(2, P, Dq), blk_expert, n_used, w1[l], w3[l], w2[l])
    yg = _sc_gather(ys.reshape(2 * P, Dq), slots.reshape(1, TOP_K * 2 * T))
    yg = yg.reshape(TOP_K, 2, T, Dq)

    out = _final(x1, h2, yg, gate.T, g2, ws1[l].astype(BF16), ws3[l].astype(BF16),
                 ws2[l].astype(BF16), row(ln2_g[l]), row(ln2_b[l]), S)
    return out.reshape(B, S, D)
```

```python
import math
from functools import partial

import jax
import jax.numpy as jnp
from jax import lax
from jax.experimental import pallas as pl
from jax.experimental.pallas import tpu as pltpu
from jax.experimental.pallas import tpu_sc as plsc

D_MODEL = 1024
N_HEADS = 8
HEAD_DIM = 64
V_DIM = 128
CONV_K = 31
N_EXPERTS = 256
TOP_K = 8
N_GROUPS = 8
GROUP_SIZE = N_EXPERTS // N_GROUPS
TOPK_GROUPS = 4
D_EXPERT = 256
ROUTED_SCALE = 2.5
LN_EPS = 1e-5
RMS_EPS = 1e-5
DEPTH = 1
DN_ALPHA = (2.0 * DEPTH) ** 0.25
LAMBDA_INIT = 0.8 - 0.6 * math.exp(-0.3 * 0)
Q_SCALE = HEAD_DIM ** -0.5 * math.log2(math.e)

LANES = 128
SUBLANES = 8
VMEM_LIMIT_BYTES = 56 * 1024 * 1024

PROJ_TM = 512
ATTN_TQ = 512
ATTN_QG = 256
ATTN_KC = 256
ONES_ROWS = 16
MIX_TM = 256
CONV_HALO = 32
CONV_ROWS = 64
ROUTE_TM = 512
EXPERT_BLK = 512
FINAL_TM = 256
SC_WINDOW = 128

F32 = jnp.float32
BF16 = jnp.bfloat16
U32 = jnp.uint32
NEG_INF = float("-inf")


def _cparams(*sem):
    return pltpu.CompilerParams(dimension_semantics=sem, vmem_limit_bytes=VMEM_LIMIT_BYTES)


def _sigmoid(x):
    return 1.0 / (1.0 + jnp.exp(-x))


def _silu(x):
    return x * _sigmoid(x)


def _pack_halves(x):
    half = x.shape[-1] // 2
    bits = lambda v: lax.bitcast_convert_type(v.astype(BF16).astype(F32), U32)
    return (bits(x[:, :half]) >> 16) | (bits(x[:, half:]) & jnp.uint32(0xFFFF0000))


def _unpack_halves(w):
    lo = lax.bitcast_convert_type(w << 16, F32)
    hi = lax.bitcast_convert_type(w & jnp.uint32(0xFFFF0000), F32)
    return lo, hi


def _store_packed(ref, x):
    w = _pack_halves(x)
    q = w.shape[-1] // 2
    ref[0] = w[:, :q]
    ref[1] = w[:, q:]


def _load_packed(ref):
    return _unpack_halves(jnp.concatenate([ref[0], ref[1]], axis=-1))


def _layer_norm(r, g, b):
    mu = jnp.mean(r, axis=-1, keepdims=True)
    d = r - mu
    var = jnp.mean(d * d, axis=-1, keepdims=True)
    return d * lax.rsqrt(var + LN_EPS) * g + b


def _ada_kernel(c_ref, w_ref, b_ref, o_ref):
    a = _silu(c_ref[...])
    o_ref[...] = jnp.dot(a, w_ref[...], preferred_element_type=F32,
                         precision=lax.Precision.HIGHEST) + b_ref[...]


def _ada_modulation(c, w, b):
    B, D = c.shape
    n = w.shape[1] // D
    return pl.pallas_call(
        _ada_kernel,
        out_shape=jax.ShapeDtypeStruct((B, n * D), F32),
        grid=(n,),
        in_specs=[pl.BlockSpec((B, D), lambda j: (0, 0)),
                  pl.BlockSpec((D, D), lambda j: (0, j)),
                  pl.BlockSpec((1, D), lambda j: (0, j))],
        out_specs=pl.BlockSpec((B, D), lambda j: (0, j)),
        compiler_params=_cparams("arbitrary"),
        name="ada_modulation",
    )(c, w, b.reshape(1, n * D))


def _inproj_kernel(x_ref, sc_ref, sh_ref, w_ref, b_ref,
                   q_ref, k_ref, v_ref, u_ref, ga_ref, gb_ref):
    D = D_MODEL
    h = (x_ref[...] * (1.0 + sc_ref[0]) + sh_ref[0]).astype(BF16)

    def proj(j):
        return (jnp.dot(h, w_ref[:, j * D:(j + 1) * D], preferred_element_type=F32)
                + b_ref[:, j * D:(j + 1) * D])

    q_ref[...] = (proj(0) * Q_SCALE).astype(BF16)
    k_ref[...] = proj(1).astype(BF16)
    v_ref[...] = proj(2).astype(BF16)
    u_ref[...] = (proj(3) * _sigmoid(proj(4))).astype(BF16)
    ga_ref[...] = _sigmoid(proj(5)).astype(BF16)
    gb_ref[...] = _sigmoid(proj(6)).astype(BF16)


def _in_projection(x2, sc1, sh1, w_in, b_in, seq):
    T, D = x2.shape
    tm = PROJ_TM
    tiles_per_seq = seq // tm
    ncols = w_in.shape[1]
    tok = pl.BlockSpec((tm, D), lambda i: (i, 0))
    mod = pl.BlockSpec((1, 1, D), lambda i: (i // tiles_per_seq, 0, 0))
    out = jax.ShapeDtypeStruct((T, D), BF16)
    return pl.pallas_call(
        _inproj_kernel,
        out_shape=[out] * 6,
        grid=(T // tm,),
        in_specs=[tok, mod, mod,
                  pl.BlockSpec((D, ncols), lambda i: (0, 0)),
                  pl.BlockSpec((1, ncols), lambda i: (0, 0))],
        out_specs=[tok] * 6,
        compiler_params=_cparams("arbitrary"),
        name="in_projection",
    )(x2, sc1, sh1, w_in, b_in)


def _attn_tasks(nq):
    tasks = [(qi, j, int(j == qi)) for qi in range(nq) for j in range(qi + 1)]
    if len(tasks) % 2:
        raise ValueError("the two-slot task loop needs an even number of tile pairs")
    qis, js, lasts = zip(*(tasks + [(0, 0, 0)]))
    as_i32 = lambda v: jnp.asarray(v, jnp.int32)
    return as_i32(qis), as_i32(js), as_i32(lasts), len(tasks)


def _attn_kernel(tqi_ref, tj_ref, tlast_ref, lq1_ref, lk1_ref, lq2_ref, lk2_ref, g_ref,
                 qt_ref, k_ref, vt_ref, o_ref, q2_ref, vx_ref, *scratch, n_tasks):
    tq = ATTN_TQ
    qg = ATTN_QG
    ng = 2 * tq // qg
    groups = [scratch[i * ng:(i + 1) * ng] for i in range(7)]
    s_slots, x_slots = groups[0:2], groups[2:4]
    m_refs, l_refs, acc_refs = groups[4:]
    seq = k_ref.shape[0]

    vx_ref[:V_DIM, :] = vt_ref[...]
    vx_ref[V_DIM:, :] = jnp.ones((ONES_ROWS, seq), BF16)

    lam = (jnp.exp(jnp.sum(lq1_ref[...] * lk1_ref[...], axis=-1, keepdims=True))
           - jnp.exp(jnp.sum(lq2_ref[...] * lk2_ref[...], axis=-1, keepdims=True))
           + LAMBDA_INIT)

    for i in range(seq // tq):
        qt = qt_ref[:, i * tq:(i + 1) * tq]
        feat = lax.broadcasted_iota(jnp.int32, qt.shape, 0)
        zero = jnp.zeros_like(qt)
        q2_ref[:, 2 * tq * i:2 * tq * i + tq] = jnp.where(feat < HEAD_DIM, qt, zero)
        q2_ref[:, 2 * tq * i + tq:2 * tq * (i + 1)] = jnp.where(feat >= HEAD_DIM, qt, zero)

    def scores(qi, j, slot, g):
        start = pl.multiple_of(j * tq, tq)
        col = pl.multiple_of(qi * (2 * tq) + g * qg, qg)
        s = jnp.dot(k_ref[pl.ds(start, tq), :], q2_ref[:, pl.ds(col, qg)],
                    preferred_element_type=F32)
        s_slots[slot][g][...] = s
        x_slots[slot][g][...] = jnp.max(s, axis=0, keepdims=True)

    def update(j, slot, g, masked):
        start = pl.multiple_of(j * tq, tq)
        s_ref = s_slots[slot][g]

        def load(k0):
            s = s_ref[k0:k0 + ATTN_KC, :]
            if masked:
                key = lax.broadcasted_iota(jnp.int32, s.shape, 0) + k0
                qry = lax.broadcasted_iota(jnp.int32, s.shape, 1) + (g * qg) % tq
                s = jnp.where(key <= qry, s, NEG_INF)
            return s

        m_prev = m_refs[g][...]
        if masked:
            tile_max = jnp.max(load(0), axis=0, keepdims=True)
            for k0 in range(ATTN_KC, tq, ATTN_KC):
                tile_max = jnp.maximum(tile_max, jnp.max(load(k0), axis=0, keepdims=True))
        else:
            tile_max = x_slots[slot][g][...]
        m_new = jnp.maximum(m_prev, tile_max)
        alpha = jnp.exp2(m_prev - m_new)
        m_refs[g][...] = m_new
        pv = None
        for k0 in range(0, tq, ATTN_KC):
            p = jnp.exp2((load(k0) - m_new).astype(BF16))
            keys = pl.ds(pl.multiple_of(start + k0, ATTN_KC), ATTN_KC)
            part = jnp.dot(vx_ref[:, keys], p, preferred_element_type=F32)
            pv = part if pv is None else pv + part
        acc_refs[g][...] = alpha * acc_refs[g][...] + pv[:V_DIM, :]
        l_refs[g][...] = alpha * l_refs[g][...] + pv[V_DIM:V_DIM + 1, :]

    def reset():
        for g in range(ng):
            m_refs[g][...] = jnp.full(m_refs[g].shape, NEG_INF, F32)
            l_refs[g][...] = jnp.zeros(l_refs[g].shape, F32)
            acc_refs[g][...] = jnp.zeros(acc_refs[g].shape, F32)

    def finish(qi):
        o = (jnp.concatenate([r[...] for r in acc_refs], axis=-1)
             / jnp.concatenate([r[...] for r in l_refs], axis=-1))
        o = o[:, :tq] - lam * o[:, tq:]
        o = o * lax.rsqrt(jnp.mean(o * o, axis=0, keepdims=True) + RMS_EPS) * g_ref[...]
        row = pl.multiple_of(qi * tq, tq)
        o_ref[pl.ds(row, tq), :] = (o * (1.0 - LAMBDA_INIT)).T.astype(BF16)

    def task(t, slot):
        qi, j = tqi_ref[t], tj_ref[t]
        nqi, nj = tqi_ref[t + 1], tj_ref[t + 1]

        def run(masked):
            for g in range(ng):
                scores(nqi, nj, 1 - slot, g)
                update(j, slot, g, masked)

        @pl.when(tlast_ref[t] == 0)
        def _():
            run(False)

        @pl.when(tlast_ref[t] == 1)
        def _():
            run(True)
            finish(qi)
            reset()

    reset()
    for g in range(ng):
        scores(0, 0, 0, g)

    def pair(i, carry):
        task(2 * i, 0)
        task(2 * i + 1, 1)
        return carry

    lax.fori_loop(0, n_tasks // 2, pair, 0)


def _diff_attention(qt, k, vt, lq1, lk1, lq2, lk2, subln_g, batch, seq):
    T, D = k.shape
    tq = ATTN_TQ
    nq = seq // tq
    tqi, tj, tlast, n_tasks = _attn_tasks(nq)
    lam_spec = pl.BlockSpec((1, HEAD_DIM), lambda b, h, *_: (0, 0))
    rows_spec = pl.BlockSpec((seq, V_DIM), lambda b, h, *_: (b, h))
    cols_spec = pl.BlockSpec((V_DIM, seq), lambda b, h, *_: (h, b))
    qg = ATTN_QG
    per_group = lambda shape, dtype: [pltpu.VMEM(shape, dtype) for _ in range(2 * tq // qg)]
    grid_spec = pltpu.PrefetchScalarGridSpec(
        num_scalar_prefetch=3,
        grid=(batch, N_HEADS),
        in_specs=[lam_spec, lam_spec, lam_spec, lam_spec,
                  pl.BlockSpec((V_DIM, 1), lambda b, h, *_: (0, 0)),
                  cols_spec, rows_spec, cols_spec],
        out_specs=rows_spec,
        scratch_shapes=(
            [pltpu.VMEM((V_DIM, 2 * seq), BF16),
             pltpu.VMEM((V_DIM + ONES_ROWS, seq), BF16)]
            + per_group((tq, qg), F32)
            + per_group((tq, qg), F32)
            + per_group((1, qg), F32)
            + per_group((1, qg), F32)
            + per_group((1, qg), F32)
            + per_group((1, qg), F32)
            + per_group((V_DIM, qg), F32)),
    )
    return pl.pallas_call(
        partial(_attn_kernel, n_tasks=n_tasks),
        out_shape=jax.ShapeDtypeStruct((T, D), BF16),
        grid_spec=grid_spec,
        compiler_params=_cparams("arbitrary", "arbitrary"),
        name="diff_attention",
    )(tqi, tj, tlast, lq1, lk1, lq2, lk2, subln_g.reshape(V_DIM, 1), qt, k, vt)


def _mixer_kernel(x_ref, o_ref, u_ref, up_ref, ga_ref, gb_ref, g1_ref, sc2_ref, sh2_ref,
                  wao_ref, cw_ref, cb_ref, clg_ref, clb_ref, wpw_ref, bpw_ref, wout_ref,
                  ln1g_ref, ln1b_ref, wrh_ref, wrl_ref,
                  x1_ref, h2_ref, lg_ref, ubuf_ref, ush_ref, conv_ref, *, tiles_per_seq):
    tm = MIX_TM
    first = (pl.program_id(0) % tiles_per_seq) == 0

    prev = up_ref[...].astype(F32)
    ubuf_ref[0:CONV_HALO, :] = jnp.where(first, jnp.zeros_like(prev), prev)
    ubuf_ref[CONV_HALO:, :] = u_ref[...].astype(F32)
    base = CONV_HALO - (CONV_K - 1)
    n_shift = tm + CONV_HALO - SUBLANES
    for r in range(1, SUBLANES):
        ush_ref[r - 1] = ubuf_ref[r:r + n_shift, :]
    for c in range(D_MODEL // LANES):
        cl = slice(c * LANES, (c + 1) * LANES)
        for r0 in range(0, tm, CONV_ROWS):
            acc = jnp.zeros((CONV_ROWS, LANES), F32)
            for j in range(CONV_K):
                a, r = divmod(base + j, SUBLANES)
                src = ubuf_ref if r == 0 else ush_ref.at[r - 1]
                lo = SUBLANES * a + r0
                acc = acc + cw_ref[j:j + 1, cl] * src[lo:lo + CONV_ROWS, cl]
            conv_ref[r0:r0 + CONV_ROWS, cl] = acc
    yc = _silu(_layer_norm(conv_ref[...] + cb_ref[...], clg_ref[...], clb_ref[...]))
    y_conv = jnp.dot(yc.astype(BF16), wpw_ref[...], preferred_element_type=F32) + bpw_ref[...]

    y_attn = jnp.dot(o_ref[...], wao_ref[...], preferred_element_type=F32)
    mix = ga_ref[...].astype(F32) * y_attn + gb_ref[...].astype(F32) * y_conv
    y = jnp.dot(mix.astype(BF16), wout_ref[...], preferred_element_type=F32)
    x1 = _layer_norm(DN_ALPHA * x_ref[...] + g1_ref[0] * y, ln1g_ref[...], ln1b_ref[...])
    x1_ref[...] = x1

    h2 = x1 * (1.0 + sc2_ref[0]) + sh2_ref[0]
    h2_hi = h2.astype(BF16)
    _store_packed(h2_ref, h2)
    h2_lo = (h2 - h2_hi.astype(F32)).astype(BF16)
    nt = lambda w, a: lax.dot_general(w, a, (((1,), (1,)), ((), ())), preferred_element_type=F32)
    lg_ref[...] = (nt(wrh_ref[...], h2_hi)
                   + (nt(wrl_ref[...], h2_hi) + nt(wrh_ref[...], h2_lo)))


def _mixer_epilogue(x2, o, u, ga, gb, g1, sc2, sh2, wao, cw, cb, clg, clb, wpw, bpw, wout,
                    ln1g, ln1b, wr_hi, wr_lo, seq):
    T, D = x2.shape
    tm = MIX_TM
    tiles_per_seq = seq // tm
    halo_per_tile = tm // CONV_HALO
    tok = pl.BlockSpec((tm, D), lambda i: (i, 0))
    prev = pl.BlockSpec((CONV_HALO, D), lambda i: (jnp.maximum(i * halo_per_tile - 1, 0), 0))
    mod = pl.BlockSpec((1, 1, D), lambda i: (i // tiles_per_seq, 0, 0))
    row = pl.BlockSpec((1, D), lambda i: (0, 0))
    mat = pl.BlockSpec((D, D), lambda i: (0, 0))
    rw = pl.BlockSpec((N_EXPERTS, D), lambda i: (0, 0))
    return pl.pallas_call(
        partial(_mixer_kernel, tiles_per_seq=tiles_per_seq),
        out_shape=[jax.ShapeDtypeStruct((T, D), F32),
                   jax.ShapeDtypeStruct((2, T, D // 4), U32),
                   jax.ShapeDtypeStruct((N_EXPERTS, T), F32)],
        grid=(T // tm,),
        in_specs=[tok, tok, tok, prev, tok, tok, mod, mod, mod,
                  mat, pl.BlockSpec((CONV_K, D), lambda i: (0, 0)), row, row, row, mat, row, mat,
                  row, row, rw, rw],
        out_specs=[tok, pl.BlockSpec((2, tm, D // 4), lambda i: (0, i, 0)),
                   pl.BlockSpec((N_EXPERTS, tm), lambda i: (0, i))],
        scratch_shapes=[pltpu.VMEM((CONV_HALO + tm, D), F32),
                        pltpu.VMEM((SUBLANES - 1, tm + CONV_HALO - SUBLANES, D), F32),
                        pltpu.VMEM((tm, D), F32)],
        compiler_params=_cparams("arbitrary"),
        name="mixer_epilogue",
    )(x2, o, u, u, ga, gb, g1, sc2, sh2, wao, cw, cb, clg, clb, wpw, bpw, wout,
      ln1g, ln1b, wr_hi, wr_lo)


def _first_argmax(v, eid):
    m = jnp.max(v, axis=0, keepdims=True)
    i = jnp.min(jnp.where(v == m, eid, float(N_EXPERTS)), axis=0, keepdims=True)
    return m, i


def _route_kernel(lg_ref, rb_ref, idx_ref, gate_ref, cnt_ref):
    tm = ROUTE_TM
    scores = _sigmoid(lg_ref[...])
    choice = scores + rb_ref[...]
    eid = lax.broadcasted_iota(jnp.int32, (N_EXPERTS, tm), 0).astype(F32)
    rows = lambda a, g: a[g * GROUP_SIZE:(g + 1) * GROUP_SIZE, :]

    gs = []
    for g in range(N_GROUPS):
        cg = rows(choice, g)
        eg = (lax.broadcasted_iota(jnp.int32, cg.shape, 0) + g * GROUP_SIZE).astype(F32)
        m1, i1 = _first_argmax(cg, eg)
        m2 = jnp.max(jnp.where(eg == i1, NEG_INF, cg), axis=0, keepdims=True)
        gs.append(jnp.broadcast_to(m1 + m2, (SUBLANES, tm)))

    kept = []
    for g in range(N_GROUPS):
        ahead = jnp.zeros((SUBLANES, tm), F32)
        for o in range(N_GROUPS):
            if o == g:
                continue
            beats = (gs[o] >= gs[g]) if o < g else (gs[o] > gs[g])
            ahead = ahead + jnp.where(beats, 1.0, 0.0)
        cap = jnp.where(ahead < TOPK_GROUPS, float("inf"), NEG_INF)
        for r in range(g * GROUP_SIZE, (g + 1) * GROUP_SIZE, SUBLANES):
            kept.append(jnp.minimum(choice[r:r + SUBLANES, :], cap))
    masked = jnp.concatenate(kept, axis=0)

    sel = jnp.zeros((N_EXPERTS, tm), F32)
    total = jnp.zeros((1, tm), F32)
    picked = []
    for k in range(TOP_K):
        _, i = _first_argmax(masked, eid)
        hit = eid == i
        sc = jnp.sum(jnp.where(hit, scores, 0.0), axis=0, keepdims=True)
        masked = jnp.where(hit, NEG_INF, masked)
        sel = sel + hit.astype(F32)
        total = total + sc
        idx_ref[k:k + 1, :] = i.astype(jnp.int32)
        picked.append(sc)
    for k in range(TOP_K):
        gate_ref[k:k + 1, :] = picked[k] / total * ROUTED_SCALE

    @pl.when(pl.program_id(0) == 0)
    def _():
        cnt_ref[...] = jnp.zeros_like(cnt_ref)
    cnt_ref[...] += jnp.sum(sel, axis=1, keepdims=True)


def _route(logits_t, router_bias):
    T = logits_t.shape[1]
    tm = ROUTE_TM
    return pl.pallas_call(
        _route_kernel,
        out_shape=[jax.ShapeDtypeStruct((TOP_K, T), jnp.int32),
                   jax.ShapeDtypeStruct((TOP_K, T), F32),
                   jax.ShapeDtypeStruct((N_EXPERTS, 1), F32)],
        grid=(T // tm,),
        in_specs=[pl.BlockSpec((N_EXPERTS, tm), lambda i: (0, i)),
                  pl.BlockSpec((N_EXPERTS, 1), lambda i: (0, 0))],
        out_specs=[pl.BlockSpec((TOP_K, tm), lambda i: (0, i)),
                   pl.BlockSpec((TOP_K, tm), lambda i: (0, i)),
                   pl.BlockSpec((N_EXPERTS, 1), lambda i: (0, 0))],
        compiler_params=_cparams("arbitrary"),
        name="route_topk",
    )(logits_t, router_bias.reshape(N_EXPERTS, 1))


def _slot_kernel(idx_ref, start_ref, pos_ref, carry_ref):
    tm = ROUTE_TM

    @pl.when(pl.program_id(0) == 0)
    def _():
        carry_ref[...] = jnp.zeros_like(carry_ref)

    idx = idx_ref[...]
    eid = lax.broadcasted_iota(jnp.int32, (N_EXPERTS, tm), 0)
    hits = [eid == idx[k:k + 1, :] for k in range(TOP_K)]
    sel = jnp.zeros((N_EXPERTS, tm), F32)
    for h in hits:
        sel = sel + h.astype(F32)
    r = lax.broadcasted_iota(jnp.int32, (tm, tm), 0)
    c = lax.broadcasted_iota(jnp.int32, (tm, tm), 1)
    earlier = (r < c).astype(BF16)
    rank = jnp.dot(sel.astype(BF16), earlier, preferred_element_type=F32)
    slot = rank + carry_ref[...] + start_ref[...]
    for k, h in enumerate(hits):
        pos_ref[k:k + 1, :] = jnp.sum(jnp.where(h, slot, 0.0), axis=0,
                                      keepdims=True).astype(jnp.int32)
    carry_ref[...] += jnp.sum(sel, axis=1, keepdims=True)


def _assign_slots(idx_t, pad_start):
    T = idx_t.shape[1]
    tm = ROUTE_TM
    return pl.pallas_call(
        _slot_kernel,
        out_shape=jax.ShapeDtypeStruct((TOP_K, T), jnp.int32),
        grid=(T // tm,),
        in_specs=[pl.BlockSpec((TOP_K, tm), lambda i: (0, i)),
                  pl.BlockSpec((N_EXPERTS, 1), lambda i: (0, 0))],
        out_specs=pl.BlockSpec((TOP_K, tm), lambda i: (0, i)),
        scratch_shapes=[pltpu.VMEM((N_EXPERTS, 1), F32)],
        compiler_params=_cparams("arbitrary"),
        name="assign_slots",
    )(idx_t, pad_start)


def _dot_halves(lo, hi, w_ref):
    half = lo.shape[-1]
    return (jnp.dot(lo.astype(BF16), w_ref[:half, :].astype(BF16), preferred_element_type=F32)
            + jnp.dot(hi.astype(BF16), w_ref[half:, :].astype(BF16), preferred_element_type=F32))


def _expert_kernel(be_ref, nu_ref, first_ref, slot_ref, nxt_ref, xs_ref, w1_hbm, w3_hbm, w2_hbm,
                   ys_ref, w1_buf, w3_buf, w2_buf, sem):
    b = pl.program_id(0)

    def weight_copies(expert, slot):
        pairs = ((w1_hbm, w1_buf), (w3_hbm, w3_buf), (w2_hbm, w2_buf))
        return [pltpu.make_async_copy(hbm.at[expert], buf.at[slot], sem.at[slot, i])
                for i, (hbm, buf) in enumerate(pairs)]

    @pl.when(b < nu_ref[0])
    def _():
        slot = slot_ref[b]

        @pl.when(b == 0)
        def _():
            for c in weight_copies(be_ref[0], 0):
                c.start()

        @pl.when(first_ref[b] == 1)
        def _():
            for c in weight_copies(be_ref[b], slot):
                c.wait()

            @pl.when(nxt_ref[b] >= 0)
            def _():
                for c in weight_copies(nxt_ref[b], 1 - slot):
                    c.start()

        lo, hi = _load_packed(xs_ref)
        half = lo.shape[-1]
        mm = lambda w: (jnp.dot(lo, w[slot, :half, :], preferred_element_type=F32)
                        + jnp.dot(hi, w[slot, half:, :], preferred_element_type=F32))
        hb = _silu(mm(w1_buf)) * mm(w3_buf)
        _store_packed(ys_ref, jnp.dot(hb, w2_buf[slot], preferred_element_type=F32))

    @pl.when(b >= nu_ref[0])
    def _():
        ys_ref[...] = jnp.zeros_like(ys_ref)


def _routed_experts(xs, blk_expert, n_used, w1, w3, w2):
    _, P, Dq = xs.shape
    blk = EXPERT_BLK
    n_blocks = P // blk
    D, F = w1.shape[-2:]

    b_ids = jnp.arange(n_blocks, dtype=jnp.int32)
    used = b_ids < n_used[0]
    prev = jnp.concatenate([jnp.full((1,), -1, jnp.int32), blk_expert[:-1]])
    first = (used & (blk_expert != prev)).astype(jnp.int32)
    slot = (jnp.cumsum(first) - 1) % 2
    start_at = jnp.where(first == 1, b_ids, n_blocks)
    later = jnp.concatenate([start_at[1:], jnp.full((1,), n_blocks, jnp.int32)])
    next_start = lax.cummin(later, axis=0, reverse=True)
    nxt = jnp.where(next_start < n_blocks,
                    blk_expert[jnp.minimum(next_start, n_blocks - 1)], -1).astype(jnp.int32)

    row_blk = lambda b, be, nu, *_: (0, jnp.minimum(b, nu[0] - 1), 0)
    grid_spec = pltpu.PrefetchScalarGridSpec(
        num_scalar_prefetch=5,
        grid=(n_blocks,),
        in_specs=[pl.BlockSpec((2, blk, Dq), row_blk),
                  pl.BlockSpec(memory_space=pl.ANY),
                  pl.BlockSpec(memory_space=pl.ANY),
                  pl.BlockSpec(memory_space=pl.ANY)],
        out_specs=pl.BlockSpec((2, blk, Dq), lambda b, *_: (0, b, 0)),
        scratch_shapes=[pltpu.VMEM((2, D, F), F32), pltpu.VMEM((2, D, F), F32),
                        pltpu.VMEM((2, F, D), F32), pltpu.SemaphoreType.DMA((2, 3))],
    )
    return pl.pallas_call(
        _expert_kernel,
        out_shape=jax.ShapeDtypeStruct((2, P, Dq), U32),
        grid_spec=grid_spec,
        compiler_params=_cparams("arbitrary"),
        name="routed_experts",
    )(blk_expert, n_used, first, slot.astype(jnp.int32), nxt, xs, w1, w3, w2)


def _sc_mesh():
    return plsc.VectorSubcoreMesh(core_axis_name="c", subcore_axis_name="s")


def _sc_dispatch(rows, slots_t, n_slots):
    T, W = rows.shape
    K = slots_t.shape[0]
    win = SC_WINDOW

    @pl.kernel(out_type=jax.ShapeDtypeStruct((n_slots, W), rows.dtype), mesh=_sc_mesh(),
               scratch_types=[])
    def dispatch(x_hbm, i_hbm, o_hbm):
        def body(x_vmem, i_vmem):
            for k in range(K):
                pltpu.sync_copy(x_vmem, o_hbm.at[i_vmem.at[k]])

        pltpu.emit_pipeline(
            body,
            grid=(T // win,),
            in_specs=[pl.BlockSpec((win, W), lambda i: (i, 0)),
                      pl.BlockSpec((K, win), lambda i: (0, i))],
            out_specs=[],
            core_axis_name=("c", "s"),
            dimension_semantics=(pltpu.PARALLEL,),
        )(x_hbm, i_hbm)

    return dispatch(rows, slots_t)


def _sc_gather(table, idx):
    W = table.shape[1]
    N = idx.shape[1]
    win = SC_WINDOW

    @pl.kernel(out_type=jax.ShapeDtypeStruct((N, W), table.dtype), mesh=_sc_mesh(),
               scratch_types=[])
    def gather(x_hbm, i_hbm, o_hbm):
        def body(i_vmem, o_vmem):
            pltpu.sync_copy(x_hbm.at[i_vmem.at[0]], o_vmem)

        pltpu.emit_pipeline(
            body,
            grid=(N // win,),
            in_specs=[pl.BlockSpec((1, win), lambda i: (0, i))],
            out_specs=[pl.BlockSpec((win, W), lambda i: (i, 0))],
            core_axis_name=("c", "s"),
            dimension_semantics=(pltpu.PARALLEL,),
        )(i_hbm, o_hbm)

    return gather(table, idx)


def _final_kernel(x1_ref, h2_ref, yg_ref, gate_ref, g2_ref, ws1_ref, ws3_ref, ws2_ref,
                  lng_ref, lnb_ref, o_ref):
    Dh = D_MODEL // 2
    gate = gate_ref[...]
    routed_lo = jnp.zeros((x1_ref.shape[0], Dh), F32)
    routed_hi = jnp.zeros((x1_ref.shape[0], Dh), F32)
    for k in range(TOP_K):
        lo, hi = _load_packed(yg_ref.at[k])
        routed_lo = routed_lo + lo * gate[:, k:k + 1]
        routed_hi = routed_hi + hi * gate[:, k:k + 1]
    lo, hi = _load_packed(h2_ref)
    a = _dot_halves(lo, hi, ws1_ref)
    b = _dot_halves(lo, hi, ws3_ref)
    shared = jnp.dot((_silu(a) * b).astype(BF16), ws2_ref[...], preferred_element_type=F32)
    y2 = jnp.concatenate([routed_lo, routed_hi], axis=-1) + shared
    o_ref[...] = _layer_norm(DN_ALPHA * x1_ref[...] + g2_ref[0] * y2, lng_ref[...], lnb_ref[...])


def _final(x1, h2, yg, gate, g2, ws1, ws3, ws2, ln2g, ln2b, seq):
    T, D = x1.shape
    tm = FINAL_TM
    tiles_per_seq = seq // tm
    F = ws1.shape[-1]
    tok = pl.BlockSpec((tm, D), lambda i: (i, 0))
    row = pl.BlockSpec((1, D), lambda i: (0, 0))
    return pl.pallas_call(
        _final_kernel,
        out_shape=jax.ShapeDtypeStruct((T, D), F32),
        grid=(T // tm,),
        in_specs=[tok, pl.BlockSpec((2, tm, D // 4), lambda i: (0, i, 0)),
                  pl.BlockSpec((TOP_K, 2, tm, D // 4), lambda i: (0, 0, i, 0)),
                  pl.BlockSpec((tm, TOP_K), lambda i: (i, 0)),
                  pl.BlockSpec((1, 1, D), lambda i: (i // tiles_per_seq, 0, 0)),
                  pl.BlockSpec((D, F), lambda i: (0, 0)),
                  pl.BlockSpec((D, F), lambda i: (0, 0)),
                  pl.BlockSpec((F, D), lambda i: (0, 0)),
                  row, row],
        out_specs=tok,
        compiler_params=_cparams("arbitrary"),
        name="moe_combine_final",
    )(x1, h2, yg, gate, g2, ws1, ws3, ws2, ln2g, ln2b)


def kernel(x, c, ada_w, ada_b, w_in, b_in, lambda_q1, lambda_k1, lambda_q2, lambda_k2, subln_g, w_attn_o, conv_w, conv_b, conv_ln_g, conv_ln_b, conv_pw_w, conv_pw_b, w_out, ln1_g, ln1_b, router_w, router_bias, w1, w3, w2, ws1, ws3, ws2, ln2_g, ln2_b):
    B, S, D = x.shape
    T = B * S
    l = 0
    x2 = x.reshape(T, D)
    row = lambda a: a.reshape(1, -1)

    mod = _ada_modulation(c, ada_w[l], ada_b[l])
    sh1, sc1, g1, sh2, sc2, g2 = [m.reshape(B, 1, D) for m in jnp.split(mod, 6, axis=-1)]

    q, k, v, u, ga, gb = _in_projection(x2, sc1, sh1, w_in[l].astype(BF16), row(b_in[l]), S)
    o = _diff_attention(q.T, k, v.T, row(lambda_q1[l]), row(lambda_k1[l]), row(lambda_q2[l]),
                        row(lambda_k2[l]), row(subln_g[l]), B, S)

    wr = router_w[l].T
    wr_hi = wr.astype(BF16)
    wr_lo = (wr - wr_hi.astype(F32)).astype(BF16)
    x1, h2, logits = _mixer_epilogue(
        x2, o, u, ga, gb, g1, sc2, sh2, w_attn_o[l].astype(BF16), conv_w[l], row(conv_b[l]),
        row(conv_ln_g[l]), row(conv_ln_b[l]), conv_pw_w[l].astype(BF16), row(conv_pw_b[l]),
        w_out[l].astype(BF16), row(ln1_g[l]), row(ln1_b[l]), wr_hi, wr_lo, S)

    idx, gate, counts = _route(logits, router_bias[l])

    blk = EXPERT_BLK
    cnt = counts.reshape(N_EXPERTS).astype(jnp.int32)
    padded = (cnt + blk - 1) // blk * blk
    pad_end = jnp.cumsum(padded)
    pad_start = pad_end - padded
    P = T * TOP_K + N_EXPERTS * blk
    n_blocks = P // blk
    blk_first_row = jnp.arange(n_blocks, dtype=jnp.int32) * blk
    blk_expert = jnp.minimum(
        jnp.sum((pad_end[None, :] <= blk_first_row[:, None]).astype(jnp.int32), axis=1),
        N_EXPERTS - 1)
    n_used = (pad_end[-1:] // blk).astype(jnp.int32)

    pos = _assign_slots(idx, pad_start.astype(F32).reshape(N_EXPERTS, 1))

    Dq = D // 4
    half_base = (jnp.arange(2, dtype=jnp.int32) * P)[None, :, None]
    slots = pos[:, None, :] + half_base
    xs = _sc_dispatch(h2.reshape(2 * T, Dq), slots.reshape(TOP_K, 2 * T), 2 * P)
    ys = _routed_experts(xs.reshape(2, P, Dq), blk_expert, n_used, w1[l], w3[l], w2[l])
    yg = _sc_gather(ys.reshape(2 * P, Dq), slots.reshape(1, TOP_K * 2 * T))
    yg = yg.reshape(TOP_K, 2, T, Dq)

    out = _final(x1, h2, yg, gate.T, g2, ws1[l].astype(BF16), ws3[l].astype(BF16),
                 ws2[l].astype(BF16), row(ln2_g[l]), row(ln2_b[l]), S)
    return out.reshape(B, S, D)
```

```python
import math
from functools import partial

import jax
import jax.numpy as jnp
from jax import lax
from jax.experimental import pallas as pl
from jax.experimental.pallas import tpu as pltpu
from jax.experimental.pallas import tpu_sc as plsc

D_MODEL = 1024
N_HEADS = 8
HEAD_DIM = 64
V_DIM = 128
CONV_K = 31
N_EXPERTS = 256
TOP_K = 8
N_GROUPS = 8
GROUP_SIZE = N_EXPERTS // N_GROUPS
TOPK_GROUPS = 4
D_EXPERT = 256
ROUTED_SCALE = 2.5
LN_EPS = 1e-5
RMS_EPS = 1e-5
DEPTH = 1
DN_ALPHA = (2.0 * DEPTH) ** 0.25
LAMBDA_INIT = 0.8 - 0.6 * math.exp(-0.3 * 0)
Q_SCALE = HEAD_DIM ** -0.5 * math.log2(math.e)

LANES = 128
SUBLANES = 8
VMEM_LIMIT_BYTES = 56 * 1024 * 1024

PROJ_TM = 512
ATTN_TQ = 512
ATTN_QG = 256
ATTN_KC = 256
ONES_ROWS = 16
MIX_TM = 256
CONV_HALO = 32
CONV_ROWS = 64
ROUTE_TM = 512
EXPERT_BLK = 256
FINAL_TM = 256
SC_WINDOW = 128

F32 = jnp.float32
BF16 = jnp.bfloat16
U32 = jnp.uint32
NEG_INF = float("-inf")


def _cparams(*sem):
    return pltpu.CompilerParams(dimension_semantics=sem, vmem_limit_bytes=VMEM_LIMIT_BYTES)


def _sigmoid(x):
    return 1.0 / (1.0 + jnp.exp(-x))


def _silu(x):
    return x * _sigmoid(x)


def _pack_halves(x):
    half = x.shape[-1] // 2
    bits = lambda v: lax.bitcast_convert_type(v.astype(BF16).astype(F32), U32)
    return (bits(x[:, :half]) >> 16) | (bits(x[:, half:]) & jnp.uint32(0xFFFF0000))


def _unpack_halves(w):
    lo = lax.bitcast_convert_type(w << 16, F32)
    hi = lax.bitcast_convert_type(w & jnp.uint32(0xFFFF0000), F32)
    return lo, hi


def _store_packed(ref, x):
    w = _pack_halves(x)
    q = w.shape[-1] // 2
    ref[0] = w[:, :q]
    ref[1] = w[:, q:]


def _load_packed(ref):
    return _unpack_halves(jnp.concatenate([ref[0], ref[1]], axis=-1))


def _layer_norm(r, g, b):
    mu = jnp.mean(r, axis=-1, keepdims=True)
    d = r - mu
    var = jnp.mean(d * d, axis=-1, keepdims=True)
    return d * lax.rsqrt(var + LN_EPS) * g + b


def _ada_kernel(c_ref, w_ref, b_ref, o_ref):
    a = _silu(c_ref[...])
    o_ref[...] = jnp.dot(a, w_ref[...], preferred_element_type=F32,
                         precision=lax.Precision.HIGHEST) + b_ref[...]


def _ada_modulation(c, w, b):
    B, D = c.shape
    n = w.shape[1] // D
    return pl.pallas_call(
        _ada_kernel,
        out_shape=jax.ShapeDtypeStruct((B, n * D), F32),
        grid=(n,),
        in_specs=[pl.BlockSpec((B, D), lambda j: (0, 0)),
                  pl.BlockSpec((D, D), lambda j: (0, j)),
                  pl.BlockSpec((1, D), lambda j: (0, j))],
        out_specs=pl.BlockSpec((B, D), lambda j: (0, j)),
        compiler_params=_cparams("arbitrary"),
        name="ada_modulation",
    )(c, w, b.reshape(1, n * D))


def _inproj_kernel(x_ref, sc_ref, sh_ref, w_ref, b_ref,
                   q_ref, k_ref, v_ref, u_ref, ga_ref, gb_ref):
    D = D_MODEL
    h = (x_ref[...] * (1.0 + sc_ref[0]) + sh_ref[0]).astype(BF16)

    def proj(j):
        return (jnp.dot(h, w_ref[:, j * D:(j + 1) * D], preferred_element_type=F32)
                + b_ref[:, j * D:(j + 1) * D])

    q_ref[...] = (proj(0) * Q_SCALE).astype(BF16)
    k_ref[...] = proj(1).astype(BF16)
    v_ref[...] = proj(2).astype(BF16)
    u_ref[...] = (proj(3) * _sigmoid(proj(4))).astype(BF16)
    ga_ref[...] = _sigmoid(proj(5)).astype(BF16)
    gb_ref[...] = _sigmoid(proj(6)).astype(BF16)


def _in_projection(x2, sc1, sh1, w_in, b_in, seq):
    T, D = x2.shape
    tm = PROJ_TM
    tiles_per_seq = seq // tm
    ncols = w_in.shape[1]
    tok = pl.BlockSpec((tm, D), lambda i: (i, 0))
    mod = pl.BlockSpec((1, 1, D), lambda i: (i // tiles_per_seq, 0, 0))
    out = jax.ShapeDtypeStruct((T, D), BF16)
    return pl.pallas_call(
        _inproj_kernel,
        out_shape=[out] * 6,
        grid=(T // tm,),
        in_specs=[tok, mod, mod,
                  pl.BlockSpec((D, ncols), lambda i: (0, 0)),
                  pl.BlockSpec((1, ncols), lambda i: (0, 0))],
        out_specs=[tok] * 6,
        compiler_params=_cparams("arbitrary"),
        name="in_projection",
    )(x2, sc1, sh1, w_in, b_in)


def _attn_tasks(nq):
    tasks = [(qi, j, int(j == qi)) for qi in range(nq) for j in range(qi + 1)]
    if len(tasks) % 2:
        raise ValueError("the two-slot task loop needs an even number of tile pairs")
    qis, js, lasts = zip(*(tasks + [(0, 0, 0)]))
    as_i32 = lambda v: jnp.asarray(v, jnp.int32)
    return as_i32(qis), as_i32(js), as_i32(lasts), len(tasks)


def _attn_kernel(tqi_ref, tj_ref, tlast_ref, lq1_ref, lk1_ref, lq2_ref, lk2_ref, g_ref,
                 qt_ref, k_ref, vt_ref, o_ref, q2_ref, vx_ref, *scratch, n_tasks):
    tq = ATTN_TQ
    qg = ATTN_QG
    ng = 2 * tq // qg
    groups = [scratch[i * ng:(i + 1) * ng] for i in range(7)]
    s_slots, x_slots = groups[0:2], groups[2:4]
    m_refs, l_refs, acc_refs = groups[4:]
    seq = k_ref.shape[0]

    vx_ref[:V_DIM, :] = vt_ref[...]
    vx_ref[V_DIM:, :] = jnp.ones((ONES_ROWS, seq), BF16)

    lam = (jnp.exp(jnp.sum(lq1_ref[...] * lk1_ref[...], axis=-1, keepdims=True))
           - jnp.exp(jnp.sum(lq2_ref[...] * lk2_ref[...], axis=-1, keepdims=True))
           + LAMBDA_INIT)

    for i in range(seq // tq):
        qt = qt_ref[:, i * tq:(i + 1) * tq]
        feat = lax.broadcasted_iota(jnp.int32, qt.shape, 0)
        zero = jnp.zeros_like(qt)
        q2_ref[:, 2 * tq * i:2 * tq * i + tq] = jnp.where(feat < HEAD_DIM, qt, zero)
        q2_ref[:, 2 * tq * i + tq:2 * tq * (i + 1)] = jnp.where(feat >= HEAD_DIM, qt, zero)

    def scores(qi, j, slot, g):
        start = pl.multiple_of(j * tq, tq)
        col = pl.multiple_of(qi * (2 * tq) + g * qg, qg)
        s = jnp.dot(k_ref[pl.ds(start, tq), :], q2_ref[:, pl.ds(col, qg)],
                    preferred_element_type=F32)
        s_slots[slot][g][...] = s
        x_slots[slot][g][...] = jnp.max(s, axis=0, keepdims=True)

    def update(j, slot, g, masked):
        start = pl.multiple_of(j * tq, tq)
        s_ref = s_slots[slot][g]

        def load(k0):
            s = s_ref[k0:k0 + ATTN_KC, :]
            if masked:
                key = lax.broadcasted_iota(jnp.int32, s.shape, 0) + k0
                qry = lax.broadcasted_iota(jnp.int32, s.shape, 1) + (g * qg) % tq
                s = jnp.where(key <= qry, s, NEG_INF)
            return s

        m_prev = m_refs[g][...]
        if masked:
            tile_max = jnp.max(load(0), axis=0, keepdims=True)
            for k0 in range(ATTN_KC, tq, ATTN_KC):
                tile_max = jnp.maximum(tile_max, jnp.max(load(k0), axis=0, keepdims=True))
        else:
            tile_max = x_slots[slot][g][...]
        m_new = jnp.maximum(m_prev, tile_max)
        alpha = jnp.exp2(m_prev - m_new)
        m_refs[g][...] = m_new
        pv = None
        for k0 in range(0, tq, ATTN_KC):
            p = jnp.exp2((load(k0) - m_new).astype(BF16))
            keys = pl.ds(pl.multiple_of(start + k0, ATTN_KC), ATTN_KC)
            part = jnp.dot(vx_ref[:, keys], p, preferred_element_type=F32)
            pv = part if pv is None else pv + part
        acc_refs[g][...] = alpha * acc_refs[g][...] + pv[:V_DIM, :]
        l_refs[g][...] = alpha * l_refs[g][...] + pv[V_DIM:V_DIM + 1, :]

    def reset():
        for g in range(ng):
            m_refs[g][...] = jnp.full(m_refs[g].shape, NEG_INF, F32)
            l_refs[g][...] = jnp.zeros(l_refs[g].shape, F32)
            acc_refs[g][...] = jnp.zeros(acc_refs[g].shape, F32)

    def finish(qi):
        o = (jnp.concatenate([r[...] for r in acc_refs], axis=-1)
             / jnp.concatenate([r[...] for r in l_refs], axis=-1))
        o = o[:, :tq] - lam * o[:, tq:]
        o = o * lax.rsqrt(jnp.mean(o * o, axis=0, keepdims=True) + RMS_EPS) * g_ref[...]
        row = pl.multiple_of(qi * tq, tq)
        o_ref[pl.ds(row, tq), :] = (o * (1.0 - LAMBDA_INIT)).T.astype(BF16)

    def task(t, slot):
        qi, j = tqi_ref[t], tj_ref[t]
        nqi, nj = tqi_ref[t + 1], tj_ref[t + 1]

        def run(masked):
            for g in range(ng):
                scores(nqi, nj, 1 - slot, g)
                update(j, slot, g, masked)

        @pl.when(tlast_ref[t] == 0)
        def _():
            run(False)

        @pl.when(tlast_ref[t] == 1)
        def _():
            run(True)
            finish(qi)
            reset()

    reset()
    for g in range(ng):
        scores(0, 0, 0, g)

    def pair(i, carry):
        task(2 * i, 0)
        task(2 * i + 1, 1)
        return carry

    lax.fori_loop(0, n_tasks // 2, pair, 0)


def _diff_attention(qt, k, vt, lq1, lk1, lq2, lk2, subln_g, batch, seq):
    T, D = k.shape
    tq = ATTN_TQ
    nq = seq // tq
    tqi, tj, tlast, n_tasks = _attn_tasks(nq)
    lam_spec = pl.BlockSpec((1, HEAD_DIM), lambda b, h, *_: (0, 0))
    rows_spec = pl.BlockSpec((seq, V_DIM), lambda b, h, *_: (b, h))
    cols_spec = pl.BlockSpec((V_DIM, seq), lambda b, h, *_: (h, b))
    qg = ATTN_QG
    per_group = lambda shape, dtype: [pltpu.VMEM(shape, dtype) for _ in range(2 * tq // qg)]
    grid_spec = pltpu.PrefetchScalarGridSpec(
        num_scalar_prefetch=3,
        grid=(batch, N_HEADS),
        in_specs=[lam_spec, lam_spec, lam_spec, lam_spec,
                  pl.BlockSpec((V_DIM, 1), lambda b, h, *_: (0, 0)),
                  cols_spec, rows_spec, cols_spec],
        out_specs=rows_spec,
        scratch_shapes=(
            [pltpu.VMEM((V_DIM, 2 * seq), BF16),
             pltpu.VMEM((V_DIM + ONES_ROWS, seq), BF16)]
            + per_group((tq, qg), F32)
            + per_group((tq, qg), F32)
            + per_group((1, qg), F32)
            + per_group((1, qg), F32)
            + per_group((1, qg), F32)
            + per_group((1, qg), F32)
            + per_group((V_DIM, qg), F32)),
    )
    return pl.pallas_call(
        partial(_attn_kernel, n_tasks=n_tasks),
        out_shape=jax.ShapeDtypeStruct((T, D), BF16),
        grid_spec=grid_spec,
        compiler_params=_cparams("arbitrary", "arbitrary"),
        name="diff_attention",
    )(tqi, tj, tlast, lq1, lk1, lq2, lk2, subln_g.reshape(V_DIM, 1), qt, k, vt)


def _mixer_kernel(x_ref, o_ref, u_ref, up_ref, ga_ref, gb_ref, g1_ref, sc2_ref, sh2_ref,
                  wao_ref, cw_ref, cb_ref, clg_ref, clb_ref, wpw_ref, bpw_ref, wout_ref,
                  ln1g_ref, ln1b_ref, wrh_ref, wrl_ref,
                  x1_ref, h2_ref, lg_ref, ubuf_ref, ush_ref, conv_ref, *, tiles_per_seq):
    tm = MIX_TM
    first = (pl.program_id(0) % tiles_per_seq) == 0

    prev = up_ref[...].astype(F32)
    ubuf_ref[0:CONV_HALO, :] = jnp.where(first, jnp.zeros_like(prev), prev)
    ubuf_ref[CONV_HALO:, :] = u_ref[...].astype(F32)
    base = CONV_HALO - (CONV_K - 1)
    n_shift = tm + CONV_HALO - SUBLANES
    for r in range(1, SUBLANES):
        ush_ref[r - 1] = ubuf_ref[r:r + n_shift, :]
    for c in range(D_MODEL // LANES):
        cl = slice(c * LANES, (c + 1) * LANES)
        for r0 in range(0, tm, CONV_ROWS):
            acc = jnp.zeros((CONV_ROWS, LANES), F32)
            for j in range(CONV_K):
                a, r = divmod(base + j, SUBLANES)
                src = ubuf_ref if r == 0 else ush_ref.at[r - 1]
                lo = SUBLANES * a + r0
                acc = acc + cw_ref[j:j + 1, cl] * src[lo:lo + CONV_ROWS, cl]
            conv_ref[r0:r0 + CONV_ROWS, cl] = acc
    yc = _silu(_layer_norm(conv_ref[...] + cb_ref[...], clg_ref[...], clb_ref[...]))
    y_conv = jnp.dot(yc.astype(BF16), wpw_ref[...], preferred_element_type=F32) + bpw_ref[...]

    y_attn = jnp.dot(o_ref[...], wao_ref[...], preferred_element_type=F32)
    mix = ga_ref[...].astype(F32) * y_attn + gb_ref[...].astype(F32) * y_conv
    y = jnp.dot(mix.astype(BF16), wout_ref[...], preferred_element_type=F32)
    x1 = _layer_norm(DN_ALPHA * x_ref[...] + g1_ref[0] * y, ln1g_ref[...], ln1b_ref[...])
    x1_ref[...] = x1

    h2 = x1 * (1.0 + sc2_ref[0]) + sh2_ref[0]
    h2_hi = h2.astype(BF16)
    _store_packed(h2_ref, h2)
    h2_lo = (h2 - h2_hi.astype(F32)).astype(BF16)
    nt = lambda w, a: lax.dot_general(w, a, (((1,), (1,)), ((), ())), preferred_element_type=F32)
    lg_ref[...] = (nt(wrh_ref[...], h2_hi)
                   + (nt(wrl_ref[...], h2_hi) + nt(wrh_ref[...], h2_lo)))


def _mixer_epilogue(x2, o, u, ga, gb, g1, sc2, sh2, wao, cw, cb, clg, clb, wpw, bpw, wout,
                    ln1g, ln1b, wr_hi, wr_lo, seq):
    T, D = x2.shape
    tm = MIX_TM
    tiles_per_seq = seq // tm
    halo_per_tile = tm // CONV_HALO
    tok = pl.BlockSpec((tm, D), lambda i: (i, 0))
    prev = pl.BlockSpec((CONV_HALO, D), lambda i: (jnp.maximum(i * halo_per_tile - 1, 0), 0))
    mod = pl.BlockSpec((1, 1, D), lambda i: (i // tiles_per_seq, 0, 0))
    row = pl.BlockSpec((1, D), lambda i: (0, 0))
    mat = pl.BlockSpec((D, D), lambda i: (0, 0))
    rw = pl.BlockSpec((N_EXPERTS, D), lambda i: (0, 0))
    return pl.pallas_call(
        partial(_mixer_kernel, tiles_per_seq=tiles_per_seq),
        out_shape=[jax.ShapeDtypeStruct((T, D), F32),
                   jax.ShapeDtypeStruct((2, T, D // 4), U32),
                   jax.ShapeDtypeStruct((N_EXPERTS, T), F32)],
        grid=(T // tm,),
        in_specs=[tok, tok, tok, prev, tok, tok, mod, mod, mod,
                  mat, pl.BlockSpec((CONV_K, D), lambda i: (0, 0)), row, row, row, mat, row, mat,
                  row, row, rw, rw],
        out_specs=[tok, pl.BlockSpec((2, tm, D // 4), lambda i: (0, i, 0)),
                   pl.BlockSpec((N_EXPERTS, tm), lambda i: (0, i))],
        scratch_shapes=[pltpu.VMEM((CONV_HALO + tm, D), F32),
                        pltpu.VMEM((SUBLANES - 1, tm + CONV_HALO - SUBLANES, D), F32),
                        pltpu.VMEM((tm, D), F32)],
        compiler_params=_cparams("arbitrary"),
        name="mixer_epilogue",
    )(x2, o, u, u, ga, gb, g1, sc2, sh2, wao, cw, cb, clg, clb, wpw, bpw, wout,
      ln1g, ln1b, wr_hi, wr_lo)


def _first_argmax(v, eid):
    m = jnp.max(v, axis=0, keepdims=True)
    i = jnp.min(jnp.where(v == m, eid, float(N_EXPERTS)), axis=0, keepdims=True)
    return m, i


def _route_kernel(lg_ref, rb_ref, idx_ref, gate_ref, cnt_ref):
    tm = ROUTE_TM
    scores = _sigmoid(lg_ref[...])
    choice = scores + rb_ref[...]
    eid = lax.broadcasted_iota(jnp.int32, (N_EXPERTS, tm), 0).astype(F32)
    rows = lambda a, g: a[g * GROUP_SIZE:(g + 1) * GROUP_SIZE, :]

    gs = []
    for g in range(N_GROUPS):
        cg = rows(choice, g)
        eg = (lax.broadcasted_iota(jnp.int32, cg.shape, 0) + g * GROUP_SIZE).astype(F32)
        m1, i1 = _first_argmax(cg, eg)
        m2 = jnp.max(jnp.where(eg == i1, NEG_INF, cg), axis=0, keepdims=True)
        gs.append(jnp.broadcast_to(m1 + m2, (SUBLANES, tm)))

    kept = []
    for g in range(N_GROUPS):
        ahead = jnp.zeros((SUBLANES, tm), F32)
        for o in range(N_GROUPS):
            if o == g:
                continue
            beats = (gs[o] >= gs[g]) if o < g else (gs[o] > gs[g])
            ahead = ahead + jnp.where(beats, 1.0, 0.0)
        cap = jnp.where(ahead < TOPK_GROUPS, float("inf"), NEG_INF)
        for r in range(g * GROUP_SIZE, (g + 1) * GROUP_SIZE, SUBLANES):
            kept.append(jnp.minimum(choice[r:r + SUBLANES, :], cap))
    masked = jnp.concatenate(kept, axis=0)

    sel = jnp.zeros((N_EXPERTS, tm), F32)
    total = jnp.zeros((1, tm), F32)
    picked = []
    for k in range(TOP_K):
        _, i = _first_argmax(masked, eid)
        hit = eid == i
        sc = jnp.sum(jnp.where(hit, scores, 0.0), axis=0, keepdims=True)
        masked = jnp.where(hit, NEG_INF, masked)
        sel = sel + hit.astype(F32)
        total = total + sc
        idx_ref[k:k + 1, :] = i.astype(jnp.int32)
        picked.append(sc)
    for k in range(TOP_K):
        gate_ref[k:k + 1, :] = picked[k] / total * ROUTED_SCALE

    @pl.when(pl.program_id(0) == 0)
    def _():
        cnt_ref[...] = jnp.zeros_like(cnt_ref)
    cnt_ref[...] += jnp.sum(sel, axis=1, keepdims=True)


def _route(logits_t, router_bias):
    T = logits_t.shape[1]
    tm = ROUTE_TM
    return pl.pallas_call(
        _route_kernel,
        out_shape=[jax.ShapeDtypeStruct((TOP_K, T), jnp.int32),
                   jax.ShapeDtypeStruct((TOP_K, T), F32),
                   jax.ShapeDtypeStruct((N_EXPERTS, 1), F32)],
        grid=(T // tm,),
        in_specs=[pl.BlockSpec((N_EXPERTS, tm), lambda i: (0, i)),
                  pl.BlockSpec((N_EXPERTS, 1), lambda i: (0, 0))],
        out_specs=[pl.BlockSpec((TOP_K, tm), lambda i: (0, i)),
                   pl.BlockSpec((TOP_K, tm), lambda i: (0, i)),
                   pl.BlockSpec((N_EXPERTS, 1), lambda i: (0, 0))],
        compiler_params=_cparams("arbitrary"),
        name="route_topk",
    )(logits_t, router_bias.reshape(N_EXPERTS, 1))


def _slot_kernel(idx_ref, start_ref, pos_ref, carry_ref):
    tm = ROUTE_TM

    @pl.when(pl.program_id(0) == 0)
    def _():
        carry_ref[...] = jnp.zeros_like(carry_ref)

    idx = idx_ref[...]
    eid = lax.broadcasted_iota(jnp.int32, (N_EXPERTS, tm), 0)
    hits = [eid == idx[k:k + 1, :] for k in range(TOP_K)]
    sel = jnp.zeros((N_EXPERTS, tm), F32)
    for h in hits:
        sel = sel + h.astype(F32)
    r = lax.broadcasted_iota(jnp.int32, (tm, tm), 0)
    c = lax.broadcasted_iota(jnp.int32, (tm, tm), 1)
    earlier = (r < c).astype(BF16)
    rank = jnp.dot(sel.astype(BF16), earlier, preferred_element_type=F32)
    slot = rank + carry_ref[...] + start_ref[...]
    for k, h in enumerate(hits):
        pos_ref[k:k + 1, :] = jnp.sum(jnp.where(h, slot, 0.0), axis=0,
                                      keepdims=True).astype(jnp.int32)
    carry_ref[...] += jnp.sum(sel, axis=1, keepdims=True)


def _assign_slots(idx_t, pad_start):
    T = idx_t.shape[1]
    tm = ROUTE_TM
    return pl.pallas_call(
        _slot_kernel,
        out_shape=jax.ShapeDtypeStruct((TOP_K, T), jnp.int32),
        grid=(T // tm,),
        in_specs=[pl.BlockSpec((TOP_K, tm), lambda i: (0, i)),
                  pl.BlockSpec((N_EXPERTS, 1), lambda i: (0, 0))],
        out_specs=pl.BlockSpec((TOP_K, tm), lambda i: (0, i)),
        scratch_shapes=[pltpu.VMEM((N_EXPERTS, 1), F32)],
        compiler_params=_cparams("arbitrary"),
        name="assign_slots",
    )(idx_t, pad_start)


def _dot_halves(lo, hi, w_ref):
    half = lo.shape[-1]
    return (jnp.dot(lo.astype(BF16), w_ref[:half, :].astype(BF16), preferred_element_type=F32)
            + jnp.dot(hi.astype(BF16), w_ref[half:, :].astype(BF16), preferred_element_type=F32))


def _expert_kernel(be_ref, nu_ref, first_ref, slot_ref, nxt_ref, xs_ref, w1_hbm, w3_hbm, w2_hbm,
                   ys_ref, w1_buf, w3_buf, w2_buf, sem):
    b = pl.program_id(0)

    def weight_copies(expert, slot):
        pairs = ((w1_hbm, w1_buf), (w3_hbm, w3_buf), (w2_hbm, w2_buf))
        return [pltpu.make_async_copy(hbm.at[expert], buf.at[slot], sem.at[slot, i])
                for i, (hbm, buf) in enumerate(pairs)]

    @pl.when(b < nu_ref[0])
    def _():
        slot = slot_ref[b]

        @pl.when(b == 0)
        def _():
            for c in weight_copies(be_ref[0], 0):
                c.start()

        @pl.when(first_ref[b] == 1)
        def _():
            for c in weight_copies(be_ref[b], slot):
                c.wait()

            @pl.when(nxt_ref[b] >= 0)
            def _():
                for c in weight_copies(nxt_ref[b], 1 - slot):
                    c.start()

        lo, hi = _load_packed(xs_ref)
        half = lo.shape[-1]
        mm = lambda w: (jnp.dot(lo, w[slot, :half, :], preferred_element_type=F32)
                        + jnp.dot(hi, w[slot, half:, :], preferred_element_type=F32))
        hb = _silu(mm(w1_buf)) * mm(w3_buf)
        _store_packed(ys_ref, jnp.dot(hb, w2_buf[slot], preferred_element_type=F32))

    @pl.when(b >= nu_ref[0])
    def _():
        ys_ref[...] = jnp.zeros_like(ys_ref)


def _routed_experts(xs, blk_expert, n_used, w1, w3, w2):
    _, P, Dq = xs.shape
    blk = EXPERT_BLK
    n_blocks = P // blk
    D, F = w1.shape[-2:]

    b_ids = jnp.arange(n_blocks, dtype=jnp.int32)
    used = b_ids < n_used[0]
    prev = jnp.concatenate([jnp.full((1,), -1, jnp.int32), blk_expert[:-1]])
    first = (used & (blk_expert != prev)).astype(jnp.int32)
    slot = (jnp.cumsum(first) - 1) % 2
    start_at = jnp.where(first == 1, b_ids, n_blocks)
    later = jnp.concatenate([start_at[1:], jnp.full((1,), n_blocks, jnp.int32)])
    next_start = lax.cummin(later, axis=0, reverse=True)
    nxt = jnp.where(next_start < n_blocks,
                    blk_expert[jnp.minimum(next_start, n_blocks - 1)], -1).astype(jnp.int32)

    row_blk = lambda b, be, nu, *_: (0, jnp.minimum(b, nu[0] - 1), 0)
    grid_spec = pltpu.PrefetchScalarGridSpec(
        num_scalar_prefetch=5,
        grid=(n_blocks,),
        in_specs=[pl.BlockSpec((2, blk, Dq), row_blk),
                  pl.BlockSpec(memory_space=pl.ANY),
                  pl.BlockSpec(memory_space=pl.ANY),
                  pl.BlockSpec(memory_space=pl.ANY)],
        out_specs=pl.BlockSpec((2, blk, Dq), lambda b, *_: (0, b, 0)),
        scratch_shapes=[pltpu.VMEM((2, D, F), F32), pltpu.VMEM((2, D, F), F32),
                        pltpu.VMEM((2, F, D), F32), pltpu.SemaphoreType.DMA((2, 3))],
    )
    return pl.pallas_call(
        _expert_kernel,
        out_shape=jax.ShapeDtypeStruct((2, P, Dq), U32),
        grid_spec=grid_spec,
        compiler_params=_cparams("arbitrary"),
        name="routed_experts",
    )(blk_expert, n_used, first, slot.astype(jnp.int32), nxt, xs, w1, w3, w2)


def _sc_mesh():
    return plsc.VectorSubcoreMesh(core_axis_name="c", subcore_axis_name="s")


def _sc_dispatch(rows, slots_t, n_slots):
    T, W = rows.shape
    K = slots_t.shape[0]
    win = SC_WINDOW

    @pl.kernel(out_type=jax.ShapeDtypeStruct((n_slots, W), rows.dtype), mesh=_sc_mesh(),
               scratch_types=[])
    def dispatch(x_hbm, i_hbm, o_hbm):
        def body(x_vmem, i_vmem):
            for k in range(K):
                pltpu.sync_copy(x_vmem, o_hbm.at[i_vmem.at[k]])

        pltpu.emit_pipeline(
            body,
            grid=(T // win,),
            in_specs=[pl.BlockSpec((win, W), lambda i: (i, 0)),
                      pl.BlockSpec((K, win), lambda i: (0, i))],
            out_specs=[],
            core_axis_name=("c", "s"),
            dimension_semantics=(pltpu.PARALLEL,),
        )(x_hbm, i_hbm)

    return dispatch(rows, slots_t)


def _sc_gather(table, idx):
    W = table.shape[1]
    N = idx.shape[1]
    win = SC_WINDOW

    @pl.kernel(out_type=jax.ShapeDtypeStruct((N, W), table.dtype), mesh=_sc_mesh(),
               scratch_types=[])
    def gather(x_hbm, i_hbm, o_hbm):
        def body(i_vmem, o_vmem):
            pltpu.sync_copy(x_hbm.at[i_vmem.at[0]], o_vmem)

        pltpu.emit_pipeline(
            body,
            grid=(N // win,),
            in_specs=[pl.BlockSpec((1, win), lambda i: (0, i))],
            out_specs=[pl.BlockSpec((win, W), lambda i: (i, 0))],
            core_axis_name=("c", "s"),
            dimension_semantics=(pltpu.PARALLEL,),
        )(i_hbm, o_hbm)

    return gather(table, idx)


def _final_kernel(x1_ref, h2_ref, yg_ref, gate_ref, g2_ref, ws1_ref, ws3_ref, ws2_ref,
                  lng_ref, lnb_ref, o_ref):
    Dh = D_MODEL // 2
    gate = gate_ref[...]
    routed_lo = jnp.zeros((x1_ref.shape[0], Dh), F32)
    routed_hi = jnp.zeros((x1_ref.shape[0], Dh), F32)
    for k in range(TOP_K):
        lo, hi = _load_packed(yg_ref.at[k])
        routed_lo = routed_lo + lo * gate[:, k:k + 1]
        routed_hi = routed_hi + hi * gate[:, k:k + 1]
    lo, hi = _load_packed(h2_ref)
    a = _dot_halves(lo, hi, ws1_ref)
    b = _dot_halves(lo, hi, ws3_ref)
    shared = jnp.dot((_silu(a) * b).astype(BF16), ws2_ref[...], preferred_element_type=F32)
    y2 = jnp.concatenate([routed_lo, routed_hi], axis=-1) + shared
    o_ref[...] = _layer_norm(DN_ALPHA * x1_ref[...] + g2_ref[0] * y2, lng_ref[...], lnb_ref[...])


def _final(x1, h2, yg, gate, g2, ws1, ws3, ws2, ln2g, ln2b, seq):
    T, D = x1.shape
    tm = FINAL_TM
    tiles_per_seq = seq // tm
    F = ws1.shape[-1]
    tok = pl.BlockSpec((tm, D), lambda i: (i, 0))
    row = pl.BlockSpec((1, D), lambda i: (0, 0))
    return pl.pallas_call(
        _final_kernel,
        out_shape=jax.ShapeDtypeStruct((T, D), F32),
        grid=(T // tm,),
        in_specs=[tok, pl.BlockSpec((2, tm, D // 4), lambda i: (0, i, 0)),
                  pl.BlockSpec((TOP_K, 2, tm, D // 4), lambda i: (0, 0, i, 0)),
                  pl.BlockSpec((tm, TOP_K), lambda i: (i, 0)),
                  pl.BlockSpec((1, 1, D), lambda i: (i // tiles_per_seq, 0, 0)),
                  pl.BlockSpec((D, F), lambda i: (0, 0)),
                  pl.BlockSpec((D, F), lambda i: (0, 0)),
                  pl.BlockSpec((F, D), lambda i: (0, 0)),
                  row, row],
        out_specs=tok,
        compiler_params=_cparams("arbitrary"),
        name="moe_combine_final",
    )(x1, h2, yg, gate, g2, ws1, ws3, ws2, ln2g, ln2b)


def kernel(x, c, ada_w, ada_b, w_in, b_in, lambda_q1, lambda_k1, lambda_q2, lambda_k2, subln_g, w_attn_o, conv_w, conv_b, conv_ln_g, conv_ln_b, conv_pw_w, conv_pw_b, w_out, ln1_g, ln1_b, router_w, router_bias, w1, w3, w2, ws1, ws3, ws2, ln2_g, ln2_b):
    B, S, D = x.shape
    T = B * S
    l = 0
    x2 = x.reshape(T, D)
    row = lambda a: a.reshape(1, -1)

    mod = _ada_modulation(c, ada_w[l], ada_b[l])
    sh1, sc1, g1, sh2, sc2, g2 = [m.reshape(B, 1, D) for m in jnp.split(mod, 6, axis=-1)]

    q, k, v, u, ga, gb = _in_projection(x2, sc1, sh1, w_in[l].astype(BF16), row(b_in[l]), S)
    o = _diff_attention(q.T, k, v.T, row(lambda_q1[l]), row(lambda_k1[l]), row(lambda_q2[l]),
                        row(lambda_k2[l]), row(subln_g[l]), B, S)

    wr = router_w[l].T
    wr_hi = wr.astype(BF16)
    wr_lo = (wr - wr_hi.astype(F32)).astype(BF16)
    x1, h2, logits = _mixer_epilogue(
        x2, o, u, ga, gb, g1, sc2, sh2, w_attn_o[l].astype(BF16), conv_w[l], row(conv_b[l]),
        row(conv_ln_g[l]), row(conv_ln_b[l]), conv_pw_w[l].astype(BF16), row(conv_pw_b[l]),
        w_out[l].astype(BF16), row(ln1_g[l]), row(ln1_b[l]), wr_hi, wr_lo, S)

    idx, gate, counts = _route(logits, router_bias[l])

    blk = EXPERT_BLK
    cnt = counts.reshape(N_EXPERTS).astype(jnp.int32)
    padded = (cnt + blk - 1) // blk * blk
    pad_end = jnp.cumsum(padded)
    pad_start = pad_end - padded
    P = T * TOP_K + N_EXPERTS * blk
    n_blocks = P // blk
    blk_first_row = jnp.arange(n_blocks, dtype=jnp.int32) * blk
    blk_expert = jnp.minimum(
        jnp.sum((pad_end[None, :] <= blk_first_row[:, None]).astype(jnp.int32), axis=1),
        N_EXPERTS - 1)
    n_used = (pad_end[-1:] // blk).astype(jnp.int32)

    pos = _assign_slots(idx, pad_start.astype(F32).reshape(N_EXPERTS, 1))

    Dq = D // 4
    half_base = (jnp.arange(2, dtype=jnp.int32) * P)[None, :, None]
    slots = pos[:, None, :] + half_base
    xs = _sc_dispatch(h2.reshape(2 * T, Dq), slots.reshape(TOP_K, 2 * T), 2 * P)
    ys = _routed_experts(xs.reshape(2, P, Dq), blk_expert, n_used, w1[l], w3[l], w2[l])
    yg = _sc_gather(ys.reshape(2 * P, Dq), slots.reshape(1, TOP_K * 2 * T))
    yg = yg.reshape(TOP_K, 2, T, Dq)

    out = _final(x1, h2, yg, gate.T, g2, ws1[l].astype(BF16), ws3[l].astype(BF16),
                 ws2[l].astype(BF16), row(ln2_g[l]), row(ln2_b[l]), S)
    return out.reshape(B, S, D)
```

```python
import math
from functools import partial

import jax
import jax.numpy as jnp
from jax import lax
from jax.experimental import pallas as pl
from jax.experimental.pallas import tpu as pltpu
from jax.experimental.pallas import tpu_sc as plsc

D_MODEL = 1024
N_HEADS = 8
HEAD_DIM = 64
V_DIM = 128
CONV_K = 31
N_EXPERTS = 256
TOP_K = 8
N_GROUPS = 8
GROUP_SIZE = N_EXPERTS // N_GROUPS
TOPK_GROUPS = 4
ROUTED_SCALE = 2.5
LN_EPS = 1e-5
RMS_EPS = 1e-5
DEPTH = 1
DN_ALPHA = (2.0 * DEPTH) ** 0.25
LAMBDA_INIT = 0.8 - 0.6 * math.exp(-0.3 * 0)
Q_SCALE = HEAD_DIM ** -0.5 * math.log2(math.e)

LANES = 128
SUBLANES = 8
VMEM_LIMIT_BYTES = 56 * 1024 * 1024

PROJ_TM = 512
ATTN_TQ = 512
ATTN_QG = 256
ATTN_KC = 256
ONES_ROWS = 16
MIX_TM = 256
CONV_HALO = 32
CONV_ROWS = 64
ROUTE_TM = 512
EXPERT_BLK = 512
FINAL_TM = 256
SC_WINDOW = 128

F32 = jnp.float32
BF16 = jnp.bfloat16
U32 = jnp.uint32
NEG_INF = float("-inf")


def _cparams(*sem):
    return pltpu.CompilerParams(dimension_semantics=sem, vmem_limit_bytes=VMEM_LIMIT_BYTES)


def _sigmoid(x):
    return 1.0 / (1.0 + jnp.exp(-x))


def _silu(x):
    return x * _sigmoid(x)


BF16_BITS = 16
HIGH_HALF_WORD = 0xFFFF0000


def _pack_halves(x):
    half = x.shape[-1] // 2
    bits = lambda v: lax.bitcast_convert_type(v.astype(BF16).astype(F32), U32)
    return (bits(x[:, :half]) >> BF16_BITS) | (bits(x[:, half:]) & jnp.uint32(HIGH_HALF_WORD))


def _unpack_halves(w):
    lo = lax.bitcast_convert_type(w << BF16_BITS, F32)
    hi = lax.bitcast_convert_type(w & jnp.uint32(HIGH_HALF_WORD), F32)
    return lo, hi


def _store_packed(ref, x):
    w = _pack_halves(x)
    q = w.shape[-1] // 2
    ref[0] = w[:, :q]
    ref[1] = w[:, q:]


def _load_packed(ref):
    return _unpack_halves(jnp.concatenate([ref[0], ref[1]], axis=-1))


def _layer_norm(r, g, b):
    mu = jnp.mean(r, axis=-1, keepdims=True)
    d = r - mu
    var = jnp.mean(d * d, axis=-1, keepdims=True)
    return d * lax.rsqrt(var + LN_EPS) * g + b


def _ada_kernel(c_ref, w_ref, b_ref, o_ref):
    a = _silu(c_ref[...])
    o_ref[...] = jnp.dot(a, w_ref[...], preferred_element_type=F32,
                         precision=lax.Precision.HIGHEST) + b_ref[...]


def _ada_modulation(c, w, b):
    B, D = c.shape
    n = w.shape[1] // D
    return pl.pallas_call(
        _ada_kernel,
        out_shape=jax.ShapeDtypeStruct((B, n * D), F32),
        grid=(n,),
        in_specs=[pl.BlockSpec((B, D), lambda j: (0, 0)),
                  pl.BlockSpec((D, D), lambda j: (0, j)),
                  pl.BlockSpec((1, D), lambda j: (0, j))],
        out_specs=pl.BlockSpec((B, D), lambda j: (0, j)),
        compiler_params=_cparams("arbitrary"),
        name="ada_modulation",
    )(c, w, b.reshape(1, n * D))


def _inproj_kernel(x_ref, sc_ref, sh_ref, w_ref, b_ref,
                   qt_ref, k_ref, vt_ref, u_ref, ga_ref, gb_ref):
    D = D_MODEL
    h = (x_ref[...] * (1.0 + sc_ref[0]) + sh_ref[0]).astype(BF16)

    def proj(j):
        return (jnp.dot(h, w_ref[:, j * D:(j + 1) * D], preferred_element_type=F32)
                + b_ref[:, j * D:(j + 1) * D])

    qt_ref[...] = (proj(0) * Q_SCALE).T.astype(BF16)
    k_ref[...] = proj(1).astype(BF16)
    vt_ref[...] = proj(2).T.astype(BF16)
    u_ref[...] = (proj(3) * _sigmoid(proj(4))).astype(BF16)
    ga_ref[...] = _sigmoid(proj(5)).astype(BF16)
    gb_ref[...] = _sigmoid(proj(6)).astype(BF16)


def _in_projection(x2, sc1, sh1, w_in, b_in, seq):
    T, D = x2.shape
    tm = PROJ_TM
    tiles_per_seq = seq // tm
    ncols = w_in.shape[1]
    tok = pl.BlockSpec((tm, D), lambda i: (i, 0))
    mod = pl.BlockSpec((1, 1, D), lambda i: (i // tiles_per_seq, 0, 0))
    tok_t = pl.BlockSpec((D, tm), lambda i: (0, i))
    out = jax.ShapeDtypeStruct((T, D), BF16)
    out_t = jax.ShapeDtypeStruct((D, T), BF16)
    return pl.pallas_call(
        _inproj_kernel,
        out_shape=[out_t, out, out_t, out, out, out],
        grid=(T // tm,),
        in_specs=[tok, mod, mod,
                  pl.BlockSpec((D, ncols), lambda i: (0, 0)),
                  pl.BlockSpec((1, ncols), lambda i: (0, 0))],
        out_specs=[tok_t, tok, tok_t, tok, tok, tok],
        compiler_params=_cparams("arbitrary"),
        name="in_projection",
    )(x2, sc1, sh1, w_in, b_in)


def _attn_tasks(nq):
    tasks = [(qi, j, int(j == qi)) for qi in range(nq) for j in range(qi + 1)]
    if len(tasks) % 2:
        raise ValueError("the two-slot task loop needs an even number of tile pairs")
    qis, js, lasts = zip(*(tasks + [(0, 0, 0)]))
    as_i32 = lambda v: jnp.asarray(v, jnp.int32)
    return as_i32(qis), as_i32(js), as_i32(lasts), len(tasks)


def _attn_kernel(tqi_ref, tj_ref, tlast_ref, lq1_ref, lk1_ref, lq2_ref, lk2_ref, g_ref,
                 qt_ref, k_ref, vt_ref, o_ref, q2_ref, vx_ref, *scratch, n_tasks):
    tq = ATTN_TQ
    qg = ATTN_QG
    ng = 2 * tq // qg
    groups = [scratch[i * ng:(i + 1) * ng] for i in range(7)]
    s_slots, x_slots = groups[0:2], groups[2:4]
    m_refs, l_refs, acc_refs = groups[4:]
    seq = k_ref.shape[0]

    vx_ref[:V_DIM, :] = vt_ref[...]
    vx_ref[V_DIM:, :] = jnp.ones((ONES_ROWS, seq), BF16)

    lam = (jnp.exp(jnp.sum(lq1_ref[...] * lk1_ref[...], axis=-1, keepdims=True))
           - jnp.exp(jnp.sum(lq2_ref[...] * lk2_ref[...], axis=-1, keepdims=True))
           + LAMBDA_INIT)

    for i in range(seq // tq):
        qt = qt_ref[:, i * tq:(i + 1) * tq]
        feat = lax.broadcasted_iota(jnp.int32, qt.shape, 0)
        zero = jnp.zeros_like(qt)
        q2_ref[:, 2 * tq * i:2 * tq * i + tq] = jnp.where(feat < HEAD_DIM, qt, zero)
        q2_ref[:, 2 * tq * i + tq:2 * tq * (i + 1)] = jnp.where(feat >= HEAD_DIM, qt, zero)

    def scores(qi, j, slot, g):
        start = pl.multiple_of(j * tq, tq)
        col = pl.multiple_of(qi * (2 * tq) + g * qg, qg)
        s = jnp.dot(k_ref[pl.ds(start, tq), :], q2_ref[:, pl.ds(col, qg)],
                    preferred_element_type=F32)
        s_slots[slot][g][...] = s
        x_slots[slot][g][...] = jnp.max(s, axis=0, keepdims=True)

    def update(j, slot, g, masked):
        start = pl.multiple_of(j * tq, tq)
        s_ref = s_slots[slot][g]

        def load(k0):
            s = s_ref[k0:k0 + ATTN_KC, :]
            if masked:
                key = lax.broadcasted_iota(jnp.int32, s.shape, 0) + k0
                qry = lax.broadcasted_iota(jnp.int32, s.shape, 1) + (g * qg) % tq
                s = jnp.where(key <= qry, s, NEG_INF)
            return s

        m_prev = m_refs[g][...]
        if masked:
            tile_max = jnp.max(load(0), axis=0, keepdims=True)
            for k0 in range(ATTN_KC, tq, ATTN_KC):
                tile_max = jnp.maximum(tile_max, jnp.max(load(k0), axis=0, keepdims=True))
        else:
            tile_max = x_slots[slot][g][...]
        m_new = jnp.maximum(m_prev, tile_max)
        alpha = jnp.exp2(m_prev - m_new)
        m_refs[g][...] = m_new
        pv = None
        for k0 in range(0, tq, ATTN_KC):
            p = jnp.exp2((load(k0) - m_new).astype(BF16))
            keys = pl.ds(pl.multiple_of(start + k0, ATTN_KC), ATTN_KC)
            part = jnp.dot(vx_ref[:, keys], p, preferred_element_type=F32)
            pv = part if pv is None else pv + part
        acc_refs[g][...] = alpha * acc_refs[g][...] + pv[:V_DIM, :]
        l_refs[g][...] = alpha * l_refs[g][...] + pv[V_DIM:V_DIM + 1, :]

    def reset():
        for g in range(ng):
            m_refs[g][...] = jnp.full(m_refs[g].shape, NEG_INF, F32)
            l_refs[g][...] = jnp.zeros(l_refs[g].shape, F32)
            acc_refs[g][...] = jnp.zeros(acc_refs[g].shape, F32)

    def finish(qi):
        o = (jnp.concatenate([r[...] for r in acc_refs], axis=-1)
             / jnp.concatenate([r[...] for r in l_refs], axis=-1))
        o = o[:, :tq] - lam * o[:, tq:]
        o = o * lax.rsqrt(jnp.mean(o * o, axis=0, keepdims=True) + RMS_EPS) * g_ref[...]
        row = pl.multiple_of(qi * tq, tq)
        o_ref[pl.ds(row, tq), :] = (o * (1.0 - LAMBDA_INIT)).T.astype(BF16)

    def task(t, slot):
        qi, j = tqi_ref[t], tj_ref[t]
        nqi, nj = tqi_ref[t + 1], tj_ref[t + 1]

        def run(masked):
            for g in range(ng):
                scores(nqi, nj, 1 - slot, g)
                update(j, slot, g, masked)

        @pl.when(tlast_ref[t] == 0)
        def _():
            run(False)

        @pl.when(tlast_ref[t] == 1)
        def _():
            run(True)
            finish(qi)
            reset()

    reset()
    for g in range(ng):
        scores(0, 0, 0, g)

    def pair(i, carry):
        task(2 * i, 0)
        task(2 * i + 1, 1)
        return carry

    lax.fori_loop(0, n_tasks // 2, pair, 0)


def _diff_attention(qt, k, vt, lq1, lk1, lq2, lk2, subln_g, batch, seq):
    T, D = k.shape
    tq = ATTN_TQ
    nq = seq // tq
    tqi, tj, tlast, n_tasks = _attn_tasks(nq)
    lam_spec = pl.BlockSpec((1, HEAD_DIM), lambda b, h, *_: (0, 0))
    rows_spec = pl.BlockSpec((seq, V_DIM), lambda b, h, *_: (b, h))
    cols_spec = pl.BlockSpec((V_DIM, seq), lambda b, h, *_: (h, b))
    qg = ATTN_QG
    per_group = lambda shape, dtype: [pltpu.VMEM(shape, dtype) for _ in range(2 * tq // qg)]
    grid_spec = pltpu.PrefetchScalarGridSpec(
        num_scalar_prefetch=3,
        grid=(batch, N_HEADS),
        in_specs=[lam_spec, lam_spec, lam_spec, lam_spec,
                  pl.BlockSpec((V_DIM, 1), lambda b, h, *_: (0, 0)),
                  cols_spec, rows_spec, cols_spec],
        out_specs=rows_spec,
        scratch_shapes=(
            [pltpu.VMEM((V_DIM, 2 * seq), BF16),
             pltpu.VMEM((V_DIM + ONES_ROWS, seq), BF16)]
            + per_group((tq, qg), F32)
            + per_group((tq, qg), F32)
            + per_group((1, qg), F32)
            + per_group((1, qg), F32)
            + per_group((1, qg), F32)
            + per_group((1, qg), F32)
            + per_group((V_DIM, qg), F32)),
    )
    return pl.pallas_call(
        partial(_attn_kernel, n_tasks=n_tasks),
        out_shape=jax.ShapeDtypeStruct((T, D), BF16),
        grid_spec=grid_spec,
        compiler_params=_cparams("arbitrary", "arbitrary"),
        name="diff_attention",
    )(tqi, tj, tlast, lq1, lk1, lq2, lk2, subln_g.reshape(V_DIM, 1), qt, k, vt)


def _mixer_kernel(x_ref, o_ref, u_ref, up_ref, ga_ref, gb_ref, g1_ref, sc2_ref, sh2_ref,
                  wao_ref, cw_ref, cb_ref, clg_ref, clb_ref, wpw_ref, bpw_ref, wout_ref,
                  ln1g_ref, ln1b_ref, wrh_ref, wrl_ref,
                  x1_ref, h2_ref, lg_ref, ubuf_ref, ush_ref, conv_ref, *, tiles_per_seq):
    tm = MIX_TM
    first = (pl.program_id(0) % tiles_per_seq) == 0

    prev = up_ref[...].astype(F32)
    ubuf_ref[0:CONV_HALO, :] = jnp.where(first, jnp.zeros_like(prev), prev)
    ubuf_ref[CONV_HALO:, :] = u_ref[...].astype(F32)
    base = CONV_HALO - (CONV_K - 1)
    n_shift = tm + CONV_HALO - SUBLANES
    for r in range(1, SUBLANES):
        ush_ref[r - 1] = ubuf_ref[r:r + n_shift, :]
    for c in range(D_MODEL // LANES):
        cl = slice(c * LANES, (c + 1) * LANES)
        for r0 in range(0, tm, CONV_ROWS):
            acc = jnp.zeros((CONV_ROWS, LANES), F32)
            for j in range(CONV_K):
                a, r = divmod(base + j, SUBLANES)
                src = ubuf_ref if r == 0 else ush_ref.at[r - 1]
                lo = SUBLANES * a + r0
                acc = acc + cw_ref[j:j + 1, cl] * src[lo:lo + CONV_ROWS, cl]
            conv_ref[r0:r0 + CONV_ROWS, cl] = acc
    yc = _silu(_layer_norm(conv_ref[...] + cb_ref[...], clg_ref[...], clb_ref[...]))
    y_conv = jnp.dot(yc.astype(BF16), wpw_ref[...], preferred_element_type=F32) + bpw_ref[...]

    y_attn = jnp.dot(o_ref[...], wao_ref[...], preferred_element_type=F32)
    mix = ga_ref[...].astype(F32) * y_attn + gb_ref[...].astype(F32) * y_conv
    y = jnp.dot(mix.astype(BF16), wout_ref[...], preferred_element_type=F32)
    x1 = _layer_norm(DN_ALPHA * x_ref[...] + g1_ref[0] * y, ln1g_ref[...], ln1b_ref[...])
    x1_ref[...] = x1

    h2 = x1 * (1.0 + sc2_ref[0]) + sh2_ref[0]
    h2_hi = h2.astype(BF16)
    _store_packed(h2_ref, h2)
    h2_lo = (h2 - h2_hi.astype(F32)).astype(BF16)
    nt = lambda w, a: lax.dot_general(w, a, (((1,), (1,)), ((), ())), preferred_element_type=F32)
    lg_ref[...] = (nt(wrh_ref[...], h2_hi)
                   + (nt(wrl_ref[...], h2_hi) + nt(wrh_ref[...], h2_lo)))


def _mixer_epilogue(x2, o, u, ga, gb, g1, sc2, sh2, wao, cw, cb, clg, clb, wpw, bpw, wout,
                    ln1g, ln1b, wr_hi, wr_lo, seq):
    T, D = x2.shape
    tm = MIX_TM
    tiles_per_seq = seq // tm
    halo_per_tile = tm // CONV_HALO
    tok = pl.BlockSpec((tm, D), lambda i: (i, 0))
    prev = pl.BlockSpec((CONV_HALO, D), lambda i: (jnp.maximum(i * halo_per_tile - 1, 0), 0))
    mod = pl.BlockSpec((1, 1, D), lambda i: (i // tiles_per_seq, 0, 0))
    row = pl.BlockSpec((1, D), lambda i: (0, 0))
    mat = pl.BlockSpec((D, D), lambda i: (0, 0))
    rw = pl.BlockSpec((N_EXPERTS, D), lambda i: (0, 0))
    return pl.pallas_call(
        partial(_mixer_kernel, tiles_per_seq=tiles_per_seq),
        out_shape=[jax.ShapeDtypeStruct((T, D), F32),
                   jax.ShapeDtypeStruct((2, T, D // 4), U32),
                   jax.ShapeDtypeStruct((N_EXPERTS, T), F32)],
        grid=(T // tm,),
        in_specs=[tok, tok, tok, prev, tok, tok, mod, mod, mod,
                  mat, pl.BlockSpec((CONV_K, D), lambda i: (0, 0)), row, row, row, mat, row, mat,
                  row, row, rw, rw],
        out_specs=[tok, pl.BlockSpec((2, tm, D // 4), lambda i: (0, i, 0)),
                   pl.BlockSpec((N_EXPERTS, tm), lambda i: (0, i))],
        scratch_shapes=[pltpu.VMEM((CONV_HALO + tm, D), F32),
                        pltpu.VMEM((SUBLANES - 1, tm + CONV_HALO - SUBLANES, D), F32),
                        pltpu.VMEM((tm, D), F32)],
        compiler_params=_cparams("arbitrary"),
        name="mixer_epilogue",
    )(x2, o, u, u, ga, gb, g1, sc2, sh2, wao, cw, cb, clg, clb, wpw, bpw, wout,
      ln1g, ln1b, wr_hi, wr_lo)


def _first_argmax(v, eid):
    m = jnp.max(v, axis=0, keepdims=True)
    i = jnp.min(jnp.where(v == m, eid, float(N_EXPERTS)), axis=0, keepdims=True)
    return m, i


def _route_kernel(lg_ref, rb_ref, idx_ref, gate_ref, cnt_ref):
    tm = ROUTE_TM
    scores = _sigmoid(lg_ref[...])
    choice = scores + rb_ref[...]
    eid = lax.broadcasted_iota(jnp.int32, (N_EXPERTS, tm), 0).astype(F32)
    rows = lambda a, g: a[g * GROUP_SIZE:(g + 1) * GROUP_SIZE, :]

    gs = []
    for g in range(N_GROUPS):
        cg = rows(choice, g)
        eg = (lax.broadcasted_iota(jnp.int32, cg.shape, 0) + g * GROUP_SIZE).astype(F32)
        m1, i1 = _first_argmax(cg, eg)
        m2 = jnp.max(jnp.where(eg == i1, NEG_INF, cg), axis=0, keepdims=True)
        gs.append(jnp.broadcast_to(m1 + m2, (SUBLANES, tm)))

    kept = []
    for g in range(N_GROUPS):
        ahead = jnp.zeros((SUBLANES, tm), F32)
        for o in range(N_GROUPS):
            if o == g:
                continue
            beats = (gs[o] >= gs[g]) if o < g else (gs[o] > gs[g])
            ahead = ahead + jnp.where(beats, 1.0, 0.0)
        cap = jnp.where(ahead < TOPK_GROUPS, float("inf"), NEG_INF)
        for r in range(g * GROUP_SIZE, (g + 1) * GROUP_SIZE, SUBLANES):
            kept.append(jnp.minimum(choice[r:r + SUBLANES, :], cap))
    masked = jnp.concatenate(kept, axis=0)

    sel = jnp.zeros((N_EXPERTS, tm), F32)
    total = jnp.zeros((1, tm), F32)
    picked = []
    for k in range(TOP_K):
        _, i = _first_argmax(masked, eid)
        hit = eid == i
        sc = jnp.sum(jnp.where(hit, scores, 0.0), axis=0, keepdims=True)
        masked = jnp.where(hit, NEG_INF, masked)
        sel = sel + hit.astype(F32)
        total = total + sc
        idx_ref[k:k + 1, :] = i.astype(jnp.int32)
        picked.append(sc)
    for k in range(TOP_K):
        gate_ref[k:k + 1, :] = picked[k] / total * ROUTED_SCALE

    @pl.when(pl.program_id(0) == 0)
    def _():
        cnt_ref[...] = jnp.zeros_like(cnt_ref)
    cnt_ref[...] += jnp.sum(sel, axis=1, keepdims=True)


def _route(logits_t, router_bias):
    T = logits_t.shape[1]
    tm = ROUTE_TM
    return pl.pallas_call(
        _route_kernel,
        out_shape=[jax.ShapeDtypeStruct((TOP_K, T), jnp.int32),
                   jax.ShapeDtypeStruct((TOP_K, T), F32),
                   jax.ShapeDtypeStruct((N_EXPERTS, 1), F32)],
        grid=(T // tm,),
        in_specs=[pl.BlockSpec((N_EXPERTS, tm), lambda i: (0, i)),
                  pl.BlockSpec((N_EXPERTS, 1), lambda i: (0, 0))],
        out_specs=[pl.BlockSpec((TOP_K, tm), lambda i: (0, i)),
                   pl.BlockSpec((TOP_K, tm), lambda i: (0, i)),
                   pl.BlockSpec((N_EXPERTS, 1), lambda i: (0, 0))],
        compiler_params=_cparams("arbitrary"),
        name="route_topk",
    )(logits_t, router_bias.reshape(N_EXPERTS, 1))


def _slot_kernel(idx_ref, start_ref, pos_ref, carry_ref):
    tm = ROUTE_TM

    @pl.when(pl.program_id(0) == 0)
    def _():
        carry_ref[...] = jnp.zeros_like(carry_ref)

    idx = idx_ref[...]
    eid = lax.broadcasted_iota(jnp.int32, (N_EXPERTS, tm), 0)
    hits = [eid == idx[k:k + 1, :] for k in range(TOP_K)]
    sel = jnp.zeros((N_EXPERTS, tm), F32)
    for h in hits:
        sel = sel + h.astype(F32)
    r = lax.broadcasted_iota(jnp.int32, (tm, tm), 0)
    c = lax.broadcasted_iota(jnp.int32, (tm, tm), 1)
    earlier = (r < c).astype(BF16)
    rank = jnp.dot(sel.astype(BF16), earlier, preferred_element_type=F32)
    slot = rank + carry_ref[...] + start_ref[...]
    for k, h in enumerate(hits):
        pos_ref[k:k + 1, :] = jnp.sum(jnp.where(h, slot, 0.0), axis=0,
                                      keepdims=True).astype(jnp.int32)
    carry_ref[...] += jnp.sum(sel, axis=1, keepdims=True)


def _assign_slots(idx_t, pad_start):
    T = idx_t.shape[1]
    tm = ROUTE_TM
    return pl.pallas_call(
        _slot_kernel,
        out_shape=jax.ShapeDtypeStruct((TOP_K, T), jnp.int32),
        grid=(T // tm,),
        in_specs=[pl.BlockSpec((TOP_K, tm), lambda i: (0, i)),
                  pl.BlockSpec((N_EXPERTS, 1), lambda i: (0, 0))],
        out_specs=pl.BlockSpec((TOP_K, tm), lambda i: (0, i)),
        scratch_shapes=[pltpu.VMEM((N_EXPERTS, 1), F32)],
        compiler_params=_cparams("arbitrary"),
        name="assign_slots",
    )(idx_t, pad_start)


def _expert_kernel(be_ref, nu_ref, first_ref, slot_ref, nxt_ref, xs_ref, w1_hbm, w3_hbm, w2_hbm,
                   ys_ref, w1_buf, w3_buf, w2_buf, sem):
    b = pl.program_id(0)

    def weight_copies(expert, slot):
        pairs = ((w1_hbm, w1_buf), (w3_hbm, w3_buf), (w2_hbm, w2_buf))
        return [pltpu.make_async_copy(hbm.at[expert], buf.at[slot], sem.at[slot, i])
                for i, (hbm, buf) in enumerate(pairs)]

    @pl.when(b < nu_ref[0])
    def _():
        slot = slot_ref[b]

        @pl.when(b == 0)
        def _():
            for c in weight_copies(be_ref[0], 0):
                c.start()

        @pl.when(first_ref[b] == 1)
        def _():
            for c in weight_copies(be_ref[b], slot):
                c.wait()

            @pl.when(nxt_ref[b] >= 0)
            def _():
                for c in weight_copies(nxt_ref[b], 1 - slot):
                    c.start()

        lo, hi = _load_packed(xs_ref)
        half = lo.shape[-1]
        mm = lambda w: (jnp.dot(lo, w[slot, :half, :], preferred_element_type=F32)
                        + jnp.dot(hi, w[slot, half:, :], preferred_element_type=F32))
        hb = _silu(mm(w1_buf)) * mm(w3_buf)
        _store_packed(ys_ref, jnp.dot(hb, w2_buf[slot], preferred_element_type=F32))

    @pl.when(b >= nu_ref[0])
    def _():
        ys_ref[...] = jnp.zeros_like(ys_ref)


def _routed_experts(xs, blk_expert, n_used, w1, w3, w2):
    _, P, Dq = xs.shape
    blk = EXPERT_BLK
    n_blocks = P // blk
    D, F = w1.shape[-2:]

    b_ids = jnp.arange(n_blocks, dtype=jnp.int32)
    used = b_ids < n_used[0]
    prev = jnp.concatenate([jnp.full((1,), -1, jnp.int32), blk_expert[:-1]])
    first = (used & (blk_expert != prev)).astype(jnp.int32)
    slot = (jnp.cumsum(first) - 1) % 2
    start_at = jnp.where(first == 1, b_ids, n_blocks)
    later = jnp.concatenate([start_at[1:], jnp.full((1,), n_blocks, jnp.int32)])
    next_start = lax.cummin(later, axis=0, reverse=True)
    nxt = jnp.where(next_start < n_blocks,
                    blk_expert[jnp.minimum(next_start, n_blocks - 1)], -1).astype(jnp.int32)

    row_blk = lambda b, be, nu, *_: (0, jnp.minimum(b, nu[0] - 1), 0)
    grid_spec = pltpu.PrefetchScalarGridSpec(
        num_scalar_prefetch=5,
        grid=(n_blocks,),
        in_specs=[pl.BlockSpec((2, blk, Dq), row_blk),
                  pl.BlockSpec(memory_space=pl.ANY),
                  pl.BlockSpec(memory_space=pl.ANY),
                  pl.BlockSpec(memory_space=pl.ANY)],
        out_specs=pl.BlockSpec((2, blk, Dq), lambda b, *_: (0, b, 0)),
        scratch_shapes=[pltpu.VMEM((2, D, F), F32), pltpu.VMEM((2, D, F), F32),
                        pltpu.VMEM((2, F, D), F32), pltpu.SemaphoreType.DMA((2, 3))],
    )
    return pl.pallas_call(
        _expert_kernel,
        out_shape=jax.ShapeDtypeStruct((2, P, Dq), U32),
        grid_spec=grid_spec,
        compiler_params=_cparams("arbitrary"),
        name="routed_experts",
    )(blk_expert, n_used, first, slot.astype(jnp.int32), nxt, xs, w1, w3, w2)


def _sc_mesh():
    return plsc.VectorSubcoreMesh(core_axis_name="c", subcore_axis_name="s")


def _sc_dispatch(rows, slots_t, n_slots):
    T, W = rows.shape
    K = slots_t.shape[0]
    win = SC_WINDOW

    @pl.kernel(out_type=jax.ShapeDtypeStruct((n_slots, W), rows.dtype), mesh=_sc_mesh(),
               scratch_types=[])
    def dispatch(x_hbm, i_hbm, o_hbm):
        def body(x_vmem, i_vmem):
            for k in range(K):
                pltpu.sync_copy(x_vmem, o_hbm.at[i_vmem.at[k]])

        pltpu.emit_pipeline(
            body,
            grid=(T // win,),
            in_specs=[pl.BlockSpec((win, W), lambda i: (i, 0)),
                      pl.BlockSpec((K, win), lambda i: (0, i))],
            out_specs=[],
            core_axis_name=("c", "s"),
            dimension_semantics=(pltpu.PARALLEL,),
        )(x_hbm, i_hbm)

    return dispatch(rows, slots_t)


def _sc_gather(table, idx):
    W = table.shape[1]
    N = idx.shape[1]
    win = SC_WINDOW

    @pl.kernel(out_type=jax.ShapeDtypeStruct((N, W), table.dtype), mesh=_sc_mesh(),
               scratch_types=[])
    def gather(x_hbm, i_hbm, o_hbm):
        def body(i_vmem, o_vmem):
            pltpu.sync_copy(x_hbm.at[i_vmem.at[0]], o_vmem)

        pltpu.emit_pipeline(
            body,
            grid=(N // win,),
            in_specs=[pl.BlockSpec((1, win), lambda i: (0, i))],
            out_specs=[pl.BlockSpec((win, W), lambda i: (i, 0))],
            core_axis_name=("c", "s"),
            dimension_semantics=(pltpu.PARALLEL,),
        )(i_hbm, o_hbm)

    return gather(table, idx)


def _final_kernel(x1_ref, h2_ref, yg_ref, gate_ref, g2_ref, ws1_ref, ws3_ref, ws2_ref,
                  lng_ref, lnb_ref, o_ref):
    Dh = D_MODEL // 2
    gate = gate_ref[...]
    routed_lo = jnp.zeros((x1_ref.shape[0], Dh), F32)
    routed_hi = jnp.zeros((x1_ref.shape[0], Dh), F32)
    for k in range(TOP_K):
        lo, hi = _load_packed(yg_ref.at[k])
        routed_lo = routed_lo + lo * gate[:, k:k + 1]
        routed_hi = routed_hi + hi * gate[:, k:k + 1]
    h2 = jnp.concatenate(_load_packed(h2_ref), axis=-1).astype(BF16)
    a = jnp.dot(h2, ws1_ref[...], preferred_element_type=F32)
    b = jnp.dot(h2, ws3_ref[...], preferred_element_type=F32)
    shared = jnp.dot((_silu(a) * b).astype(BF16), ws2_ref[...], preferred_element_type=F32)
    y2 = jnp.concatenate([routed_lo, routed_hi], axis=-1) + shared
    o_ref[...] = _layer_norm(DN_ALPHA * x1_ref[...] + g2_ref[0] * y2, lng_ref[...], lnb_ref[...])


def _final(x1, h2, yg, gate, g2, ws1, ws3, ws2, ln2g, ln2b, seq):
    T, D = x1.shape
    tm = FINAL_TM
    tiles_per_seq = seq // tm
    F = ws1.shape[-1]
    tok = pl.BlockSpec((tm, D), lambda i: (i, 0))
    row = pl.BlockSpec((1, D), lambda i: (0, 0))
    return pl.pallas_call(
        _final_kernel,
        out_shape=jax.ShapeDtypeStruct((T, D), F32),
        grid=(T // tm,),
        in_specs=[tok, pl.BlockSpec((2, tm, D // 4), lambda i: (0, i, 0)),
                  pl.BlockSpec((TOP_K, 2, tm, D // 4), lambda i: (0, 0, i, 0)),
                  pl.BlockSpec((tm, TOP_K), lambda i: (i, 0)),
                  pl.BlockSpec((1, 1, D), lambda i: (i // tiles_per_seq, 0, 0)),
                  pl.BlockSpec((D, F), lambda i: (0, 0)),
                  pl.BlockSpec((D, F), lambda i: (0, 0)),
                  pl.BlockSpec((F, D), lambda i: (0, 0)),
                  row, row],
        out_specs=tok,
        compiler_params=_cparams("arbitrary"),
        name="moe_combine_final",
    )(x1, h2, yg, gate, g2, ws1, ws3, ws2, ln2g, ln2b)


def kernel(x, c, ada_w, ada_b, w_in, b_in, lambda_q1, lambda_k1, lambda_q2, lambda_k2, subln_g, w_attn_o, conv_w, conv_b, conv_ln_g, conv_ln_b, conv_pw_w, conv_pw_b, w_out, ln1_g, ln1_b, router_w, router_bias, w1, w3, w2, ws1, ws3, ws2, ln2_g, ln2_b):
    B, S, D = x.shape
    T = B * S
    l = 0
    x2 = x.reshape(T, D)
    row = lambda a: a.reshape(1, -1)

    mod = _ada_modulation(c, ada_w[l], ada_b[l])
    sh1, sc1, g1, sh2, sc2, g2 = [m.reshape(B, 1, D) for m in jnp.split(mod, 6, axis=-1)]

    qt, k, vt, u, ga, gb = _in_projection(x2, sc1, sh1, w_in[l].astype(BF16), row(b_in[l]), S)
    o = _diff_attention(qt, k, vt, row(lambda_q1[l]), row(lambda_k1[l]), row(lambda_q2[l]),
                        row(lambda_k2[l]), row(subln_g[l]), B, S)

    wr = router_w[l].T
    wr_hi = wr.astype(BF16)
    wr_lo = (wr - wr_hi.astype(F32)).astype(BF16)
    x1, h2, logits = _mixer_epilogue(
        x2, o, u, ga, gb, g1, sc2, sh2, w_attn_o[l].astype(BF16), conv_w[l], row(conv_b[l]),
        row(conv_ln_g[l]), row(conv_ln_b[l]), conv_pw_w[l].astype(BF16), row(conv_pw_b[l]),
        w_out[l].astype(BF16), row(ln1_g[l]), row(ln1_b[l]), wr_hi, wr_lo, S)

    idx, gate, counts = _route(logits, router_bias[l])

    blk = EXPERT_BLK
    cnt = counts.reshape(N_EXPERTS).astype(jnp.int32)
    padded = (cnt + blk - 1) // blk * blk
    pad_end = jnp.cumsum(padded)
    pad_start = pad_end - padded
    P = T * TOP_K + N_EXPERTS * blk
    n_blocks = P // blk
    blk_first_row = jnp.arange(n_blocks, dtype=jnp.int32) * blk
    blk_expert = jnp.minimum(
        jnp.sum((pad_end[None, :] <= blk_first_row[:, None]).astype(jnp.int32), axis=1),
        N_EXPERTS - 1)
    n_used = (pad_end[-1:] // blk).astype(jnp.int32)

    pos = _assign_slots(idx, pad_start.astype(F32).reshape(N_EXPERTS, 1))

    Dq = D // 4
    half_base = (jnp.arange(2, dtype=jnp.int32) * P)[None, :, None]
    slots = pos[:, None, :] + half_base
    xs = _sc_dispatch(h2.reshape(2 * T, Dq), slots.reshape(TOP_K, 2 * T), 2 * P)
    ys = _routed_experts(xs.reshape(2, P, Dq), blk_expert, n_used, w1[l], w3[l], w2[l])
    yg = _sc_gather(ys.reshape(2 * P, Dq), slots.reshape(1, TOP_K * 2 * T))
    yg = yg.reshape(TOP_K, 2, T, Dq)

    out = _final(x1, h2, yg, gate.T, g2, ws1[l].astype(BF16), ws3[l].astype(BF16),
                 ws2[l].astype(BF16), row(ln2_g[l]), row(ln2_b[l]), S)
    return out.reshape(B, S, D)
```

```python
import math
from functools import partial

import jax
import jax.numpy as jnp
from jax import lax
from jax.experimental import pallas as pl
from jax.experimental.pallas import tpu as pltpu
from jax.experimental.pallas import tpu_sc as plsc

D_MODEL = 1024
N_HEADS = 8
HEAD_DIM = 64
V_DIM = 128
CONV_K = 31
N_EXPERTS = 256
TOP_K = 8
N_GROUPS = 8
GROUP_SIZE = N_EXPERTS // N_GROUPS
TOPK_GROUPS = 4
ROUTED_SCALE = 2.5
LN_EPS = 1e-5
RMS_EPS = 1e-5
DEPTH = 1
DN_ALPHA = (2.0 * DEPTH) ** 0.25
LAMBDA_INIT = 0.8 - 0.6 * math.exp(-0.3 * 0)
Q_SCALE = HEAD_DIM ** -0.5 * math.log2(math.e)

LANES = 128
SUBLANES = 8
VMEM_LIMIT_BYTES = 56 * 1024 * 1024

PROJ_TM = 512
ATTN_TQ = 512
ATTN_QG = 256
ATTN_KC = 256
ONES_ROWS = 16
MIX_TM = 512
MIX_SUB = 256
CONV_HALO = 32
CONV_ROWS = 64
ROUTE_TM = 512
EXPERT_BLK = 512
FINAL_TM = 256
SC_WINDOW = 128

F32 = jnp.float32
BF16 = jnp.bfloat16
U32 = jnp.uint32
NEG_INF = float("-inf")


def _cparams(*sem):
    return pltpu.CompilerParams(dimension_semantics=sem, vmem_limit_bytes=VMEM_LIMIT_BYTES)


def _sigmoid(x):
    return 1.0 / (1.0 + jnp.exp(-x))


def _silu(x):
    return x * _sigmoid(x)


BF16_BITS = 16
HIGH_HALF_WORD = 0xFFFF0000


def _pack_halves(x):
    half = x.shape[-1] // 2
    bits = lambda v: lax.bitcast_convert_type(v.astype(BF16).astype(F32), U32)
    return (bits(x[:, :half]) >> BF16_BITS) | (bits(x[:, half:]) & jnp.uint32(HIGH_HALF_WORD))


def _unpack_halves(w):
    lo = lax.bitcast_convert_type(w << BF16_BITS, F32)
    hi = lax.bitcast_convert_type(w & jnp.uint32(HIGH_HALF_WORD), F32)
    return lo, hi


def _store_packed(ref, x):
    w = _pack_halves(x)
    q = w.shape[-1] // 2
    ref[0] = w[:, :q]
    ref[1] = w[:, q:]


def _load_packed(ref):
    return _unpack_halves(jnp.concatenate([ref[0], ref[1]], axis=-1))


def _layer_norm(r, g, b):
    mu = jnp.mean(r, axis=-1, keepdims=True)
    d = r - mu
    var = jnp.mean(d * d, axis=-1, keepdims=True)
    return d * lax.rsqrt(var + LN_EPS) * g + b


def _ada_kernel(c_ref, w_ref, b_ref, o_ref):
    a = _silu(c_ref[...])
    o_ref[...] = jnp.dot(a, w_ref[...], preferred_element_type=F32,
                         precision=lax.Precision.HIGHEST) + b_ref[...]


def _ada_modulation(c, w, b):
    B, D = c.shape
    n = w.shape[1] // D
    return pl.pallas_call(
        _ada_kernel,
        out_shape=jax.ShapeDtypeStruct((B, n * D), F32),
        grid=(n,),
        in_specs=[pl.BlockSpec((B, D), lambda j: (0, 0)),
                  pl.BlockSpec((D, D), lambda j: (0, j)),
                  pl.BlockSpec((1, D), lambda j: (0, j))],
        out_specs=pl.BlockSpec((B, D), lambda j: (0, j)),
        compiler_params=_cparams("arbitrary"),
        name="ada_modulation",
    )(c, w, b.reshape(1, n * D))


def _inproj_kernel(x_ref, sc_ref, sh_ref, w_ref, b_ref,
                   qt_ref, k_ref, vt_ref, u_ref, ga_ref, gb_ref):
    D = D_MODEL
    h = (x_ref[...] * (1.0 + sc_ref[0]) + sh_ref[0]).astype(BF16)

    def proj(j):
        return (jnp.dot(h, w_ref[:, j * D:(j + 1) * D], preferred_element_type=F32)
                + b_ref[:, j * D:(j + 1) * D])

    qt_ref[...] = (proj(0) * Q_SCALE).T.astype(BF16)
    k_ref[...] = proj(1).astype(BF16)
    vt_ref[...] = proj(2).T.astype(BF16)
    u_ref[...] = (proj(3) * _sigmoid(proj(4))).astype(BF16)
    ga_ref[...] = _sigmoid(proj(5)).astype(BF16)
    gb_ref[...] = _sigmoid(proj(6)).astype(BF16)


def _in_projection(x2, sc1, sh1, w_in, b_in, seq):
    T, D = x2.shape
    tm = PROJ_TM
    tiles_per_seq = seq // tm
    ncols = w_in.shape[1]
    tok = pl.BlockSpec((tm, D), lambda i: (i, 0))
    mod = pl.BlockSpec((1, 1, D), lambda i: (i // tiles_per_seq, 0, 0))
    tok_t = pl.BlockSpec((D, tm), lambda i: (0, i))
    out = jax.ShapeDtypeStruct((T, D), BF16)
    out_t = jax.ShapeDtypeStruct((D, T), BF16)
    return pl.pallas_call(
        _inproj_kernel,
        out_shape=[out_t, out, out_t, out, out, out],
        grid=(T // tm,),
        in_specs=[tok, mod, mod,
                  pl.BlockSpec((D, ncols), lambda i: (0, 0)),
                  pl.BlockSpec((1, ncols), lambda i: (0, 0))],
        out_specs=[tok_t, tok, tok_t, tok, tok, tok],
        compiler_params=_cparams("arbitrary"),
        name="in_projection",
    )(x2, sc1, sh1, w_in, b_in)


def _attn_tasks(nq):
    tasks = [(qi, j, int(j == qi)) for qi in range(nq) for j in range(qi + 1)]
    if len(tasks) % 2:
        raise ValueError("the two-slot task loop needs an even number of tile pairs")
    qis, js, lasts = zip(*(tasks + [(0, 0, 0)]))
    as_i32 = lambda v: jnp.asarray(v, jnp.int32)
    return as_i32(qis), as_i32(js), as_i32(lasts), len(tasks)


def _attn_kernel(tqi_ref, tj_ref, tlast_ref, lq1_ref, lk1_ref, lq2_ref, lk2_ref, g_ref,
                 qt_ref, k_ref, vt_ref, o_ref, q2_ref, vx_ref, *scratch, n_tasks):
    tq = ATTN_TQ
    qg = ATTN_QG
    ng = 2 * tq // qg
    groups = [scratch[i * ng:(i + 1) * ng] for i in range(7)]
    s_slots, x_slots = groups[0:2], groups[2:4]
    m_refs, l_refs, acc_refs = groups[4:]
    seq = k_ref.shape[0]

    vx_ref[:V_DIM, :] = vt_ref[...]
    vx_ref[V_DIM:, :] = jnp.ones((ONES_ROWS, seq), BF16)

    lam = (jnp.exp(jnp.sum(lq1_ref[...] * lk1_ref[...], axis=-1, keepdims=True))
           - jnp.exp(jnp.sum(lq2_ref[...] * lk2_ref[...], axis=-1, keepdims=True))
           + LAMBDA_INIT)

    for i in range(seq // tq):
        qt = qt_ref[:, i * tq:(i + 1) * tq]
        feat = lax.broadcasted_iota(jnp.int32, qt.shape, 0)
        zero = jnp.zeros_like(qt)
        q2_ref[:, 2 * tq * i:2 * tq * i + tq] = jnp.where(feat < HEAD_DIM, qt, zero)
        q2_ref[:, 2 * tq * i + tq:2 * tq * (i + 1)] = jnp.where(feat >= HEAD_DIM, qt, zero)

    def scores(qi, j, slot, g):
        start = pl.multiple_of(j * tq, tq)
        col = pl.multiple_of(qi * (2 * tq) + g * qg, qg)
        s = jnp.dot(k_ref[pl.ds(start, tq), :], q2_ref[:, pl.ds(col, qg)],
                    preferred_element_type=F32)
        s_slots[slot][g][...] = s
        x_slots[slot][g][...] = jnp.max(s, axis=0, keepdims=True)

    def update(j, slot, g, masked):
        start = pl.multiple_of(j * tq, tq)
        s_ref = s_slots[slot][g]

        def load(k0):
            s = s_ref[k0:k0 + ATTN_KC, :]
            if masked:
                key = lax.broadcasted_iota(jnp.int32, s.shape, 0) + k0
                qry = lax.broadcasted_iota(jnp.int32, s.shape, 1) + (g * qg) % tq
                s = jnp.where(key <= qry, s, NEG_INF)
            return s

        m_prev = m_refs[g][...]
        if masked:
            tile_max = jnp.max(load(0), axis=0, keepdims=True)
            for k0 in range(ATTN_KC, tq, ATTN_KC):
                tile_max = jnp.maximum(tile_max, jnp.max(load(k0), axis=0, keepdims=True))
        else:
            tile_max = x_slots[slot][g][...]
        m_new = jnp.maximum(m_prev, tile_max)
        alpha = jnp.exp2(m_prev - m_new)
        m_refs[g][...] = m_new
        pv = None
        for k0 in range(0, tq, ATTN_KC):
            p = jnp.exp2((load(k0) - m_new).astype(BF16))
            keys = pl.ds(pl.multiple_of(start + k0, ATTN_KC), ATTN_KC)
            part = jnp.dot(vx_ref[:, keys], p, preferred_element_type=F32)
            pv = part if pv is None else pv + part
        acc_refs[g][...] = alpha * acc_refs[g][...] + pv[:V_DIM, :]
        l_refs[g][...] = alpha * l_refs[g][...] + pv[V_DIM:V_DIM + 1, :]

    def reset():
        for g in range(ng):
            m_refs[g][...] = jnp.full(m_refs[g].shape, NEG_INF, F32)
            l_refs[g][...] = jnp.zeros(l_refs[g].shape, F32)
            acc_refs[g][...] = jnp.zeros(acc_refs[g].shape, F32)

    def finish(qi):
        o = (jnp.concatenate([r[...] for r in acc_refs], axis=-1)
             / jnp.concatenate([r[...] for r in l_refs], axis=-1))
        o = o[:, :tq] - lam * o[:, tq:]
        o = o * lax.rsqrt(jnp.mean(o * o, axis=0, keepdims=True) + RMS_EPS) * g_ref[...]
        row = pl.multiple_of(qi * tq, tq)
        o_ref[pl.ds(row, tq), :] = (o * (1.0 - LAMBDA_INIT)).T.astype(BF16)

    def task(t, slot):
        qi, j = tqi_ref[t], tj_ref[t]
        nqi, nj = tqi_ref[t + 1], tj_ref[t + 1]

        def run(masked):
            for g in range(ng):
                scores(nqi, nj, 1 - slot, g)
                update(j, slot, g, masked)

        @pl.when(tlast_ref[t] == 0)
        def _():
            run(False)

        @pl.when(tlast_ref[t] == 1)
        def _():
            run(True)
            finish(qi)
            reset()

    reset()
    for g in range(ng):
        scores(0, 0, 0, g)

    def pair(i, carry):
        task(2 * i, 0)
        task(2 * i + 1, 1)
        return carry

    lax.fori_loop(0, n_tasks // 2, pair, 0)


def _diff_attention(qt, k, vt, lq1, lk1, lq2, lk2, subln_g, batch, seq):
    T, D = k.shape
    tq = ATTN_TQ
    nq = seq // tq
    tqi, tj, tlast, n_tasks = _attn_tasks(nq)
    lam_spec = pl.BlockSpec((1, HEAD_DIM), lambda b, h, *_: (0, 0))
    rows_spec = pl.BlockSpec((seq, V_DIM), lambda b, h, *_: (b, h))
    cols_spec = pl.BlockSpec((V_DIM, seq), lambda b, h, *_: (h, b))
    qg = ATTN_QG
    per_group = lambda shape, dtype: [pltpu.VMEM(shape, dtype) for _ in range(2 * tq // qg)]
    grid_spec = pltpu.PrefetchScalarGridSpec(
        num_scalar_prefetch=3,
        grid=(batch, N_HEADS),
        in_specs=[lam_spec, lam_spec, lam_spec, lam_spec,
                  pl.BlockSpec((V_DIM, 1), lambda b, h, *_: (0, 0)),
                  cols_spec, rows_spec, cols_spec],
        out_specs=rows_spec,
        scratch_shapes=(
            [pltpu.VMEM((V_DIM, 2 * seq), BF16),
             pltpu.VMEM((V_DIM + ONES_ROWS, seq), BF16)]
            + per_group((tq, qg), F32)
            + per_group((tq, qg), F32)
            + per_group((1, qg), F32)
            + per_group((1, qg), F32)
            + per_group((1, qg), F32)
            + per_group((1, qg), F32)
            + per_group((V_DIM, qg), F32)),
    )
    return pl.pallas_call(
        partial(_attn_kernel, n_tasks=n_tasks),
        out_shape=jax.ShapeDtypeStruct((T, D), BF16),
        grid_spec=grid_spec,
        compiler_params=_cparams("arbitrary", "arbitrary"),
        name="diff_attention",
    )(tqi, tj, tlast, lq1, lk1, lq2, lk2, subln_g.reshape(V_DIM, 1), qt, k, vt)


def _mixer_kernel(x_ref, o_ref, u_ref, up_ref, ga_ref, gb_ref, g1_ref, sc2_ref, sh2_ref,
                  wao_ref, cw_ref, cb_ref, clg_ref, clb_ref, wpw_ref, bpw_ref, wout_ref,
                  ln1g_ref, ln1b_ref, wrh_ref, wrl_ref,
                  x1_ref, h2_ref, lg_ref, ubuf_ref, ush_ref, conv_ref, *, tiles_per_seq):
    tm = MIX_TM
    first = (pl.program_id(0) % tiles_per_seq) == 0

    prev = up_ref[...].astype(F32)
    ubuf_ref[0:CONV_HALO, :] = jnp.where(first, jnp.zeros_like(prev), prev)
    ubuf_ref[CONV_HALO:, :] = u_ref[...].astype(F32)
    for t0 in range(0, tm, MIX_SUB):
        rows = slice(t0, t0 + MIX_SUB)

        base = CONV_HALO - (CONV_K - 1)
        n_shift = MIX_SUB + CONV_HALO - SUBLANES
        for r in range(1, SUBLANES):
            ush_ref[r - 1] = ubuf_ref[t0 + r:t0 + r + n_shift, :]
        for c in range(D_MODEL // LANES):
            cl = slice(c * LANES, (c + 1) * LANES)
            for r0 in range(0, MIX_SUB, CONV_ROWS):
                acc = jnp.zeros((CONV_ROWS, LANES), F32)
                for j in range(CONV_K):
                    a, r = divmod(base + j, SUBLANES)
                    lo = SUBLANES * a + r0
                    win = (ubuf_ref[t0 + lo:t0 + lo + CONV_ROWS, cl] if r == 0
                           else ush_ref[r - 1, lo:lo + CONV_ROWS, cl])
                    acc = acc + cw_ref[j:j + 1, cl] * win
                conv_ref[t0 + r0:t0 + r0 + CONV_ROWS, cl] = acc
        yc = _silu(_layer_norm(conv_ref[rows, :] + cb_ref[...], clg_ref[...], clb_ref[...]))
        y_conv = (jnp.dot(yc.astype(BF16), wpw_ref[...], preferred_element_type=F32)
                  + bpw_ref[...])

        y_attn = jnp.dot(o_ref[rows, :], wao_ref[...], preferred_element_type=F32)
        mix = ga_ref[rows, :].astype(F32) * y_attn + gb_ref[rows, :].astype(F32) * y_conv
        y = jnp.dot(mix.astype(BF16), wout_ref[...], preferred_element_type=F32)
        x1 = _layer_norm(DN_ALPHA * x_ref[rows, :] + g1_ref[0] * y, ln1g_ref[...], ln1b_ref[...])
        x1_ref[rows, :] = x1

        h2 = x1 * (1.0 + sc2_ref[0]) + sh2_ref[0]
        h2_hi = h2.astype(BF16)
        _store_packed(h2_ref.at[:, rows, :], h2)
        h2_lo = (h2 - h2_hi.astype(F32)).astype(BF16)
        nt = lambda w, a: lax.dot_general(w, a, (((1,), (1,)), ((), ())),
                                          preferred_element_type=F32)
        lg_ref[:, rows] = (nt(wrh_ref[...], h2_hi)
                           + (nt(wrl_ref[...], h2_hi) + nt(wrh_ref[...], h2_lo)))


def _mixer_epilogue(x2, o, u, ga, gb, g1, sc2, sh2, wao, cw, cb, clg, clb, wpw, bpw, wout,
                    ln1g, ln1b, wr_hi, wr_lo, seq):
    T, D = x2.shape
    tm = MIX_TM
    tiles_per_seq = seq // tm
    halo_per_tile = tm // CONV_HALO
    tok = pl.BlockSpec((tm, D), lambda i: (i, 0))
    prev = pl.BlockSpec((CONV_HALO, D), lambda i: (jnp.maximum(i * halo_per_tile - 1, 0), 0))
    mod = pl.BlockSpec((1, 1, D), lambda i: (i // tiles_per_seq, 0, 0))
    row = pl.BlockSpec((1, D), lambda i: (0, 0))
    mat = pl.BlockSpec((D, D), lambda i: (0, 0), pipeline_mode=pl.Buffered(1))
    rw = pl.BlockSpec((N_EXPERTS, D), lambda i: (0, 0), pipeline_mode=pl.Buffered(1))
    return pl.pallas_call(
        partial(_mixer_kernel, tiles_per_seq=tiles_per_seq),
        out_shape=[jax.ShapeDtypeStruct((T, D), F32),
                   jax.ShapeDtypeStruct((2, T, D // 4), U32),
                   jax.ShapeDtypeStruct((N_EXPERTS, T), F32)],
        grid=(T // tm,),
        in_specs=[tok, tok, tok, prev, tok, tok, mod, mod, mod,
                  mat, pl.BlockSpec((CONV_K, D), lambda i: (0, 0)), row, row, row, mat, row, mat,
                  row, row, rw, rw],
        out_specs=[tok, pl.BlockSpec((2, tm, D // 4), lambda i: (0, i, 0)),
                   pl.BlockSpec((N_EXPERTS, tm), lambda i: (0, i))],
        scratch_shapes=[pltpu.VMEM((CONV_HALO + tm, D), F32),
                        pltpu.VMEM((SUBLANES - 1, MIX_SUB + CONV_HALO - SUBLANES, D), F32),
                        pltpu.VMEM((tm, D), F32)],
        compiler_params=_cparams("arbitrary"),
        name="mixer_epilogue",
    )(x2, o, u, u, ga, gb, g1, sc2, sh2, wao, cw, cb, clg, clb, wpw, bpw, wout,
      ln1g, ln1b, wr_hi, wr_lo)


def _first_argmax(v, eid):
    m = jnp.max(v, axis=0, keepdims=True)
    i = jnp.min(jnp.where(v == m, eid, float(N_EXPERTS)), axis=0, keepdims=True)
    return m, i


def _route_kernel(lg_ref, rb_ref, idx_ref, gate_ref, cnt_ref):
    tm = ROUTE_TM
    scores = _sigmoid(lg_ref[...])
    choice = scores + rb_ref[...]
    eid = lax.broadcasted_iota(jnp.int32, (N_EXPERTS, tm), 0).astype(F32)
    rows = lambda a, g: a[g * GROUP_SIZE:(g + 1) * GROUP_SIZE, :]

    gs = []
    for g in range(N_GROUPS):
        cg = rows(choice, g)
        eg = (lax.broadcasted_iota(jnp.int32, cg.shape, 0) + g * GROUP_SIZE).astype(F32)
        m1, i1 = _first_argmax(cg, eg)
        m2 = jnp.max(jnp.where(eg == i1, NEG_INF, cg), axis=0, keepdims=True)
        gs.append(jnp.broadcast_to(m1 + m2, (SUBLANES, tm)))

    kept = []
    for g in range(N_GROUPS):
        ahead = jnp.zeros((SUBLANES, tm), F32)
        for o in range(N_GROUPS):
            if o == g:
                continue
            beats = (gs[o] >= gs[g]) if o < g else (gs[o] > gs[g])
            ahead = ahead + jnp.where(beats, 1.0, 0.0)
        cap = jnp.where(ahead < TOPK_GROUPS, float("inf"), NEG_INF)
        for r in range(g * GROUP_SIZE, (g + 1) * GROUP_SIZE, SUBLANES):
            kept.append(jnp.minimum(choice[r:r + SUBLANES, :], cap))
    masked = jnp.concatenate(kept, axis=0)

    sel = jnp.zeros((N_EXPERTS, tm), F32)
    total = jnp.zeros((1, tm), F32)
    picked = []
    for k in range(TOP_K):
        _, i = _first_argmax(masked, eid)
        hit = eid == i
        sc = jnp.sum(jnp.where(hit, scores, 0.0), axis=0, keepdims=True)
        masked = jnp.where(hit, NEG_INF, masked)
        sel = sel + hit.astype(F32)
        total = total + sc
        idx_ref[k:k + 1, :] = i.astype(jnp.int32)
        picked.append(sc)
    for k in range(TOP_K):
        gate_ref[k:k + 1, :] = picked[k] / total * ROUTED_SCALE

    @pl.when(pl.program_id(0) == 0)
    def _():
        cnt_ref[...] = jnp.zeros_like(cnt_ref)
    cnt_ref[...] += jnp.sum(sel, axis=1, keepdims=True)


def _route(logits_t, router_bias):
    T = logits_t.shape[1]
    tm = ROUTE_TM
    return pl.pallas_call(
        _route_kernel,
        out_shape=[jax.ShapeDtypeStruct((TOP_K, T), jnp.int32),
                   jax.ShapeDtypeStruct((TOP_K, T), F32),
                   jax.ShapeDtypeStruct((N_EXPERTS, 1), F32)],
        grid=(T // tm,),
        in_specs=[pl.BlockSpec((N_EXPERTS, tm), lambda i: (0, i)),
                  pl.BlockSpec((N_EXPERTS, 1), lambda i: (0, 0))],
        out_specs=[pl.BlockSpec((TOP_K, tm), lambda i: (0, i)),
                   pl.BlockSpec((TOP_K, tm), lambda i: (0, i)),
                   pl.BlockSpec((N_EXPERTS, 1), lambda i: (0, 0))],
        compiler_params=_cparams("arbitrary"),
        name="route_topk",
    )(logits_t, router_bias.reshape(N_EXPERTS, 1))


def _slot_kernel(idx_ref, start_ref, pos_ref, carry_ref):
    tm = ROUTE_TM

    @pl.when(pl.program_id(0) == 0)
    def _():
        carry_ref[...] = jnp.zeros_like(carry_ref)

    idx = idx_ref[...]
    eid = lax.broadcasted_iota(jnp.int32, (N_EXPERTS, tm), 0)
    hits = [eid == idx[k:k + 1, :] for k in range(TOP_K)]
    sel = jnp.zeros((N_EXPERTS, tm), F32)
    for h in hits:
        sel = sel + h.astype(F32)
    r = lax.broadcasted_iota(jnp.int32, (tm, tm), 0)
    c = lax.broadcasted_iota(jnp.int32, (tm, tm), 1)
    earlier = (r < c).astype(BF16)
    rank = jnp.dot(sel.astype(BF16), earlier, preferred_element_type=F32)
    slot = rank + carry_ref[...] + start_ref[...]
    for k, h in enumerate(hits):
        pos_ref[k:k + 1, :] = jnp.sum(jnp.where(h, slot, 0.0), axis=0,
                                      keepdims=True).astype(jnp.int32)
    carry_ref[...] += jnp.sum(sel, axis=1, keepdims=True)


def _assign_slots(idx_t, pad_start):
    T = idx_t.shape[1]
    tm = ROUTE_TM
    return pl.pallas_call(
        _slot_kernel,
        out_shape=jax.ShapeDtypeStruct((TOP_K, T), jnp.int32),
        grid=(T // tm,),
        in_specs=[pl.BlockSpec((TOP_K, tm), lambda i: (0, i)),
                  pl.BlockSpec((N_EXPERTS, 1), lambda i: (0, 0))],
        out_specs=pl.BlockSpec((TOP_K, tm), lambda i: (0, i)),
        scratch_shapes=[pltpu.VMEM((N_EXPERTS, 1), F32)],
        compiler_params=_cparams("arbitrary"),
        name="assign_slots",
    )(idx_t, pad_start)


def _expert_kernel(be_ref, nu_ref, first_ref, slot_ref, nxt_ref, xs_ref, w1_hbm, w3_hbm, w2_hbm,
                   ys_ref, w1_buf, w3_buf, w2_buf, sem):
    b = pl.program_id(0)

    def weight_copies(expert, slot):
        pairs = ((w1_hbm, w1_buf), (w3_hbm, w3_buf), (w2_hbm, w2_buf))
        return [pltpu.make_async_copy(hbm.at[expert], buf.at[slot], sem.at[slot, i])
                for i, (hbm, buf) in enumerate(pairs)]

    @pl.when(b < nu_ref[0])
    def _():
        slot = slot_ref[b]

        @pl.when(b == 0)
        def _():
            for c in weight_copies(be_ref[0], 0):
                c.start()

        @pl.when(first_ref[b] == 1)
        def _():
            for c in weight_copies(be_ref[b], slot):
                c.wait()

            @pl.when(nxt_ref[b] >= 0)
            def _():
                for c in weight_copies(nxt_ref[b], 1 - slot):
                    c.start()

        lo, hi = _load_packed(xs_ref)
        half = lo.shape[-1]
        mm = lambda w: (jnp.dot(lo, w[slot, :half, :], preferred_element_type=F32)
                        + jnp.dot(hi, w[slot, half:, :], preferred_element_type=F32))
        hb = _silu(mm(w1_buf)) * mm(w3_buf)
        _store_packed(ys_ref, jnp.dot(hb, w2_buf[slot], preferred_element_type=F32))

    @pl.when(b >= nu_ref[0])
    def _():
        ys_ref[...] = jnp.zeros_like(ys_ref)


def _routed_experts(xs, blk_expert, n_used, w1, w3, w2):
    _, P, Dq = xs.shape
    blk = EXPERT_BLK
    n_blocks = P // blk
    D, F = w1.shape[-2:]

    b_ids = jnp.arange(n_blocks, dtype=jnp.int32)
    used = b_ids < n_used[0]
    prev = jnp.concatenate([jnp.full((1,), -1, jnp.int32), blk_expert[:-1]])
    first = (used & (blk_expert != prev)).astype(jnp.int32)
    slot = (jnp.cumsum(first) - 1) % 2
    start_at = jnp.where(first == 1, b_ids, n_blocks)
    later = jnp.concatenate([start_at[1:], jnp.full((1,), n_blocks, jnp.int32)])
    next_start = lax.cummin(later, axis=0, reverse=True)
    nxt = jnp.where(next_start < n_blocks,
                    blk_expert[jnp.minimum(next_start, n_blocks - 1)], -1).astype(jnp.int32)

    row_blk = lambda b, be, nu, *_: (0, jnp.minimum(b, nu[0] - 1), 0)
    grid_spec = pltpu.PrefetchScalarGridSpec(
        num_scalar_prefetch=5,
        grid=(n_blocks,),
        in_specs=[pl.BlockSpec((2, blk, Dq), row_blk),
                  pl.BlockSpec(memory_space=pl.ANY),
                  pl.BlockSpec(memory_space=pl.ANY),
                  pl.BlockSpec(memory_space=pl.ANY)],
        out_specs=pl.BlockSpec((2, blk, Dq), lambda b, *_: (0, b, 0)),
        scratch_shapes=[pltpu.VMEM((2, D, F), F32), pltpu.VMEM((2, D, F), F32),
                        pltpu.VMEM((2, F, D), F32), pltpu.SemaphoreType.DMA((2, 3))],
    )
    return pl.pallas_call(
        _expert_kernel,
        out_shape=jax.ShapeDtypeStruct((2, P, Dq), U32),
        grid_spec=grid_spec,
        compiler_params=_cparams("arbitrary"),
        name="routed_experts",
    )(blk_expert, n_used, first, slot.astype(jnp.int32), nxt, xs, w1, w3, w2)


def _sc_mesh():
    return plsc.VectorSubcoreMesh(core_axis_name="c", subcore_axis_name="s")


def _sc_dispatch(rows, slots_t, n_slots):
    T, W = rows.shape
    K = slots_t.shape[0]
    win = SC_WINDOW

    @pl.kernel(out_type=jax.ShapeDtypeStruct((n_slots, W), rows.dtype), mesh=_sc_mesh(),
               scratch_types=[])
    def dispatch(x_hbm, i_hbm, o_hbm):
        def body(x_vmem, i_vmem):
            for k in range(K):
                pltpu.sync_copy(x_vmem, o_hbm.at[i_vmem.at[k]])

        pltpu.emit_pipeline(
            body,
            grid=(T // win,),
            in_specs=[pl.BlockSpec((win, W), lambda i: (i, 0)),
                      pl.BlockSpec((K, win), lambda i: (0, i))],
            out_specs=[],
            core_axis_name=("c", "s"),
            dimension_semantics=(pltpu.PARALLEL,),
        )(x_hbm, i_hbm)

    return dispatch(rows, slots_t)


def _sc_gather(table, idx):
    W = table.shape[1]
    N = idx.shape[1]
    win = SC_WINDOW

    @pl.kernel(out_type=jax.ShapeDtypeStruct((N, W), table.dtype), mesh=_sc_mesh(),
               scratch_types=[])
    def gather(x_hbm, i_hbm, o_hbm):
        def body(i_vmem, o_vmem):
            pltpu.sync_copy(x_hbm.at[i_vmem.at[0]], o_vmem)

        pltpu.emit_pipeline(
            body,
            grid=(N // win,),
            in_specs=[pl.BlockSpec((1, win), lambda i: (0, i))],
            out_specs=[pl.BlockSpec((win, W), lambda i: (i, 0))],
            core_axis_name=("c", "s"),
            dimension_semantics=(pltpu.PARALLEL,),
        )(i_hbm, o_hbm)

    return gather(table, idx)


def _final_kernel(x1_ref, h2_ref, yg_ref, gate_ref, g2_ref, ws1_ref, ws3_ref, ws2_ref,
                  lng_ref, lnb_ref, o_ref):
    Dh = D_MODEL // 2
    gate = gate_ref[...]
    routed_lo = jnp.zeros((x1_ref.shape[0], Dh), F32)
    routed_hi = jnp.zeros((x1_ref.shape[0], Dh), F32)
    for k in range(TOP_K):
        lo, hi = _load_packed(yg_ref.at[k])
        routed_lo = routed_lo + lo * gate[:, k:k + 1]
        routed_hi = routed_hi + hi * gate[:, k:k + 1]
    h2 = jnp.concatenate(_load_packed(h2_ref), axis=-1).astype(BF16)
    a = jnp.dot(h2, ws1_ref[...], preferred_element_type=F32)
    b = jnp.dot(h2, ws3_ref[...], preferred_element_type=F32)
    shared = jnp.dot((_silu(a) * b).astype(BF16), ws2_ref[...], preferred_element_type=F32)
    y2 = jnp.concatenate([routed_lo, routed_hi], axis=-1) + shared
    o_ref[...] = _layer_norm(DN_ALPHA * x1_ref[...] + g2_ref[0] * y2, lng_ref[...], lnb_ref[...])


def _final(x1, h2, yg, gate, g2, ws1, ws3, ws2, ln2g, ln2b, seq):
    T, D = x1.shape
    tm = FINAL_TM
    tiles_per_seq = seq // tm
    F = ws1.shape[-1]
    tok = pl.BlockSpec((tm, D), lambda i: (i, 0))
    row = pl.BlockSpec((1, D), lambda i: (0, 0))
    return pl.pallas_call(
        _final_kernel,
        out_shape=jax.ShapeDtypeStruct((T, D), F32),
        grid=(T // tm,),
        in_specs=[tok, pl.BlockSpec((2, tm, D // 4), lambda i: (0, i, 0)),
                  pl.BlockSpec((TOP_K, 2, tm, D // 4), lambda i: (0, 0, i, 0)),
                  pl.BlockSpec((tm, TOP_K), lambda i: (i, 0)),
                  pl.BlockSpec((1, 1, D), lambda i: (i // tiles_per_seq, 0, 0)),
                  pl.BlockSpec((D, F), lambda i: (0, 0)),
                  pl.BlockSpec((D, F), lambda i: (0, 0)),
                  pl.BlockSpec((F, D), lambda i: (0, 0)),
                  row, row],
        out_specs=tok,
        compiler_params=_cparams("arbitrary"),
        name="moe_combine_final",
    )(x1, h2, yg, gate, g2, ws1, ws3, ws2, ln2g, ln2b)


def kernel(x, c, ada_w, ada_b, w_in, b_in, lambda_q1, lambda_k1, lambda_q2, lambda_k2, subln_g, w_attn_o, conv_w, conv_b, conv_ln_g, conv_ln_b, conv_pw_w, conv_pw_b, w_out, ln1_g, ln1_b, router_w, router_bias, w1, w3, w2, ws1, ws3, ws2, ln2_g, ln2_b):
    B, S, D = x.shape
    T = B * S
    l = 0
    x2 = x.reshape(T, D)
    row = lambda a: a.reshape(1, -1)

    mod = _ada_modulation(c, ada_w[l], ada_b[l])
    sh1, sc1, g1, sh2, sc2, g2 = [m.reshape(B, 1, D) for m in jnp.split(mod, 6, axis=-1)]

    qt, k, vt, u, ga, gb = _in_projection(x2, sc1, sh1, w_in[l].astype(BF16), row(b_in[l]), S)
    o = _diff_attention(qt, k, vt, row(lambda_q1[l]), row(lambda_k1[l]), row(lambda_q2[l]),
                        row(lambda_k2[l]), row(subln_g[l]), B, S)

    wr = router_w[l].T
    wr_hi = wr.astype(BF16)
    wr_lo = (wr - wr_hi.astype(F32)).astype(BF16)
    x1, h2, logits = _mixer_epilogue(
        x2, o, u, ga, gb, g1, sc2, sh2, w_attn_o[l].astype(BF16), conv_w[l], row(conv_b[l]),
        row(conv_ln_g[l]), row(conv_ln_b[l]), conv_pw_w[l].astype(BF16), row(conv_pw_b[l]),
        w_out[l].astype(BF16), row(ln1_g[l]), row(ln1_b[l]), wr_hi, wr_lo, S)

    idx, gate, counts = _route(logits, router_bias[l])

    blk = EXPERT_BLK
    cnt = counts.reshape(N_EXPERTS).astype(jnp.int32)
    padded = (cnt + blk - 1) // blk * blk
    pad_end = jnp.cumsum(padded)
    pad_start = pad_end - padded
    P = T * TOP_K + N_EXPERTS * blk
    n_blocks = P // blk
    blk_first_row = jnp.arange(n_blocks, dtype=jnp.int32) * blk
    blk_expert = jnp.minimum(
        jnp.sum((pad_end[None, :] <= blk_first_row[:, None]).astype(jnp.int32), axis=1),
        N_EXPERTS - 1)
    n_used = (pad_end[-1:] // blk).astype(jnp.int32)

    pos = _assign_slots(idx, pad_start.astype(F32).reshape(N_EXPERTS, 1))

    Dq = D // 4
    half_base = (jnp.arange(2, dtype=jnp.int32) * P)[None, :, None]
    slots = pos[:, None, :] + half_base
    xs = _sc_dispatch(h2.reshape(2 * T, Dq), slots.reshape(TOP_K, 2 * T), 2 * P)
    ys = _routed_experts(xs.reshape(2, P, Dq), blk_expert, n_used, w1[l], w3[l], w2[l])
    yg = _sc_gather(ys.reshape(2 * P, Dq), slots.reshape(1, TOP_K * 2 * T))
    yg = yg.reshape(TOP_K, 2, T, Dq)

    out = _final(x1, h2, yg, gate.T, g2, ws1[l].astype(BF16), ws3[l].astype(BF16),
                 ws2[l].astype(BF16), row(ln2_g[l]), row(ln2_b[l]), S)
    return out.reshape(B, S, D)
```

```python
import math
from functools import partial

import jax
import jax.numpy as jnp
from jax import lax
from jax.experimental import pallas as pl
from jax.experimental.pallas import tpu as pltpu
from jax.experimental.pallas import tpu_sc as plsc

D_MODEL = 1024
N_HEADS = 8
HEAD_DIM = 64
V_DIM = 128
CONV_K = 31
N_EXPERTS = 256
TOP_K = 8
N_GROUPS = 8
GROUP_SIZE = N_EXPERTS // N_GROUPS
TOPK_GROUPS = 4
ROUTED_SCALE = 2.5
LN_EPS = 1e-5
RMS_EPS = 1e-5
DEPTH = 1
DN_ALPHA = (2.0 * DEPTH) ** 0.25
LAMBDA_INIT = 0.8 - 0.6 * math.exp(-0.3 * 0)
Q_SCALE = HEAD_DIM ** -0.5 * math.log2(math.e)

LANES = 128
SUBLANES = 8
VMEM_LIMIT_BYTES = 56 * 1024 * 1024

PROJ_TM = 512
ATTN_TQ = 512
ATTN_QG = 256
ATTN_KC = 256
ONES_ROWS = 16
MIX_TM = 512
MIX_SUB = 256
CONV_HALO = 32
CONV_ROWS = 64
ROUTE_TM = 512
EXPERT_BLK = 512
FINAL_TM = 256
SC_WINDOW = 128

F32 = jnp.float32
BF16 = jnp.bfloat16
U32 = jnp.uint32
NEG_INF = float("-inf")


def _cparams(*sem):
    return pltpu.CompilerParams(dimension_semantics=sem, vmem_limit_bytes=VMEM_LIMIT_BYTES)


def _sigmoid(x):
    return 1.0 / (1.0 + jnp.exp(-x))


def _silu(x):
    return x * _sigmoid(x)


BF16_BITS = 16
HIGH_HALF_WORD = 0xFFFF0000


def _pack_halves(x):
    half = x.shape[-1] // 2
    bits = lambda v: lax.bitcast_convert_type(v.astype(BF16).astype(F32), U32)
    return (bits(x[:, :half]) >> BF16_BITS) | (bits(x[:, half:]) & jnp.uint32(HIGH_HALF_WORD))


def _unpack_halves(w):
    lo = lax.bitcast_convert_type(w << BF16_BITS, F32)
    hi = lax.bitcast_convert_type(w & jnp.uint32(HIGH_HALF_WORD), F32)
    return lo, hi


def _store_packed(ref, x):
    w = _pack_halves(x)
    q = w.shape[-1] // 2
    ref[0] = w[:, :q]
    ref[1] = w[:, q:]


def _load_packed(ref):
    return _unpack_halves(jnp.concatenate([ref[0], ref[1]], axis=-1))


def _layer_norm(r, g, b):
    mu = jnp.mean(r, axis=-1, keepdims=True)
    d = r - mu
    var = jnp.mean(d * d, axis=-1, keepdims=True)
    return d * lax.rsqrt(var + LN_EPS) * g + b


def _ada_kernel(c_ref, w_ref, b_ref, o_ref):
    a = _silu(c_ref[...])
    o_ref[...] = jnp.dot(a, w_ref[...], preferred_element_type=F32,
                         precision=lax.Precision.HIGHEST) + b_ref[...]


def _ada_modulation(c, w, b):
    B, D = c.shape
    n = w.shape[1] // D
    return pl.pallas_call(
        _ada_kernel,
        out_shape=jax.ShapeDtypeStruct((B, n * D), F32),
        grid=(n,),
        in_specs=[pl.BlockSpec((B, D), lambda j: (0, 0)),
                  pl.BlockSpec((D, D), lambda j: (0, j)),
                  pl.BlockSpec((1, D), lambda j: (0, j))],
        out_specs=pl.BlockSpec((B, D), lambda j: (0, j)),
        compiler_params=_cparams("arbitrary"),
        name="ada_modulation",
    )(c, w, b.reshape(1, n * D))


def _inproj_kernel(x_ref, sc_ref, sh_ref, w_ref, b_ref,
                   qt_ref, k_ref, vt_ref, u_ref, ga_ref, gb_ref):
    D = D_MODEL
    h = (x_ref[...] * (1.0 + sc_ref[0]) + sh_ref[0]).astype(BF16)

    def proj(j):
        return (jnp.dot(h, w_ref[:, j * D:(j + 1) * D], preferred_element_type=F32)
                + b_ref[:, j * D:(j + 1) * D])

    qt_ref[...] = (proj(0) * Q_SCALE).T.astype(BF16)
    k_ref[...] = proj(1).astype(BF16)
    vt_ref[...] = proj(2).T.astype(BF16)
    u_ref[...] = (proj(3) * _sigmoid(proj(4))).astype(BF16)
    ga_ref[...] = _sigmoid(proj(5)).astype(BF16)
    gb_ref[...] = _sigmoid(proj(6)).astype(BF16)


def _in_projection(x2, sc1, sh1, w_in, b_in, seq):
    T, D = x2.shape
    tm = PROJ_TM
    tiles_per_seq = seq // tm
    ncols = w_in.shape[1]
    tok = pl.BlockSpec((tm, D), lambda i: (i, 0))
    mod = pl.BlockSpec((1, 1, D), lambda i: (i // tiles_per_seq, 0, 0))
    tok_t = pl.BlockSpec((D, tm), lambda i: (0, i))
    out = jax.ShapeDtypeStruct((T, D), BF16)
    out_t = jax.ShapeDtypeStruct((D, T), BF16)
    return pl.pallas_call(
        _inproj_kernel,
        out_shape=[out_t, out, out_t, out, out, out],
        grid=(T // tm,),
        in_specs=[tok, mod, mod,
                  pl.BlockSpec((D, ncols), lambda i: (0, 0)),
                  pl.BlockSpec((1, ncols), lambda i: (0, 0))],
        out_specs=[tok_t, tok, tok_t, tok, tok, tok],
        compiler_params=_cparams("arbitrary"),
        name="in_projection",
    )(x2, sc1, sh1, w_in, b_in)


def _attn_tasks(nq):
    tasks = [(qi, j, int(j == qi)) for qi in range(nq) for j in range(qi + 1)]
    if len(tasks) % 2:
        raise ValueError("the two-slot task loop needs an even number of tile pairs")
    qis, js, lasts = zip(*(tasks + [(0, 0, 0)]))
    as_i32 = lambda v: jnp.asarray(v, jnp.int32)
    return as_i32(qis), as_i32(js), as_i32(lasts), len(tasks)


def _attn_kernel(tqi_ref, tj_ref, tlast_ref, lq1_ref, lk1_ref, lq2_ref, lk2_ref, g_ref,
                 qt_ref, k_ref, vt_ref, o_ref, q2_ref, vx_ref, *scratch, n_tasks):
    tq = ATTN_TQ
    qg = ATTN_QG
    ng = 2 * tq // qg
    groups = [scratch[i * ng:(i + 1) * ng] for i in range(7)]
    s_slots, x_slots = groups[0:2], groups[2:4]
    m_refs, l_refs, acc_refs = groups[4:]
    seq = k_ref.shape[0]

    vx_ref[:V_DIM, :] = vt_ref[...]
    vx_ref[V_DIM:, :] = jnp.ones((ONES_ROWS, seq), BF16)

    lam = (jnp.exp(jnp.sum(lq1_ref[...] * lk1_ref[...], axis=-1, keepdims=True))
           - jnp.exp(jnp.sum(lq2_ref[...] * lk2_ref[...], axis=-1, keepdims=True))
           + LAMBDA_INIT)

    for i in range(seq // tq):
        qt = qt_ref[:, i * tq:(i + 1) * tq]
        feat = lax.broadcasted_iota(jnp.int32, qt.shape, 0)
        zero = jnp.zeros_like(qt)
        q2_ref[:, 2 * tq * i:2 * tq * i + tq] = jnp.where(feat < HEAD_DIM, qt, zero)
        q2_ref[:, 2 * tq * i + tq:2 * tq * (i + 1)] = jnp.where(feat >= HEAD_DIM, qt, zero)

    def scores(qi, j, slot, g):
        start = pl.multiple_of(j * tq, tq)
        col = pl.multiple_of(qi * (2 * tq) + g * qg, qg)
        s = jnp.dot(k_ref[pl.ds(start, tq), :], q2_ref[:, pl.ds(col, qg)],
                    preferred_element_type=F32)
        s_slots[slot][g][...] = s
        x_slots[slot][g][...] = jnp.max(s, axis=0, keepdims=True)

    def update(j, slot, g, masked):
        start = pl.multiple_of(j * tq, tq)
        s_ref = s_slots[slot][g]

        def load(k0):
            s = s_ref[k0:k0 + ATTN_KC, :]
            if masked:
                key = lax.broadcasted_iota(jnp.int32, s.shape, 0) + k0
                qry = lax.broadcasted_iota(jnp.int32, s.shape, 1) + (g * qg) % tq
                s = jnp.where(key <= qry, s, NEG_INF)
            return s

        last_query = (g * qg) % tq + qg - 1
        chunks = [k0 for k0 in range(0, tq, ATTN_KC) if not masked or k0 <= last_query]

        m_prev = m_refs[g][...]
        if masked:
            tile_max = jnp.max(load(chunks[0]), axis=0, keepdims=True)
            for k0 in chunks[1:]:
                tile_max = jnp.maximum(tile_max, jnp.max(load(k0), axis=0, keepdims=True))
        else:
            tile_max = x_slots[slot][g][...]
        m_new = jnp.maximum(m_prev, tile_max)
        alpha = jnp.exp2(m_prev - m_new)
        m_refs[g][...] = m_new
        pv = None
        for k0 in chunks:
            p = jnp.exp2((load(k0) - m_new).astype(BF16))
            keys = pl.ds(pl.multiple_of(start + k0, ATTN_KC), ATTN_KC)
            part = jnp.dot(vx_ref[:, keys], p, preferred_element_type=F32)
            pv = part if pv is None else pv + part
        acc_refs[g][...] = alpha * acc_refs[g][...] + pv[:V_DIM, :]
        l_refs[g][...] = alpha * l_refs[g][...] + pv[V_DIM:V_DIM + 1, :]

    def reset():
        for g in range(ng):
            m_refs[g][...] = jnp.full(m_refs[g].shape, NEG_INF, F32)
            l_refs[g][...] = jnp.zeros(l_refs[g].shape, F32)
            acc_refs[g][...] = jnp.zeros(acc_refs[g].shape, F32)

    def finish(qi):
        o = (jnp.concatenate([r[...] for r in acc_refs], axis=-1)
             / jnp.concatenate([r[...] for r in l_refs], axis=-1))
        o = o[:, :tq] - lam * o[:, tq:]
        o = o * lax.rsqrt(jnp.mean(o * o, axis=0, keepdims=True) + RMS_EPS) * g_ref[...]
        row = pl.multiple_of(qi * tq, tq)
        o_ref[pl.ds(row, tq), :] = (o * (1.0 - LAMBDA_INIT)).T.astype(BF16)

    def task(t, slot):
        qi, j = tqi_ref[t], tj_ref[t]
        nqi, nj = tqi_ref[t + 1], tj_ref[t + 1]

        def run(masked):
            for g in range(ng):
                scores(nqi, nj, 1 - slot, g)
                update(j, slot, g, masked)

        @pl.when(tlast_ref[t] == 0)
        def _():
            run(False)

        @pl.when(tlast_ref[t] == 1)
        def _():
            run(True)
            finish(qi)
            reset()

    reset()
    for g in range(ng):
        scores(0, 0, 0, g)

    def pair(i, carry):
        task(2 * i, 0)
        task(2 * i + 1, 1)
        return carry

    lax.fori_loop(0, n_tasks // 2, pair, 0)


def _diff_attention(qt, k, vt, lq1, lk1, lq2, lk2, subln_g, batch, seq):
    T, D = k.shape
    tq = ATTN_TQ
    nq = seq // tq
    tqi, tj, tlast, n_tasks = _attn_tasks(nq)
    lam_spec = pl.BlockSpec((1, HEAD_DIM), lambda b, h, *_: (0, 0))
    rows_spec = pl.BlockSpec((seq, V_DIM), lambda b, h, *_: (b, h))
    cols_spec = pl.BlockSpec((V_DIM, seq), lambda b, h, *_: (h, b))
    qg = ATTN_QG
    per_group = lambda shape, dtype: [pltpu.VMEM(shape, dtype) for _ in range(2 * tq // qg)]
    grid_spec = pltpu.PrefetchScalarGridSpec(
        num_scalar_prefetch=3,
        grid=(batch, N_HEADS),
        in_specs=[lam_spec, lam_spec, lam_spec, lam_spec,
                  pl.BlockSpec((V_DIM, 1), lambda b, h, *_: (0, 0)),
                  cols_spec, rows_spec, cols_spec],
        out_specs=rows_spec,
        scratch_shapes=(
            [pltpu.VMEM((V_DIM, 2 * seq), BF16),
             pltpu.VMEM((V_DIM + ONES_ROWS, seq), BF16)]
            + per_group((tq, qg), F32)
            + per_group((tq, qg), F32)
            + per_group((1, qg), F32)
            + per_group((1, qg), F32)
            + per_group((1, qg), F32)
            + per_group((1, qg), F32)
            + per_group((V_DIM, qg), F32)),
    )
    return pl.pallas_call(
        partial(_attn_kernel, n_tasks=n_tasks),
        out_shape=jax.ShapeDtypeStruct((T, D), BF16),
        grid_spec=grid_spec,
        compiler_params=_cparams("arbitrary", "arbitrary"),
        name="diff_attention",
    )(tqi, tj, tlast, lq1, lk1, lq2, lk2, subln_g.reshape(V_DIM, 1), qt, k, vt)


def _mixer_kernel(x_ref, o_ref, u_ref, up_ref, ga_ref, gb_ref, g1_ref, sc2_ref, sh2_ref,
                  wao_ref, cw_ref, cb_ref, clg_ref, clb_ref, wpw_ref, bpw_ref, wout_ref,
                  ln1g_ref, ln1b_ref, wrh_ref, wrl_ref,
                  x1_ref, h2_ref, lg_ref, ubuf_ref, ush_ref, conv_ref, *, tiles_per_seq):
    tm = MIX_TM
    first = (pl.program_id(0) % tiles_per_seq) == 0

    prev = up_ref[...].astype(F32)
    ubuf_ref[0:CONV_HALO, :] = jnp.where(first, jnp.zeros_like(prev), prev)
    ubuf_ref[CONV_HALO:, :] = u_ref[...].astype(F32)
    for t0 in range(0, tm, MIX_SUB):
        rows = slice(t0, t0 + MIX_SUB)

        base = CONV_HALO - (CONV_K - 1)
        n_shift = MIX_SUB + CONV_HALO - SUBLANES
        for r in range(1, SUBLANES):
            ush_ref[r - 1] = ubuf_ref[t0 + r:t0 + r + n_shift, :]
        for c in range(D_MODEL // LANES):
            cl = slice(c * LANES, (c + 1) * LANES)
            for r0 in range(0, MIX_SUB, CONV_ROWS):
                acc = jnp.zeros((CONV_ROWS, LANES), F32)
                for j in range(CONV_K):
                    a, r = divmod(base + j, SUBLANES)
                    lo = SUBLANES * a + r0
                    win = (ubuf_ref[t0 + lo:t0 + lo + CONV_ROWS, cl] if r == 0
                           else ush_ref[r - 1, lo:lo + CONV_ROWS, cl])
                    acc = acc + cw_ref[j:j + 1, cl] * win
                conv_ref[t0 + r0:t0 + r0 + CONV_ROWS, cl] = acc
        yc = _silu(_layer_norm(conv_ref[rows, :] + cb_ref[...], clg_ref[...], clb_ref[...]))
        y_conv = (jnp.dot(yc.astype(BF16), wpw_ref[...], preferred_element_type=F32)
                  + bpw_ref[...])

        y_attn = jnp.dot(o_ref[rows, :], wao_ref[...], preferred_element_type=F32)
        mix = ga_ref[rows, :].astype(F32) * y_attn + gb_ref[rows, :].astype(F32) * y_conv
        y = jnp.dot(mix.astype(BF16), wout_ref[...], preferred_element_type=F32)
        x1 = _layer_norm(DN_ALPHA * x_ref[rows, :] + g1_ref[0] * y, ln1g_ref[...], ln1b_ref[...])
        x1_ref[rows, :] = x1

        h2 = x1 * (1.0 + sc2_ref[0]) + sh2_ref[0]
        h2_hi = h2.astype(BF16)
        _store_packed(h2_ref.at[:, rows, :], h2)
        h2_lo = (h2 - h2_hi.astype(F32)).astype(BF16)
        nt = lambda w, a: lax.dot_general(w, a, (((1,), (1,)), ((), ())),
                                          preferred_element_type=F32)
        lg_ref[:, rows] = (nt(wrh_ref[...], h2_hi)
                           + (nt(wrl_ref[...], h2_hi) + nt(wrh_ref[...], h2_lo)))


def _mixer_epilogue(x2, o, u, ga, gb, g1, sc2, sh2, wao, cw, cb, clg, clb, wpw, bpw, wout,
                    ln1g, ln1b, wr_hi, wr_lo, seq):
    T, D = x2.shape
    tm = MIX_TM
    tiles_per_seq = seq // tm
    halo_per_tile = tm // CONV_HALO
    tok = pl.BlockSpec((tm, D), lambda i: (i, 0))
    prev = pl.BlockSpec((CONV_HALO, D), lambda i: (jnp.maximum(i * halo_per_tile - 1, 0), 0))
    mod = pl.BlockSpec((1, 1, D), lambda i: (i // tiles_per_seq, 0, 0))
    row = pl.BlockSpec((1, D), lambda i: (0, 0))
    mat = pl.BlockSpec((D, D), lambda i: (0, 0), pipeline_mode=pl.Buffered(1))
    rw = pl.BlockSpec((N_EXPERTS, D), lambda i: (0, 0), pipeline_mode=pl.Buffered(1))
    return pl.pallas_call(
        partial(_mixer_kernel, tiles_per_seq=tiles_per_seq),
        out_shape=[jax.ShapeDtypeStruct((T, D), F32),
                   jax.ShapeDtypeStruct((2, T, D // 4), U32),
                   jax.ShapeDtypeStruct((N_EXPERTS, T), F32)],
        grid=(T // tm,),
        in_specs=[tok, tok, tok, prev, tok, tok, mod, mod, mod,
                  mat, pl.BlockSpec((CONV_K, D), lambda i: (0, 0)), row, row, row, mat, row, mat,
                  row, row, rw, rw],
        out_specs=[tok, pl.BlockSpec((2, tm, D // 4), lambda i: (0, i, 0)),
                   pl.BlockSpec((N_EXPERTS, tm), lambda i: (0, i))],
        scratch_shapes=[pltpu.VMEM((CONV_HALO + tm, D), F32),
                        pltpu.VMEM((SUBLANES - 1, MIX_SUB + CONV_HALO - SUBLANES, D), F32),
                        pltpu.VMEM((tm, D), F32)],
        compiler_params=_cparams("arbitrary"),
        name="mixer_epilogue",
    )(x2, o, u, u, ga, gb, g1, sc2, sh2, wao, cw, cb, clg, clb, wpw, bpw, wout,
      ln1g, ln1b, wr_hi, wr_lo)


def _first_argmax(v, eid):
    m = jnp.max(v, axis=0, keepdims=True)
    i = jnp.min(jnp.where(v == m, eid, float(N_EXPERTS)), axis=0, keepdims=True)
    return m, i


def _route_kernel(lg_ref, rb_ref, idx_ref, gate_ref, cnt_ref):
    tm = ROUTE_TM
    scores = _sigmoid(lg_ref[...])
    choice = scores + rb_ref[...]
    eid = lax.broadcasted_iota(jnp.int32, (N_EXPERTS, tm), 0).astype(F32)
    rows = lambda a, g: a[g * GROUP_SIZE:(g + 1) * GROUP_SIZE, :]

    gs = []
    for g in range(N_GROUPS):
        cg = rows(choice, g)
        eg = (lax.broadcasted_iota(jnp.int32, cg.shape, 0) + g * GROUP_SIZE).astype(F32)
        m1, i1 = _first_argmax(cg, eg)
        m2 = jnp.max(jnp.where(eg == i1, NEG_INF, cg), axis=0, keepdims=True)
        gs.append(jnp.broadcast_to(m1 + m2, (SUBLANES, tm)))

    kept = []
    for g in range(N_GROUPS):
        ahead = jnp.zeros((SUBLANES, tm), F32)
        for o in range(N_GROUPS):
            if o == g:
                continue
            beats = (gs[o] >= gs[g]) if o < g else (gs[o] > gs[g])
            ahead = ahead + jnp.where(beats, 1.0, 0.0)
        cap = jnp.where(ahead < TOPK_GROUPS, float("inf"), NEG_INF)
        for r in range(g * GROUP_SIZE, (g + 1) * GROUP_SIZE, SUBLANES):
            kept.append(jnp.minimum(choice[r:r + SUBLANES, :], cap))
    masked = jnp.concatenate(kept, axis=0)

    sel = jnp.zeros((N_EXPERTS, tm), F32)
    total = jnp.zeros((1, tm), F32)
    picked = []
    for k in range(TOP_K):
        _, i = _first_argmax(masked, eid)
        hit = eid == i
        sc = jnp.sum(jnp.where(hit, scores, 0.0), axis=0, keepdims=True)
        masked = jnp.where(hit, NEG_INF, masked)
        sel = sel + hit.astype(F32)
        total = total + sc
        idx_ref[k:k + 1, :] = i.astype(jnp.int32)
        picked.append(sc)
    for k in range(TOP_K):
        gate_ref[k:k + 1, :] = picked[k] / total * ROUTED_SCALE

    @pl.when(pl.program_id(0) == 0)
    def _():
        cnt_ref[...] = jnp.zeros_like(cnt_ref)
    cnt_ref[...] += jnp.sum(sel, axis=1, keepdims=True)


def _route(logits_t, router_bias):
    T = logits_t.shape[1]
    tm = ROUTE_TM
    return pl.pallas_call(
        _route_kernel,
        out_shape=[jax.ShapeDtypeStruct((TOP_K, T), jnp.int32),
                   jax.ShapeDtypeStruct((TOP_K, T), F32),
                   jax.ShapeDtypeStruct((N_EXPERTS, 1), F32)],
        grid=(T // tm,),
        in_specs=[pl.BlockSpec((N_EXPERTS, tm), lambda i: (0, i)),
                  pl.BlockSpec((N_EXPERTS, 1), lambda i: (0, 0))],
        out_specs=[pl.BlockSpec((TOP_K, tm), lambda i: (0, i)),
                   pl.BlockSpec((TOP_K, tm), lambda i: (0, i)),
                   pl.BlockSpec((N_EXPERTS, 1), lambda i: (0, 0))],
        compiler_params=_cparams("arbitrary"),
        name="route_topk",
    )(logits_t, router_bias.reshape(N_EXPERTS, 1))


def _slot_kernel(idx_ref, start_ref, pos_ref, carry_ref):
    tm = ROUTE_TM

    @pl.when(pl.program_id(0) == 0)
    def _():
        carry_ref[...] = jnp.zeros_like(carry_ref)

    idx = idx_ref[...]
    eid = lax.broadcasted_iota(jnp.int32, (N_EXPERTS, tm), 0)
    hits = [eid == idx[k:k + 1, :] for k in range(TOP_K)]
    sel = jnp.zeros((N_EXPERTS, tm), F32)
    for h in hits:
        sel = sel + h.astype(F32)
    r = lax.broadcasted_iota(jnp.int32, (tm, tm), 0)
    c = lax.broadcasted_iota(jnp.int32, (tm, tm), 1)
    earlier = (r < c).astype(BF16)
    rank = jnp.dot(sel.astype(BF16), earlier, preferred_element_type=F32)
    slot = rank + carry_ref[...] + start_ref[...]
    for k, h in enumerate(hits):
        pos_ref[k:k + 1, :] = jnp.sum(jnp.where(h, slot, 0.0), axis=0,
                                      keepdims=True).astype(jnp.int32)
    carry_ref[...] += jnp.sum(sel, axis=1, keepdims=True)


def _assign_slots(idx_t, pad_start):
    T = idx_t.shape[1]
    tm = ROUTE_TM
    return pl.pallas_call(
        _slot_kernel,
        out_shape=jax.ShapeDtypeStruct((TOP_K, T), jnp.int32),
        grid=(T // tm,),
        in_specs=[pl.BlockSpec((TOP_K, tm), lambda i: (0, i)),
                  pl.BlockSpec((N_EXPERTS, 1), lambda i: (0, 0))],
        out_specs=pl.BlockSpec((TOP_K, tm), lambda i: (0, i)),
        scratch_shapes=[pltpu.VMEM((N_EXPERTS, 1), F32)],
        compiler_params=_cparams("arbitrary"),
        name="assign_slots",
    )(idx_t, pad_start)


def _expert_kernel(be_ref, nu_ref, first_ref, slot_ref, nxt_ref, xs_ref, w1_hbm, w3_hbm, w2_hbm,
                   ys_ref, w1_buf, w3_buf, w2_buf, sem):
    b = pl.program_id(0)

    def weight_copies(expert, slot):
        pairs = ((w1_hbm, w1_buf), (w3_hbm, w3_buf), (w2_hbm, w2_buf))
        return [pltpu.make_async_copy(hbm.at[expert], buf.at[slot], sem.at[slot, i])
                for i, (hbm, buf) in enumerate(pairs)]

    @pl.when(b < nu_ref[0])
    def _():
        slot = slot_ref[b]

        @pl.when(b == 0)
        def _():
            for c in weight_copies(be_ref[0], 0):
                c.start()

        @pl.when(first_ref[b] == 1)
        def _():
            for c in weight_copies(be_ref[b], slot):
                c.wait()

            @pl.when(nxt_ref[b] >= 0)
            def _():
                for c in weight_copies(nxt_ref[b], 1 - slot):
                    c.start()

        lo, hi = _load_packed(xs_ref)
        half = lo.shape[-1]
        mm = lambda w: (jnp.dot(lo, w[slot, :half, :], preferred_element_type=F32)
                        + jnp.dot(hi, w[slot, half:, :], preferred_element_type=F32))
        hb = _silu(mm(w1_buf)) * mm(w3_buf)
        _store_packed(ys_ref, jnp.dot(hb, w2_buf[slot], preferred_element_type=F32))

    @pl.when(b >= nu_ref[0])
    def _():
        ys_ref[...] = jnp.zeros_like(ys_ref)


def _routed_experts(xs, blk_expert, n_used, w1, w3, w2):
    _, P, Dq = xs.shape
    blk = EXPERT_BLK
    n_blocks = P // blk
    D, F = w1.shape[-2:]

    b_ids = jnp.arange(n_blocks, dtype=jnp.int32)
    used = b_ids < n_used[0]
    prev = jnp.concatenate([jnp.full((1,), -1, jnp.int32), blk_expert[:-1]])
    first = (used & (blk_expert != prev)).astype(jnp.int32)
    slot = (jnp.cumsum(first) - 1) % 2
    start_at = jnp.where(first == 1, b_ids, n_blocks)
    later = jnp.concatenate([start_at[1:], jnp.full((1,), n_blocks, jnp.int32)])
    next_start = lax.cummin(later, axis=0, reverse=True)
    nxt = jnp.where(next_start < n_blocks,
                    blk_expert[jnp.minimum(next_start, n_blocks - 1)], -1).astype(jnp.int32)

    row_blk = lambda b, be, nu, *_: (0, jnp.minimum(b, nu[0] - 1), 0)
    grid_spec = pltpu.PrefetchScalarGridSpec(
        num_scalar_prefetch=5,
        grid=(n_blocks,),
        in_specs=[pl.BlockSpec((2, blk, Dq), row_blk),
                  pl.BlockSpec(memory_space=pl.ANY),
                  pl.BlockSpec(memory_space=pl.ANY),
                  pl.BlockSpec(memory_space=pl.ANY)],
        out_specs=pl.BlockSpec((2, blk, Dq), lambda b, *_: (0, b, 0)),
        scratch_shapes=[pltpu.VMEM((2, D, F), F32), pltpu.VMEM((2, D, F), F32),
                        pltpu.VMEM((2, F, D), F32), pltpu.SemaphoreType.DMA((2, 3))],
    )
    return pl.pallas_call(
        _expert_kernel,
        out_shape=jax.ShapeDtypeStruct((2, P, Dq), U32),
        grid_spec=grid_spec,
        compiler_params=_cparams("arbitrary"),
        name="routed_experts",
    )(blk_expert, n_used, first, slot.astype(jnp.int32), nxt, xs, w1, w3, w2)


def _sc_mesh():
    return plsc.VectorSubcoreMesh(core_axis_name="c", subcore_axis_name="s")


def _sc_dispatch(rows, slots_t, n_slots):
    T, W = rows.shape
    K = slots_t.shape[0]
    win = SC_WINDOW

    @pl.kernel(out_type=jax.ShapeDtypeStruct((n_slots, W), rows.dtype), mesh=_sc_mesh(),
               scratch_types=[])
    def dispatch(x_hbm, i_hbm, o_hbm):
        def body(x_vmem, i_vmem):
            for k in range(K):
                pltpu.sync_copy(x_vmem, o_hbm.at[i_vmem.at[k]])

        pltpu.emit_pipeline(
            body,
            grid=(T // win,),
            in_specs=[pl.BlockSpec((win, W), lambda i: (i, 0)),
                      pl.BlockSpec((K, win), lambda i: (0, i))],
            out_specs=[],
            core_axis_name=("c", "s"),
            dimension_semantics=(pltpu.PARALLEL,),
        )(x_hbm, i_hbm)

    return dispatch(rows, slots_t)


def _sc_gather(table, idx):
    W = table.shape[1]
    N = idx.shape[1]
    win = SC_WINDOW

    @pl.kernel(out_type=jax.ShapeDtypeStruct((N, W), table.dtype), mesh=_sc_mesh(),
               scratch_types=[])
    def gather(x_hbm, i_hbm, o_hbm):
        def body(i_vmem, o_vmem):
            pltpu.sync_copy(x_hbm.at[i_vmem.at[0]], o_vmem)

        pltpu.emit_pipeline(
            body,
            grid=(N // win,),
            in_specs=[pl.BlockSpec((1, win), lambda i: (0, i))],
            out_specs=[pl.BlockSpec((win, W), lambda i: (i, 0))],
            core_axis_name=("c", "s"),
            dimension_semantics=(pltpu.PARALLEL,),
        )(i_hbm, o_hbm)

    return gather(table, idx)


def _final_kernel(x1_ref, h2_ref, yg_ref, gate_ref, g2_ref, ws1_ref, ws3_ref, ws2_ref,
                  lng_ref, lnb_ref, o_ref):
    Dh = D_MODEL // 2
    gate = gate_ref[...]
    routed_lo = jnp.zeros((x1_ref.shape[0], Dh), F32)
    routed_hi = jnp.zeros((x1_ref.shape[0], Dh), F32)
    for k in range(TOP_K):
        lo, hi = _load_packed(yg_ref.at[k])
        routed_lo = routed_lo + lo * gate[:, k:k + 1]
        routed_hi = routed_hi + hi * gate[:, k:k + 1]
    h2 = jnp.concatenate(_load_packed(h2_ref), axis=-1).astype(BF16)
    a = jnp.dot(h2, ws1_ref[...], preferred_element_type=F32)
    b = jnp.dot(h2, ws3_ref[...], preferred_element_type=F32)
    shared = jnp.dot((_silu(a) * b).astype(BF16), ws2_ref[...], preferred_element_type=F32)
    y2 = jnp.concatenate([routed_lo, routed_hi], axis=-1) + shared
    o_ref[...] = _layer_norm(DN_ALPHA * x1_ref[...] + g2_ref[0] * y2, lng_ref[...], lnb_ref[...])


def _final(x1, h2, yg, gate, g2, ws1, ws3, ws2, ln2g, ln2b, seq):
    T, D = x1.shape
    tm = FINAL_TM
    tiles_per_seq = seq // tm
    F = ws1.shape[-1]
    tok = pl.BlockSpec((tm, D), lambda i: (i, 0))
    row = pl.BlockSpec((1, D), lambda i: (0, 0))
    return pl.pallas_call(
        _final_kernel,
        out_shape=jax.ShapeDtypeStruct((T, D), F32),
        grid=(T // tm,),
        in_specs=[tok, pl.BlockSpec((2, tm, D // 4), lambda i: (0, i, 0)),
                  pl.BlockSpec((TOP_K, 2, tm, D // 4), lambda i: (0, 0, i, 0)),
                  pl.BlockSpec((tm, TOP_K), lambda i: (i, 0)),
                  pl.BlockSpec((1, 1, D), lambda i: (i // tiles_per_seq, 0, 0)),
                  pl.BlockSpec((D, F), lambda i: (0, 0)),
                  pl.BlockSpec((D, F), lambda i: (0, 0)),
                  pl.BlockSpec((F, D), lambda i: (0, 0)),
                  row, row],
        out_specs=tok,
        compiler_params=_cparams("arbitrary"),
        name="moe_combine_final",
    )(x1, h2, yg, gate, g2, ws1, ws3, ws2, ln2g, ln2b)


def kernel(x, c, ada_w, ada_b, w_in, b_in, lambda_q1, lambda_k1, lambda_q2, lambda_k2, subln_g, w_attn_o, conv_w, conv_b, conv_ln_g, conv_ln_b, conv_pw_w, conv_pw_b, w_out, ln1_g, ln1_b, router_w, router_bias, w1, w3, w2, ws1, ws3, ws2, ln2_g, ln2_b):
    B, S, D = x.shape
    T = B * S
    l = 0
    x2 = x.reshape(T, D)
    row = lambda a: a.reshape(1, -1)

    mod = _ada_modulation(c, ada_w[l], ada_b[l])
    sh1, sc1, g1, sh2, sc2, g2 = [m.reshape(B, 1, D) for m in jnp.split(mod, 6, axis=-1)]

    qt, k, vt, u, ga, gb = _in_projection(x2, sc1, sh1, w_in[l].astype(BF16), row(b_in[l]), S)
    o = _diff_attention(qt, k, vt, row(lambda_q1[l]), row(lambda_k1[l]), row(lambda_q2[l]),
                        row(lambda_k2[l]), row(subln_g[l]), B, S)

    wr = router_w[l].T
    wr_hi = wr.astype(BF16)
    wr_lo = (wr - wr_hi.astype(F32)).astype(BF16)
    x1, h2, logits = _mixer_epilogue(
        x2, o, u, ga, gb, g1, sc2, sh2, w_attn_o[l].astype(BF16), conv_w[l], row(conv_b[l]),
        row(conv_ln_g[l]), row(conv_ln_b[l]), conv_pw_w[l].astype(BF16), row(conv_pw_b[l]),
        w_out[l].astype(BF16), row(ln1_g[l]), row(ln1_b[l]), wr_hi, wr_lo, S)

    idx, gate, counts = _route(logits, router_bias[l])

    blk = EXPERT_BLK
    cnt = counts.reshape(N_EXPERTS).astype(jnp.int32)
    padded = (cnt + blk - 1) // blk * blk
    pad_end = jnp.cumsum(padded)
    pad_start = pad_end - padded
    P = T * TOP_K + N_EXPERTS * blk
    n_blocks = P // blk
    blk_first_row = jnp.arange(n_blocks, dtype=jnp.int32) * blk
    blk_expert = jnp.minimum(
        jnp.sum((pad_end[None, :] <= blk_first_row[:, None]).astype(jnp.int32), axis=1),
        N_EXPERTS - 1)
    n_used = (pad_end[-1:] // blk).astype(jnp.int32)

    pos = _assign_slots(idx, pad_start.astype(F32).reshape(N_EXPERTS, 1))

    Dq = D // 4
    half_base = (jnp.arange(2, dtype=jnp.int32) * P)[None, :, None]
    slots = pos[:, None, :] + half_base
    xs = _sc_dispatch(h2.reshape(2 * T, Dq), slots.reshape(TOP_K, 2 * T), 2 * P)
    ys = _routed_experts(xs.reshape(2, P, Dq), blk_expert, n_used, w1[l], w3[l], w2[l])
    yg = _sc_gather(ys.reshape(2 * P, Dq), slots.reshape(1, TOP_K * 2 * T))
    yg = yg.reshape(TOP_K, 2, T, Dq)

    out = _final(x1, h2, yg, gate.T, g2, ws1[l].astype(BF16), ws3[l].astype(BF16),
                 ws2[l].astype(BF16), row(ln2_g[l]), row(ln2_b[l]), S)
    return out.reshape(B, S, D)
```

```python
import math
from functools import partial

import jax
import jax.numpy as jnp
from jax import lax
from jax.experimental import pallas as pl
from jax.experimental.pallas import tpu as pltpu
from jax.experimental.pallas import tpu_sc as plsc

D_MODEL = 1024
N_HEADS = 8
HEAD_DIM = 64
V_DIM = 128
CONV_K = 31
N_EXPERTS = 256
TOP_K = 8
N_GROUPS = 8
GROUP_SIZE = N_EXPERTS // N_GROUPS
TOPK_GROUPS = 4
ROUTED_SCALE = 2.5
LN_EPS = 1e-5
RMS_EPS = 1e-5
DEPTH = 1
DN_ALPHA = (2.0 * DEPTH) ** 0.25
LAMBDA_INIT = 0.8 - 0.6 * math.exp(-0.3 * 0)
Q_SCALE = HEAD_DIM ** -0.5 * math.log2(math.e)

LANES = 128
SUBLANES = 8
VMEM_LIMIT_BYTES = 56 * 1024 * 1024

PROJ_TM = 512
ATTN_TQ = 512
ATTN_QG = 256
ATTN_KC = 256
ATTN_RC = 128
REF_FLOOR = -1e30
ONES_ROWS = 16
MIX_TM = 512
MIX_SUB = 256
CONV_HALO = 32
CONV_ROWS = 64
ROUTE_TM = 512
EXPERT_BLK = 512
FINAL_TM = 256
SC_WINDOW = 128

F32 = jnp.float32
BF16 = jnp.bfloat16
U32 = jnp.uint32
NEG_INF = float("-inf")


def _cparams(*sem):
    return pltpu.CompilerParams(dimension_semantics=sem, vmem_limit_bytes=VMEM_LIMIT_BYTES)


def _sigmoid(x):
    return 1.0 / (1.0 + jnp.exp(-x))


def _silu(x):
    return x * _sigmoid(x)


BF16_BITS = 16
HIGH_HALF_WORD = 0xFFFF0000


def _pack_halves(x):
    half = x.shape[-1] // 2
    bits = lambda v: lax.bitcast_convert_type(v.astype(BF16).astype(F32), U32)
    return (bits(x[:, :half]) >> BF16_BITS) | (bits(x[:, half:]) & jnp.uint32(HIGH_HALF_WORD))


def _unpack_halves(w):
    lo = lax.bitcast_convert_type(w << BF16_BITS, F32)
    hi = lax.bitcast_convert_type(w & jnp.uint32(HIGH_HALF_WORD), F32)
    return lo, hi


def _store_packed(ref, x):
    w = _pack_halves(x)
    q = w.shape[-1] // 2
    ref[0] = w[:, :q]
    ref[1] = w[:, q:]


def _load_packed(ref):
    return _unpack_halves(jnp.concatenate([ref[0], ref[1]], axis=-1))


def _layer_norm(r, g, b):
    mu = jnp.mean(r, axis=-1, keepdims=True)
    d = r - mu
    var = jnp.mean(d * d, axis=-1, keepdims=True)
    return d * lax.rsqrt(var + LN_EPS) * g + b


def _ada_kernel(c_ref, w_ref, b_ref, o_ref):
    a = _silu(c_ref[...])
    o_ref[...] = jnp.dot(a, w_ref[...], preferred_element_type=F32,
                         precision=lax.Precision.HIGHEST) + b_ref[...]


def _ada_modulation(c, w, b):
    B, D = c.shape
    n = w.shape[1] // D
    return pl.pallas_call(
        _ada_kernel,
        out_shape=jax.ShapeDtypeStruct((B, n * D), F32),
        grid=(n,),
        in_specs=[pl.BlockSpec((B, D), lambda j: (0, 0)),
                  pl.BlockSpec((D, D), lambda j: (0, j)),
                  pl.BlockSpec((1, D), lambda j: (0, j))],
        out_specs=pl.BlockSpec((B, D), lambda j: (0, j)),
        compiler_params=_cparams("arbitrary"),
        name="ada_modulation",
    )(c, w, b.reshape(1, n * D))


def _inproj_kernel(x_ref, sc_ref, sh_ref, w_ref, b_ref,
                   qt_ref, k_ref, vt_ref, u_ref, ga_ref, gb_ref):
    D = D_MODEL
    h = (x_ref[...] * (1.0 + sc_ref[0]) + sh_ref[0]).astype(BF16)

    def proj(j):
        return (jnp.dot(h, w_ref[:, j * D:(j + 1) * D], preferred_element_type=F32)
                + b_ref[:, j * D:(j + 1) * D])

    qt_ref[...] = (proj(0) * Q_SCALE).T.astype(BF16)
    k_ref[...] = proj(1).astype(BF16)
    vt_ref[...] = proj(2).T.astype(BF16)
    u_ref[...] = (proj(3) * _sigmoid(proj(4))).astype(BF16)
    ga_ref[...] = _sigmoid(proj(5)).astype(BF16)
    gb_ref[...] = _sigmoid(proj(6)).astype(BF16)


def _in_projection(x2, sc1, sh1, w_in, b_in, seq):
    T, D = x2.shape
    tm = PROJ_TM
    tiles_per_seq = seq // tm
    ncols = w_in.shape[1]
    tok = pl.BlockSpec((tm, D), lambda i: (i, 0))
    mod = pl.BlockSpec((1, 1, D), lambda i: (i // tiles_per_seq, 0, 0))
    tok_t = pl.BlockSpec((D, tm), lambda i: (0, i))
    out = jax.ShapeDtypeStruct((T, D), BF16)
    out_t = jax.ShapeDtypeStruct((D, T), BF16)
    return pl.pallas_call(
        _inproj_kernel,
        out_shape=[out_t, out, out_t, out, out, out],
        grid=(T // tm,),
        in_specs=[tok, mod, mod,
                  pl.BlockSpec((D, ncols), lambda i: (0, 0)),
                  pl.BlockSpec((1, ncols), lambda i: (0, 0))],
        out_specs=[tok_t, tok, tok_t, tok, tok, tok],
        compiler_params=_cparams("arbitrary"),
        name="in_projection",
    )(x2, sc1, sh1, w_in, b_in)


def _attn_tasks(nq):
    tasks = [(qi, j, int(j == qi)) for qi in range(nq) for j in range(qi + 1)]
    if len(tasks) % 2:
        raise ValueError("the two-slot task loop needs an even number of tile pairs")
    qis, js, lasts = zip(*(tasks + [(0, 0, 0)]))
    as_i32 = lambda v: jnp.asarray(v, jnp.int32)
    return as_i32(qis), as_i32(js), as_i32(lasts), len(tasks)


def _attn_kernel(tqi_ref, tj_ref, tlast_ref, lq1_ref, lk1_ref, lq2_ref, lk2_ref, g_ref,
                 qt_ref, k_ref, vt_ref, o_ref, q2_ref, vx_ref, *scratch, n_tasks):
    tq = ATTN_TQ
    qg = ATTN_QG
    ng = 2 * tq // qg
    groups = [scratch[i * ng:(i + 1) * ng] for i in range(9)]
    s_slots, x_slots, r_slots = groups[0:2], groups[2:4], groups[4:6]
    m_refs, l_refs, acc_refs = groups[6:]
    seq = k_ref.shape[0]

    vx_ref[:V_DIM, :] = vt_ref[...]
    vx_ref[V_DIM:, :] = jnp.ones((ONES_ROWS, seq), BF16)

    lam = (jnp.exp(jnp.sum(lq1_ref[...] * lk1_ref[...], axis=-1, keepdims=True))
           - jnp.exp(jnp.sum(lq2_ref[...] * lk2_ref[...], axis=-1, keepdims=True))
           + LAMBDA_INIT)

    for i in range(seq // tq):
        qt = qt_ref[:, i * tq:(i + 1) * tq]
        feat = lax.broadcasted_iota(jnp.int32, qt.shape, 0)
        zero = jnp.zeros_like(qt)
        q2_ref[:, 2 * tq * i:2 * tq * i + tq] = jnp.where(feat < HEAD_DIM, qt, zero)
        q2_ref[:, 2 * tq * i + tq:2 * tq * (i + 1)] = jnp.where(feat >= HEAD_DIM, qt, zero)

    n_rc = tq // ATTN_RC

    def scores(qi, j, slot, g, causal):
        start = pl.multiple_of(j * tq, tq)
        col = pl.multiple_of(qi * (2 * tq) + g * qg, qg)
        s = jnp.dot(k_ref[pl.ds(start, tq), :], q2_ref[:, pl.ds(col, qg)],
                    preferred_element_type=F32)
        tile_max = None
        for c in range(n_rc):
            sc = s[c * ATTN_RC:(c + 1) * ATTN_RC, :]
            if causal:
                key = lax.broadcasted_iota(jnp.int32, sc.shape, 0) + c * ATTN_RC
                qry = lax.broadcasted_iota(jnp.int32, sc.shape, 1) + (g * qg) % tq
                sc = jnp.where(key <= qry, sc, NEG_INF)
            cmax = jnp.max(sc, axis=0, keepdims=True)
            ref = jnp.maximum(cmax, REF_FLOOR) if causal else cmax
            s_slots[slot][g][c * ATTN_RC:(c + 1) * ATTN_RC, :] = (sc - ref).astype(BF16)
            r_slots[slot][g][c:c + 1, :] = ref
            tile_max = cmax if tile_max is None else jnp.maximum(tile_max, cmax)
        x_slots[slot][g][...] = tile_max

    def update(j, slot, g):
        start = pl.multiple_of(j * tq, tq)
        m_prev = m_refs[g][...]
        m_new = jnp.maximum(m_prev, x_slots[slot][g][...])
        alpha = jnp.exp2(m_prev - m_new)
        m_refs[g][...] = m_new
        pv = None
        for k0 in range(0, tq, ATTN_KC):
            parts = []
            for c in range(k0 // ATTN_RC, (k0 + ATTN_KC) // ATTN_RC):
                shift = (m_new - r_slots[slot][g][c:c + 1, :]).astype(BF16)
                d = s_slots[slot][g][c * ATTN_RC:(c + 1) * ATTN_RC, :]
                parts.append(jnp.exp2(d - shift))
            p = jnp.concatenate(parts, axis=0)
            keys = pl.ds(pl.multiple_of(start + k0, ATTN_KC), ATTN_KC)
            part = jnp.dot(vx_ref[:, keys], p, preferred_element_type=F32)
            pv = part if pv is None else pv + part
        acc_refs[g][...] = alpha * acc_refs[g][...] + pv[:V_DIM, :]
        l_refs[g][...] = alpha * l_refs[g][...] + pv[V_DIM:V_DIM + 1, :]

    def reset():
        for g in range(ng):
            m_refs[g][...] = jnp.full(m_refs[g].shape, NEG_INF, F32)
            l_refs[g][...] = jnp.zeros(l_refs[g].shape, F32)
            acc_refs[g][...] = jnp.zeros(acc_refs[g].shape, F32)

    def finish(qi):
        o = (jnp.concatenate([r[...] for r in acc_refs], axis=-1)
             / jnp.concatenate([r[...] for r in l_refs], axis=-1))
        o = o[:, :tq] - lam * o[:, tq:]
        o = o * lax.rsqrt(jnp.mean(o * o, axis=0, keepdims=True) + RMS_EPS) * g_ref[...]
        row = pl.multiple_of(qi * tq, tq)
        o_ref[pl.ds(row, tq), :] = (o * (1.0 - LAMBDA_INIT)).T.astype(BF16)

    def task(t, slot):
        qi, j = tqi_ref[t], tj_ref[t]
        nqi, nj = tqi_ref[t + 1], tj_ref[t + 1]

        def run(next_causal):
            for g in range(ng):
                scores(nqi, nj, 1 - slot, g, next_causal)
                update(j, slot, g)

        @pl.when(tlast_ref[t + 1] == 0)
        def _():
            run(False)

        @pl.when(tlast_ref[t + 1] == 1)
        def _():
            run(True)

        @pl.when(tlast_ref[t] == 1)
        def _():
            finish(qi)
            reset()

    reset()
    for g in range(ng):
        scores(0, 0, 0, g, True)

    def pair(i, carry):
        task(2 * i, 0)
        task(2 * i + 1, 1)
        return carry

    lax.fori_loop(0, n_tasks // 2, pair, 0)


def _diff_attention(qt, k, vt, lq1, lk1, lq2, lk2, subln_g, batch, seq):
    T, D = k.shape
    tq = ATTN_TQ
    nq = seq // tq
    tqi, tj, tlast, n_tasks = _attn_tasks(nq)
    lam_spec = pl.BlockSpec((1, HEAD_DIM), lambda b, h, *_: (0, 0))
    rows_spec = pl.BlockSpec((seq, V_DIM), lambda b, h, *_: (b, h))
    cols_spec = pl.BlockSpec((V_DIM, seq), lambda b, h, *_: (h, b))
    qg = ATTN_QG
    per_group = lambda shape, dtype: [pltpu.VMEM(shape, dtype) for _ in range(2 * tq // qg)]
    grid_spec = pltpu.PrefetchScalarGridSpec(
        num_scalar_prefetch=3,
        grid=(batch, N_HEADS),
        in_specs=[lam_spec, lam_spec, lam_spec, lam_spec,
                  pl.BlockSpec((V_DIM, 1), lambda b, h, *_: (0, 0)),
                  cols_spec, rows_spec, cols_spec],
        out_specs=rows_spec,
        scratch_shapes=(
            [pltpu.VMEM((V_DIM, 2 * seq), BF16),
             pltpu.VMEM((V_DIM + ONES_ROWS, seq), BF16)]
            + per_group((tq, qg), BF16)
            + per_group((tq, qg), BF16)
            + per_group((1, qg), F32)
            + per_group((1, qg), F32)
            + per_group((tq // ATTN_RC, qg), F32)
            + per_group((tq // ATTN_RC, qg), F32)
            + per_group((1, qg), F32)
            + per_group((1, qg), F32)
            + per_group((V_DIM, qg), F32)),
    )
    return pl.pallas_call(
        partial(_attn_kernel, n_tasks=n_tasks),
        out_shape=jax.ShapeDtypeStruct((T, D), BF16),
        grid_spec=grid_spec,
        compiler_params=_cparams("arbitrary", "arbitrary"),
        name="diff_attention",
    )(tqi, tj, tlast, lq1, lk1, lq2, lk2, subln_g.reshape(V_DIM, 1), qt, k, vt)


def _mixer_kernel(x_ref, o_ref, u_ref, up_ref, ga_ref, gb_ref, g1_ref, sc2_ref, sh2_ref,
                  wao_ref, cw_ref, cb_ref, clg_ref, clb_ref, wpw_ref, bpw_ref, wout_ref,
                  ln1g_ref, ln1b_ref, wrh_ref, wrl_ref,
                  x1_ref, h2_ref, lg_ref, ubuf_ref, ush_ref, conv_ref, *, tiles_per_seq):
    tm = MIX_TM
    first = (pl.program_id(0) % tiles_per_seq) == 0

    prev = up_ref[...].astype(F32)
    ubuf_ref[0:CONV_HALO, :] = jnp.where(first, jnp.zeros_like(prev), prev)
    ubuf_ref[CONV_HALO:, :] = u_ref[...].astype(F32)
    for t0 in range(0, tm, MIX_SUB):
        rows = slice(t0, t0 + MIX_SUB)

        base = CONV_HALO - (CONV_K - 1)
        n_shift = MIX_SUB + CONV_HALO - SUBLANES
        for r in range(1, SUBLANES):
            ush_ref[r - 1] = ubuf_ref[t0 + r:t0 + r + n_shift, :]
        for c in range(D_MODEL // LANES):
            cl = slice(c * LANES, (c + 1) * LANES)
            for r0 in range(0, MIX_SUB, CONV_ROWS):
                acc = jnp.zeros((CONV_ROWS, LANES), F32)
                for j in range(CONV_K):
                    a, r = divmod(base + j, SUBLANES)
                    lo = SUBLANES * a + r0
                    win = (ubuf_ref[t0 + lo:t0 + lo + CONV_ROWS, cl] if r == 0
                           else ush_ref[r - 1, lo:lo + CONV_ROWS, cl])
                    acc = acc + cw_ref[j:j + 1, cl] * win
                conv_ref[t0 + r0:t0 + r0 + CONV_ROWS, cl] = acc
        yc = _silu(_layer_norm(conv_ref[rows, :] + cb_ref[...], clg_ref[...], clb_ref[...]))
        y_conv = (jnp.dot(yc.astype(BF16), wpw_ref[...], preferred_element_type=F32)
                  + bpw_ref[...])

        y_attn = jnp.dot(o_ref[rows, :], wao_ref[...], preferred_element_type=F32)
        mix = ga_ref[rows, :].astype(F32) * y_attn + gb_ref[rows, :].astype(F32) * y_conv
        y = jnp.dot(mix.astype(BF16), wout_ref[...], preferred_element_type=F32)
        x1 = _layer_norm(DN_ALPHA * x_ref[rows, :] + g1_ref[0] * y, ln1g_ref[...], ln1b_ref[...])
        x1_ref[rows, :] = x1

        h2 = x1 * (1.0 + sc2_ref[0]) + sh2_ref[0]
        h2_hi = h2.astype(BF16)
        _store_packed(h2_ref.at[:, rows, :], h2)
        h2_lo = (h2 - h2_hi.astype(F32)).astype(BF16)
        nt = lambda w, a: lax.dot_general(w, a, (((1,), (1,)), ((), ())),
                                          preferred_element_type=F32)
        lg_ref[:, rows] = (nt(wrh_ref[...], h2_hi)
                           + (nt(wrl_ref[...], h2_hi) + nt(wrh_ref[...], h2_lo)))


def _mixer_epilogue(x2, o, u, ga, gb, g1, sc2, sh2, wao, cw, cb, clg, clb, wpw, bpw, wout,
                    ln1g, ln1b, wr_hi, wr_lo, seq):
    T, D = x2.shape
    tm = MIX_TM
    tiles_per_seq = seq // tm
    halo_per_tile = tm // CONV_HALO
    tok = pl.BlockSpec((tm, D), lambda i: (i, 0))
    prev = pl.BlockSpec((CONV_HALO, D), lambda i: (jnp.maximum(i * halo_per_tile - 1, 0), 0))
    mod = pl.BlockSpec((1, 1, D), lambda i: (i // tiles_per_seq, 0, 0))
    row = pl.BlockSpec((1, D), lambda i: (0, 0))
    mat = pl.BlockSpec((D, D), lambda i: (0, 0), pipeline_mode=pl.Buffered(1))
    rw = pl.BlockSpec((N_EXPERTS, D), lambda i: (0, 0), pipeline_mode=pl.Buffered(1))
    return pl.pallas_call(
        partial(_mixer_kernel, tiles_per_seq=tiles_per_seq),
        out_shape=[jax.ShapeDtypeStruct((T, D), F32),
                   jax.ShapeDtypeStruct((2, T, D // 4), U32),
                   jax.ShapeDtypeStruct((N_EXPERTS, T), F32)],
        grid=(T // tm,),
        in_specs=[tok, tok, tok, prev, tok, tok, mod, mod, mod,
                  mat, pl.BlockSpec((CONV_K, D), lambda i: (0, 0)), row, row, row, mat, row, mat,
                  row, row, rw, rw],
        out_specs=[tok, pl.BlockSpec((2, tm, D // 4), lambda i: (0, i, 0)),
                   pl.BlockSpec((N_EXPERTS, tm), lambda i: (0, i))],
        scratch_shapes=[pltpu.VMEM((CONV_HALO + tm, D), F32),
                        pltpu.VMEM((SUBLANES - 1, MIX_SUB + CONV_HALO - SUBLANES, D), F32),
                        pltpu.VMEM((tm, D), F32)],
        compiler_params=_cparams("arbitrary"),
        name="mixer_epilogue",
    )(x2, o, u, u, ga, gb, g1, sc2, sh2, wao, cw, cb, clg, clb, wpw, bpw, wout,
      ln1g, ln1b, wr_hi, wr_lo)


def _first_argmax(v, eid):
    m = jnp.max(v, axis=0, keepdims=True)
    i = jnp.min(jnp.where(v == m, eid, float(N_EXPERTS)), axis=0, keepdims=True)
    return m, i


def _route_kernel(lg_ref, rb_ref, idx_ref, gate_ref, cnt_ref):
    tm = ROUTE_TM
    scores = _sigmoid(lg_ref[...])
    choice = scores + rb_ref[...]
    eid = lax.broadcasted_iota(jnp.int32, (N_EXPERTS, tm), 0).astype(F32)
    rows = lambda a, g: a[g * GROUP_SIZE:(g + 1) * GROUP_SIZE, :]

    gs = []
    for g in range(N_GROUPS):
        cg = rows(choice, g)
        eg = (lax.broadcasted_iota(jnp.int32, cg.shape, 0) + g * GROUP_SIZE).astype(F32)
        m1, i1 = _first_argmax(cg, eg)
        m2 = jnp.max(jnp.where(eg == i1, NEG_INF, cg), axis=0, keepdims=True)
        gs.append(jnp.broadcast_to(m1 + m2, (SUBLANES, tm)))

    kept = []
    for g in range(N_GROUPS):
        ahead = jnp.zeros((SUBLANES, tm), F32)
        for o in range(N_GROUPS):
            if o == g:
                continue
            beats = (gs[o] >= gs[g]) if o < g else (gs[o] > gs[g])
            ahead = ahead + jnp.where(beats, 1.0, 0.0)
        cap = jnp.where(ahead < TOPK_GROUPS, float("inf"), NEG_INF)
        for r in range(g * GROUP_SIZE, (g + 1) * GROUP_SIZE, SUBLANES):
            kept.append(jnp.minimum(choice[r:r + SUBLANES, :], cap))
    masked = jnp.concatenate(kept, axis=0)

    sel = jnp.zeros((N_EXPERTS, tm), F32)
    total = jnp.zeros((1, tm), F32)
    picked = []
    for k in range(TOP_K):
        _, i = _first_argmax(masked, eid)
        hit = eid == i
        sc = jnp.sum(jnp.where(hit, scores, 0.0), axis=0, keepdims=True)
        masked = jnp.where(hit, NEG_INF, masked)
        sel = sel + hit.astype(F32)
        total = total + sc
        idx_ref[k:k + 1, :] = i.astype(jnp.int32)
        picked.append(sc)
    for k in range(TOP_K):
        gate_ref[k:k + 1, :] = picked[k] / total * ROUTED_SCALE

    @pl.when(pl.program_id(0) == 0)
    def _():
        cnt_ref[...] = jnp.zeros_like(cnt_ref)
    cnt_ref[...] += jnp.sum(sel, axis=1, keepdims=True)


def _route(logits_t, router_bias):
    T = logits_t.shape[1]
    tm = ROUTE_TM
    return pl.pallas_call(
        _route_kernel,
        out_shape=[jax.ShapeDtypeStruct((TOP_K, T), jnp.int32),
                   jax.ShapeDtypeStruct((TOP_K, T), F32),
                   jax.ShapeDtypeStruct((N_EXPERTS, 1), F32)],
        grid=(T // tm,),
        in_specs=[pl.BlockSpec((N_EXPERTS, tm), lambda i: (0, i)),
                  pl.BlockSpec((N_EXPERTS, 1), lambda i: (0, 0))],
        out_specs=[pl.BlockSpec((TOP_K, tm), lambda i: (0, i)),
                   pl.BlockSpec((TOP_K, tm), lambda i: (0, i)),
                   pl.BlockSpec((N_EXPERTS, 1), lambda i: (0, 0))],
        compiler_params=_cparams("arbitrary"),
        name="route_topk",
    )(logits_t, router_bias.reshape(N_EXPERTS, 1))


def _slot_kernel(idx_ref, start_ref, pos_ref, carry_ref):
    tm = ROUTE_TM

    @pl.when(pl.program_id(0) == 0)
    def _():
        carry_ref[...] = jnp.zeros_like(carry_ref)

    idx = idx_ref[...]
    eid = lax.broadcasted_iota(jnp.int32, (N_EXPERTS, tm), 0)
    hits = [eid == idx[k:k + 1, :] for k in range(TOP_K)]
    sel = jnp.zeros((N_EXPERTS, tm), F32)
    for h in hits:
        sel = sel + h.astype(F32)
    r = lax.broadcasted_iota(jnp.int32, (tm, tm), 0)
    c = lax.broadcasted_iota(jnp.int32, (tm, tm), 1)
    earlier = (r < c).astype(BF16)
    rank = jnp.dot(sel.astype(BF16), earlier, preferred_element_type=F32)
    slot = rank + carry_ref[...] + start_ref[...]
    for k, h in enumerate(hits):
        pos_ref[k:k + 1, :] = jnp.sum(jnp.where(h, slot, 0.0), axis=0,
                                      keepdims=True).astype(jnp.int32)
    carry_ref[...] += jnp.sum(sel, axis=1, keepdims=True)


def _assign_slots(idx_t, pad_start):
    T = idx_t.shape[1]
    tm = ROUTE_TM
    return pl.pallas_call(
        _slot_kernel,
        out_shape=jax.ShapeDtypeStruct((TOP_K, T), jnp.int32),
        grid=(T // tm,),
        in_specs=[pl.BlockSpec((TOP_K, tm), lambda i: (0, i)),
                  pl.BlockSpec((N_EXPERTS, 1), lambda i: (0, 0))],
        out_specs=pl.BlockSpec((TOP_K, tm), lambda i: (0, i)),
        scratch_shapes=[pltpu.VMEM((N_EXPERTS, 1), F32)],
        compiler_params=_cparams("arbitrary"),
        name="assign_slots",
    )(idx_t, pad_start)


def _expert_kernel(be_ref, nu_ref, first_ref, slot_ref, nxt_ref, xs_ref, w1_hbm, w3_hbm, w2_hbm,
                   ys_ref, w1_buf, w3_buf, w2_buf, sem):
    b = pl.program_id(0)

    def weight_copies(expert, slot):
        pairs = ((w1_hbm, w1_buf), (w3_hbm, w3_buf), (w2_hbm, w2_buf))
        return [pltpu.make_async_copy(hbm.at[expert], buf.at[slot], sem.at[slot, i])
                for i, (hbm, buf) in enumerate(pairs)]

    @pl.when(b < nu_ref[0])
    def _():
        slot = slot_ref[b]

        @pl.when(b == 0)
        def _():
            for c in weight_copies(be_ref[0], 0):
                c.start()

        @pl.when(first_ref[b] == 1)
        def _():
            for c in weight_copies(be_ref[b], slot):
                c.wait()

            @pl.when(nxt_ref[b] >= 0)
            def _():
                for c in weight_copies(nxt_ref[b], 1 - slot):
                    c.start()

        lo, hi = _load_packed(xs_ref)
        half = lo.shape[-1]
        mm = lambda w: (jnp.dot(lo, w[slot, :half, :], preferred_element_type=F32)
                        + jnp.dot(hi, w[slot, half:, :], preferred_element_type=F32))
        hb = _silu(mm(w1_buf)) * mm(w3_buf)
        _store_packed(ys_ref, jnp.dot(hb, w2_buf[slot], preferred_element_type=F32))

    @pl.when(b >= nu_ref[0])
    def _():
        ys_ref[...] = jnp.zeros_like(ys_ref)


def _routed_experts(xs, blk_expert, n_used, w1, w3, w2):
    _, P, Dq = xs.shape
    blk = EXPERT_BLK
    n_blocks = P // blk
    D, F = w1.shape[-2:]

    b_ids = jnp.arange(n_blocks, dtype=jnp.int32)
    used = b_ids < n_used[0]
    prev = jnp.concatenate([jnp.full((1,), -1, jnp.int32), blk_expert[:-1]])
    first = (used & (blk_expert != prev)).astype(jnp.int32)
    slot = (jnp.cumsum(first) - 1) % 2
    start_at = jnp.where(first == 1, b_ids, n_blocks)
    later = jnp.concatenate([start_at[1:], jnp.full((1,), n_blocks, jnp.int32)])
    next_start = lax.cummin(later, axis=0, reverse=True)
    nxt = jnp.where(next_start < n_blocks,
                    blk_expert[jnp.minimum(next_start, n_blocks - 1)], -1).astype(jnp.int32)

    row_blk = lambda b, be, nu, *_: (0, jnp.minimum(b, nu[0] - 1), 0)
    grid_spec = pltpu.PrefetchScalarGridSpec(
        num_scalar_prefetch=5,
        grid=(n_blocks,),
        in_specs=[pl.BlockSpec((2, blk, Dq), row_blk),
                  pl.BlockSpec(memory_space=pl.ANY),
                  pl.BlockSpec(memory_space=pl.ANY),
                  pl.BlockSpec(memory_space=pl.ANY)],
        out_specs=pl.BlockSpec((2, blk, Dq), lambda b, *_: (0, b, 0)),
        scratch_shapes=[pltpu.VMEM((2, D, F), F32), pltpu.VMEM((2, D, F), F32),
                        pltpu.VMEM((2, F, D), F32), pltpu.SemaphoreType.DMA((2, 3))],
    )
    return pl.pallas_call(
        _expert_kernel,
        out_shape=jax.ShapeDtypeStruct((2, P, Dq), U32),
        grid_spec=grid_spec,
        compiler_params=_cparams("arbitrary"),
        name="routed_experts",
    )(blk_expert, n_used, first, slot.astype(jnp.int32), nxt, xs, w1, w3, w2)


def _sc_mesh():
    return plsc.VectorSubcoreMesh(core_axis_name="c", subcore_axis_name="s")


def _sc_dispatch(rows, slots_t, n_slots):
    T, W = rows.shape
    K = slots_t.shape[0]
    win = SC_WINDOW

    @pl.kernel(out_type=jax.ShapeDtypeStruct((n_slots, W), rows.dtype), mesh=_sc_mesh(),
               scratch_types=[])
    def dispatch(x_hbm, i_hbm, o_hbm):
        def body(x_vmem, i_vmem):
            for k in range(K):
                pltpu.sync_copy(x_vmem, o_hbm.at[i_vmem.at[k]])

        pltpu.emit_pipeline(
            body,
            grid=(T // win,),
            in_specs=[pl.BlockSpec((win, W), lambda i: (i, 0)),
                      pl.BlockSpec((K, win), lambda i: (0, i))],
            out_specs=[],
            core_axis_name=("c", "s"),
            dimension_semantics=(pltpu.PARALLEL,),
        )(x_hbm, i_hbm)

    return dispatch(rows, slots_t)


def _sc_gather(table, idx):
    W = table.shape[1]
    N = idx.shape[1]
    win = SC_WINDOW

    @pl.kernel(out_type=jax.ShapeDtypeStruct((N, W), table.dtype), mesh=_sc_mesh(),
               scratch_types=[])
    def gather(x_hbm, i_hbm, o_hbm):
        def body(i_vmem, o_vmem):
            pltpu.sync_copy(x_hbm.at[i_vmem.at[0]], o_vmem)

        pltpu.emit_pipeline(
            body,
            grid=(N // win,),
            in_specs=[pl.BlockSpec((1, win), lambda i: (0, i))],
            out_specs=[pl.BlockSpec((win, W), lambda i: (i, 0))],
            core_axis_name=("c", "s"),
            dimension_semantics=(pltpu.PARALLEL,),
        )(i_hbm, o_hbm)

    return gather(table, idx)


def _final_kernel(x1_ref, h2_ref, yg_ref, gate_ref, g2_ref, ws1_ref, ws3_ref, ws2_ref,
                  lng_ref, lnb_ref, o_ref):
    Dh = D_MODEL // 2
    gate = gate_ref[...]
    routed_lo = jnp.zeros((x1_ref.shape[0], Dh), F32)
    routed_hi = jnp.zeros((x1_ref.shape[0], Dh), F32)
    for k in range(TOP_K):
        lo, hi = _load_packed(yg_ref.at[k])
        routed_lo = routed_lo + lo * gate[:, k:k + 1]
        routed_hi = routed_hi + hi * gate[:, k:k + 1]
    h2 = jnp.concatenate(_load_packed(h2_ref), axis=-1).astype(BF16)
    a = jnp.dot(h2, ws1_ref[...], preferred_element_type=F32)
    b = jnp.dot(h2, ws3_ref[...], preferred_element_type=F32)
    shared = jnp.dot((_silu(a) * b).astype(BF16), ws2_ref[...], preferred_element_type=F32)
    y2 = jnp.concatenate([routed_lo, routed_hi], axis=-1) + shared
    o_ref[...] = _layer_norm(DN_ALPHA * x1_ref[...] + g2_ref[0] * y2, lng_ref[...], lnb_ref[...])


def _final(x1, h2, yg, gate, g2, ws1, ws3, ws2, ln2g, ln2b, seq):
    T, D = x1.shape
    tm = FINAL_TM
    tiles_per_seq = seq // tm
    F = ws1.shape[-1]
    tok = pl.BlockSpec((tm, D), lambda i: (i, 0))
    row = pl.BlockSpec((1, D), lambda i: (0, 0))
    return pl.pallas_call(
        _final_kernel,
        out_shape=jax.ShapeDtypeStruct((T, D), F32),
        grid=(T // tm,),
        in_specs=[tok, pl.BlockSpec((2, tm, D // 4), lambda i: (0, i, 0)),
                  pl.BlockSpec((TOP_K, 2, tm, D // 4), lambda i: (0, 0, i, 0)),
                  pl.BlockSpec((tm, TOP_K), lambda i: (i, 0)),
                  pl.BlockSpec((1, 1, D), lambda i: (i // tiles_per_seq, 0, 0)),
                  pl.BlockSpec((D, F), lambda i: (0, 0)),
                  pl.BlockSpec((D, F), lambda i: (0, 0)),
                  pl.BlockSpec((F, D), lambda i: (0, 0)),
                  row, row],
        out_specs=tok,
        compiler_params=_cparams("arbitrary"),
        name="moe_combine_final",
    )(x1, h2, yg, gate, g2, ws1, ws3, ws2, ln2g, ln2b)


def kernel(x, c, ada_w, ada_b, w_in, b_in, lambda_q1, lambda_k1, lambda_q2, lambda_k2, subln_g, w_attn_o, conv_w, conv_b, conv_ln_g, conv_ln_b, conv_pw_w, conv_pw_b, w_out, ln1_g, ln1_b, router_w, router_bias, w1, w3, w2, ws1, ws3, ws2, ln2_g, ln2_b):
    B, S, D = x.shape
    T = B * S
    l = 0
    x2 = x.reshape(T, D)
    row = lambda a: a.reshape(1, -1)

    mod = _ada_modulation(c, ada_w[l], ada_b[l])
    sh1, sc1, g1, sh2, sc2, g2 = [m.reshape(B, 1, D) for m in jnp.split(mod, 6, axis=-1)]

    qt, k, vt, u, ga, gb = _in_projection(x2, sc1, sh1, w_in[l].astype(BF16), row(b_in[l]), S)
    o = _diff_attention(qt, k, vt, row(lambda_q1[l]), row(lambda_k1[l]), row(lambda_q2[l]),
                        row(lambda_k2[l]), row(subln_g[l]), B, S)

    wr = router_w[l].T
    wr_hi = wr.astype(BF16)
    wr_lo = (wr - wr_hi.astype(F32)).astype(BF16)
    x1, h2, logits = _mixer_epilogue(
        x2, o, u, ga, gb, g1, sc2, sh2, w_attn_o[l].astype(BF16), conv_w[l], row(conv_b[l]),
        row(conv_ln_g[l]), row(conv_ln_b[l]), conv_pw_w[l].astype(BF16), row(conv_pw_b[l]),
        w_out[l].astype(BF16), row(ln1_g[l]), row(ln1_b[l]), wr_hi, wr_lo, S)

    idx, gate, counts = _route(logits, router_bias[l])

    blk = EXPERT_BLK
    cnt = counts.reshape(N_EXPERTS).astype(jnp.int32)
    padded = (cnt + blk - 1) // blk * blk
    pad_end = jnp.cumsum(padded)
    pad_start = pad_end - padded
    P = T * TOP_K + N_EXPERTS * blk
    n_blocks = P // blk
    blk_first_row = jnp.arange(n_blocks, dtype=jnp.int32) * blk
    blk_expert = jnp.minimum(
        jnp.sum((pad_end[None, :] <= blk_first_row[:, None]).astype(jnp.int32), axis=1),
        N_EXPERTS - 1)
    n_used = (pad_end[-1:] // blk).astype(jnp.int32)

    pos = _assign_slots(idx, pad_start.astype(F32).reshape(N_EXPERTS, 1))

    Dq = D // 4
    half_base = (jnp.arange(2, dtype=jnp.int32) * P)[None, :, None]
    slots = pos[:, None, :] + half_base
    xs = _sc_dispatch(h2.reshape(2 * T, Dq), slots.reshape(TOP_K, 2 * T), 2 * P)
    ys = _routed_experts(xs.reshape(2, P, Dq), blk_expert, n_used, w1[l], w3[l], w2[l])
    yg = _sc_gather(ys.reshape(2 * P, Dq), slots.reshape(1, TOP_K * 2 * T))
    yg = yg.reshape(TOP_K, 2, T, Dq)

    out = _final(x1, h2, yg, gate.T, g2, ws1[l].astype(BF16), ws3[l].astype(BF16),
                 ws2[l].astype(BF16), row(ln2_g[l]), row(ln2_b[l]), S)
    return out.reshape(B, S, D)
```

```python
import math
from functools import partial

import jax
import jax.numpy as jnp
from jax import lax
from jax.experimental import pallas as pl
from jax.experimental.pallas import tpu as pltpu
from jax.experimental.pallas import tpu_sc as plsc

D_MODEL = 1024
N_HEADS = 8
HEAD_DIM = 64
V_DIM = 128
CONV_K = 31
N_EXPERTS = 256
TOP_K = 8
N_GROUPS = 8
GROUP_SIZE = N_EXPERTS // N_GROUPS
TOPK_GROUPS = 4
ROUTED_SCALE = 2.5
LN_EPS = 1e-5
RMS_EPS = 1e-5
DEPTH = 1
DN_ALPHA = (2.0 * DEPTH) ** 0.25
LAMBDA_INIT = 0.8 - 0.6 * math.exp(-0.3 * 0)
Q_SCALE = HEAD_DIM ** -0.5 * math.log2(math.e)

LANES = 128
SUBLANES = 8
VMEM_LIMIT_BYTES = 56 * 1024 * 1024

PROJ_TM = 512
ATTN_TQ = 512
ATTN_QG = 512
ATTN_KC = 256
ONES_ROWS = 16
MIX_TM = 512
MIX_SUB = 256
CONV_HALO = 32
CONV_ROWS = 64
ROUTE_TM = 512
EXPERT_BLK = 512
FINAL_TM = 256
SC_WINDOW = 128

F32 = jnp.float32
BF16 = jnp.bfloat16
U32 = jnp.uint32
NEG_INF = float("-inf")


def _cparams(*sem):
    return pltpu.CompilerParams(dimension_semantics=sem, vmem_limit_bytes=VMEM_LIMIT_BYTES)


def _sigmoid(x):
    return 1.0 / (1.0 + jnp.exp(-x))


def _silu(x):
    return x * _sigmoid(x)


BF16_BITS = 16
HIGH_HALF_WORD = 0xFFFF0000


def _pack_halves(x):
    half = x.shape[-1] // 2
    bits = lambda v: lax.bitcast_convert_type(v.astype(BF16).astype(F32), U32)
    return (bits(x[:, :half]) >> BF16_BITS) | (bits(x[:, half:]) & jnp.uint32(HIGH_HALF_WORD))


def _unpack_halves(w):
    lo = lax.bitcast_convert_type(w << BF16_BITS, F32)
    hi = lax.bitcast_convert_type(w & jnp.uint32(HIGH_HALF_WORD), F32)
    return lo, hi


def _store_packed(ref, x):
    w = _pack_halves(x)
    q = w.shape[-1] // 2
    ref[0] = w[:, :q]
    ref[1] = w[:, q:]


def _load_packed(ref):
    return _unpack_halves(jnp.concatenate([ref[0], ref[1]], axis=-1))


def _layer_norm(r, g, b):
    mu = jnp.mean(r, axis=-1, keepdims=True)
    d = r - mu
    var = jnp.mean(d * d, axis=-1, keepdims=True)
    return d * lax.rsqrt(var + LN_EPS) * g + b


def _ada_kernel(c_ref, w_ref, b_ref, o_ref):
    a = _silu(c_ref[...])
    o_ref[...] = jnp.dot(a, w_ref[...], preferred_element_type=F32,
                         precision=lax.Precision.HIGHEST) + b_ref[...]


def _ada_modulation(c, w, b):
    B, D = c.shape
    n = w.shape[1] // D
    return pl.pallas_call(
        _ada_kernel,
        out_shape=jax.ShapeDtypeStruct((B, n * D), F32),
        grid=(n,),
        in_specs=[pl.BlockSpec((B, D), lambda j: (0, 0)),
                  pl.BlockSpec((D, D), lambda j: (0, j)),
                  pl.BlockSpec((1, D), lambda j: (0, j))],
        out_specs=pl.BlockSpec((B, D), lambda j: (0, j)),
        compiler_params=_cparams("arbitrary"),
        name="ada_modulation",
    )(c, w, b.reshape(1, n * D))


def _inproj_kernel(x_ref, sc_ref, sh_ref, w_ref, b_ref,
                   qt_ref, k_ref, vt_ref, u_ref, ga_ref, gb_ref):
    D = D_MODEL
    h = (x_ref[...] * (1.0 + sc_ref[0]) + sh_ref[0]).astype(BF16)

    def proj(j):
        return (jnp.dot(h, w_ref[:, j * D:(j + 1) * D], preferred_element_type=F32)
                + b_ref[:, j * D:(j + 1) * D])

    qt_ref[...] = (proj(0) * Q_SCALE).T.astype(BF16)
    k_ref[...] = proj(1).astype(BF16)
    vt_ref[...] = proj(2).T.astype(BF16)
    u_ref[...] = (proj(3) * _sigmoid(proj(4))).astype(BF16)
    ga_ref[...] = _sigmoid(proj(5)).astype(BF16)
    gb_ref[...] = _sigmoid(proj(6)).astype(BF16)


def _in_projection(x2, sc1, sh1, w_in, b_in, seq):
    T, D = x2.shape
    tm = PROJ_TM
    tiles_per_seq = seq // tm
    ncols = w_in.shape[1]
    tok = pl.BlockSpec((tm, D), lambda i: (i, 0))
    mod = pl.BlockSpec((1, 1, D), lambda i: (i // tiles_per_seq, 0, 0))
    tok_t = pl.BlockSpec((D, tm), lambda i: (0, i))
    out = jax.ShapeDtypeStruct((T, D), BF16)
    out_t = jax.ShapeDtypeStruct((D, T), BF16)
    return pl.pallas_call(
        _inproj_kernel,
        out_shape=[out_t, out, out_t, out, out, out],
        grid=(T // tm,),
        in_specs=[tok, mod, mod,
                  pl.BlockSpec((D, ncols), lambda i: (0, 0)),
                  pl.BlockSpec((1, ncols), lambda i: (0, 0))],
        out_specs=[tok_t, tok, tok_t, tok, tok, tok],
        compiler_params=_cparams("arbitrary"),
        name="in_projection",
    )(x2, sc1, sh1, w_in, b_in)


def _attn_tasks(nq):
    tasks = [(qi, j, int(j == qi)) for qi in range(nq) for j in range(qi + 1)]
    if len(tasks) % 2:
        raise ValueError("the two-slot task loop needs an even number of tile pairs")
    qis, js, lasts = zip(*(tasks + [(0, 0, 0)]))
    as_i32 = lambda v: jnp.asarray(v, jnp.int32)
    return as_i32(qis), as_i32(js), as_i32(lasts), len(tasks)


def _attn_kernel(tqi_ref, tj_ref, tlast_ref, lq1_ref, lk1_ref, lq2_ref, lk2_ref, g_ref,
                 qt_ref, k_ref, vt_ref, o_ref, q2_ref, vx_ref, *scratch, n_tasks):
    tq = ATTN_TQ
    qg = ATTN_QG
    ng = 2 * tq // qg
    groups = [scratch[i * ng:(i + 1) * ng] for i in range(7)]
    s_slots, x_slots = groups[0:2], groups[2:4]
    m_refs, l_refs, acc_refs = groups[4:]
    seq = k_ref.shape[0]

    vx_ref[:V_DIM, :] = vt_ref[...]
    vx_ref[V_DIM:, :] = jnp.ones((ONES_ROWS, seq), BF16)

    lam = (jnp.exp(jnp.sum(lq1_ref[...] * lk1_ref[...], axis=-1, keepdims=True))
           - jnp.exp(jnp.sum(lq2_ref[...] * lk2_ref[...], axis=-1, keepdims=True))
           + LAMBDA_INIT)

    for i in range(seq // tq):
        qt = qt_ref[:, i * tq:(i + 1) * tq]
        feat = lax.broadcasted_iota(jnp.int32, qt.shape, 0)
        zero = jnp.zeros_like(qt)
        q2_ref[:, 2 * tq * i:2 * tq * i + tq] = jnp.where(feat < HEAD_DIM, qt, zero)
        q2_ref[:, 2 * tq * i + tq:2 * tq * (i + 1)] = jnp.where(feat >= HEAD_DIM, qt, zero)

    def scores(qi, j, slot, g):
        start = pl.multiple_of(j * tq, tq)
        col = pl.multiple_of(qi * (2 * tq) + g * qg, qg)
        s = jnp.dot(k_ref[pl.ds(start, tq), :], q2_ref[:, pl.ds(col, qg)],
                    preferred_element_type=F32)
        s_slots[slot][g][...] = s
        x_slots[slot][g][...] = jnp.max(s, axis=0, keepdims=True)

    def update(j, slot, g, masked):
        start = pl.multiple_of(j * tq, tq)
        s_ref = s_slots[slot][g]

        def load(k0):
            s = s_ref[k0:k0 + ATTN_KC, :]
            if masked:
                key = lax.broadcasted_iota(jnp.int32, s.shape, 0) + k0
                qry = lax.broadcasted_iota(jnp.int32, s.shape, 1) + (g * qg) % tq
                s = jnp.where(key <= qry, s, NEG_INF)
            return s

        m_prev = m_refs[g][...]
        if masked:
            tile_max = jnp.max(load(0), axis=0, keepdims=True)
            for k0 in range(ATTN_KC, tq, ATTN_KC):
                tile_max = jnp.maximum(tile_max, jnp.max(load(k0), axis=0, keepdims=True))
        else:
            tile_max = x_slots[slot][g][...]
        m_new = jnp.maximum(m_prev, tile_max)
        alpha = jnp.exp2(m_prev - m_new)
        m_refs[g][...] = m_new
        pv = None
        for k0 in range(0, tq, ATTN_KC):
            p = jnp.exp2((load(k0) - m_new).astype(BF16))
            keys = pl.ds(pl.multiple_of(start + k0, ATTN_KC), ATTN_KC)
            part = jnp.dot(vx_ref[:, keys], p, preferred_element_type=F32)
            pv = part if pv is None else pv + part
        acc_refs[g][...] = alpha * acc_refs[g][...] + pv[:V_DIM, :]
        l_refs[g][...] = alpha * l_refs[g][...] + pv[V_DIM:V_DIM + 1, :]

    def reset():
        for g in range(ng):
            m_refs[g][...] = jnp.full(m_refs[g].shape, NEG_INF, F32)
            l_refs[g][...] = jnp.zeros(l_refs[g].shape, F32)
            acc_refs[g][...] = jnp.zeros(acc_refs[g].shape, F32)

    def finish(qi):
        o = (jnp.concatenate([r[...] for r in acc_refs], axis=-1)
             / jnp.concatenate([r[...] for r in l_refs], axis=-1))
        o = o[:, :tq] - lam * o[:, tq:]
        o = o * lax.rsqrt(jnp.mean(o * o, axis=0, keepdims=True) + RMS_EPS) * g_ref[...]
        row = pl.multiple_of(qi * tq, tq)
        o_ref[pl.ds(row, tq), :] = (o * (1.0 - LAMBDA_INIT)).T.astype(BF16)

    def task(t, slot):
        qi, j = tqi_ref[t], tj_ref[t]
        nqi, nj = tqi_ref[t + 1], tj_ref[t + 1]

        def run(masked):
            for g in range(ng):
                scores(nqi, nj, 1 - slot, g)
                update(j, slot, g, masked)

        @pl.when(tlast_ref[t] == 0)
        def _():
            run(False)

        @pl.when(tlast_ref[t] == 1)
        def _():
            run(True)
            finish(qi)
            reset()

    reset()
    for g in range(ng):
        scores(0, 0, 0, g)

    def pair(i, carry):
        task(2 * i, 0)
        task(2 * i + 1, 1)
        return carry

    lax.fori_loop(0, n_tasks // 2, pair, 0)


def _diff_attention(qt, k, vt, lq1, lk1, lq2, lk2, subln_g, batch, seq):
    T, D = k.shape
    tq = ATTN_TQ
    nq = seq // tq
    tqi, tj, tlast, n_tasks = _attn_tasks(nq)
    lam_spec = pl.BlockSpec((1, HEAD_DIM), lambda b, h, *_: (0, 0))
    rows_spec = pl.BlockSpec((seq, V_DIM), lambda b, h, *_: (b, h))
    cols_spec = pl.BlockSpec((V_DIM, seq), lambda b, h, *_: (h, b))
    qg = ATTN_QG
    per_group = lambda shape, dtype: [pltpu.VMEM(shape, dtype) for _ in range(2 * tq // qg)]
    grid_spec = pltpu.PrefetchScalarGridSpec(
        num_scalar_prefetch=3,
        grid=(batch, N_HEADS),
        in_specs=[lam_spec, lam_spec, lam_spec, lam_spec,
                  pl.BlockSpec((V_DIM, 1), lambda b, h, *_: (0, 0)),
                  cols_spec, rows_spec, cols_spec],
        out_specs=rows_spec,
        scratch_shapes=(
            [pltpu.VMEM((V_DIM, 2 * seq), BF16),
             pltpu.VMEM((V_DIM + ONES_ROWS, seq), BF16)]
            + per_group((tq, qg), F32)
            + per_group((tq, qg), F32)
            + per_group((1, qg), F32)
            + per_group((1, qg), F32)
            + per_group((1, qg), F32)
            + per_group((1, qg), F32)
            + per_group((V_DIM, qg), F32)),
    )
    return pl.pallas_call(
        partial(_attn_kernel, n_tasks=n_tasks),
        out_shape=jax.ShapeDtypeStruct((T, D), BF16),
        grid_spec=grid_spec,
        compiler_params=_cparams("arbitrary", "arbitrary"),
        name="diff_attention",
    )(tqi, tj, tlast, lq1, lk1, lq2, lk2, subln_g.reshape(V_DIM, 1), qt, k, vt)


def _mixer_kernel(x_ref, o_ref, u_ref, up_ref, ga_ref, gb_ref, g1_ref, sc2_ref, sh2_ref,
                  wao_ref, cw_ref, cb_ref, clg_ref, clb_ref, wpw_ref, bpw_ref, wout_ref,
                  ln1g_ref, ln1b_ref, wrh_ref, wrl_ref,
                  x1_ref, h2_ref, lg_ref, ubuf_ref, ush_ref, conv_ref, *, tiles_per_seq):
    tm = MIX_TM
    first = (pl.program_id(0) % tiles_per_seq) == 0

    prev = up_ref[...].astype(F32)
    ubuf_ref[0:CONV_HALO, :] = jnp.where(first, jnp.zeros_like(prev), prev)
    ubuf_ref[CONV_HALO:, :] = u_ref[...].astype(F32)
    for t0 in range(0, tm, MIX_SUB):
        rows = slice(t0, t0 + MIX_SUB)

        base = CONV_HALO - (CONV_K - 1)
        n_shift = MIX_SUB + CONV_HALO - SUBLANES
        for r in range(1, SUBLANES):
            ush_ref[r - 1] = ubuf_ref[t0 + r:t0 + r + n_shift, :]
        for c in range(D_MODEL // LANES):
            cl = slice(c * LANES, (c + 1) * LANES)
            for r0 in range(0, MIX_SUB, CONV_ROWS):
                acc = jnp.zeros((CONV_ROWS, LANES), F32)
                for j in range(CONV_K):
                    a, r = divmod(base + j, SUBLANES)
                    lo = SUBLANES * a + r0
                    win = (ubuf_ref[t0 + lo:t0 + lo + CONV_ROWS, cl] if r == 0
                           else ush_ref[r - 1, lo:lo + CONV_ROWS, cl])
                    acc = acc + cw_ref[j:j + 1, cl] * win
                conv_ref[t0 + r0:t0 + r0 + CONV_ROWS, cl] = acc
        yc = _silu(_layer_norm(conv_ref[rows, :] + cb_ref[...], clg_ref[...], clb_ref[...]))
        y_conv = (jnp.dot(yc.astype(BF16), wpw_ref[...], preferred_element_type=F32)
                  + bpw_ref[...])

        y_attn = jnp.dot(o_ref[rows, :], wao_ref[...], preferred_element_type=F32)
        mix = ga_ref[rows, :].astype(F32) * y_attn + gb_ref[rows, :].astype(F32) * y_conv
        y = jnp.dot(mix.astype(BF16), wout_ref[...], preferred_element_type=F32)
        x1 = _layer_norm(DN_ALPHA * x_ref[rows, :] + g1_ref[0] * y, ln1g_ref[...], ln1b_ref[...])
        x1_ref[rows, :] = x1

        h2 = x1 * (1.0 + sc2_ref[0]) + sh2_ref[0]
        h2_hi = h2.astype(BF16)
        _store_packed(h2_ref.at[:, rows, :], h2)
        h2_lo = (h2 - h2_hi.astype(F32)).astype(BF16)
        nt = lambda w, a: lax.dot_general(w, a, (((1,), (1,)), ((), ())),
                                          preferred_element_type=F32)
        lg_ref[:, rows] = (nt(wrh_ref[...], h2_hi)
                           + (nt(wrl_ref[...], h2_hi) + nt(wrh_ref[...], h2_lo)))


def _mixer_epilogue(x2, o, u, ga, gb, g1, sc2, sh2, wao, cw, cb, clg, clb, wpw, bpw, wout,
                    ln1g, ln1b, wr_hi, wr_lo, seq):
    T, D = x2.shape
    tm = MIX_TM
    tiles_per_seq = seq // tm
    halo_per_tile = tm // CONV_HALO
    tok = pl.BlockSpec((tm, D), lambda i: (i, 0))
    prev = pl.BlockSpec((CONV_HALO, D), lambda i: (jnp.maximum(i * halo_per_tile - 1, 0), 0))
    mod = pl.BlockSpec((1, 1, D), lambda i: (i // tiles_per_seq, 0, 0))
    row = pl.BlockSpec((1, D), lambda i: (0, 0))
    mat = pl.BlockSpec((D, D), lambda i: (0, 0), pipeline_mode=pl.Buffered(1))
    rw = pl.BlockSpec((N_EXPERTS, D), lambda i: (0, 0), pipeline_mode=pl.Buffered(1))
    return pl.pallas_call(
        partial(_mixer_kernel, tiles_per_seq=tiles_per_seq),
        out_shape=[jax.ShapeDtypeStruct((T, D), F32),
                   jax.ShapeDtypeStruct((2, T, D // 4), U32),
                   jax.ShapeDtypeStruct((N_EXPERTS, T), F32)],
        grid=(T // tm,),
        in_specs=[tok, tok, tok, prev, tok, tok, mod, mod, mod,
                  mat, pl.BlockSpec((CONV_K, D), lambda i: (0, 0)), row, row, row, mat, row, mat,
                  row, row, rw, rw],
        out_specs=[tok, pl.BlockSpec((2, tm, D // 4), lambda i: (0, i, 0)),
                   pl.BlockSpec((N_EXPERTS, tm), lambda i: (0, i))],
        scratch_shapes=[pltpu.VMEM((CONV_HALO + tm, D), F32),
                        pltpu.VMEM((SUBLANES - 1, MIX_SUB + CONV_HALO - SUBLANES, D), F32),
                        pltpu.VMEM((tm, D), F32)],
        compiler_params=_cparams("arbitrary"),
        name="mixer_epilogue",
    )(x2, o, u, u, ga, gb, g1, sc2, sh2, wao, cw, cb, clg, clb, wpw, bpw, wout,
      ln1g, ln1b, wr_hi, wr_lo)


def _first_argmax(v, eid):
    m = jnp.max(v, axis=0, keepdims=True)
    i = jnp.min(jnp.where(v == m, eid, float(N_EXPERTS)), axis=0, keepdims=True)
    return m, i


def _route_kernel(lg_ref, rb_ref, idx_ref, gate_ref, cnt_ref):
    tm = ROUTE_TM
    scores = _sigmoid(lg_ref[...])
    choice = scores + rb_ref[...]
    eid = lax.broadcasted_iota(jnp.int32, (N_EXPERTS, tm), 0).astype(F32)
    rows = lambda a, g: a[g * GROUP_SIZE:(g + 1) * GROUP_SIZE, :]

    gs = []
    for g in range(N_GROUPS):
        cg = rows(choice, g)
        eg = (lax.broadcasted_iota(jnp.int32, cg.shape, 0) + g * GROUP_SIZE).astype(F32)
        m1, i1 = _first_argmax(cg, eg)
        m2 = jnp.max(jnp.where(eg == i1, NEG_INF, cg), axis=0, keepdims=True)
        gs.append(jnp.broadcast_to(m1 + m2, (SUBLANES, tm)))

    kept = []
    for g in range(N_GROUPS):
        ahead = jnp.zeros((SUBLANES, tm), F32)
        for o in range(N_GROUPS):
            if o == g:
                continue
            beats = (gs[o] >= gs[g]) if o < g else (gs[o] > gs[g])
            ahead = ahead + jnp.where(beats, 1.0, 0.0)
        cap = jnp.where(ahead < TOPK_GROUPS, float("inf"), NEG_INF)
        for r in range(g * GROUP_SIZE, (g + 1) * GROUP_SIZE, SUBLANES):
            kept.append(jnp.minimum(choice[r:r + SUBLANES, :], cap))
    masked = jnp.concatenate(kept, axis=0)

    sel = jnp.zeros((N_EXPERTS, tm), F32)
    total = jnp.zeros((1, tm), F32)
    picked = []
    for k in range(TOP_K):
        _, i = _first_argmax(masked, eid)
        hit = eid == i
        sc = jnp.sum(jnp.where(hit, scores, 0.0), axis=0, keepdims=True)
        masked = jnp.where(hit, NEG_INF, masked)
        sel = sel + hit.astype(F32)
        total = total + sc
        idx_ref[k:k + 1, :] = i.astype(jnp.int32)
        picked.append(sc)
    for k in range(TOP_K):
        gate_ref[k:k + 1, :] = picked[k] / total * ROUTED_SCALE

    @pl.when(pl.program_id(0) == 0)
    def _():
        cnt_ref[...] = jnp.zeros_like(cnt_ref)
    cnt_ref[...] += jnp.sum(sel, axis=1, keepdims=True)


def _route(logits_t, router_bias):
    T = logits_t.shape[1]
    tm = ROUTE_TM
    return pl.pallas_call(
        _route_kernel,
        out_shape=[jax.ShapeDtypeStruct((TOP_K, T), jnp.int32),
                   jax.ShapeDtypeStruct((TOP_K, T), F32),
                   jax.ShapeDtypeStruct((N_EXPERTS, 1), F32)],
        grid=(T // tm,),
        in_specs=[pl.BlockSpec((N_EXPERTS, tm), lambda i: (0, i)),
                  pl.BlockSpec((N_EXPERTS, 1), lambda i: (0, 0))],
        out_specs=[pl.BlockSpec((TOP_K, tm), lambda i: (0, i)),
                   pl.BlockSpec((TOP_K, tm), lambda i: (0, i)),
                   pl.BlockSpec((N_EXPERTS, 1), lambda i: (0, 0))],
        compiler_params=_cparams("arbitrary"),
        name="route_topk",
    )(logits_t, router_bias.reshape(N_EXPERTS, 1))


def _slot_kernel(idx_ref, start_ref, pos_ref, carry_ref):
    tm = ROUTE_TM

    @pl.when(pl.program_id(0) == 0)
    def _():
        carry_ref[...] = jnp.zeros_like(carry_ref)

    idx = idx_ref[...]
    eid = lax.broadcasted_iota(jnp.int32, (N_EXPERTS, tm), 0)
    hits = [eid == idx[k:k + 1, :] for k in range(TOP_K)]
    sel = jnp.zeros((N_EXPERTS, tm), F32)
    for h in hits:
        sel = sel + h.astype(F32)
    r = lax.broadcasted_iota(jnp.int32, (tm, tm), 0)
    c = lax.broadcasted_iota(jnp.int32, (tm, tm), 1)
    earlier = (r < c).astype(BF16)
    rank = jnp.dot(sel.astype(BF16), earlier, preferred_element_type=F32)
    slot = rank + carry_ref[...] + start_ref[...]
    for k, h in enumerate(hits):
        pos_ref[k:k + 1, :] = jnp.sum(jnp.where(h, slot, 0.0), axis=0,
                                      keepdims=True).astype(jnp.int32)
    carry_ref[...] += jnp.sum(sel, axis=1, keepdims=True)


def _assign_slots(idx_t, pad_start):
    T = idx_t.shape[1]
    tm = ROUTE_TM
    return pl.pallas_call(
        _slot_kernel,
        out_shape=jax.ShapeDtypeStruct((TOP_K, T), jnp.int32),
        grid=(T // tm,),
        in_specs=[pl.BlockSpec((TOP_K, tm), lambda i: (0, i)),
                  pl.BlockSpec((N_EXPERTS, 1), lambda i: (0, 0))],
        out_specs=pl.BlockSpec((TOP_K, tm), lambda i: (0, i)),
        scratch_shapes=[pltpu.VMEM((N_EXPERTS, 1), F32)],
        compiler_params=_cparams("arbitrary"),
        name="assign_slots",
    )(idx_t, pad_start)


def _expert_kernel(be_ref, nu_ref, first_ref, slot_ref, nxt_ref, xs_ref, w1_hbm, w3_hbm, w2_hbm,
                   ys_ref, w1_buf, w3_buf, w2_buf, sem):
    b = pl.program_id(0)

    def weight_copies(expert, slot):
        pairs = ((w1_hbm, w1_buf), (w3_hbm, w3_buf), (w2_hbm, w2_buf))
        return [pltpu.make_async_copy(hbm.at[expert], buf.at[slot], sem.at[slot, i])
                for i, (hbm, buf) in enumerate(pairs)]

    @pl.when(b < nu_ref[0])
    def _():
        slot = slot_ref[b]

        @pl.when(b == 0)
        def _():
            for c in weight_copies(be_ref[0], 0):
                c.start()

        @pl.when(first_ref[b] == 1)
        def _():
            for c in weight_copies(be_ref[b], slot):
                c.wait()

            @pl.when(nxt_ref[b] >= 0)
            def _():
                for c in weight_copies(nxt_ref[b], 1 - slot):
                    c.start()

        lo, hi = _load_packed(xs_ref)
        half = lo.shape[-1]
        mm = lambda w: (jnp.dot(lo, w[slot, :half, :], preferred_element_type=F32)
                        + jnp.dot(hi, w[slot, half:, :], preferred_element_type=F32))
        hb = _silu(mm(w1_buf)) * mm(w3_buf)
        _store_packed(ys_ref, jnp.dot(hb, w2_buf[slot], preferred_element_type=F32))

    @pl.when(b >= nu_ref[0])
    def _():
        ys_ref[...] = jnp.zeros_like(ys_ref)


def _routed_experts(xs, blk_expert, n_used, w1, w3, w2):
    _, P, Dq = xs.shape
    blk = EXPERT_BLK
    n_blocks = P // blk
    D, F = w1.shape[-2:]

    b_ids = jnp.arange(n_blocks, dtype=jnp.int32)
    used = b_ids < n_used[0]
    prev = jnp.concatenate([jnp.full((1,), -1, jnp.int32), blk_expert[:-1]])
    first = (used & (blk_expert != prev)).astype(jnp.int32)
    slot = (jnp.cumsum(first) - 1) % 2
    start_at = jnp.where(first == 1, b_ids, n_blocks)
    later = jnp.concatenate([start_at[1:], jnp.full((1,), n_blocks, jnp.int32)])
    next_start = lax.cummin(later, axis=0, reverse=True)
    nxt = jnp.where(next_start < n_blocks,
                    blk_expert[jnp.minimum(next_start, n_blocks - 1)], -1).astype(jnp.int32)

    row_blk = lambda b, be, nu, *_: (0, jnp.minimum(b, nu[0] - 1), 0)
    grid_spec = pltpu.PrefetchScalarGridSpec(
        num_scalar_prefetch=5,
        grid=(n_blocks,),
        in_specs=[pl.BlockSpec((2, blk, Dq), row_blk),
                  pl.BlockSpec(memory_space=pl.ANY),
                  pl.BlockSpec(memory_space=pl.ANY),
                  pl.BlockSpec(memory_space=pl.ANY)],
        out_specs=pl.BlockSpec((2, blk, Dq), lambda b, *_: (0, b, 0)),
        scratch_shapes=[pltpu.VMEM((2, D, F), F32), pltpu.VMEM((2, D, F), F32),
                        pltpu.VMEM((2, F, D), F32), pltpu.SemaphoreType.DMA((2, 3))],
    )
    return pl.pallas_call(
        _expert_kernel,
        out_shape=jax.ShapeDtypeStruct((2, P, Dq), U32),
        grid_spec=grid_spec,
        compiler_params=_cparams("arbitrary"),
        name="routed_experts",
    )(blk_expert, n_used, first, slot.astype(jnp.int32), nxt, xs, w1, w3, w2)


def _sc_mesh():
    return plsc.VectorSubcoreMesh(core_axis_name="c", subcore_axis_name="s")


def _sc_dispatch(rows, slots_t, n_slots):
    T, W = rows.shape
    K = slots_t.shape[0]
    win = SC_WINDOW

    @pl.kernel(out_type=jax.ShapeDtypeStruct((n_slots, W), rows.dtype), mesh=_sc_mesh(),
               scratch_types=[])
    def dispatch(x_hbm, i_hbm, o_hbm):
        def body(x_vmem, i_vmem):
            for k in range(K):
                pltpu.sync_copy(x_vmem, o_hbm.at[i_vmem.at[k]])

        pltpu.emit_pipeline(
            body,
            grid=(T // win,),
            in_specs=[pl.BlockSpec((win, W), lambda i: (i, 0)),
                      pl.BlockSpec((K, win), lambda i: (0, i))],
            out_specs=[],
            core_axis_name=("c", "s"),
            dimension_semantics=(pltpu.PARALLEL,),
        )(x_hbm, i_hbm)

    return dispatch(rows, slots_t)


def _sc_gather(table, idx):
    W = table.shape[1]
    N = idx.shape[1]
    win = SC_WINDOW

    @pl.kernel(out_type=jax.ShapeDtypeStruct((N, W), table.dtype), mesh=_sc_mesh(),
               scratch_types=[])
    def gather(x_hbm, i_hbm, o_hbm):
        def body(i_vmem, o_vmem):
            pltpu.sync_copy(x_hbm.at[i_vmem.at[0]], o_vmem)

        pltpu.emit_pipeline(
            body,
            grid=(N // win,),
            in_specs=[pl.BlockSpec((1, win), lambda i: (0, i))],
            out_specs=[pl.BlockSpec((win, W), lambda i: (i, 0))],
            core_axis_name=("c", "s"),
            dimension_semantics=(pltpu.PARALLEL,),
        )(i_hbm, o_hbm)

    return gather(table, idx)


def _final_kernel(x1_ref, h2_ref, yg_ref, gate_ref, g2_ref, ws1_ref, ws3_ref, ws2_ref,
                  lng_ref, lnb_ref, o_ref):
    Dh = D_MODEL // 2
    gate = gate_ref[...]
    routed_lo = jnp.zeros((x1_ref.shape[0], Dh), F32)
    routed_hi = jnp.zeros((x1_ref.shape[0], Dh), F32)
    for k in range(TOP_K):
        lo, hi = _load_packed(yg_ref.at[k])
        routed_lo = routed_lo + lo * gate[:, k:k + 1]
        routed_hi = routed_hi + hi * gate[:, k:k + 1]
    h2 = jnp.concatenate(_load_packed(h2_ref), axis=-1).astype(BF16)
    a = jnp.dot(h2, ws1_ref[...], preferred_element_type=F32)
    b = jnp.dot(h2, ws3_ref[...], preferred_element_type=F32)
    shared = jnp.dot((_silu(a) * b).astype(BF16), ws2_ref[...], preferred_element_type=F32)
    y2 = jnp.concatenate([routed_lo, routed_hi], axis=-1) + shared
    o_ref[...] = _layer_norm(DN_ALPHA * x1_ref[...] + g2_ref[0] * y2, lng_ref[...], lnb_ref[...])


def _final(x1, h2, yg, gate, g2, ws1, ws3, ws2, ln2g, ln2b, seq):
    T, D = x1.shape
    tm = FINAL_TM
    tiles_per_seq = seq // tm
    F = ws1.shape[-1]
    tok = pl.BlockSpec((tm, D), lambda i: (i, 0))
    row = pl.BlockSpec((1, D), lambda i: (0, 0))
    return pl.pallas_call(
        _final_kernel,
        out_shape=jax.ShapeDtypeStruct((T, D), F32),
        grid=(T // tm,),
        in_specs=[tok, pl.BlockSpec((2, tm, D // 4), lambda i: (0, i, 0)),
                  pl.BlockSpec((TOP_K, 2, tm, D // 4), lambda i: (0, 0, i, 0)),
                  pl.BlockSpec((tm, TOP_K), lambda i: (i, 0)),
                  pl.BlockSpec((1, 1, D), lambda i: (i // tiles_per_seq, 0, 0)),
                  pl.BlockSpec((D, F), lambda i: (0, 0)),
                  pl.BlockSpec((D, F), lambda i: (0, 0)),
                  pl.BlockSpec((F, D), lambda i: (0, 0)),
                  row, row],
        out_specs=tok,
        compiler_params=_cparams("arbitrary"),
        name="moe_combine_final",
    )(x1, h2, yg, gate, g2, ws1, ws3, ws2, ln2g, ln2b)


def kernel(x, c, ada_w, ada_b, w_in, b_in, lambda_q1, lambda_k1, lambda_q2, lambda_k2, subln_g, w_attn_o, conv_w, conv_b, conv_ln_g, conv_ln_b, conv_pw_w, conv_pw_b, w_out, ln1_g, ln1_b, router_w, router_bias, w1, w3, w2, ws1, ws3, ws2, ln2_g, ln2_b):
    B, S, D = x.shape
    T = B * S
    l = 0
    x2 = x.reshape(T, D)
    row = lambda a: a.reshape(1, -1)

    mod = _ada_modulation(c, ada_w[l], ada_b[l])
    sh1, sc1, g1, sh2, sc2, g2 = [m.reshape(B, 1, D) for m in jnp.split(mod, 6, axis=-1)]

    qt, k, vt, u, ga, gb = _in_projection(x2, sc1, sh1, w_in[l].astype(BF16), row(b_in[l]), S)
    o = _diff_attention(qt, k, vt, row(lambda_q1[l]), row(lambda_k1[l]), row(lambda_q2[l]),
                        row(lambda_k2[l]), row(subln_g[l]), B, S)

    wr = router_w[l].T
    wr_hi = wr.astype(BF16)
    wr_lo = (wr - wr_hi.astype(F32)).astype(BF16)
    x1, h2, logits = _mixer_epilogue(
        x2, o, u, ga, gb, g1, sc2, sh2, w_attn_o[l].astype(BF16), conv_w[l], row(conv_b[l]),
        row(conv_ln_g[l]), row(conv_ln_b[l]), conv_pw_w[l].astype(BF16), row(conv_pw_b[l]),
        w_out[l].astype(BF16), row(ln1_g[l]), row(ln1_b[l]), wr_hi, wr_lo, S)

    idx, gate, counts = _route(logits, router_bias[l])

    blk = EXPERT_BLK
    cnt = counts.reshape(N_EXPERTS).astype(jnp.int32)
    padded = (cnt + blk - 1) // blk * blk
    pad_end = jnp.cumsum(padded)
    pad_start = pad_end - padded
    P = T * TOP_K + N_EXPERTS * blk
    n_blocks = P // blk
    blk_first_row = jnp.arange(n_blocks, dtype=jnp.int32) * blk
    blk_expert = jnp.minimum(
        jnp.sum((pad_end[None, :] <= blk_first_row[:, None]).astype(jnp.int32), axis=1),
        N_EXPERTS - 1)
    n_used = (pad_end[-1:] // blk).astype(jnp.int32)

    pos = _assign_slots(idx, pad_start.astype(F32).reshape(N_EXPERTS, 1))

    Dq = D // 4
    half_base = (jnp.arange(2, dtype=jnp.int32) * P)[None, :, None]
    slots = pos[:, None, :] + half_base
    xs = _sc_dispatch(h2.reshape(2 * T, Dq), slots.reshape(TOP_K, 2 * T), 2 * P)
    ys = _routed_experts(xs.reshape(2, P, Dq), blk_expert, n_used, w1[l], w3[l], w2[l])
    yg = _sc_gather(ys.reshape(2 * P, Dq), slots.reshape(1, TOP_K * 2 * T))
    yg = yg.reshape(TOP_K, 2, T, Dq)

    out = _final(x1, h2, yg, gate.T, g2, ws1[l].astype(BF16), ws3[l].astype(BF16),
                 ws2[l].astype(BF16), row(ln2_g[l]), row(ln2_b[l]), S)
    return out.reshape(B, S, D)
```

```python
import math
from functools import partial

import jax
import jax.numpy as jnp
from jax import lax
from jax.experimental import pallas as pl
from jax.experimental.pallas import tpu as pltpu
from jax.experimental.pallas import tpu_sc as plsc

D_MODEL = 1024
N_HEADS = 8
HEAD_DIM = 64
V_DIM = 128
CONV_K = 31
N_EXPERTS = 256
TOP_K = 8
N_GROUPS = 8
GROUP_SIZE = N_EXPERTS // N_GROUPS
TOPK_GROUPS = 4
ROUTED_SCALE = 2.5
LN_EPS = 1e-5
RMS_EPS = 1e-5
DEPTH = 1
DN_ALPHA = (2.0 * DEPTH) ** 0.25
LAMBDA_INIT = 0.8 - 0.6 * math.exp(-0.3 * 0)
Q_SCALE = HEAD_DIM ** -0.5 * math.log2(math.e)

LANES = 128
SUBLANES = 8
VMEM_LIMIT_BYTES = 56 * 1024 * 1024

PROJ_TM = 512
ATTN_TQ = 512
ATTN_QG = 512
ATTN_KC = 256
ONES_ROWS = 16
MIX_TM = 512
MIX_SUB = 256
CONV_HALO = 32
CONV_ROWS = 64
ROUTE_TM = 512
EXPERT_BLK = 512
FINAL_TM = 512
SC_WINDOW = 128

F32 = jnp.float32
BF16 = jnp.bfloat16
U32 = jnp.uint32
NEG_INF = float("-inf")


def _cparams(*sem):
    return pltpu.CompilerParams(dimension_semantics=sem, vmem_limit_bytes=VMEM_LIMIT_BYTES)


def _sigmoid(x):
    return 1.0 / (1.0 + jnp.exp(-x))


def _silu(x):
    return x * _sigmoid(x)


BF16_BITS = 16
HIGH_HALF_WORD = 0xFFFF0000


def _pack_halves(x):
    half = x.shape[-1] // 2
    bits = lambda v: lax.bitcast_convert_type(v.astype(BF16).astype(F32), U32)
    return (bits(x[:, :half]) >> BF16_BITS) | (bits(x[:, half:]) & jnp.uint32(HIGH_HALF_WORD))


def _unpack_halves(w):
    lo = lax.bitcast_convert_type(w << BF16_BITS, F32)
    hi = lax.bitcast_convert_type(w & jnp.uint32(HIGH_HALF_WORD), F32)
    return lo, hi


def _store_packed(ref, x):
    w = _pack_halves(x)
    q = w.shape[-1] // 2
    ref[0] = w[:, :q]
    ref[1] = w[:, q:]


def _load_packed(ref):
    return _unpack_halves(jnp.concatenate([ref[0], ref[1]], axis=-1))


def _layer_norm(r, g, b):
    mu = jnp.mean(r, axis=-1, keepdims=True)
    d = r - mu
    var = jnp.mean(d * d, axis=-1, keepdims=True)
    return d * lax.rsqrt(var + LN_EPS) * g + b


def _ada_kernel(c_ref, w_ref, b_ref, o_ref):
    a = _silu(c_ref[...])
    o_ref[...] = jnp.dot(a, w_ref[...], preferred_element_type=F32,
                         precision=lax.Precision.HIGHEST) + b_ref[...]


def _ada_modulation(c, w, b):
    B, D = c.shape
    n = w.shape[1] // D
    return pl.pallas_call(
        _ada_kernel,
        out_shape=jax.ShapeDtypeStruct((B, n * D), F32),
        grid=(n,),
        in_specs=[pl.BlockSpec((B, D), lambda j: (0, 0)),
                  pl.BlockSpec((D, D), lambda j: (0, j)),
                  pl.BlockSpec((1, D), lambda j: (0, j))],
        out_specs=pl.BlockSpec((B, D), lambda j: (0, j)),
        compiler_params=_cparams("arbitrary"),
        name="ada_modulation",
    )(c, w, b.reshape(1, n * D))


def _inproj_kernel(x_ref, sc_ref, sh_ref, w_ref, b_ref,
                   qt_ref, k_ref, vt_ref, u_ref, ga_ref, gb_ref):
    D = D_MODEL
    h = (x_ref[...] * (1.0 + sc_ref[0]) + sh_ref[0]).astype(BF16)

    def proj(j):
        return (jnp.dot(h, w_ref[:, j * D:(j + 1) * D], preferred_element_type=F32)
                + b_ref[:, j * D:(j + 1) * D])

    qt_ref[...] = (proj(0) * Q_SCALE).T.astype(BF16)
    k_ref[...] = proj(1).astype(BF16)
    vt_ref[...] = proj(2).T.astype(BF16)
    u_ref[...] = (proj(3) * _sigmoid(proj(4))).astype(BF16)
    ga_ref[...] = _sigmoid(proj(5)).astype(BF16)
    gb_ref[...] = _sigmoid(proj(6)).astype(BF16)


def _in_projection(x2, sc1, sh1, w_in, b_in, seq):
    T, D = x2.shape
    tm = PROJ_TM
    tiles_per_seq = seq // tm
    ncols = w_in.shape[1]
    tok = pl.BlockSpec((tm, D), lambda i: (i, 0))
    mod = pl.BlockSpec((1, 1, D), lambda i: (i // tiles_per_seq, 0, 0))
    tok_t = pl.BlockSpec((D, tm), lambda i: (0, i))
    out = jax.ShapeDtypeStruct((T, D), BF16)
    out_t = jax.ShapeDtypeStruct((D, T), BF16)
    return pl.pallas_call(
        _inproj_kernel,
        out_shape=[out_t, out, out_t, out, out, out],
        grid=(T // tm,),
        in_specs=[tok, mod, mod,
                  pl.BlockSpec((D, ncols), lambda i: (0, 0)),
                  pl.BlockSpec((1, ncols), lambda i: (0, 0))],
        out_specs=[tok_t, tok, tok_t, tok, tok, tok],
        compiler_params=_cparams("arbitrary"),
        name="in_projection",
    )(x2, sc1, sh1, w_in, b_in)


def _attn_tasks(nq):
    tasks = [(qi, j, int(j == qi)) for qi in range(nq) for j in range(qi + 1)]
    if len(tasks) % 2:
        raise ValueError("the two-slot task loop needs an even number of tile pairs")
    qis, js, lasts = zip(*(tasks + [(0, 0, 0)]))
    as_i32 = lambda v: jnp.asarray(v, jnp.int32)
    return as_i32(qis), as_i32(js), as_i32(lasts), len(tasks)


def _attn_kernel(tqi_ref, tj_ref, tlast_ref, lq1_ref, lk1_ref, lq2_ref, lk2_ref, g_ref,
                 qt_ref, k_ref, vt_ref, o_ref, q2_ref, vx_ref, *scratch, n_tasks):
    tq = ATTN_TQ
    qg = ATTN_QG
    ng = 2 * tq // qg
    groups = [scratch[i * ng:(i + 1) * ng] for i in range(7)]
    s_slots, x_slots = groups[0:2], groups[2:4]
    m_refs, l_refs, acc_refs = groups[4:]
    seq = k_ref.shape[0]

    vx_ref[:V_DIM, :] = vt_ref[...]
    vx_ref[V_DIM:, :] = jnp.ones((ONES_ROWS, seq), BF16)

    lam = (jnp.exp(jnp.sum(lq1_ref[...] * lk1_ref[...], axis=-1, keepdims=True))
           - jnp.exp(jnp.sum(lq2_ref[...] * lk2_ref[...], axis=-1, keepdims=True))
           + LAMBDA_INIT)

    for i in range(seq // tq):
        qt = qt_ref[:, i * tq:(i + 1) * tq]
        feat = lax.broadcasted_iota(jnp.int32, qt.shape, 0)
        zero = jnp.zeros_like(qt)
        q2_ref[:, 2 * tq * i:2 * tq * i + tq] = jnp.where(feat < HEAD_DIM, qt, zero)
        q2_ref[:, 2 * tq * i + tq:2 * tq * (i + 1)] = jnp.where(feat >= HEAD_DIM, qt, zero)

    def scores(qi, j, slot, g):
        start = pl.multiple_of(j * tq, tq)
        col = pl.multiple_of(qi * (2 * tq) + g * qg, qg)
        s = jnp.dot(k_ref[pl.ds(start, tq), :], q2_ref[:, pl.ds(col, qg)],
                    preferred_element_type=F32)
        s_slots[slot][g][...] = s
        x_slots[slot][g][...] = jnp.max(s, axis=0, keepdims=True)

    def update(j, slot, g, masked):
        start = pl.multiple_of(j * tq, tq)
        s_ref = s_slots[slot][g]

        def load(k0):
            s = s_ref[k0:k0 + ATTN_KC, :]
            if masked:
                key = lax.broadcasted_iota(jnp.int32, s.shape, 0) + k0
                qry = lax.broadcasted_iota(jnp.int32, s.shape, 1) + (g * qg) % tq
                s = jnp.where(key <= qry, s, NEG_INF)
            return s

        m_prev = m_refs[g][...]
        if masked:
            tile_max = jnp.max(load(0), axis=0, keepdims=True)
            for k0 in range(ATTN_KC, tq, ATTN_KC):
                tile_max = jnp.maximum(tile_max, jnp.max(load(k0), axis=0, keepdims=True))
        else:
            tile_max = x_slots[slot][g][...]
        m_new = jnp.maximum(m_prev, tile_max)
        alpha = jnp.exp2(m_prev - m_new)
        m_refs[g][...] = m_new
        pv = None
        for k0 in range(0, tq, ATTN_KC):
            p = jnp.exp2((load(k0) - m_new).astype(BF16))
            keys = pl.ds(pl.multiple_of(start + k0, ATTN_KC), ATTN_KC)
            part = jnp.dot(vx_ref[:, keys], p, preferred_element_type=F32)
            pv = part if pv is None else pv + part
        acc_refs[g][...] = alpha * acc_refs[g][...] + pv[:V_DIM, :]
        l_refs[g][...] = alpha * l_refs[g][...] + pv[V_DIM:V_DIM + 1, :]

    def reset():
        for g in range(ng):
            m_refs[g][...] = jnp.full(m_refs[g].shape, NEG_INF, F32)
            l_refs[g][...] = jnp.zeros(l_refs[g].shape, F32)
            acc_refs[g][...] = jnp.zeros(acc_refs[g].shape, F32)

    def finish(qi):
        o = (jnp.concatenate([r[...] for r in acc_refs], axis=-1)
             / jnp.concatenate([r[...] for r in l_refs], axis=-1))
        o = o[:, :tq] - lam * o[:, tq:]
        o = o * lax.rsqrt(jnp.mean(o * o, axis=0, keepdims=True) + RMS_EPS) * g_ref[...]
        row = pl.multiple_of(qi * tq, tq)
        o_ref[pl.ds(row, tq), :] = (o * (1.0 - LAMBDA_INIT)).T.astype(BF16)

    def task(t, slot):
        qi, j = tqi_ref[t], tj_ref[t]
        nqi, nj = tqi_ref[t + 1], tj_ref[t + 1]

        def run(masked):
            for g in range(ng):
                scores(nqi, nj, 1 - slot, g)
                update(j, slot, g, masked)

        @pl.when(tlast_ref[t] == 0)
        def _():
            run(False)

        @pl.when(tlast_ref[t] == 1)
        def _():
            run(True)
            finish(qi)
            reset()

    reset()
    for g in range(ng):
        scores(0, 0, 0, g)

    def pair(i, carry):
        task(2 * i, 0)
        task(2 * i + 1, 1)
        return carry

    lax.fori_loop(0, n_tasks // 2, pair, 0)


def _diff_attention(qt, k, vt, lq1, lk1, lq2, lk2, subln_g, batch, seq):
    T, D = k.shape
    tq = ATTN_TQ
    nq = seq // tq
    tqi, tj, tlast, n_tasks = _attn_tasks(nq)
    lam_spec = pl.BlockSpec((1, HEAD_DIM), lambda b, h, *_: (0, 0))
    rows_spec = pl.BlockSpec((seq, V_DIM), lambda b, h, *_: (b, h))
    cols_spec = pl.BlockSpec((V_DIM, seq), lambda b, h, *_: (h, b))
    qg = ATTN_QG
    per_group = lambda shape, dtype: [pltpu.VMEM(shape, dtype) for _ in range(2 * tq // qg)]
    grid_spec = pltpu.PrefetchScalarGridSpec(
        num_scalar_prefetch=3,
        grid=(batch, N_HEADS),
        in_specs=[lam_spec, lam_spec, lam_spec, lam_spec,
                  pl.BlockSpec((V_DIM, 1), lambda b, h, *_: (0, 0)),
                  cols_spec, rows_spec, cols_spec],
        out_specs=rows_spec,
        scratch_shapes=(
            [pltpu.VMEM((V_DIM, 2 * seq), BF16),
             pltpu.VMEM((V_DIM + ONES_ROWS, seq), BF16)]
            + per_group((tq, qg), F32)
            + per_group((tq, qg), F32)
            + per_group((1, qg), F32)
            + per_group((1, qg), F32)
            + per_group((1, qg), F32)
            + per_group((1, qg), F32)
            + per_group((V_DIM, qg), F32)),
    )
    return pl.pallas_call(
        partial(_attn_kernel, n_tasks=n_tasks),
        out_shape=jax.ShapeDtypeStruct((T, D), BF16),
        grid_spec=grid_spec,
        compiler_params=_cparams("arbitrary", "arbitrary"),
        name="diff_attention",
    )(tqi, tj, tlast, lq1, lk1, lq2, lk2, subln_g.reshape(V_DIM, 1), qt, k, vt)


def _mixer_kernel(x_ref, o_ref, u_ref, up_ref, ga_ref, gb_ref, g1_ref, sc2_ref, sh2_ref,
                  wao_ref, cw_ref, cb_ref, clg_ref, clb_ref, wpw_ref, bpw_ref, wout_ref,
                  ln1g_ref, ln1b_ref, wrh_ref, wrl_ref,
                  x1_ref, h2_ref, lg_ref, ubuf_ref, ush_ref, conv_ref, *, tiles_per_seq):
    tm = MIX_TM
    first = (pl.program_id(0) % tiles_per_seq) == 0

    prev = up_ref[...].astype(F32)
    ubuf_ref[0:CONV_HALO, :] = jnp.where(first, jnp.zeros_like(prev), prev)
    ubuf_ref[CONV_HALO:, :] = u_ref[...].astype(F32)
    for t0 in range(0, tm, MIX_SUB):
        rows = slice(t0, t0 + MIX_SUB)

        base = CONV_HALO - (CONV_K - 1)
        n_shift = MIX_SUB + CONV_HALO - SUBLANES
        for r in range(1, SUBLANES):
            ush_ref[r - 1] = ubuf_ref[t0 + r:t0 + r + n_shift, :]
        for c in range(D_MODEL // LANES):
            cl = slice(c * LANES, (c + 1) * LANES)
            for r0 in range(0, MIX_SUB, CONV_ROWS):
                acc = jnp.zeros((CONV_ROWS, LANES), F32)
                for j in range(CONV_K):
                    a, r = divmod(base + j, SUBLANES)
                    lo = SUBLANES * a + r0
                    win = (ubuf_ref[t0 + lo:t0 + lo + CONV_ROWS, cl] if r == 0
                           else ush_ref[r - 1, lo:lo + CONV_ROWS, cl])
                    acc = acc + cw_ref[j:j + 1, cl] * win
                conv_ref[t0 + r0:t0 + r0 + CONV_ROWS, cl] = acc
        yc = _silu(_layer_norm(conv_ref[rows, :] + cb_ref[...], clg_ref[...], clb_ref[...]))
        y_conv = (jnp.dot(yc.astype(BF16), wpw_ref[...], preferred_element_type=F32)
                  + bpw_ref[...])

        y_attn = jnp.dot(o_ref[rows, :], wao_ref[...], preferred_element_type=F32)
        mix = ga_ref[rows, :].astype(F32) * y_attn + gb_ref[rows, :].astype(F32) * y_conv
        y = jnp.dot(mix.astype(BF16), wout_ref[...], preferred_element_type=F32)
        x1 = _layer_norm(DN_ALPHA * x_ref[rows, :] + g1_ref[0] * y, ln1g_ref[...], ln1b_ref[...])
        x1_ref[rows, :] = x1

        h2 = x1 * (1.0 + sc2_ref[0]) + sh2_ref[0]
        h2_hi = h2.astype(BF16)
        _store_packed(h2_ref.at[:, rows, :], h2)
        h2_lo = (h2 - h2_hi.astype(F32)).astype(BF16)
        nt = lambda w, a: lax.dot_general(w, a, (((1,), (1,)), ((), ())),
                                          preferred_element_type=F32)
        lg_ref[:, rows] = (nt(wrh_ref[...], h2_hi)
                           + (nt(wrl_ref[...], h2_hi) + nt(wrh_ref[...], h2_lo)))


def _mixer_epilogue(x2, o, u, ga, gb, g1, sc2, sh2, wao, cw, cb, clg, clb, wpw, bpw, wout,
                    ln1g, ln1b, wr_hi, wr_lo, seq):
    T, D = x2.shape
    tm = MIX_TM
    tiles_per_seq = seq // tm
    halo_per_tile = tm // CONV_HALO
    tok = pl.BlockSpec((tm, D), lambda i: (i, 0))
    prev = pl.BlockSpec((CONV_HALO, D), lambda i: (jnp.maximum(i * halo_per_tile - 1, 0), 0))
    mod = pl.BlockSpec((1, 1, D), lambda i: (i // tiles_per_seq, 0, 0))
    row = pl.BlockSpec((1, D), lambda i: (0, 0))
    mat = pl.BlockSpec((D, D), lambda i: (0, 0), pipeline_mode=pl.Buffered(1))
    rw = pl.BlockSpec((N_EXPERTS, D), lambda i: (0, 0), pipeline_mode=pl.Buffered(1))
    return pl.pallas_call(
        partial(_mixer_kernel, tiles_per_seq=tiles_per_seq),
        out_shape=[jax.ShapeDtypeStruct((T, D), F32),
                   jax.ShapeDtypeStruct((2, T, D // 4), U32),
                   jax.ShapeDtypeStruct((N_EXPERTS, T), F32)],
        grid=(T // tm,),
        in_specs=[tok, tok, tok, prev, tok, tok, mod, mod, mod,
                  mat, pl.BlockSpec((CONV_K, D), lambda i: (0, 0)), row, row, row, mat, row, mat,
                  row, row, rw, rw],
        out_specs=[tok, pl.BlockSpec((2, tm, D // 4), lambda i: (0, i, 0)),
                   pl.BlockSpec((N_EXPERTS, tm), lambda i: (0, i))],
        scratch_shapes=[pltpu.VMEM((CONV_HALO + tm, D), F32),
                        pltpu.VMEM((SUBLANES - 1, MIX_SUB + CONV_HALO - SUBLANES, D), F32),
                        pltpu.VMEM((tm, D), F32)],
        compiler_params=_cparams("arbitrary"),
        name="mixer_epilogue",
    )(x2, o, u, u, ga, gb, g1, sc2, sh2, wao, cw, cb, clg, clb, wpw, bpw, wout,
      ln1g, ln1b, wr_hi, wr_lo)


def _first_argmax(v, eid):
    m = jnp.max(v, axis=0, keepdims=True)
    i = jnp.min(jnp.where(v == m, eid, float(N_EXPERTS)), axis=0, keepdims=True)
    return m, i


def _route_kernel(lg_ref, rb_ref, idx_ref, gate_ref, cnt_ref):
    tm = ROUTE_TM
    scores = _sigmoid(lg_ref[...])
    choice = scores + rb_ref[...]
    eid = lax.broadcasted_iota(jnp.int32, (N_EXPERTS, tm), 0).astype(F32)
    rows = lambda a, g: a[g * GROUP_SIZE:(g + 1) * GROUP_SIZE, :]

    gs = []
    for g in range(N_GROUPS):
        cg = rows(choice, g)
        eg = (lax.broadcasted_iota(jnp.int32, cg.shape, 0) + g * GROUP_SIZE).astype(F32)
        m1, i1 = _first_argmax(cg, eg)
        m2 = jnp.max(jnp.where(eg == i1, NEG_INF, cg), axis=0, keepdims=True)
        gs.append(jnp.broadcast_to(m1 + m2, (SUBLANES, tm)))

    kept = []
    for g in range(N_GROUPS):
        ahead = jnp.zeros((SUBLANES, tm), F32)
        for o in range(N_GROUPS):
            if o == g:
                continue
            beats = (gs[o] >= gs[g]) if o < g else (gs[o] > gs[g])
            ahead = ahead + jnp.where(beats, 1.0, 0.0)
        cap = jnp.where(ahead < TOPK_GROUPS, float("inf"), NEG_INF)
        for r in range(g * GROUP_SIZE, (g + 1) * GROUP_SIZE, SUBLANES):
            kept.append(jnp.minimum(choice[r:r + SUBLANES, :], cap))
    masked = jnp.concatenate(kept, axis=0)

    sel = jnp.zeros((N_EXPERTS, tm), F32)
    total = jnp.zeros((1, tm), F32)
    picked = []
    for k in range(TOP_K):
        _, i = _first_argmax(masked, eid)
        hit = eid == i
        sc = jnp.sum(jnp.where(hit, scores, 0.0), axis=0, keepdims=True)
        masked = jnp.where(hit, NEG_INF, masked)
        sel = sel + hit.astype(F32)
        total = total + sc
        idx_ref[k:k + 1, :] = i.astype(jnp.int32)
        picked.append(sc)
    for k in range(TOP_K):
        gate_ref[k:k + 1, :] = picked[k] / total * ROUTED_SCALE

    @pl.when(pl.program_id(0) == 0)
    def _():
        cnt_ref[...] = jnp.zeros_like(cnt_ref)
    cnt_ref[...] += jnp.sum(sel, axis=1, keepdims=True)


def _route(logits_t, router_bias):
    T = logits_t.shape[1]
    tm = ROUTE_TM
    return pl.pallas_call(
        _route_kernel,
        out_shape=[jax.ShapeDtypeStruct((TOP_K, T), jnp.int32),
                   jax.ShapeDtypeStruct((TOP_K, T), F32),
                   jax.ShapeDtypeStruct((N_EXPERTS, 1), F32)],
        grid=(T // tm,),
        in_specs=[pl.BlockSpec((N_EXPERTS, tm), lambda i: (0, i)),
                  pl.BlockSpec((N_EXPERTS, 1), lambda i: (0, 0))],
        out_specs=[pl.BlockSpec((TOP_K, tm), lambda i: (0, i)),
                   pl.BlockSpec((TOP_K, tm), lambda i: (0, i)),
                   pl.BlockSpec((N_EXPERTS, 1), lambda i: (0, 0))],
        compiler_params=_cparams("arbitrary"),
        name="route_topk",
    )(logits_t, router_bias.reshape(N_EXPERTS, 1))


def _slot_kernel(idx_ref, start_ref, pos_ref, carry_ref):
    tm = ROUTE_TM

    @pl.when(pl.program_id(0) == 0)
    def _():
        carry_ref[...] = jnp.zeros_like(carry_ref)

    idx = idx_ref[...]
    eid = lax.broadcasted_iota(jnp.int32, (N_EXPERTS, tm), 0)
    hits = [eid == idx[k:k + 1, :] for k in range(TOP_K)]
    sel = jnp.zeros((N_EXPERTS, tm), F32)
    for h in hits:
        sel = sel + h.astype(F32)
    r = lax.broadcasted_iota(jnp.int32, (tm, tm), 0)
    c = lax.broadcasted_iota(jnp.int32, (tm, tm), 1)
    earlier = (r < c).astype(BF16)
    rank = jnp.dot(sel.astype(BF16), earlier, preferred_element_type=F32)
    slot = rank + carry_ref[...] + start_ref[...]
    for k, h in enumerate(hits):
        pos_ref[k:k + 1, :] = jnp.sum(jnp.where(h, slot, 0.0), axis=0,
                                      keepdims=True).astype(jnp.int32)
    carry_ref[...] += jnp.sum(sel, axis=1, keepdims=True)


def _assign_slots(idx_t, pad_start):
    T = idx_t.shape[1]
    tm = ROUTE_TM
    return pl.pallas_call(
        _slot_kernel,
        out_shape=jax.ShapeDtypeStruct((TOP_K, T), jnp.int32),
        grid=(T // tm,),
        in_specs=[pl.BlockSpec((TOP_K, tm), lambda i: (0, i)),
                  pl.BlockSpec((N_EXPERTS, 1), lambda i: (0, 0))],
        out_specs=pl.BlockSpec((TOP_K, tm), lambda i: (0, i)),
        scratch_shapes=[pltpu.VMEM((N_EXPERTS, 1), F32)],
        compiler_params=_cparams("arbitrary"),
        name="assign_slots",
    )(idx_t, pad_start)


def _expert_kernel(be_ref, nu_ref, first_ref, slot_ref, nxt_ref, xs_ref, w1_hbm, w3_hbm, w2_hbm,
                   ys_ref, w1_buf, w3_buf, w2_buf, sem):
    b = pl.program_id(0)

    def weight_copies(expert, slot):
        pairs = ((w1_hbm, w1_buf), (w3_hbm, w3_buf), (w2_hbm, w2_buf))
        return [pltpu.make_async_copy(hbm.at[expert], buf.at[slot], sem.at[slot, i])
                for i, (hbm, buf) in enumerate(pairs)]

    @pl.when(b < nu_ref[0])
    def _():
        slot = slot_ref[b]

        @pl.when(b == 0)
        def _():
            for c in weight_copies(be_ref[0], 0):
                c.start()

        @pl.when(first_ref[b] == 1)
        def _():
            for c in weight_copies(be_ref[b], slot):
                c.wait()

            @pl.when(nxt_ref[b] >= 0)
            def _():
                for c in weight_copies(nxt_ref[b], 1 - slot):
                    c.start()

        lo, hi = _load_packed(xs_ref)
        half = lo.shape[-1]
        mm = lambda w: (jnp.dot(lo, w[slot, :half, :], preferred_element_type=F32)
                        + jnp.dot(hi, w[slot, half:, :], preferred_element_type=F32))
        hb = _silu(mm(w1_buf)) * mm(w3_buf)
        _store_packed(ys_ref, jnp.dot(hb, w2_buf[slot], preferred_element_type=F32))

    @pl.when(b >= nu_ref[0])
    def _():
        ys_ref[...] = jnp.zeros_like(ys_ref)


def _routed_experts(xs, blk_expert, n_used, w1, w3, w2):
    _, P, Dq = xs.shape
    blk = EXPERT_BLK
    n_blocks = P // blk
    D, F = w1.shape[-2:]

    b_ids = jnp.arange(n_blocks, dtype=jnp.int32)
    used = b_ids < n_used[0]
    prev = jnp.concatenate([jnp.full((1,), -1, jnp.int32), blk_expert[:-1]])
    first = (used & (blk_expert != prev)).astype(jnp.int32)
    slot = (jnp.cumsum(first) - 1) % 2
    start_at = jnp.where(first == 1, b_ids, n_blocks)
    later = jnp.concatenate([start_at[1:], jnp.full((1,), n_blocks, jnp.int32)])
    next_start = lax.cummin(later, axis=0, reverse=True)
    nxt = jnp.where(next_start < n_blocks,
                    blk_expert[jnp.minimum(next_start, n_blocks - 1)], -1).astype(jnp.int32)

    row_blk = lambda b, be, nu, *_: (0, jnp.minimum(b, nu[0] - 1), 0)
    grid_spec = pltpu.PrefetchScalarGridSpec(
        num_scalar_prefetch=5,
        grid=(n_blocks,),
        in_specs=[pl.BlockSpec((2, blk, Dq), row_blk),
                  pl.BlockSpec(memory_space=pl.ANY),
                  pl.BlockSpec(memory_space=pl.ANY),
                  pl.BlockSpec(memory_space=pl.ANY)],
        out_specs=pl.BlockSpec((2, blk, Dq), lambda b, *_: (0, b, 0)),
        scratch_shapes=[pltpu.VMEM((2, D, F), F32), pltpu.VMEM((2, D, F), F32),
                        pltpu.VMEM((2, F, D), F32), pltpu.SemaphoreType.DMA((2, 3))],
    )
    return pl.pallas_call(
        _expert_kernel,
        out_shape=jax.ShapeDtypeStruct((2, P, Dq), U32),
        grid_spec=grid_spec,
        compiler_params=_cparams("arbitrary"),
        name="routed_experts",
    )(blk_expert, n_used, first, slot.astype(jnp.int32), nxt, xs, w1, w3, w2)


def _sc_mesh():
    return plsc.VectorSubcoreMesh(core_axis_name="c", subcore_axis_name="s")


def _sc_dispatch(rows, slots_t, n_slots):
    T, W = rows.shape
    K = slots_t.shape[0]
    win = SC_WINDOW

    @pl.kernel(out_type=jax.ShapeDtypeStruct((n_slots, W), rows.dtype), mesh=_sc_mesh(),
               scratch_types=[])
    def dispatch(x_hbm, i_hbm, o_hbm):
        def body(x_vmem, i_vmem):
            for k in range(K):
                pltpu.sync_copy(x_vmem, o_hbm.at[i_vmem.at[k]])

        pltpu.emit_pipeline(
            body,
            grid=(T // win,),
            in_specs=[pl.BlockSpec((win, W), lambda i: (i, 0)),
                      pl.BlockSpec((K, win), lambda i: (0, i))],
            out_specs=[],
            core_axis_name=("c", "s"),
            dimension_semantics=(pltpu.PARALLEL,),
        )(x_hbm, i_hbm)

    return dispatch(rows, slots_t)


def _sc_gather(table, idx):
    W = table.shape[1]
    N = idx.shape[1]
    win = SC_WINDOW

    @pl.kernel(out_type=jax.ShapeDtypeStruct((N, W), table.dtype), mesh=_sc_mesh(),
               scratch_types=[])
    def gather(x_hbm, i_hbm, o_hbm):
        def body(i_vmem, o_vmem):
            pltpu.sync_copy(x_hbm.at[i_vmem.at[0]], o_vmem)

        pltpu.emit_pipeline(
            body,
            grid=(N // win,),
            in_specs=[pl.BlockSpec((1, win), lambda i: (0, i))],
            out_specs=[pl.BlockSpec((win, W), lambda i: (i, 0))],
            core_axis_name=("c", "s"),
            dimension_semantics=(pltpu.PARALLEL,),
        )(i_hbm, o_hbm)

    return gather(table, idx)


def _final_kernel(x1_ref, h2_ref, yg_ref, gate_ref, g2_ref, ws1_ref, ws3_ref, ws2_ref,
                  lng_ref, lnb_ref, o_ref):
    Dh = D_MODEL // 2
    gate = gate_ref[...]
    routed_lo = jnp.zeros((x1_ref.shape[0], Dh), F32)
    routed_hi = jnp.zeros((x1_ref.shape[0], Dh), F32)
    for k in range(TOP_K):
        lo, hi = _load_packed(yg_ref.at[k])
        routed_lo = routed_lo + lo * gate[:, k:k + 1]
        routed_hi = routed_hi + hi * gate[:, k:k + 1]
    h2 = jnp.concatenate(_load_packed(h2_ref), axis=-1).astype(BF16)
    a = jnp.dot(h2, ws1_ref[...], preferred_element_type=F32)
    b = jnp.dot(h2, ws3_ref[...], preferred_element_type=F32)
    shared = jnp.dot((_silu(a) * b).astype(BF16), ws2_ref[...], preferred_element_type=F32)
    y2 = jnp.concatenate([routed_lo, routed_hi], axis=-1) + shared
    o_ref[...] = _layer_norm(DN_ALPHA * x1_ref[...] + g2_ref[0] * y2, lng_ref[...], lnb_ref[...])


def _final(x1, h2, yg, gate, g2, ws1, ws3, ws2, ln2g, ln2b, seq):
    T, D = x1.shape
    tm = FINAL_TM
    tiles_per_seq = seq // tm
    F = ws1.shape[-1]
    tok = pl.BlockSpec((tm, D), lambda i: (i, 0))
    row = pl.BlockSpec((1, D), lambda i: (0, 0))
    return pl.pallas_call(
        _final_kernel,
        out_shape=jax.ShapeDtypeStruct((T, D), F32),
        grid=(T // tm,),
        in_specs=[tok, pl.BlockSpec((2, tm, D // 4), lambda i: (0, i, 0)),
                  pl.BlockSpec((TOP_K, 2, tm, D // 4), lambda i: (0, 0, i, 0)),
                  pl.BlockSpec((tm, TOP_K), lambda i: (i, 0)),
                  pl.BlockSpec((1, 1, D), lambda i: (i // tiles_per_seq, 0, 0)),
                  pl.BlockSpec((D, F), lambda i: (0, 0)),
                  pl.BlockSpec((D, F), lambda i: (0, 0)),
                  pl.BlockSpec((F, D), lambda i: (0, 0)),
                  row, row],
        out_specs=tok,
        compiler_params=_cparams("arbitrary"),
        name="moe_combine_final",
    )(x1, h2, yg, gate, g2, ws1, ws3, ws2, ln2g, ln2b)


def kernel(x, c, ada_w, ada_b, w_in, b_in, lambda_q1, lambda_k1, lambda_q2, lambda_k2, subln_g, w_attn_o, conv_w, conv_b, conv_ln_g, conv_ln_b, conv_pw_w, conv_pw_b, w_out, ln1_g, ln1_b, router_w, router_bias, w1, w3, w2, ws1, ws3, ws2, ln2_g, ln2_b):
    B, S, D = x.shape
    T = B * S
    l = 0
    x2 = x.reshape(T, D)
    row = lambda a: a.reshape(1, -1)

    mod = _ada_modulation(c, ada_w[l], ada_b[l])
    sh1, sc1, g1, sh2, sc2, g2 = [m.reshape(B, 1, D) for m in jnp.split(mod, 6, axis=-1)]

    qt, k, vt, u, ga, gb = _in_projection(x2, sc1, sh1, w_in[l].astype(BF16), row(b_in[l]), S)
    o = _diff_attention(qt, k, vt, row(lambda_q1[l]), row(lambda_k1[l]), row(lambda_q2[l]),
                        row(lambda_k2[l]), row(subln_g[l]), B, S)

    wr = router_w[l].T
    wr_hi = wr.astype(BF16)
    wr_lo = (wr - wr_hi.astype(F32)).astype(BF16)
    x1, h2, logits = _mixer_epilogue(
        x2, o, u, ga, gb, g1, sc2, sh2, w_attn_o[l].astype(BF16), conv_w[l], row(conv_b[l]),
        row(conv_ln_g[l]), row(conv_ln_b[l]), conv_pw_w[l].astype(BF16), row(conv_pw_b[l]),
        w_out[l].astype(BF16), row(ln1_g[l]), row(ln1_b[l]), wr_hi, wr_lo, S)

    idx, gate, counts = _route(logits, router_bias[l])

    blk = EXPERT_BLK
    cnt = counts.reshape(N_EXPERTS).astype(jnp.int32)
    padded = (cnt + blk - 1) // blk * blk
    pad_end = jnp.cumsum(padded)
    pad_start = pad_end - padded
    P = T * TOP_K + N_EXPERTS * blk
    n_blocks = P // blk
    blk_first_row = jnp.arange(n_blocks, dtype=jnp.int32) * blk
    blk_expert = jnp.minimum(
        jnp.sum((pad_end[None, :] <= blk_first_row[:, None]).astype(jnp.int32), axis=1),
        N_EXPERTS - 1)
    n_used = (pad_end[-1:] // blk).astype(jnp.int32)

    pos = _assign_slots(idx, pad_start.astype(F32).reshape(N_EXPERTS, 1))

    Dq = D // 4
    half_base = (jnp.arange(2, dtype=jnp.int32) * P)[None, :, None]
    slots = pos[:, None, :] + half_base
    xs = _sc_dispatch(h2.reshape(2 * T, Dq), slots.reshape(TOP_K, 2 * T), 2 * P)
    ys = _routed_experts(xs.reshape(2, P, Dq), blk_expert, n_used, w1[l], w3[l], w2[l])
    yg = _sc_gather(ys.reshape(2 * P, Dq), slots.reshape(1, TOP_K * 2 * T))
    yg = yg.reshape(TOP_K, 2, T, Dq)

    out = _final(x1, h2, yg, gate.T, g2, ws1[l].astype(BF16), ws3[l].astype(BF16),
                 ws2[l].astype(BF16), row(ln2_g[l]), row(ln2_b[l]), S)
    return out.reshape(B, S, D)
```

```python
import math
from functools import partial

import jax
import jax.numpy as jnp
from jax import lax
from jax.experimental import pallas as pl
from jax.experimental.pallas import tpu as pltpu
from jax.experimental.pallas import tpu_sc as plsc

D_MODEL = 1024
N_HEADS = 8
HEAD_DIM = 64
V_DIM = 128
CONV_K = 31
N_EXPERTS = 256
TOP_K = 8
N_GROUPS = 8
GROUP_SIZE = N_EXPERTS // N_GROUPS
TOPK_GROUPS = 4
ROUTED_SCALE = 2.5
LN_EPS = 1e-5
RMS_EPS = 1e-5
DEPTH = 1
DN_ALPHA = (2.0 * DEPTH) ** 0.25
LAMBDA_INIT = 0.8 - 0.6 * math.exp(-0.3 * 0)
Q_SCALE = HEAD_DIM ** -0.5 * math.log2(math.e)

LANES = 128
SUBLANES = 8
VMEM_LIMIT_BYTES = 56 * 1024 * 1024

PROJ_TM = 512
ATTN_TQ = 512
ATTN_QG = 512
ATTN_KC = 256
ONES_ROWS = 16
MIX_TM = 512
MIX_SUB = 256
CONV_HALO = 32
CONV_ROWS = 64
ROUTE_TM = 512
EXPERT_BLK = 512
FINAL_TM = 512
SC_WINDOW = 128

F32 = jnp.float32
BF16 = jnp.bfloat16
U32 = jnp.uint32
NEG_INF = float("-inf")


def _cparams(*sem):
    return pltpu.CompilerParams(dimension_semantics=sem, vmem_limit_bytes=VMEM_LIMIT_BYTES)


def _sigmoid(x):
    return 1.0 / (1.0 + jnp.exp(-x))


def _silu(x):
    return x * _sigmoid(x)


BF16_BITS = 16
HIGH_HALF_WORD = 0xFFFF0000


def _pack_halves(x):
    half = x.shape[-1] // 2
    bits = lambda v: lax.bitcast_convert_type(v.astype(BF16).astype(F32), U32)
    return (bits(x[:, :half]) >> BF16_BITS) | (bits(x[:, half:]) & jnp.uint32(HIGH_HALF_WORD))


def _unpack_halves(w):
    lo = lax.bitcast_convert_type(w << BF16_BITS, F32)
    hi = lax.bitcast_convert_type(w & jnp.uint32(HIGH_HALF_WORD), F32)
    return lo, hi


def _store_packed(ref, x):
    w = _pack_halves(x)
    q = w.shape[-1] // 2
    ref[0] = w[:, :q]
    ref[1] = w[:, q:]


def _load_packed(ref):
    return _unpack_halves(jnp.concatenate([ref[0], ref[1]], axis=-1))


def _layer_norm(r, g, b):
    mu = jnp.mean(r, axis=-1, keepdims=True)
    d = r - mu
    var = jnp.mean(d * d, axis=-1, keepdims=True)
    return d * lax.rsqrt(var + LN_EPS) * g + b


def _ada_kernel(c_ref, w_ref, b_ref, o_ref):
    a = _silu(c_ref[...])
    o_ref[...] = jnp.dot(a, w_ref[...], preferred_element_type=F32,
                         precision=lax.Precision.HIGHEST) + b_ref[...]


def _ada_modulation(c, w, b):
    B, D = c.shape
    n = w.shape[1] // D
    return pl.pallas_call(
        _ada_kernel,
        out_shape=jax.ShapeDtypeStruct((B, n * D), F32),
        grid=(n,),
        in_specs=[pl.BlockSpec((B, D), lambda j: (0, 0)),
                  pl.BlockSpec((D, D), lambda j: (0, j)),
                  pl.BlockSpec((1, D), lambda j: (0, j))],
        out_specs=pl.BlockSpec((B, D), lambda j: (0, j)),
        compiler_params=_cparams("arbitrary"),
        name="ada_modulation",
    )(c, w, b.reshape(1, n * D))


def _inproj_kernel(x_ref, sc_ref, sh_ref, w_ref, b_ref,
                   qt_ref, k_ref, vt_ref, u_ref, ga_ref, gb_ref):
    D = D_MODEL
    h = (x_ref[...] * (1.0 + sc_ref[0]) + sh_ref[0]).astype(BF16)

    def proj(j):
        return (jnp.dot(h, w_ref[:, j * D:(j + 1) * D], preferred_element_type=F32)
                + b_ref[:, j * D:(j + 1) * D])

    qt_ref[...] = (proj(0) * Q_SCALE).T.astype(BF16)
    k_ref[...] = proj(1).astype(BF16)
    vt_ref[...] = proj(2).T.astype(BF16)
    u_ref[...] = (proj(3) * _sigmoid(proj(4))).astype(BF16)
    ga_ref[...] = _sigmoid(proj(5)).astype(BF16)
    gb_ref[...] = _sigmoid(proj(6)).astype(BF16)


def _in_projection(x2, sc1, sh1, w_in, b_in, seq):
    T, D = x2.shape
    tm = PROJ_TM
    tiles_per_seq = seq // tm
    ncols = w_in.shape[1]
    tok = pl.BlockSpec((tm, D), lambda i: (i, 0))
    mod = pl.BlockSpec((1, 1, D), lambda i: (i // tiles_per_seq, 0, 0))
    tok_t = pl.BlockSpec((D, tm), lambda i: (0, i))
    out = jax.ShapeDtypeStruct((T, D), BF16)
    out_t = jax.ShapeDtypeStruct((D, T), BF16)
    return pl.pallas_call(
        _inproj_kernel,
        out_shape=[out_t, out, out_t, out, out, out],
        grid=(T // tm,),
        in_specs=[tok, mod, mod,
                  pl.BlockSpec((D, ncols), lambda i: (0, 0)),
                  pl.BlockSpec((1, ncols), lambda i: (0, 0))],
        out_specs=[tok_t, tok, tok_t, tok, tok, tok],
        compiler_params=_cparams("arbitrary"),
        name="in_projection",
    )(x2, sc1, sh1, w_in, b_in)


def _attn_tasks(nq):
    tasks = [(qi, j, int(j == qi)) for qi in range(nq) for j in range(qi + 1)]
    if len(tasks) % 2:
        raise ValueError("the two-slot task loop needs an even number of tile pairs")
    qis, js, lasts = zip(*(tasks + [(0, 0, 0)]))
    as_i32 = lambda v: jnp.asarray(v, jnp.int32)
    return as_i32(qis), as_i32(js), as_i32(lasts), len(tasks)


def _attn_kernel(tqi_ref, tj_ref, tlast_ref, lq1_ref, lk1_ref, lq2_ref, lk2_ref, g_ref,
                 qt_ref, k_ref, vt_ref, o_ref, q2_ref, vx_ref, *scratch, n_tasks):
    tq = ATTN_TQ
    qg = ATTN_QG
    if tq % qg:
        raise ValueError("a query group must not span the two softmax maps of a tile")
    ng = 2 * tq // qg
    groups = [scratch[i * ng:(i + 1) * ng] for i in range(7)]
    s_slots, x_slots = groups[0:2], groups[2:4]
    m_refs, l_refs, acc_refs = groups[4:]
    seq = k_ref.shape[0]

    vx_ref[:V_DIM, :] = vt_ref[...]
    vx_ref[V_DIM:, :] = jnp.ones((ONES_ROWS, seq), BF16)

    lam = (jnp.exp(jnp.sum(lq1_ref[...] * lk1_ref[...], axis=-1, keepdims=True))
           - jnp.exp(jnp.sum(lq2_ref[...] * lk2_ref[...], axis=-1, keepdims=True))
           + LAMBDA_INIT)

    for i in range(seq // tq):
        qt = qt_ref[:, i * tq:(i + 1) * tq]
        feat = lax.broadcasted_iota(jnp.int32, qt.shape, 0)
        zero = jnp.zeros_like(qt)
        q2_ref[:, 2 * tq * i:2 * tq * i + tq] = jnp.where(feat < HEAD_DIM, qt, zero)
        q2_ref[:, 2 * tq * i + tq:2 * tq * (i + 1)] = jnp.where(feat >= HEAD_DIM, qt, zero)

    def scores(qi, j, slot, g):
        start = pl.multiple_of(j * tq, tq)
        col = pl.multiple_of(qi * (2 * tq) + g * qg, qg)
        s = jnp.dot(k_ref[pl.ds(start, tq), :], q2_ref[:, pl.ds(col, qg)],
                    preferred_element_type=F32)
        s_slots[slot][g][...] = s
        x_slots[slot][g][...] = jnp.max(s, axis=0, keepdims=True)

    def update(j, slot, g, masked):
        start = pl.multiple_of(j * tq, tq)
        s_ref = s_slots[slot][g]

        def load(k0):
            s = s_ref[k0:k0 + ATTN_KC, :]
            if masked:
                key = lax.broadcasted_iota(jnp.int32, s.shape, 0) + k0
                qry = lax.broadcasted_iota(jnp.int32, s.shape, 1) + (g * qg) % tq
                s = jnp.where(key <= qry, s, NEG_INF)
            return s

        m_prev = m_refs[g][...]
        if masked:
            tile_max = jnp.max(load(0), axis=0, keepdims=True)
            for k0 in range(ATTN_KC, tq, ATTN_KC):
                tile_max = jnp.maximum(tile_max, jnp.max(load(k0), axis=0, keepdims=True))
        else:
            tile_max = x_slots[slot][g][...]
        m_new = jnp.maximum(m_prev, tile_max)
        alpha = jnp.exp2(m_prev - m_new)
        m_refs[g][...] = m_new
        pv = None
        for k0 in range(0, tq, ATTN_KC):
            p = jnp.exp2((load(k0) - m_new).astype(BF16))
            keys = pl.ds(pl.multiple_of(start + k0, ATTN_KC), ATTN_KC)
            part = jnp.dot(vx_ref[:, keys], p, preferred_element_type=F32)
            pv = part if pv is None else pv + part
        acc_refs[g][...] = alpha * acc_refs[g][...] + pv[:V_DIM, :]
        l_refs[g][...] = alpha * l_refs[g][...] + pv[V_DIM:V_DIM + 1, :]

    def reset():
        for g in range(ng):
            m_refs[g][...] = jnp.full(m_refs[g].shape, NEG_INF, F32)
            l_refs[g][...] = jnp.zeros(l_refs[g].shape, F32)
            acc_refs[g][...] = jnp.zeros(acc_refs[g].shape, F32)

    def finish(qi):
        o = (jnp.concatenate([r[...] for r in acc_refs], axis=-1)
             / jnp.concatenate([r[...] for r in l_refs], axis=-1))
        o = o[:, :tq] - lam * o[:, tq:]
        o = o * lax.rsqrt(jnp.mean(o * o, axis=0, keepdims=True) + RMS_EPS) * g_ref[...]
        row = pl.multiple_of(qi * tq, tq)
        o_ref[pl.ds(row, tq), :] = (o * (1.0 - LAMBDA_INIT)).T.astype(BF16)

    def task(t, slot):
        qi, j = tqi_ref[t], tj_ref[t]
        nqi, nj = tqi_ref[t + 1], tj_ref[t + 1]

        def run(masked):
            for g in range(ng):
                scores(nqi, nj, 1 - slot, g)
                update(j, slot, g, masked)

        @pl.when(tlast_ref[t] == 0)
        def _():
            run(False)

        @pl.when(tlast_ref[t] == 1)
        def _():
            run(True)
            finish(qi)
            reset()

    reset()
    for g in range(ng):
        scores(0, 0, 0, g)

    def pair(i, carry):
        task(2 * i, 0)
        task(2 * i + 1, 1)
        return carry

    lax.fori_loop(0, n_tasks // 2, pair, 0)


def _diff_attention(qt, k, vt, lq1, lk1, lq2, lk2, subln_g, batch, seq):
    T, D = k.shape
    tq = ATTN_TQ
    nq = seq // tq
    tqi, tj, tlast, n_tasks = _attn_tasks(nq)
    lam_spec = pl.BlockSpec((1, HEAD_DIM), lambda b, h, *_: (0, 0))
    rows_spec = pl.BlockSpec((seq, V_DIM), lambda b, h, *_: (b, h))
    cols_spec = pl.BlockSpec((V_DIM, seq), lambda b, h, *_: (h, b))
    qg = ATTN_QG
    per_group = lambda shape, dtype: [pltpu.VMEM(shape, dtype) for _ in range(2 * tq // qg)]
    grid_spec = pltpu.PrefetchScalarGridSpec(
        num_scalar_prefetch=3,
        grid=(batch, N_HEADS),
        in_specs=[lam_spec, lam_spec, lam_spec, lam_spec,
                  pl.BlockSpec((V_DIM, 1), lambda b, h, *_: (0, 0)),
                  cols_spec, rows_spec, cols_spec],
        out_specs=rows_spec,
        scratch_shapes=(
            [pltpu.VMEM((V_DIM, 2 * seq), BF16),
             pltpu.VMEM((V_DIM + ONES_ROWS, seq), BF16)]
            + per_group((tq, qg), F32)
            + per_group((tq, qg), F32)
            + per_group((1, qg), F32)
            + per_group((1, qg), F32)
            + per_group((1, qg), F32)
            + per_group((1, qg), F32)
            + per_group((V_DIM, qg), F32)),
    )
    return pl.pallas_call(
        partial(_attn_kernel, n_tasks=n_tasks),
        out_shape=jax.ShapeDtypeStruct((T, D), BF16),
        grid_spec=grid_spec,
        compiler_params=_cparams("arbitrary", "arbitrary"),
        name="diff_attention",
    )(tqi, tj, tlast, lq1, lk1, lq2, lk2, subln_g.reshape(V_DIM, 1), qt, k, vt)


def _mixer_kernel(x_ref, o_ref, u_ref, up_ref, ga_ref, gb_ref, g1_ref, sc2_ref, sh2_ref,
                  wao_ref, cw_ref, cb_ref, clg_ref, clb_ref, wpw_ref, bpw_ref, wout_ref,
                  ln1g_ref, ln1b_ref, wrh_ref, wrl_ref,
                  x1_ref, h2_ref, lg_ref, ubuf_ref, ush_ref, conv_ref, *, tiles_per_seq):
    tm = MIX_TM
    first = (pl.program_id(0) % tiles_per_seq) == 0

    prev = up_ref[...].astype(F32)
    ubuf_ref[0:CONV_HALO, :] = jnp.where(first, jnp.zeros_like(prev), prev)
    ubuf_ref[CONV_HALO:, :] = u_ref[...].astype(F32)
    for t0 in range(0, tm, MIX_SUB):
        rows = slice(t0, t0 + MIX_SUB)

        base = CONV_HALO - (CONV_K - 1)
        n_shift = MIX_SUB + CONV_HALO - SUBLANES
        for r in range(1, SUBLANES):
            ush_ref[r - 1] = ubuf_ref[t0 + r:t0 + r + n_shift, :]
        for c in range(D_MODEL // LANES):
            cl = slice(c * LANES, (c + 1) * LANES)
            for r0 in range(0, MIX_SUB, CONV_ROWS):
                acc = jnp.zeros((CONV_ROWS, LANES), F32)
                for j in range(CONV_K):
                    a, r = divmod(base + j, SUBLANES)
                    lo = SUBLANES * a + r0
                    win = (ubuf_ref[t0 + lo:t0 + lo + CONV_ROWS, cl] if r == 0
                           else ush_ref[r - 1, lo:lo + CONV_ROWS, cl])
                    acc = acc + cw_ref[j:j + 1, cl] * win
                conv_ref[t0 + r0:t0 + r0 + CONV_ROWS, cl] = acc
        yc = _silu(_layer_norm(conv_ref[rows, :] + cb_ref[...], clg_ref[...], clb_ref[...]))
        y_conv = (jnp.dot(yc.astype(BF16), wpw_ref[...], preferred_element_type=F32)
                  + bpw_ref[...])

        y_attn = jnp.dot(o_ref[rows, :], wao_ref[...], preferred_element_type=F32)
        mix = ga_ref[rows, :].astype(F32) * y_attn + gb_ref[rows, :].astype(F32) * y_conv
        y = jnp.dot(mix.astype(BF16), wout_ref[...], preferred_element_type=F32)
        x1 = _layer_norm(DN_ALPHA * x_ref[rows, :] + g1_ref[0] * y, ln1g_ref[...], ln1b_ref[...])
        x1_ref[rows, :] = x1

        h2 = x1 * (1.0 + sc2_ref[0]) + sh2_ref[0]
        h2_hi = h2.astype(BF16)
        _store_packed(h2_ref.at[:, rows, :], h2)
        h2_lo = (h2 - h2_hi.astype(F32)).astype(BF16)
        nt = lambda w, a: lax.dot_general(w, a, (((1,), (1,)), ((), ())),
                                          preferred_element_type=F32)
        lg_ref[:, rows] = (nt(wrh_ref[...], h2_hi)
                           + (nt(wrl_ref[...], h2_hi) + nt(wrh_ref[...], h2_lo)))


def _mixer_epilogue(x2, o, u, ga, gb, g1, sc2, sh2, wao, cw, cb, clg, clb, wpw, bpw, wout,
                    ln1g, ln1b, wr_hi, wr_lo, seq):
    T, D = x2.shape
    tm = MIX_TM
    tiles_per_seq = seq // tm
    halo_per_tile = tm // CONV_HALO
    tok = pl.BlockSpec((tm, D), lambda i: (i, 0))
    prev = pl.BlockSpec((CONV_HALO, D), lambda i: (jnp.maximum(i * halo_per_tile - 1, 0), 0))
    mod = pl.BlockSpec((1, 1, D), lambda i: (i // tiles_per_seq, 0, 0))
    row = pl.BlockSpec((1, D), lambda i: (0, 0))
    mat = pl.BlockSpec((D, D), lambda i: (0, 0), pipeline_mode=pl.Buffered(1))
    rw = pl.BlockSpec((N_EXPERTS, D), lambda i: (0, 0), pipeline_mode=pl.Buffered(1))
    return pl.pallas_call(
        partial(_mixer_kernel, tiles_per_seq=tiles_per_seq),
        out_shape=[jax.ShapeDtypeStruct((T, D), F32),
                   jax.ShapeDtypeStruct((2, T, D // 4), U32),
                   jax.ShapeDtypeStruct((N_EXPERTS, T), F32)],
        grid=(T // tm,),
        in_specs=[tok, tok, tok, prev, tok, tok, mod, mod, mod,
                  mat, pl.BlockSpec((CONV_K, D), lambda i: (0, 0)), row, row, row, mat, row, mat,
                  row, row, rw, rw],
        out_specs=[tok, pl.BlockSpec((2, tm, D // 4), lambda i: (0, i, 0)),
                   pl.BlockSpec((N_EXPERTS, tm), lambda i: (0, i))],
        scratch_shapes=[pltpu.VMEM((CONV_HALO + tm, D), F32),
                        pltpu.VMEM((SUBLANES - 1, MIX_SUB + CONV_HALO - SUBLANES, D), F32),
                        pltpu.VMEM((tm, D), F32)],
        compiler_params=_cparams("arbitrary"),
        name="mixer_epilogue",
    )(x2, o, u, u, ga, gb, g1, sc2, sh2, wao, cw, cb, clg, clb, wpw, bpw, wout,
      ln1g, ln1b, wr_hi, wr_lo)


def _first_argmax(v, eid):
    m = jnp.max(v, axis=0, keepdims=True)
    i = jnp.min(jnp.where(v == m, eid, float(N_EXPERTS)), axis=0, keepdims=True)
    return m, i


def _route_kernel(lg_ref, rb_ref, idx_ref, gate_ref, cnt_ref):
    tm = ROUTE_TM
    scores = _sigmoid(lg_ref[...])
    choice = scores + rb_ref[...]
    eid = lax.broadcasted_iota(jnp.int32, (N_EXPERTS, tm), 0).astype(F32)
    rows = lambda a, g: a[g * GROUP_SIZE:(g + 1) * GROUP_SIZE, :]

    gs = []
    for g in range(N_GROUPS):
        cg = rows(choice, g)
        eg = (lax.broadcasted_iota(jnp.int32, cg.shape, 0) + g * GROUP_SIZE).astype(F32)
        m1, i1 = _first_argmax(cg, eg)
        m2 = jnp.max(jnp.where(eg == i1, NEG_INF, cg), axis=0, keepdims=True)
        gs.append(jnp.broadcast_to(m1 + m2, (SUBLANES, tm)))

    kept = []
    for g in range(N_GROUPS):
        ahead = jnp.zeros((SUBLANES, tm), F32)
        for o in range(N_GROUPS):
            if o == g:
                continue
            beats = (gs[o] >= gs[g]) if o < g else (gs[o] > gs[g])
            ahead = ahead + jnp.where(beats, 1.0, 0.0)
        cap = jnp.where(ahead < TOPK_GROUPS, float("inf"), NEG_INF)
        for r in range(g * GROUP_SIZE, (g + 1) * GROUP_SIZE, SUBLANES):
            kept.append(jnp.minimum(choice[r:r + SUBLANES, :], cap))
    masked = jnp.concatenate(kept, axis=0)

    sel = jnp.zeros((N_EXPERTS, tm), F32)
    total = jnp.zeros((1, tm), F32)
    picked = []
    for k in range(TOP_K):
        _, i = _first_argmax(masked, eid)
        hit = eid == i
        sc = jnp.sum(jnp.where(hit, scores, 0.0), axis=0, keepdims=True)
        masked = jnp.where(hit, NEG_INF, masked)
        sel = sel + hit.astype(F32)
        total = total + sc
        idx_ref[k:k + 1, :] = i.astype(jnp.int32)
        picked.append(sc)
    for k in range(TOP_K):
        gate_ref[k:k + 1, :] = picked[k] / total * ROUTED_SCALE

    @pl.when(pl.program_id(0) == 0)
    def _():
        cnt_ref[...] = jnp.zeros_like(cnt_ref)
    cnt_ref[...] += jnp.sum(sel, axis=1, keepdims=True)


def _route(logits_t, router_bias):
    T = logits_t.shape[1]
    tm = ROUTE_TM
    return pl.pallas_call(
        _route_kernel,
        out_shape=[jax.ShapeDtypeStruct((TOP_K, T), jnp.int32),
                   jax.ShapeDtypeStruct((TOP_K, T), F32),
                   jax.ShapeDtypeStruct((N_EXPERTS, 1), F32)],
        grid=(T // tm,),
        in_specs=[pl.BlockSpec((N_EXPERTS, tm), lambda i: (0, i)),
                  pl.BlockSpec((N_EXPERTS, 1), lambda i: (0, 0))],
        out_specs=[pl.BlockSpec((TOP_K, tm), lambda i: (0, i)),
                   pl.BlockSpec((TOP_K, tm), lambda i: (0, i)),
                   pl.BlockSpec((N_EXPERTS, 1), lambda i: (0, 0))],
        compiler_params=_cparams("arbitrary"),
        name="route_topk",
    )(logits_t, router_bias.reshape(N_EXPERTS, 1))


def _slot_kernel(idx_ref, start_ref, pos_ref, carry_ref):
    tm = ROUTE_TM

    @pl.when(pl.program_id(0) == 0)
    def _():
        carry_ref[...] = jnp.zeros_like(carry_ref)

    idx = idx_ref[...]
    eid = lax.broadcasted_iota(jnp.int32, (N_EXPERTS, tm), 0)
    hits = [eid == idx[k:k + 1, :] for k in range(TOP_K)]
    sel = jnp.zeros((N_EXPERTS, tm), F32)
    for h in hits:
        sel = sel + h.astype(F32)
    r = lax.broadcasted_iota(jnp.int32, (tm, tm), 0)
    c = lax.broadcasted_iota(jnp.int32, (tm, tm), 1)
    earlier = (r < c).astype(BF16)
    rank = jnp.dot(sel.astype(BF16), earlier, preferred_element_type=F32)
    slot = rank + carry_ref[...] + start_ref[...]
    for k, h in enumerate(hits):
        pos_ref[k:k + 1, :] = jnp.sum(jnp.where(h, slot, 0.0), axis=0,
                                      keepdims=True).astype(jnp.int32)
    carry_ref[...] += jnp.sum(sel, axis=1, keepdims=True)


def _assign_slots(idx_t, pad_start):
    T = idx_t.shape[1]
    tm = ROUTE_TM
    return pl.pallas_call(
        _slot_kernel,
        out_shape=jax.ShapeDtypeStruct((TOP_K, T), jnp.int32),
        grid=(T // tm,),
        in_specs=[pl.BlockSpec((TOP_K, tm), lambda i: (0, i)),
                  pl.BlockSpec((N_EXPERTS, 1), lambda i: (0, 0))],
        out_specs=pl.BlockSpec((TOP_K, tm), lambda i: (0, i)),
        scratch_shapes=[pltpu.VMEM((N_EXPERTS, 1), F32)],
        compiler_params=_cparams("arbitrary"),
        name="assign_slots",
    )(idx_t, pad_start)


def _expert_kernel(be_ref, nu_ref, first_ref, slot_ref, nxt_ref, xs_ref, w1_hbm, w3_hbm, w2_hbm,
                   ys_ref, w1_buf, w3_buf, w2_buf, sem):
    b = pl.program_id(0)

    def weight_copies(expert, slot):
        pairs = ((w1_hbm, w1_buf), (w3_hbm, w3_buf), (w2_hbm, w2_buf))
        return [pltpu.make_async_copy(hbm.at[expert], buf.at[slot], sem.at[slot, i])
                for i, (hbm, buf) in enumerate(pairs)]

    @pl.when(b < nu_ref[0])
    def _():
        slot = slot_ref[b]

        @pl.when(b == 0)
        def _():
            for c in weight_copies(be_ref[0], 0):
                c.start()

        @pl.when(first_ref[b] == 1)
        def _():
            for c in weight_copies(be_ref[b], slot):
                c.wait()

            @pl.when(nxt_ref[b] >= 0)
            def _():
                for c in weight_copies(nxt_ref[b], 1 - slot):
                    c.start()

        lo, hi = _load_packed(xs_ref)
        half = lo.shape[-1]
        mm = lambda w: (jnp.dot(lo, w[slot, :half, :], preferred_element_type=F32)
                        + jnp.dot(hi, w[slot, half:, :], preferred_element_type=F32))
        hb = _silu(mm(w1_buf)) * mm(w3_buf)
        _store_packed(ys_ref, jnp.dot(hb, w2_buf[slot], preferred_element_type=F32))

    @pl.when(b >= nu_ref[0])
    def _():
        ys_ref[...] = jnp.zeros_like(ys_ref)


def _routed_experts(xs, blk_expert, n_used, w1, w3, w2):
    _, P, Dq = xs.shape
    blk = EXPERT_BLK
    n_blocks = P // blk
    D, F = w1.shape[-2:]

    b_ids = jnp.arange(n_blocks, dtype=jnp.int32)
    used = b_ids < n_used[0]
    prev = jnp.concatenate([jnp.full((1,), -1, jnp.int32), blk_expert[:-1]])
    first = (used & (blk_expert != prev)).astype(jnp.int32)
    slot = (jnp.cumsum(first) - 1) % 2
    start_at = jnp.where(first == 1, b_ids, n_blocks)
    later = jnp.concatenate([start_at[1:], jnp.full((1,), n_blocks, jnp.int32)])
    next_start = lax.cummin(later, axis=0, reverse=True)
    nxt = jnp.where(next_start < n_blocks,
                    blk_expert[jnp.minimum(next_start, n_blocks - 1)], -1).astype(jnp.int32)

    row_blk = lambda b, be, nu, *_: (0, jnp.minimum(b, nu[0] - 1), 0)
    grid_spec = pltpu.PrefetchScalarGridSpec(
        num_scalar_prefetch=5,
        grid=(n_blocks,),
        in_specs=[pl.BlockSpec((2, blk, Dq), row_blk),
                  pl.BlockSpec(memory_space=pl.ANY),
                  pl.BlockSpec(memory_space=pl.ANY),
                  pl.BlockSpec(memory_space=pl.ANY)],
        out_specs=pl.BlockSpec((2, blk, Dq), lambda b, *_: (0, b, 0)),
        scratch_shapes=[pltpu.VMEM((2, D, F), F32), pltpu.VMEM((2, D, F), F32),
                        pltpu.VMEM((2, F, D), F32), pltpu.SemaphoreType.DMA((2, 3))],
    )
    return pl.pallas_call(
        _expert_kernel,
        out_shape=jax.ShapeDtypeStruct((2, P, Dq), U32),
        grid_spec=grid_spec,
        compiler_params=_cparams("arbitrary"),
        name="routed_experts",
    )(blk_expert, n_used, first, slot.astype(jnp.int32), nxt, xs, w1, w3, w2)


def _sc_mesh():
    return plsc.VectorSubcoreMesh(core_axis_name="c", subcore_axis_name="s")


def _sc_dispatch(rows, slots_t, n_slots):
    T, W = rows.shape
    K = slots_t.shape[0]
    win = SC_WINDOW

    @pl.kernel(out_type=jax.ShapeDtypeStruct((n_slots, W), rows.dtype), mesh=_sc_mesh(),
               scratch_types=[])
    def dispatch(x_hbm, i_hbm, o_hbm):
        def body(x_vmem, i_vmem):
            for k in range(K):
                pltpu.sync_copy(x_vmem, o_hbm.at[i_vmem.at[k]])

        pltpu.emit_pipeline(
            body,
            grid=(T // win,),
            in_specs=[pl.BlockSpec((win, W), lambda i: (i, 0)),
                      pl.BlockSpec((K, win), lambda i: (0, i))],
            out_specs=[],
            core_axis_name=("c", "s"),
            dimension_semantics=(pltpu.PARALLEL,),
        )(x_hbm, i_hbm)

    return dispatch(rows, slots_t)


def _sc_gather(table, idx):
    W = table.shape[1]
    N = idx.shape[1]
    win = SC_WINDOW

    @pl.kernel(out_type=jax.ShapeDtypeStruct((N, W), table.dtype), mesh=_sc_mesh(),
               scratch_types=[])
    def gather(x_hbm, i_hbm, o_hbm):
        def body(i_vmem, o_vmem):
            pltpu.sync_copy(x_hbm.at[i_vmem.at[0]], o_vmem)

        pltpu.emit_pipeline(
            body,
            grid=(N // win,),
            in_specs=[pl.BlockSpec((1, win), lambda i: (0, i))],
            out_specs=[pl.BlockSpec((win, W), lambda i: (i, 0))],
            core_axis_name=("c", "s"),
            dimension_semantics=(pltpu.PARALLEL,),
        )(i_hbm, o_hbm)

    return gather(table, idx)


def _final_kernel(x1_ref, h2_ref, yg_ref, gate_ref, g2_ref, ws1_ref, ws3_ref, ws2_ref,
                  lng_ref, lnb_ref, o_ref):
    Dh = D_MODEL // 2
    gate = gate_ref[...]
    routed_lo = jnp.zeros((x1_ref.shape[0], Dh), F32)
    routed_hi = jnp.zeros((x1_ref.shape[0], Dh), F32)
    for k in range(TOP_K):
        lo, hi = _load_packed(yg_ref.at[k])
        routed_lo = routed_lo + lo * gate[:, k:k + 1]
        routed_hi = routed_hi + hi * gate[:, k:k + 1]
    h2 = jnp.concatenate(_load_packed(h2_ref), axis=-1).astype(BF16)
    a = jnp.dot(h2, ws1_ref[...], preferred_element_type=F32)
    b = jnp.dot(h2, ws3_ref[...], preferred_element_type=F32)
    shared = jnp.dot((_silu(a) * b).astype(BF16), ws2_ref[...], preferred_element_type=F32)
    y2 = jnp.concatenate([routed_lo, routed_hi], axis=-1) + shared
    o_ref[...] = _layer_norm(DN_ALPHA * x1_ref[...] + g2_ref[0] * y2, lng_ref[...], lnb_ref[...])


def _final(x1, h2, yg, gate, g2, ws1, ws3, ws2, ln2g, ln2b, seq):
    T, D = x1.shape
    tm = FINAL_TM
    tiles_per_seq = seq // tm
    F = ws1.shape[-1]
    tok = pl.BlockSpec((tm, D), lambda i: (i, 0))
    row = pl.BlockSpec((1, D), lambda i: (0, 0))
    return pl.pallas_call(
        _final_kernel,
        out_shape=jax.ShapeDtypeStruct((T, D), F32),
        grid=(T // tm,),
        in_specs=[tok, pl.BlockSpec((2, tm, D // 4), lambda i: (0, i, 0)),
                  pl.BlockSpec((TOP_K, 2, tm, D // 4), lambda i: (0, 0, i, 0)),
                  pl.BlockSpec((tm, TOP_K), lambda i: (i, 0)),
                  pl.BlockSpec((1, 1, D), lambda i: (i // tiles_per_seq, 0, 0)),
                  pl.BlockSpec((D, F), lambda i: (0, 0)),
                  pl.BlockSpec((D, F), lambda i: (0, 0)),
                  pl.BlockSpec((F, D), lambda i: (0, 0)),
                  row, row],
        out_specs=tok,
        compiler_params=_cparams("arbitrary"),
        name="moe_combine_final",
    )(x1, h2, yg, gate, g2, ws1, ws3, ws2, ln2g, ln2b)


def kernel(x, c, ada_w, ada_b, w_in, b_in, lambda_q1, lambda_k1, lambda_q2, lambda_k2, subln_g, w_attn_o, conv_w, conv_b, conv_ln_g, conv_ln_b, conv_pw_w, conv_pw_b, w_out, ln1_g, ln1_b, router_w, router_bias, w1, w3, w2, ws1, ws3, ws2, ln2_g, ln2_b):
    B, S, D = x.shape
    T = B * S
    l = 0
    x2 = x.reshape(T, D)
    row = lambda a: a.reshape(1, -1)

    mod = _ada_modulation(c, ada_w[l], ada_b[l])
    sh1, sc1, g1, sh2, sc2, g2 = [m.reshape(B, 1, D) for m in jnp.split(mod, 6, axis=-1)]

    qt, k, vt, u, ga, gb = _in_projection(x2, sc1, sh1, w_in[l].astype(BF16), row(b_in[l]), S)
    o = _diff_attention(qt, k, vt, row(lambda_q1[l]), row(lambda_k1[l]), row(lambda_q2[l]),
                        row(lambda_k2[l]), row(subln_g[l]), B, S)

    wr = router_w[l].T
    wr_hi = wr.astype(BF16)
    wr_lo = (wr - wr_hi.astype(F32)).astype(BF16)
    x1, h2, logits = _mixer_epilogue(
        x2, o, u, ga, gb, g1, sc2, sh2, w_attn_o[l].astype(BF16), conv_w[l], row(conv_b[l]),
        row(conv_ln_g[l]), row(conv_ln_b[l]), conv_pw_w[l].astype(BF16), row(conv_pw_b[l]),
        w_out[l].astype(BF16), row(ln1_g[l]), row(ln1_b[l]), wr_hi, wr_lo, S)

    idx, gate, counts = _route(logits, router_bias[l])

    blk = EXPERT_BLK
    cnt = counts.reshape(N_EXPERTS).astype(jnp.int32)
    padded = (cnt + blk - 1) // blk * blk
    pad_end = jnp.cumsum(padded)
    pad_start = pad_end - padded
    P = T * TOP_K + N_EXPERTS * blk
    n_blocks = P // blk
    blk_first_row = jnp.arange(n_blocks, dtype=jnp.int32) * blk
    blk_expert = jnp.minimum(
        jnp.sum((pad_end[None, :] <= blk_first_row[:, None]).astype(jnp.int32), axis=1),
        N_EXPERTS - 1)
    n_used = (pad_end[-1:] // blk).astype(jnp.int32)

    pos = _assign_slots(idx, pad_start.astype(F32).reshape(N_EXPERTS, 1))

    Dq = D // 4
    half_base = (jnp.arange(2, dtype=jnp.int32) * P)[None, :, None]
    slots = pos[:, None, :] + half_base
    xs = _sc_dispatch(h2.reshape(2 * T, Dq), slots.reshape(TOP_K, 2 * T), 2 * P)
    ys = _routed_experts(xs.reshape(2, P, Dq), blk_expert, n_used, w1[l], w3[l], w2[l])
    yg = _sc_gather(ys.reshape(2 * P, Dq), slots.reshape(1, TOP_K * 2 * T))
    yg = yg.reshape(TOP_K, 2, T, Dq)

    out = _final(x1, h2, yg, gate.T, g2, ws1[l].astype(BF16), ws3[l].astype(BF16),
                 ws2[l].astype(BF16), row(ln2_g[l]), row(ln2_b[l]), S)
    return out.reshape(B, S, D)
```

```python
import math
from functools import partial

import jax
import jax.numpy as jnp
from jax import lax
from jax.experimental import pallas as pl
from jax.experimental.pallas import tpu as pltpu
from jax.experimental.pallas import tpu_sc as plsc

D_MODEL = 1024
N_HEADS = 8
HEAD_DIM = 64
V_DIM = 128
CONV_K = 31
N_EXPERTS = 256
TOP_K = 8
N_GROUPS = 8
GROUP_SIZE = N_EXPERTS // N_GROUPS
TOPK_GROUPS = 4
ROUTED_SCALE = 2.5
LN_EPS = 1e-5
RMS_EPS = 1e-5
DEPTH = 1
DN_ALPHA = (2.0 * DEPTH) ** 0.25
LAMBDA_INIT = 0.8 - 0.6 * math.exp(-0.3 * 0)
Q_SCALE = HEAD_DIM ** -0.5 * math.log2(math.e)

LANES = 128
SUBLANES = 8
BF16_ROWS = 16
VMEM_LIMIT_BYTES = 56 * 1024 * 1024

PROJ_TM = 512
ATTN_TQ = 512
ATTN_QG = 512
ATTN_KC = 256
ONES_ROWS = 16
MIX_TM = 512
MIX_SUB = 256
CONV_HALO = 32
CONV_ROWS = 64
CONV_TM = 512
CONV_SUB = 256
ROUTE_TM = 512
EXPERT_BLK = 512
FINAL_TM = 512
SC_WINDOW = 128

F32 = jnp.float32
BF16 = jnp.bfloat16
U32 = jnp.uint32
NEG_INF = float("-inf")


def _cparams(*sem):
    return pltpu.CompilerParams(dimension_semantics=sem, vmem_limit_bytes=VMEM_LIMIT_BYTES)


def _sigmoid(x):
    return 1.0 / (1.0 + jnp.exp(-x))


def _silu(x):
    return x * _sigmoid(x)


BF16_BITS = 16
HIGH_HALF_WORD = 0xFFFF0000


def _pack_halves(x):
    half = x.shape[-1] // 2
    bits = lambda v: lax.bitcast_convert_type(v.astype(BF16).astype(F32), U32)
    return (bits(x[:, :half]) >> BF16_BITS) | (bits(x[:, half:]) & jnp.uint32(HIGH_HALF_WORD))


def _unpack_halves(w):
    lo = lax.bitcast_convert_type(w << BF16_BITS, F32)
    hi = lax.bitcast_convert_type(w & jnp.uint32(HIGH_HALF_WORD), F32)
    return lo, hi


def _store_packed(ref, x):
    w = _pack_halves(x)
    q = w.shape[-1] // 2
    ref[0] = w[:, :q]
    ref[1] = w[:, q:]


def _load_packed(ref):
    return _unpack_halves(jnp.concatenate([ref[0], ref[1]], axis=-1))


def _layer_norm(r, g, b):
    mu = jnp.mean(r, axis=-1, keepdims=True)
    d = r - mu
    var = jnp.mean(d * d, axis=-1, keepdims=True)
    return d * lax.rsqrt(var + LN_EPS) * g + b


def _ada_kernel(c_ref, w_ref, b_ref, o_ref):
    a = _silu(c_ref[...])
    o_ref[...] = jnp.dot(a, w_ref[...], preferred_element_type=F32,
                         precision=lax.Precision.HIGHEST) + b_ref[...]


def _ada_modulation(c, w, b):
    B, D = c.shape
    n = w.shape[1] // D
    return pl.pallas_call(
        _ada_kernel,
        out_shape=jax.ShapeDtypeStruct((B, n * D), F32),
        grid=(n,),
        in_specs=[pl.BlockSpec((B, D), lambda j: (0, 0)),
                  pl.BlockSpec((D, D), lambda j: (0, j)),
                  pl.BlockSpec((1, D), lambda j: (0, j))],
        out_specs=pl.BlockSpec((B, D), lambda j: (0, j)),
        compiler_params=_cparams("arbitrary"),
        name="ada_modulation",
    )(c, w, b.reshape(1, n * D))


def _inproj_kernel(x_ref, sc_ref, sh_ref, w_ref, b_ref,
                   qt_ref, k_ref, vt_ref, u_ref, ga_ref, gb_ref):
    D = D_MODEL
    h = (x_ref[...] * (1.0 + sc_ref[0]) + sh_ref[0]).astype(BF16)

    def proj(j):
        return (jnp.dot(h, w_ref[:, j * D:(j + 1) * D], preferred_element_type=F32)
                + b_ref[:, j * D:(j + 1) * D])

    qt_ref[...] = (proj(0) * Q_SCALE).T.astype(BF16)
    k_ref[...] = proj(1).astype(BF16)
    vt_ref[...] = proj(2).T.astype(BF16)
    u_ref[...] = (proj(3) * _sigmoid(proj(4))).astype(BF16)
    ga_ref[...] = _sigmoid(proj(5)).astype(BF16)
    gb_ref[...] = _sigmoid(proj(6)).astype(BF16)


def _in_projection(x2, sc1, sh1, w_in, b_in, seq):
    T, D = x2.shape
    tm = PROJ_TM
    tiles_per_seq = seq // tm
    ncols = w_in.shape[1]
    tok = pl.BlockSpec((tm, D), lambda i: (i, 0))
    mod = pl.BlockSpec((1, 1, D), lambda i: (i // tiles_per_seq, 0, 0))
    tok_t = pl.BlockSpec((D, tm), lambda i: (0, i))
    out = jax.ShapeDtypeStruct((T, D), BF16)
    out_t = jax.ShapeDtypeStruct((D, T), BF16)
    return pl.pallas_call(
        _inproj_kernel,
        out_shape=[out_t, out, out_t, out, out, out],
        grid=(T // tm,),
        in_specs=[tok, mod, mod,
                  pl.BlockSpec((D, ncols), lambda i: (0, 0)),
                  pl.BlockSpec((1, ncols), lambda i: (0, 0))],
        out_specs=[tok_t, tok, tok_t, tok, tok, tok],
        compiler_params=_cparams("arbitrary"),
        name="in_projection",
    )(x2, sc1, sh1, w_in, b_in)


def _attn_tasks(nq):
    tasks = [(qi, j, int(j == qi)) for qi in range(nq) for j in range(qi + 1)]
    if len(tasks) % 2:
        raise ValueError("the two-slot task loop needs an even number of tile pairs")
    qis, js, lasts = zip(*(tasks + [(0, 0, 0)]))
    as_i32 = lambda v: jnp.asarray(v, jnp.int32)
    return as_i32(qis), as_i32(js), as_i32(lasts), len(tasks)


def _attn_kernel(tqi_ref, tj_ref, tlast_ref, lq1_ref, lk1_ref, lq2_ref, lk2_ref, g_ref,
                 qt_ref, k_ref, vt_ref, o_ref, q2_ref, vx_ref, *scratch, n_tasks):
    tq = ATTN_TQ
    qg = ATTN_QG
    if tq % qg:
        raise ValueError("a query group must not span the two softmax maps of a tile")
    ng = 2 * tq // qg
    groups = [scratch[i * ng:(i + 1) * ng] for i in range(7)]
    s_slots, x_slots = groups[0:2], groups[2:4]
    m_refs, l_refs, acc_refs = groups[4:]
    seq = k_ref.shape[0]

    vx_ref[:V_DIM, :] = vt_ref[...]
    vx_ref[V_DIM:, :] = jnp.ones((ONES_ROWS, seq), BF16)

    lam = (jnp.exp(jnp.sum(lq1_ref[...] * lk1_ref[...], axis=-1, keepdims=True))
           - jnp.exp(jnp.sum(lq2_ref[...] * lk2_ref[...], axis=-1, keepdims=True))
           + LAMBDA_INIT)

    for i in range(seq // tq):
        qt = qt_ref[:, i * tq:(i + 1) * tq]
        feat = lax.broadcasted_iota(jnp.int32, qt.shape, 0)
        zero = jnp.zeros_like(qt)
        q2_ref[:, 2 * tq * i:2 * tq * i + tq] = jnp.where(feat < HEAD_DIM, qt, zero)
        q2_ref[:, 2 * tq * i + tq:2 * tq * (i + 1)] = jnp.where(feat >= HEAD_DIM, qt, zero)

    def scores(qi, j, slot, g):
        start = pl.multiple_of(j * tq, tq)
        col = pl.multiple_of(qi * (2 * tq) + g * qg, qg)
        s = jnp.dot(k_ref[pl.ds(start, tq), :], q2_ref[:, pl.ds(col, qg)],
                    preferred_element_type=F32)
        s_slots[slot][g][...] = s
        x_slots[slot][g][...] = jnp.max(s, axis=0, keepdims=True)

    def update(j, slot, g, masked):
        start = pl.multiple_of(j * tq, tq)
        s_ref = s_slots[slot][g]

        def load(k0):
            s = s_ref[k0:k0 + ATTN_KC, :]
            if masked:
                key = lax.broadcasted_iota(jnp.int32, s.shape, 0) + k0
                qry = lax.broadcasted_iota(jnp.int32, s.shape, 1) + (g * qg) % tq
                s = jnp.where(key <= qry, s, NEG_INF)
            return s

        m_prev = m_refs[g][...]
        if masked:
            tile_max = jnp.max(load(0), axis=0, keepdims=True)
            for k0 in range(ATTN_KC, tq, ATTN_KC):
                tile_max = jnp.maximum(tile_max, jnp.max(load(k0), axis=0, keepdims=True))
        else:
            tile_max = x_slots[slot][g][...]
        m_new = jnp.maximum(m_prev, tile_max)
        alpha = jnp.exp2(m_prev - m_new)
        m_refs[g][...] = m_new
        pv = None
        for k0 in range(0, tq, ATTN_KC):
            p = jnp.exp2((load(k0) - m_new).astype(BF16))
            keys = pl.ds(pl.multiple_of(start + k0, ATTN_KC), ATTN_KC)
            part = jnp.dot(vx_ref[:, keys], p, preferred_element_type=F32)
            pv = part if pv is None else pv + part
        acc_refs[g][...] = alpha * acc_refs[g][...] + pv[:V_DIM, :]
        l_refs[g][...] = alpha * l_refs[g][...] + pv[V_DIM:V_DIM + 1, :]

    def reset():
        for g in range(ng):
            m_refs[g][...] = jnp.full(m_refs[g].shape, NEG_INF, F32)
            l_refs[g][...] = jnp.zeros(l_refs[g].shape, F32)
            acc_refs[g][...] = jnp.zeros(acc_refs[g].shape, F32)

    def finish(qi):
        o = (jnp.concatenate([r[...] for r in acc_refs], axis=-1)
             / jnp.concatenate([r[...] for r in l_refs], axis=-1))
        o = o[:, :tq] - lam * o[:, tq:]
        o = o * lax.rsqrt(jnp.mean(o * o, axis=0, keepdims=True) + RMS_EPS) * g_ref[...]
        row = pl.multiple_of(qi * tq, tq)
        o_ref[pl.ds(row, tq), :] = (o * (1.0 - LAMBDA_INIT)).T.astype(BF16)

    def task(t, slot):
        qi, j = tqi_ref[t], tj_ref[t]
        nqi, nj = tqi_ref[t + 1], tj_ref[t + 1]

        def run(masked):
            for g in range(ng):
                scores(nqi, nj, 1 - slot, g)
                update(j, slot, g, masked)

        @pl.when(tlast_ref[t] == 0)
        def _():
            run(False)

        @pl.when(tlast_ref[t] == 1)
        def _():
            run(True)
            finish(qi)
            reset()

    reset()
    for g in range(ng):
        scores(0, 0, 0, g)

    def pair(i, carry):
        task(2 * i, 0)
        task(2 * i + 1, 1)
        return carry

    lax.fori_loop(0, n_tasks // 2, pair, 0)


def _diff_attention(qt, k, vt, lq1, lk1, lq2, lk2, subln_g, batch, seq):
    T, D = k.shape
    tq = ATTN_TQ
    nq = seq // tq
    tqi, tj, tlast, n_tasks = _attn_tasks(nq)
    lam_spec = pl.BlockSpec((1, HEAD_DIM), lambda b, h, *_: (0, 0))
    rows_spec = pl.BlockSpec((seq, V_DIM), lambda b, h, *_: (b, h))
    cols_spec = pl.BlockSpec((V_DIM, seq), lambda b, h, *_: (h, b))
    qg = ATTN_QG
    per_group = lambda shape, dtype: [pltpu.VMEM(shape, dtype) for _ in range(2 * tq // qg)]
    grid_spec = pltpu.PrefetchScalarGridSpec(
        num_scalar_prefetch=3,
        grid=(batch, N_HEADS),
        in_specs=[lam_spec, lam_spec, lam_spec, lam_spec,
                  pl.BlockSpec((V_DIM, 1), lambda b, h, *_: (0, 0)),
                  cols_spec, rows_spec, cols_spec],
        out_specs=rows_spec,
        scratch_shapes=(
            [pltpu.VMEM((V_DIM, 2 * seq), BF16),
             pltpu.VMEM((V_DIM + ONES_ROWS, seq), BF16)]
            + per_group((tq, qg), F32)
            + per_group((tq, qg), F32)
            + per_group((1, qg), F32)
            + per_group((1, qg), F32)
            + per_group((1, qg), F32)
            + per_group((1, qg), F32)
            + per_group((V_DIM, qg), F32)),
    )
    return pl.pallas_call(
        partial(_attn_kernel, n_tasks=n_tasks),
        out_shape=jax.ShapeDtypeStruct((T, D), BF16),
        grid_spec=grid_spec,
        compiler_params=_cparams("arbitrary", "arbitrary"),
        name="diff_attention",
    )(tqi, tj, tlast, lq1, lk1, lq2, lk2, subln_g.reshape(V_DIM, 1), qt, k, vt)


def _conv_kernel(u_ref, up_ref, cw_ref, o_ref, ubuf_ref, ush_ref, ub_ref, *, tiles_per_seq):
    tm = CONV_TM
    first = (pl.program_id(0) % tiles_per_seq) == 0
    prev = up_ref[...].astype(F32)
    ubuf_ref[0:CONV_HALO, :] = jnp.where(first, jnp.zeros_like(prev), prev)
    ubuf_ref[CONV_HALO:, :] = u_ref[...].astype(F32)
    base = CONV_HALO - (CONV_K - 1)
    n_shift = CONV_SUB + CONV_HALO - SUBLANES
    for t0 in range(0, tm, CONV_SUB):
        for r in range(1, SUBLANES):
            ush_ref[r - 1] = ubuf_ref[t0 + r:t0 + r + n_shift, :]
        for q in range(BF16_ROWS):
            r, up = q % SUBLANES, SUBLANES * (q // SUBLANES)
            n = ub_ref.shape[2] if q == 0 else n_shift - SUBLANES
            for c in range(D_MODEL // LANES):
                cl = slice(c * LANES, (c + 1) * LANES)
                src = (ubuf_ref[t0 + up:t0 + up + n, cl] if r == 0
                       else ush_ref[r - 1, up:up + n, cl])
                ub_ref[q, c, 0:n, :] = src.astype(BF16)
        for c in range(D_MODEL // LANES):
            cl = slice(c * LANES, (c + 1) * LANES)
            for r0 in range(0, CONV_SUB, CONV_ROWS):
                acc = jnp.zeros((CONV_ROWS // BF16_ROWS, BF16_ROWS, LANES), F32)
                for j in range(CONV_K):
                    b, q = divmod(base + j, BF16_ROWS)
                    lo = BF16_ROWS * b + r0
                    win = ub_ref[q, c, lo:lo + CONV_ROWS, :].astype(F32)
                    w = cw_ref[j, c].astype(F32)
                    acc = acc + win.reshape(acc.shape) * w[None]
                o_ref[t0 + r0:t0 + r0 + CONV_ROWS, cl] = acc.reshape(CONV_ROWS, LANES).astype(BF16)


def _causal_conv(u, conv_w, seq):
    T, D = u.shape
    tm = CONV_TM
    tiles_per_seq = seq // tm
    halo_per_tile = tm // CONV_HALO
    taps = jnp.broadcast_to(conv_w.astype(BF16).reshape(CONV_K, D // LANES, 1, LANES),
                            (CONV_K, D // LANES, BF16_ROWS, LANES))
    tok = pl.BlockSpec((tm, D), lambda i: (i, 0))
    return pl.pallas_call(
        partial(_conv_kernel, tiles_per_seq=tiles_per_seq),
        out_shape=jax.ShapeDtypeStruct((T, D), BF16),
        grid=(T // tm,),
        in_specs=[tok,
                  pl.BlockSpec((CONV_HALO, D), lambda i: (jnp.maximum(i * halo_per_tile - 1, 0), 0)),
                  pl.BlockSpec((CONV_K, D // LANES, BF16_ROWS, LANES), lambda i: (0, 0, 0, 0))],
        out_specs=tok,
        scratch_shapes=[pltpu.VMEM((CONV_HALO + tm, D), F32),
                        pltpu.VMEM((SUBLANES - 1, CONV_SUB + CONV_HALO - SUBLANES, D), F32),
                        pltpu.VMEM((BF16_ROWS, D // LANES, CONV_SUB + CONV_HALO, LANES), BF16)],
        compiler_params=_cparams("arbitrary"),
        name="causal_conv",
    )(u, u, taps)


def _mixer_kernel(x_ref, o_ref, uc_ref, ga_ref, gb_ref, g1_ref, sc2_ref, sh2_ref,
                  wao_ref, cb_ref, clg_ref, clb_ref, wpw_ref, bpw_ref, wout_ref,
                  ln1g_ref, ln1b_ref, wrh_ref, wrl_ref,
                  x1_ref, h2_ref, lg_ref):
    tm = MIX_TM
    for t0 in range(0, tm, MIX_SUB):
        rows = slice(t0, t0 + MIX_SUB)
        conv = uc_ref[rows, :].astype(F32)
        yc = _silu(_layer_norm(conv + cb_ref[...], clg_ref[...], clb_ref[...]))
        y_conv = (jnp.dot(yc.astype(BF16), wpw_ref[...], preferred_element_type=F32)
                  + bpw_ref[...])

        y_attn = jnp.dot(o_ref[rows, :], wao_ref[...], preferred_element_type=F32)
        mix = ga_ref[rows, :].astype(F32) * y_attn + gb_ref[rows, :].astype(F32) * y_conv
        y = jnp.dot(mix.astype(BF16), wout_ref[...], preferred_element_type=F32)
        x1 = _layer_norm(DN_ALPHA * x_ref[rows, :] + g1_ref[0] * y, ln1g_ref[...], ln1b_ref[...])
        x1_ref[rows, :] = x1

        h2 = x1 * (1.0 + sc2_ref[0]) + sh2_ref[0]
        h2_hi = h2.astype(BF16)
        _store_packed(h2_ref.at[:, rows, :], h2)
        h2_lo = (h2 - h2_hi.astype(F32)).astype(BF16)
        nt = lambda w, a: lax.dot_general(w, a, (((1,), (1,)), ((), ())),
                                          preferred_element_type=F32)
        lg_ref[:, rows] = (nt(wrh_ref[...], h2_hi)
                           + (nt(wrl_ref[...], h2_hi) + nt(wrh_ref[...], h2_lo)))


def _mixer_epilogue(x2, o, uc, ga, gb, g1, sc2, sh2, wao, cb, clg, clb, wpw, bpw, wout,
                    ln1g, ln1b, wr_hi, wr_lo, seq):
    T, D = x2.shape
    tm = MIX_TM
    tiles_per_seq = seq // tm
    tok = pl.BlockSpec((tm, D), lambda i: (i, 0))
    mod = pl.BlockSpec((1, 1, D), lambda i: (i // tiles_per_seq, 0, 0))
    row = pl.BlockSpec((1, D), lambda i: (0, 0))
    mat = pl.BlockSpec((D, D), lambda i: (0, 0), pipeline_mode=pl.Buffered(1))
    rw = pl.BlockSpec((N_EXPERTS, D), lambda i: (0, 0), pipeline_mode=pl.Buffered(1))
    return pl.pallas_call(
        _mixer_kernel,
        out_shape=[jax.ShapeDtypeStruct((T, D), F32),
                   jax.ShapeDtypeStruct((2, T, D // 4), U32),
                   jax.ShapeDtypeStruct((N_EXPERTS, T), F32)],
        grid=(T // tm,),
        in_specs=[tok, tok, tok, tok, tok, mod, mod, mod,
                  mat, row, row, row, mat, row, mat,
                  row, row, rw, rw],
        out_specs=[tok, pl.BlockSpec((2, tm, D // 4), lambda i: (0, i, 0)),
                   pl.BlockSpec((N_EXPERTS, tm), lambda i: (0, i))],
        compiler_params=_cparams("arbitrary"),
        name="mixer_epilogue",
    )(x2, o, uc, ga, gb, g1, sc2, sh2, wao, cb, clg, clb, wpw, bpw, wout,
      ln1g, ln1b, wr_hi, wr_lo)


def _first_argmax(v, eid):
    m = jnp.max(v, axis=0, keepdims=True)
    i = jnp.min(jnp.where(v == m, eid, float(N_EXPERTS)), axis=0, keepdims=True)
    return m, i


def _route_kernel(lg_ref, rb_ref, idx_ref, gate_ref, cnt_ref):
    tm = ROUTE_TM
    scores = _sigmoid(lg_ref[...])
    choice = scores + rb_ref[...]
    eid = lax.broadcasted_iota(jnp.int32, (N_EXPERTS, tm), 0).astype(F32)
    rows = lambda a, g: a[g * GROUP_SIZE:(g + 1) * GROUP_SIZE, :]

    gs = []
    for g in range(N_GROUPS):
        cg = rows(choice, g)
        eg = (lax.broadcasted_iota(jnp.int32, cg.shape, 0) + g * GROUP_SIZE).astype(F32)
        m1, i1 = _first_argmax(cg, eg)
        m2 = jnp.max(jnp.where(eg == i1, NEG_INF, cg), axis=0, keepdims=True)
        gs.append(jnp.broadcast_to(m1 + m2, (SUBLANES, tm)))

    kept = []
    for g in range(N_GROUPS):
        ahead = jnp.zeros((SUBLANES, tm), F32)
        for o in range(N_GROUPS):
            if o == g:
                continue
            beats = (gs[o] >= gs[g]) if o < g else (gs[o] > gs[g])
            ahead = ahead + jnp.where(beats, 1.0, 0.0)
        cap = jnp.where(ahead < TOPK_GROUPS, float("inf"), NEG_INF)
        for r in range(g * GROUP_SIZE, (g + 1) * GROUP_SIZE, SUBLANES):
            kept.append(jnp.minimum(choice[r:r + SUBLANES, :], cap))
    masked = jnp.concatenate(kept, axis=0)

    sel = jnp.zeros((N_EXPERTS, tm), F32)
    total = jnp.zeros((1, tm), F32)
    picked = []
    for k in range(TOP_K):
        _, i = _first_argmax(masked, eid)
        hit = eid == i
        sc = jnp.sum(jnp.where(hit, scores, 0.0), axis=0, keepdims=True)
        masked = jnp.where(hit, NEG_INF, masked)
        sel = sel + hit.astype(F32)
        total = total + sc
        idx_ref[k:k + 1, :] = i.astype(jnp.int32)
        picked.append(sc)
    for k in range(TOP_K):
        gate_ref[k:k + 1, :] = picked[k] / total * ROUTED_SCALE

    @pl.when(pl.program_id(0) == 0)
    def _():
        cnt_ref[...] = jnp.zeros_like(cnt_ref)
    cnt_ref[...] += jnp.sum(sel, axis=1, keepdims=True)


def _route(logits_t, router_bias):
    T = logits_t.shape[1]
    tm = ROUTE_TM
    return pl.pallas_call(
        _route_kernel,
        out_shape=[jax.ShapeDtypeStruct((TOP_K, T), jnp.int32),
                   jax.ShapeDtypeStruct((TOP_K, T), F32),
                   jax.ShapeDtypeStruct((N_EXPERTS, 1), F32)],
        grid=(T // tm,),
        in_specs=[pl.BlockSpec((N_EXPERTS, tm), lambda i: (0, i)),
                  pl.BlockSpec((N_EXPERTS, 1), lambda i: (0, 0))],
        out_specs=[pl.BlockSpec((TOP_K, tm), lambda i: (0, i)),
                   pl.BlockSpec((TOP_K, tm), lambda i: (0, i)),
                   pl.BlockSpec((N_EXPERTS, 1), lambda i: (0, 0))],
        compiler_params=_cparams("arbitrary"),
        name="route_topk",
    )(logits_t, router_bias.reshape(N_EXPERTS, 1))


def _slot_kernel(idx_ref, start_ref, pos_ref, carry_ref):
    tm = ROUTE_TM

    @pl.when(pl.program_id(0) == 0)
    def _():
        carry_ref[...] = jnp.zeros_like(carry_ref)

    idx = idx_ref[...]
    eid = lax.broadcasted_iota(jnp.int32, (N_EXPERTS, tm), 0)
    hits = [eid == idx[k:k + 1, :] for k in range(TOP_K)]
    sel = jnp.zeros((N_EXPERTS, tm), F32)
    for h in hits:
        sel = sel + h.astype(F32)
    r = lax.broadcasted_iota(jnp.int32, (tm, tm), 0)
    c = lax.broadcasted_iota(jnp.int32, (tm, tm), 1)
    earlier = (r < c).astype(BF16)
    rank = jnp.dot(sel.astype(BF16), earlier, preferred_element_type=F32)
    slot = rank + carry_ref[...] + start_ref[...]
    for k, h in enumerate(hits):
        pos_ref[k:k + 1, :] = jnp.sum(jnp.where(h, slot, 0.0), axis=0,
                                      keepdims=True).astype(jnp.int32)
    carry_ref[...] += jnp.sum(sel, axis=1, keepdims=True)


def _assign_slots(idx_t, pad_start):
    T = idx_t.shape[1]
    tm = ROUTE_TM
    return pl.pallas_call(
        _slot_kernel,
        out_shape=jax.ShapeDtypeStruct((TOP_K, T), jnp.int32),
        grid=(T // tm,),
        in_specs=[pl.BlockSpec((TOP_K, tm), lambda i: (0, i)),
                  pl.BlockSpec((N_EXPERTS, 1), lambda i: (0, 0))],
        out_specs=pl.BlockSpec((TOP_K, tm), lambda i: (0, i)),
        scratch_shapes=[pltpu.VMEM((N_EXPERTS, 1), F32)],
        compiler_params=_cparams("arbitrary"),
        name="assign_slots",
    )(idx_t, pad_start)


def _expert_kernel(be_ref, nu_ref, first_ref, slot_ref, nxt_ref, xs_ref, w1_hbm, w3_hbm, w2_hbm,
                   ys_ref, w1_buf, w3_buf, w2_buf, sem):
    b = pl.program_id(0)

    def weight_copies(expert, slot):
        pairs = ((w1_hbm, w1_buf), (w3_hbm, w3_buf), (w2_hbm, w2_buf))
        return [pltpu.make_async_copy(hbm.at[expert], buf.at[slot], sem.at[slot, i])
                for i, (hbm, buf) in enumerate(pairs)]

    @pl.when(b < nu_ref[0])
    def _():
        slot = slot_ref[b]

        @pl.when(b == 0)
        def _():
            for c in weight_copies(be_ref[0], 0):
                c.start()

        @pl.when(first_ref[b] == 1)
        def _():
            for c in weight_copies(be_ref[b], slot):
                c.wait()

            @pl.when(nxt_ref[b] >= 0)
            def _():
                for c in weight_copies(nxt_ref[b], 1 - slot):
                    c.start()

        lo, hi = _load_packed(xs_ref)
        half = lo.shape[-1]
        mm = lambda w: (jnp.dot(lo, w[slot, :half, :], preferred_element_type=F32)
                        + jnp.dot(hi, w[slot, half:, :], preferred_element_type=F32))
        hb = _silu(mm(w1_buf)) * mm(w3_buf)
        _store_packed(ys_ref, jnp.dot(hb, w2_buf[slot], preferred_element_type=F32))

    @pl.when(b >= nu_ref[0])
    def _():
        ys_ref[...] = jnp.zeros_like(ys_ref)


def _routed_experts(xs, blk_expert, n_used, w1, w3, w2):
    _, P, Dq = xs.shape
    blk = EXPERT_BLK
    n_blocks = P // blk
    D, F = w1.shape[-2:]

    b_ids = jnp.arange(n_blocks, dtype=jnp.int32)
    used = b_ids < n_used[0]
    prev = jnp.concatenate([jnp.full((1,), -1, jnp.int32), blk_expert[:-1]])
    first = (used & (blk_expert != prev)).astype(jnp.int32)
    slot = (jnp.cumsum(first) - 1) % 2
    start_at = jnp.where(first == 1, b_ids, n_blocks)
    later = jnp.concatenate([start_at[1:], jnp.full((1,), n_blocks, jnp.int32)])
    next_start = lax.cummin(later, axis=0, reverse=True)
    nxt = jnp.where(next_start < n_blocks,
                    blk_expert[jnp.minimum(next_start, n_blocks - 1)], -1).astype(jnp.int32)

    row_blk = lambda b, be, nu, *_: (0, jnp.minimum(b, nu[0] - 1), 0)
    grid_spec = pltpu.PrefetchScalarGridSpec(
        num_scalar_prefetch=5,
        grid=(n_blocks,),
        in_specs=[pl.BlockSpec((2, blk, Dq), row_blk),
                  pl.BlockSpec(memory_space=pl.ANY),
                  pl.BlockSpec(memory_space=pl.ANY),
                  pl.BlockSpec(memory_space=pl.ANY)],
        out_specs=pl.BlockSpec((2, blk, Dq), lambda b, *_: (0, b, 0)),
        scratch_shapes=[pltpu.VMEM((2, D, F), F32), pltpu.VMEM((2, D, F), F32),
                        pltpu.VMEM((2, F, D), F32), pltpu.SemaphoreType.DMA((2, 3))],
    )
    return pl.pallas_call(
        _expert_kernel,
        out_shape=jax.ShapeDtypeStruct((2, P, Dq), U32),
        grid_spec=grid_spec,
        compiler_params=_cparams("arbitrary"),
        name="routed_experts",
    )(blk_expert, n_used, first, slot.astype(jnp.int32), nxt, xs, w1, w3, w2)


def _sc_mesh():
    return plsc.VectorSubcoreMesh(core_axis_name="c", subcore_axis_name="s")


def _sc_dispatch(rows, slots_t, n_slots):
    T, W = rows.shape
    K = slots_t.shape[0]
    win = SC_WINDOW

    @pl.kernel(out_type=jax.ShapeDtypeStruct((n_slots, W), rows.dtype), mesh=_sc_mesh(),
               scratch_types=[])
    def dispatch(x_hbm, i_hbm, o_hbm):
        def body(x_vmem, i_vmem):
            for k in range(K):
                pltpu.sync_copy(x_vmem, o_hbm.at[i_vmem.at[k]])

        pltpu.emit_pipeline(
            body,
            grid=(T // win,),
            in_specs=[pl.BlockSpec((win, W), lambda i: (i, 0)),
                      pl.BlockSpec((K, win), lambda i: (0, i))],
            out_specs=[],
            core_axis_name=("c", "s"),
            dimension_semantics=(pltpu.PARALLEL,),
        )(x_hbm, i_hbm)

    return dispatch(rows, slots_t)


def _sc_gather(table, idx):
    W = table.shape[1]
    N = idx.shape[1]
    win = SC_WINDOW

    @pl.kernel(out_type=jax.ShapeDtypeStruct((N, W), table.dtype), mesh=_sc_mesh(),
               scratch_types=[])
    def gather(x_hbm, i_hbm, o_hbm):
        def body(i_vmem, o_vmem):
            pltpu.sync_copy(x_hbm.at[i_vmem.at[0]], o_vmem)

        pltpu.emit_pipeline(
            body,
            grid=(N // win,),
            in_specs=[pl.BlockSpec((1, win), lambda i: (0, i))],
            out_specs=[pl.BlockSpec((win, W), lambda i: (i, 0))],
            core_axis_name=("c", "s"),
            dimension_semantics=(pltpu.PARALLEL,),
        )(i_hbm, o_hbm)

    return gather(table, idx)


def _final_kernel(x1_ref, h2_ref, yg_ref, gate_ref, g2_ref, ws1_ref, ws3_ref, ws2_ref,
                  lng_ref, lnb_ref, o_ref):
    Dh = D_MODEL // 2
    gate = gate_ref[...]
    routed_lo = jnp.zeros((x1_ref.shape[0], Dh), F32)
    routed_hi = jnp.zeros((x1_ref.shape[0], Dh), F32)
    for k in range(TOP_K):
        lo, hi = _load_packed(yg_ref.at[k])
        routed_lo = routed_lo + lo * gate[:, k:k + 1]
        routed_hi = routed_hi + hi * gate[:, k:k + 1]
    h2 = jnp.concatenate(_load_packed(h2_ref), axis=-1).astype(BF16)
    a = jnp.dot(h2, ws1_ref[...], preferred_element_type=F32)
    b = jnp.dot(h2, ws3_ref[...], preferred_element_type=F32)
    shared = jnp.dot((_silu(a) * b).astype(BF16), ws2_ref[...], preferred_element_type=F32)
    y2 = jnp.concatenate([routed_lo, routed_hi], axis=-1) + shared
    o_ref[...] = _layer_norm(DN_ALPHA * x1_ref[...] + g2_ref[0] * y2, lng_ref[...], lnb_ref[...])


def _final(x1, h2, yg, gate, g2, ws1, ws3, ws2, ln2g, ln2b, seq):
    T, D = x1.shape
    tm = FINAL_TM
    tiles_per_seq = seq // tm
    F = ws1.shape[-1]
    tok = pl.BlockSpec((tm, D), lambda i: (i, 0))
    row = pl.BlockSpec((1, D), lambda i: (0, 0))
    return pl.pallas_call(
        _final_kernel,
        out_shape=jax.ShapeDtypeStruct((T, D), F32),
        grid=(T // tm,),
        in_specs=[tok, pl.BlockSpec((2, tm, D // 4), lambda i: (0, i, 0)),
                  pl.BlockSpec((TOP_K, 2, tm, D // 4), lambda i: (0, 0, i, 0)),
                  pl.BlockSpec((tm, TOP_K), lambda i: (i, 0)),
                  pl.BlockSpec((1, 1, D), lambda i: (i // tiles_per_seq, 0, 0)),
                  pl.BlockSpec((D, F), lambda i: (0, 0)),
                  pl.BlockSpec((D, F), lambda i: (0, 0)),
                  pl.BlockSpec((F, D), lambda i: (0, 0)),
                  row, row],
        out_specs=tok,
        compiler_params=_cparams("arbitrary"),
        name="moe_combine_final",
    )(x1, h2, yg, gate, g2, ws1, ws3, ws2, ln2g, ln2b)


def kernel(x, c, ada_w, ada_b, w_in, b_in, lambda_q1, lambda_k1, lambda_q2, lambda_k2, subln_g, w_attn_o, conv_w, conv_b, conv_ln_g, conv_ln_b, conv_pw_w, conv_pw_b, w_out, ln1_g, ln1_b, router_w, router_bias, w1, w3, w2, ws1, ws3, ws2, ln2_g, ln2_b):
    B, S, D = x.shape
    T = B * S
    l = 0
    x2 = x.reshape(T, D)
    row = lambda a: a.reshape(1, -1)

    mod = _ada_modulation(c, ada_w[l], ada_b[l])
    sh1, sc1, g1, sh2, sc2, g2 = [m.reshape(B, 1, D) for m in jnp.split(mod, 6, axis=-1)]

    qt, k, vt, u, ga, gb = _in_projection(x2, sc1, sh1, w_in[l].astype(BF16), row(b_in[l]), S)
    o = _diff_attention(qt, k, vt, row(lambda_q1[l]), row(lambda_k1[l]), row(lambda_q2[l]),
                        row(lambda_k2[l]), row(subln_g[l]), B, S)

    wr = router_w[l].T
    wr_hi = wr.astype(BF16)
    wr_lo = (wr - wr_hi.astype(F32)).astype(BF16)
    x1, h2, logits = _mixer_epilogue(
        x2, o, _causal_conv(u, conv_w[l], S), ga, gb, g1, sc2, sh2, w_attn_o[l].astype(BF16),
        row(conv_b[l]),
        row(conv_ln_g[l]), row(conv_ln_b[l]), conv_pw_w[l].astype(BF16), row(conv_pw_b[l]),
        w_out[l].astype(BF16), row(ln1_g[l]), row(ln1_b[l]), wr_hi, wr_lo, S)

    idx, gate, counts = _route(logits, router_bias[l])

    blk = EXPERT_BLK
    cnt = counts.reshape(N_EXPERTS).astype(jnp.int32)
    padded = (cnt + blk - 1) // blk * blk
    pad_end = jnp.cumsum(padded)
    pad_start = pad_end - padded
    P = T * TOP_K + N_EXPERTS * blk
    n_blocks = P // blk
    blk_first_row = jnp.arange(n_blocks, dtype=jnp.int32) * blk
    blk_expert = jnp.minimum(
        jnp.sum((pad_end[None, :] <= blk_first_row[:, None]).astype(jnp.int32), axis=1),
        N_EXPERTS - 1)
    n_used = (pad_end[-1:] // blk).astype(jnp.int32)

    pos = _assign_slots(idx, pad_start.astype(F32).reshape(N_EXPERTS, 1))

    Dq = D // 4
    half_base = (jnp.arange(2, dtype=jnp.int32) * P)[None, :, None]
    slots = pos[:, None, :] + half_base
    xs = _sc_dispatch(h2.reshape(2 * T, Dq), slots.reshape(TOP_K, 2 * T), 2 * P)
    ys = _routed_experts(xs.reshape(2, P, Dq), blk_expert, n_used, w1[l], w3[l], w2[l])
    yg = _sc_gather(ys.reshape(2 * P, Dq), slots.reshape(1, TOP_K * 2 * T))
    yg = yg.reshape(TOP_K, 2, T, Dq)

    out = _final(x1, h2, yg, gate.T, g2, ws1[l].astype(BF16), ws3[l].astype(BF16),
                 ws2[l].astype(BF16), row(ln2_g[l]), row(ln2_b[l]), S)
    return out.reshape(B, S, D)
```

```python
import math
from functools import partial

import jax
import jax.numpy as jnp
from jax import lax
from jax.experimental import pallas as pl
from jax.experimental.pallas import tpu as pltpu
from jax.experimental.pallas import tpu_sc as plsc

D_MODEL = 1024
N_HEADS = 8
HEAD_DIM = 64
V_DIM = 128
CONV_K = 31
N_EXPERTS = 256
TOP_K = 8
N_GROUPS = 8
GROUP_SIZE = N_EXPERTS // N_GROUPS
TOPK_GROUPS = 4
ROUTED_SCALE = 2.5
LN_EPS = 1e-5
RMS_EPS = 1e-5
DEPTH = 1
DN_ALPHA = (2.0 * DEPTH) ** 0.25
LAMBDA_INIT = 0.8 - 0.6 * math.exp(-0.3 * 0)
Q_SCALE = HEAD_DIM ** -0.5 * math.log2(math.e)

LANES = 128
SUBLANES = 8
BF16_ROWS = 16
VMEM_LIMIT_BYTES = 56 * 1024 * 1024

PROJ_TM = 512
ATTN_TQ = 512
ATTN_QG = 512
ATTN_KC = 256
ONES_ROWS = 16
MIX_TM = 512
MIX_SUB = 512
CONV_HALO = 32
CONV_ROWS = 64
CONV_TM = 512
CONV_SUB = 256
ROUTE_TM = 512
EXPERT_BLK = 512
FINAL_TM = 512
SC_WINDOW = 128

F32 = jnp.float32
BF16 = jnp.bfloat16
U32 = jnp.uint32
NEG_INF = float("-inf")


def _cparams(*sem):
    return pltpu.CompilerParams(dimension_semantics=sem, vmem_limit_bytes=VMEM_LIMIT_BYTES)


def _sigmoid(x):
    return 1.0 / (1.0 + jnp.exp(-x))


def _silu(x):
    return x * _sigmoid(x)


BF16_BITS = 16
HIGH_HALF_WORD = 0xFFFF0000


def _pack_halves(x):
    half = x.shape[-1] // 2
    bits = lambda v: lax.bitcast_convert_type(v.astype(BF16).astype(F32), U32)
    return (bits(x[:, :half]) >> BF16_BITS) | (bits(x[:, half:]) & jnp.uint32(HIGH_HALF_WORD))


def _unpack_halves(w):
    lo = lax.bitcast_convert_type(w << BF16_BITS, F32)
    hi = lax.bitcast_convert_type(w & jnp.uint32(HIGH_HALF_WORD), F32)
    return lo, hi


def _store_packed(ref, x):
    w = _pack_halves(x)
    q = w.shape[-1] // 2
    ref[0] = w[:, :q]
    ref[1] = w[:, q:]


def _load_packed(ref):
    return _unpack_halves(jnp.concatenate([ref[0], ref[1]], axis=-1))


def _layer_norm(r, g, b):
    mu = jnp.mean(r, axis=-1, keepdims=True)
    d = r - mu
    var = jnp.mean(d * d, axis=-1, keepdims=True)
    return d * lax.rsqrt(var + LN_EPS) * g + b


def _ada_kernel(c_ref, w_ref, b_ref, o_ref):
    a = _silu(c_ref[...])
    o_ref[...] = jnp.dot(a, w_ref[...], preferred_element_type=F32,
                         precision=lax.Precision.HIGHEST) + b_ref[...]


def _ada_modulation(c, w, b):
    B, D = c.shape
    n = w.shape[1] // D
    return pl.pallas_call(
        _ada_kernel,
        out_shape=jax.ShapeDtypeStruct((B, n * D), F32),
        grid=(n,),
        in_specs=[pl.BlockSpec((B, D), lambda j: (0, 0)),
                  pl.BlockSpec((D, D), lambda j: (0, j)),
                  pl.BlockSpec((1, D), lambda j: (0, j))],
        out_specs=pl.BlockSpec((B, D), lambda j: (0, j)),
        compiler_params=_cparams("arbitrary"),
        name="ada_modulation",
    )(c, w, b.reshape(1, n * D))


def _inproj_kernel(x_ref, sc_ref, sh_ref, w_ref, b_ref,
                   qt_ref, k_ref, vt_ref, u_ref, ga_ref, gb_ref):
    D = D_MODEL
    h = (x_ref[...] * (1.0 + sc_ref[0]) + sh_ref[0]).astype(BF16)

    def proj(j):
        return (jnp.dot(h, w_ref[:, j * D:(j + 1) * D], preferred_element_type=F32)
                + b_ref[:, j * D:(j + 1) * D])

    qt_ref[...] = (proj(0) * Q_SCALE).T.astype(BF16)
    k_ref[...] = proj(1).astype(BF16)
    vt_ref[...] = proj(2).T.astype(BF16)
    u_ref[...] = (proj(3) * _sigmoid(proj(4))).astype(BF16)
    ga_ref[...] = _sigmoid(proj(5)).astype(BF16)
    gb_ref[...] = _sigmoid(proj(6)).astype(BF16)


def _in_projection(x2, sc1, sh1, w_in, b_in, seq):
    T, D = x2.shape
    tm = PROJ_TM
    tiles_per_seq = seq // tm
    ncols = w_in.shape[1]
    tok = pl.BlockSpec((tm, D), lambda i: (i, 0))
    mod = pl.BlockSpec((1, 1, D), lambda i: (i // tiles_per_seq, 0, 0))
    tok_t = pl.BlockSpec((D, tm), lambda i: (0, i))
    out = jax.ShapeDtypeStruct((T, D), BF16)
    out_t = jax.ShapeDtypeStruct((D, T), BF16)
    return pl.pallas_call(
        _inproj_kernel,
        out_shape=[out_t, out, out_t, out, out, out],
        grid=(T // tm,),
        in_specs=[tok, mod, mod,
                  pl.BlockSpec((D, ncols), lambda i: (0, 0)),
                  pl.BlockSpec((1, ncols), lambda i: (0, 0))],
        out_specs=[tok_t, tok, tok_t, tok, tok, tok],
        compiler_params=_cparams("arbitrary"),
        name="in_projection",
    )(x2, sc1, sh1, w_in, b_in)


def _attn_tasks(nq):
    tasks = [(qi, j, int(j == qi)) for qi in range(nq) for j in range(qi + 1)]
    if len(tasks) % 2:
        raise ValueError("the two-slot task loop needs an even number of tile pairs")
    qis, js, lasts = zip(*(tasks + [(0, 0, 0)]))
    as_i32 = lambda v: jnp.asarray(v, jnp.int32)
    return as_i32(qis), as_i32(js), as_i32(lasts), len(tasks)


def _attn_kernel(tqi_ref, tj_ref, tlast_ref, lq1_ref, lk1_ref, lq2_ref, lk2_ref, g_ref,
                 qt_ref, k_ref, vt_ref, o_ref, q2_ref, vx_ref, *scratch, n_tasks):
    tq = ATTN_TQ
    qg = ATTN_QG
    if tq % qg:
        raise ValueError("a query group must not span the two softmax maps of a tile")
    ng = 2 * tq // qg
    groups = [scratch[i * ng:(i + 1) * ng] for i in range(7)]
    s_slots, x_slots = groups[0:2], groups[2:4]
    m_refs, l_refs, acc_refs = groups[4:]
    seq = k_ref.shape[0]

    vx_ref[:V_DIM, :] = vt_ref[...]
    vx_ref[V_DIM:, :] = jnp.ones((ONES_ROWS, seq), BF16)

    lam = (jnp.exp(jnp.sum(lq1_ref[...] * lk1_ref[...], axis=-1, keepdims=True))
           - jnp.exp(jnp.sum(lq2_ref[...] * lk2_ref[...], axis=-1, keepdims=True))
           + LAMBDA_INIT)

    for i in range(seq // tq):
        qt = qt_ref[:, i * tq:(i + 1) * tq]
        feat = lax.broadcasted_iota(jnp.int32, qt.shape, 0)
        zero = jnp.zeros_like(qt)
        q2_ref[:, 2 * tq * i:2 * tq * i + tq] = jnp.where(feat < HEAD_DIM, qt, zero)
        q2_ref[:, 2 * tq * i + tq:2 * tq * (i + 1)] = jnp.where(feat >= HEAD_DIM, qt, zero)

    def scores(qi, j, slot, g):
        start = pl.multiple_of(j * tq, tq)
        col = pl.multiple_of(qi * (2 * tq) + g * qg, qg)
        s = jnp.dot(k_ref[pl.ds(start, tq), :], q2_ref[:, pl.ds(col, qg)],
                    preferred_element_type=F32)
        s_slots[slot][g][...] = s
        x_slots[slot][g][...] = jnp.max(s, axis=0, keepdims=True)

    def update(j, slot, g, masked):
        start = pl.multiple_of(j * tq, tq)
        s_ref = s_slots[slot][g]

        def load(k0):
            s = s_ref[k0:k0 + ATTN_KC, :]
            if masked:
                key = lax.broadcasted_iota(jnp.int32, s.shape, 0) + k0
                qry = lax.broadcasted_iota(jnp.int32, s.shape, 1) + (g * qg) % tq
                s = jnp.where(key <= qry, s, NEG_INF)
            return s

        m_prev = m_refs[g][...]
        if masked:
            tile_max = jnp.max(load(0), axis=0, keepdims=True)
            for k0 in range(ATTN_KC, tq, ATTN_KC):
                tile_max = jnp.maximum(tile_max, jnp.max(load(k0), axis=0, keepdims=True))
        else:
            tile_max = x_slots[slot][g][...]
        m_new = jnp.maximum(m_prev, tile_max)
        alpha = jnp.exp2(m_prev - m_new)
        m_refs[g][...] = m_new
        pv = None
        for k0 in range(0, tq, ATTN_KC):
            p = jnp.exp2((load(k0) - m_new).astype(BF16))
            keys = pl.ds(pl.multiple_of(start + k0, ATTN_KC), ATTN_KC)
            part = jnp.dot(vx_ref[:, keys], p, preferred_element_type=F32)
            pv = part if pv is None else pv + part
        acc_refs[g][...] = alpha * acc_refs[g][...] + pv[:V_DIM, :]
        l_refs[g][...] = alpha * l_refs[g][...] + pv[V_DIM:V_DIM + 1, :]

    def reset():
        for g in range(ng):
            m_refs[g][...] = jnp.full(m_refs[g].shape, NEG_INF, F32)
            l_refs[g][...] = jnp.zeros(l_refs[g].shape, F32)
            acc_refs[g][...] = jnp.zeros(acc_refs[g].shape, F32)

    def finish(qi):
        o = (jnp.concatenate([r[...] for r in acc_refs], axis=-1)
             / jnp.concatenate([r[...] for r in l_refs], axis=-1))
        o = o[:, :tq] - lam * o[:, tq:]
        o = o * lax.rsqrt(jnp.mean(o * o, axis=0, keepdims=True) + RMS_EPS) * g_ref[...]
        row = pl.multiple_of(qi * tq, tq)
        o_ref[pl.ds(row, tq), :] = (o * (1.0 - LAMBDA_INIT)).T.astype(BF16)

    def task(t, slot):
        qi, j = tqi_ref[t], tj_ref[t]
        nqi, nj = tqi_ref[t + 1], tj_ref[t + 1]

        def run(masked):
            for g in range(ng):
                scores(nqi, nj, 1 - slot, g)
                update(j, slot, g, masked)

        @pl.when(tlast_ref[t] == 0)
        def _():
            run(False)

        @pl.when(tlast_ref[t] == 1)
        def _():
            run(True)
            finish(qi)
            reset()

    reset()
    for g in range(ng):
        scores(0, 0, 0, g)

    def pair(i, carry):
        task(2 * i, 0)
        task(2 * i + 1, 1)
        return carry

    lax.fori_loop(0, n_tasks // 2, pair, 0)


def _diff_attention(qt, k, vt, lq1, lk1, lq2, lk2, subln_g, batch, seq):
    T, D = k.shape
    tq = ATTN_TQ
    nq = seq // tq
    tqi, tj, tlast, n_tasks = _attn_tasks(nq)
    lam_spec = pl.BlockSpec((1, HEAD_DIM), lambda b, h, *_: (0, 0))
    rows_spec = pl.BlockSpec((seq, V_DIM), lambda b, h, *_: (b, h))
    cols_spec = pl.BlockSpec((V_DIM, seq), lambda b, h, *_: (h, b))
    qg = ATTN_QG
    per_group = lambda shape, dtype: [pltpu.VMEM(shape, dtype) for _ in range(2 * tq // qg)]
    grid_spec = pltpu.PrefetchScalarGridSpec(
        num_scalar_prefetch=3,
        grid=(batch, N_HEADS),
        in_specs=[lam_spec, lam_spec, lam_spec, lam_spec,
                  pl.BlockSpec((V_DIM, 1), lambda b, h, *_: (0, 0)),
                  cols_spec, rows_spec, cols_spec],
        out_specs=rows_spec,
        scratch_shapes=(
            [pltpu.VMEM((V_DIM, 2 * seq), BF16),
             pltpu.VMEM((V_DIM + ONES_ROWS, seq), BF16)]
            + per_group((tq, qg), F32)
            + per_group((tq, qg), F32)
            + per_group((1, qg), F32)
            + per_group((1, qg), F32)
            + per_group((1, qg), F32)
            + per_group((1, qg), F32)
            + per_group((V_DIM, qg), F32)),
    )
    return pl.pallas_call(
        partial(_attn_kernel, n_tasks=n_tasks),
        out_shape=jax.ShapeDtypeStruct((T, D), BF16),
        grid_spec=grid_spec,
        compiler_params=_cparams("arbitrary", "arbitrary"),
        name="diff_attention",
    )(tqi, tj, tlast, lq1, lk1, lq2, lk2, subln_g.reshape(V_DIM, 1), qt, k, vt)


def _conv_kernel(u_ref, up_ref, cw_ref, o_ref, ubuf_ref, ush_ref, ub_ref, *, tiles_per_seq):
    tm = CONV_TM
    first = (pl.program_id(0) % tiles_per_seq) == 0
    prev = up_ref[...].astype(F32)
    ubuf_ref[0:CONV_HALO, :] = jnp.where(first, jnp.zeros_like(prev), prev)
    ubuf_ref[CONV_HALO:, :] = u_ref[...].astype(F32)
    base = CONV_HALO - (CONV_K - 1)
    n_shift = CONV_SUB + CONV_HALO - SUBLANES
    for t0 in range(0, tm, CONV_SUB):
        for r in range(1, SUBLANES):
            ush_ref[r - 1] = ubuf_ref[t0 + r:t0 + r + n_shift, :]
        for q in range(BF16_ROWS):
            r, up = q % SUBLANES, SUBLANES * (q // SUBLANES)
            n = ub_ref.shape[2] if q == 0 else n_shift - SUBLANES
            for c in range(D_MODEL // LANES):
                cl = slice(c * LANES, (c + 1) * LANES)
                src = (ubuf_ref[t0 + up:t0 + up + n, cl] if r == 0
                       else ush_ref[r - 1, up:up + n, cl])
                ub_ref[q, c, 0:n, :] = src.astype(BF16)
        for c in range(D_MODEL // LANES):
            cl = slice(c * LANES, (c + 1) * LANES)
            for r0 in range(0, CONV_SUB, CONV_ROWS):
                acc = jnp.zeros((CONV_ROWS // BF16_ROWS, BF16_ROWS, LANES), F32)
                for j in range(CONV_K):
                    b, q = divmod(base + j, BF16_ROWS)
                    lo = BF16_ROWS * b + r0
                    win = ub_ref[q, c, lo:lo + CONV_ROWS, :].astype(F32)
                    w = cw_ref[j, c].astype(F32)
                    acc = acc + win.reshape(acc.shape) * w[None]
                o_ref[t0 + r0:t0 + r0 + CONV_ROWS, cl] = acc.reshape(CONV_ROWS, LANES).astype(BF16)


def _causal_conv(u, conv_w, seq):
    T, D = u.shape
    tm = CONV_TM
    tiles_per_seq = seq // tm
    halo_per_tile = tm // CONV_HALO
    taps = jnp.broadcast_to(conv_w.astype(BF16).reshape(CONV_K, D // LANES, 1, LANES),
                            (CONV_K, D // LANES, BF16_ROWS, LANES))
    tok = pl.BlockSpec((tm, D), lambda i: (i, 0))
    return pl.pallas_call(
        partial(_conv_kernel, tiles_per_seq=tiles_per_seq),
        out_shape=jax.ShapeDtypeStruct((T, D), BF16),
        grid=(T // tm,),
        in_specs=[tok,
                  pl.BlockSpec((CONV_HALO, D), lambda i: (jnp.maximum(i * halo_per_tile - 1, 0), 0)),
                  pl.BlockSpec((CONV_K, D // LANES, BF16_ROWS, LANES), lambda i: (0, 0, 0, 0))],
        out_specs=tok,
        scratch_shapes=[pltpu.VMEM((CONV_HALO + tm, D), F32),
                        pltpu.VMEM((SUBLANES - 1, CONV_SUB + CONV_HALO - SUBLANES, D), F32),
                        pltpu.VMEM((BF16_ROWS, D // LANES, CONV_SUB + CONV_HALO, LANES), BF16)],
        compiler_params=_cparams("arbitrary"),
        name="causal_conv",
    )(u, u, taps)


def _mixer_kernel(x_ref, o_ref, uc_ref, ga_ref, gb_ref, g1_ref, sc2_ref, sh2_ref,
                  wao_ref, cb_ref, clg_ref, clb_ref, wpw_ref, bpw_ref, wout_ref,
                  ln1g_ref, ln1b_ref, wrh_ref, wrl_ref,
                  x1_ref, h2_ref, lg_ref):
    tm = MIX_TM
    for t0 in range(0, tm, MIX_SUB):
        rows = slice(t0, t0 + MIX_SUB)
        conv = uc_ref[rows, :].astype(F32)
        yc = _silu(_layer_norm(conv + cb_ref[...], clg_ref[...], clb_ref[...]))
        y_conv = (jnp.dot(yc.astype(BF16), wpw_ref[...], preferred_element_type=F32)
                  + bpw_ref[...])

        y_attn = jnp.dot(o_ref[rows, :], wao_ref[...], preferred_element_type=F32)
        mix = ga_ref[rows, :].astype(F32) * y_attn + gb_ref[rows, :].astype(F32) * y_conv
        y = jnp.dot(mix.astype(BF16), wout_ref[...], preferred_element_type=F32)
        x1 = _layer_norm(DN_ALPHA * x_ref[rows, :] + g1_ref[0] * y, ln1g_ref[...], ln1b_ref[...])
        x1_ref[rows, :] = x1

        h2 = x1 * (1.0 + sc2_ref[0]) + sh2_ref[0]
        h2_hi = h2.astype(BF16)
        _store_packed(h2_ref.at[:, rows, :], h2)
        h2_lo = (h2 - h2_hi.astype(F32)).astype(BF16)
        nt = lambda w, a: lax.dot_general(w, a, (((1,), (1,)), ((), ())),
                                          preferred_element_type=F32)
        lg_ref[:, rows] = (nt(wrh_ref[...], h2_hi)
                           + (nt(wrl_ref[...], h2_hi) + nt(wrh_ref[...], h2_lo)))


def _mixer_epilogue(x2, o, uc, ga, gb, g1, sc2, sh2, wao, cb, clg, clb, wpw, bpw, wout,
                    ln1g, ln1b, wr_hi, wr_lo, seq):
    T, D = x2.shape
    tm = MIX_TM
    tiles_per_seq = seq // tm
    tok = pl.BlockSpec((tm, D), lambda i: (i, 0))
    mod = pl.BlockSpec((1, 1, D), lambda i: (i // tiles_per_seq, 0, 0))
    row = pl.BlockSpec((1, D), lambda i: (0, 0))
    mat = pl.BlockSpec((D, D), lambda i: (0, 0), pipeline_mode=pl.Buffered(1))
    rw = pl.BlockSpec((N_EXPERTS, D), lambda i: (0, 0), pipeline_mode=pl.Buffered(1))
    return pl.pallas_call(
        _mixer_kernel,
        out_shape=[jax.ShapeDtypeStruct((T, D), F32),
                   jax.ShapeDtypeStruct((2, T, D // 4), U32),
                   jax.ShapeDtypeStruct((N_EXPERTS, T), F32)],
        grid=(T // tm,),
        in_specs=[tok, tok, tok, tok, tok, mod, mod, mod,
                  mat, row, row, row, mat, row, mat,
                  row, row, rw, rw],
        out_specs=[tok, pl.BlockSpec((2, tm, D // 4), lambda i: (0, i, 0)),
                   pl.BlockSpec((N_EXPERTS, tm), lambda i: (0, i))],
        compiler_params=_cparams("arbitrary"),
        name="mixer_epilogue",
    )(x2, o, uc, ga, gb, g1, sc2, sh2, wao, cb, clg, clb, wpw, bpw, wout,
      ln1g, ln1b, wr_hi, wr_lo)


def _first_argmax(v, eid):
    m = jnp.max(v, axis=0, keepdims=True)
    i = jnp.min(jnp.where(v == m, eid, float(N_EXPERTS)), axis=0, keepdims=True)
    return m, i


def _route_kernel(lg_ref, rb_ref, idx_ref, gate_ref, cnt_ref):
    tm = ROUTE_TM
    scores = _sigmoid(lg_ref[...])
    choice = scores + rb_ref[...]
    eid = lax.broadcasted_iota(jnp.int32, (N_EXPERTS, tm), 0).astype(F32)
    rows = lambda a, g: a[g * GROUP_SIZE:(g + 1) * GROUP_SIZE, :]

    gs = []
    for g in range(N_GROUPS):
        cg = rows(choice, g)
        eg = (lax.broadcasted_iota(jnp.int32, cg.shape, 0) + g * GROUP_SIZE).astype(F32)
        m1, i1 = _first_argmax(cg, eg)
        m2 = jnp.max(jnp.where(eg == i1, NEG_INF, cg), axis=0, keepdims=True)
        gs.append(jnp.broadcast_to(m1 + m2, (SUBLANES, tm)))

    kept = []
    for g in range(N_GROUPS):
        ahead = jnp.zeros((SUBLANES, tm), F32)
        for o in range(N_GROUPS):
            if o == g:
                continue
            beats = (gs[o] >= gs[g]) if o < g else (gs[o] > gs[g])
            ahead = ahead + jnp.where(beats, 1.0, 0.0)
        cap = jnp.where(ahead < TOPK_GROUPS, float("inf"), NEG_INF)
        for r in range(g * GROUP_SIZE, (g + 1) * GROUP_SIZE, SUBLANES):
            kept.append(jnp.minimum(choice[r:r + SUBLANES, :], cap))
    masked = jnp.concatenate(kept, axis=0)

    sel = jnp.zeros((N_EXPERTS, tm), F32)
    total = jnp.zeros((1, tm), F32)
    picked = []
    for k in range(TOP_K):
        _, i = _first_argmax(masked, eid)
        hit = eid == i
        sc = jnp.sum(jnp.where(hit, scores, 0.0), axis=0, keepdims=True)
        masked = jnp.where(hit, NEG_INF, masked)
        sel = sel + hit.astype(F32)
        total = total + sc
        idx_ref[k:k + 1, :] = i.astype(jnp.int32)
        picked.append(sc)
    for k in range(TOP_K):
        gate_ref[k:k + 1, :] = picked[k] / total * ROUTED_SCALE

    @pl.when(pl.program_id(0) == 0)
    def _():
        cnt_ref[...] = jnp.zeros_like(cnt_ref)
    cnt_ref[...] += jnp.sum(sel, axis=1, keepdims=True)


def _route(logits_t, router_bias):
    T = logits_t.shape[1]
    tm = ROUTE_TM
    return pl.pallas_call(
        _route_kernel,
        out_shape=[jax.ShapeDtypeStruct((TOP_K, T), jnp.int32),
                   jax.ShapeDtypeStruct((TOP_K, T), F32),
                   jax.ShapeDtypeStruct((N_EXPERTS, 1), F32)],
        grid=(T // tm,),
        in_specs=[pl.BlockSpec((N_EXPERTS, tm), lambda i: (0, i)),
                  pl.BlockSpec((N_EXPERTS, 1), lambda i: (0, 0))],
        out_specs=[pl.BlockSpec((TOP_K, tm), lambda i: (0, i)),
                   pl.BlockSpec((TOP_K, tm), lambda i: (0, i)),
                   pl.BlockSpec((N_EXPERTS, 1), lambda i: (0, 0))],
        compiler_params=_cparams("arbitrary"),
        name="route_topk",
    )(logits_t, router_bias.reshape(N_EXPERTS, 1))


def _slot_kernel(idx_ref, start_ref, pos_ref, carry_ref):
    tm = ROUTE_TM

    @pl.when(pl.program_id(0) == 0)
    def _():
        carry_ref[...] = jnp.zeros_like(carry_ref)

    idx = idx_ref[...]
    eid = lax.broadcasted_iota(jnp.int32, (N_EXPERTS, tm), 0)
    hits = [eid == idx[k:k + 1, :] for k in range(TOP_K)]
    sel = jnp.zeros((N_EXPERTS, tm), F32)
    for h in hits:
        sel = sel + h.astype(F32)
    r = lax.broadcasted_iota(jnp.int32, (tm, tm), 0)
    c = lax.broadcasted_iota(jnp.int32, (tm, tm), 1)
    earlier = (r < c).astype(BF16)
    rank = jnp.dot(sel.astype(BF16), earlier, preferred_element_type=F32)
    slot = rank + carry_ref[...] + start_ref[...]
    for k, h in enumerate(hits):
        pos_ref[k:k + 1, :] = jnp.sum(jnp.where(h, slot, 0.0), axis=0,
                                      keepdims=True).astype(jnp.int32)
    carry_ref[...] += jnp.sum(sel, axis=1, keepdims=True)


def _assign_slots(idx_t, pad_start):
    T = idx_t.shape[1]
    tm = ROUTE_TM
    return pl.pallas_call(
        _slot_kernel,
        out_shape=jax.ShapeDtypeStruct((TOP_K, T), jnp.int32),
        grid=(T // tm,),
        in_specs=[pl.BlockSpec((TOP_K, tm), lambda i: (0, i)),
                  pl.BlockSpec((N_EXPERTS, 1), lambda i: (0, 0))],
        out_specs=pl.BlockSpec((TOP_K, tm), lambda i: (0, i)),
        scratch_shapes=[pltpu.VMEM((N_EXPERTS, 1), F32)],
        compiler_params=_cparams("arbitrary"),
        name="assign_slots",
    )(idx_t, pad_start)


def _expert_kernel(be_ref, nu_ref, first_ref, slot_ref, nxt_ref, xs_ref, w1_hbm, w3_hbm, w2_hbm,
                   ys_ref, w1_buf, w3_buf, w2_buf, sem):
    b = pl.program_id(0)

    def weight_copies(expert, slot):
        pairs = ((w1_hbm, w1_buf), (w3_hbm, w3_buf), (w2_hbm, w2_buf))
        return [pltpu.make_async_copy(hbm.at[expert], buf.at[slot], sem.at[slot, i])
                for i, (hbm, buf) in enumerate(pairs)]

    @pl.when(b < nu_ref[0])
    def _():
        slot = slot_ref[b]

        @pl.when(b == 0)
        def _():
            for c in weight_copies(be_ref[0], 0):
                c.start()

        @pl.when(first_ref[b] == 1)
        def _():
            for c in weight_copies(be_ref[b], slot):
                c.wait()

            @pl.when(nxt_ref[b] >= 0)
            def _():
                for c in weight_copies(nxt_ref[b], 1 - slot):
                    c.start()

        lo, hi = _load_packed(xs_ref)
        half = lo.shape[-1]
        mm = lambda w: (jnp.dot(lo, w[slot, :half, :], preferred_element_type=F32)
                        + jnp.dot(hi, w[slot, half:, :], preferred_element_type=F32))
        hb = _silu(mm(w1_buf)) * mm(w3_buf)
        _store_packed(ys_ref, jnp.dot(hb, w2_buf[slot], preferred_element_type=F32))

    @pl.when(b >= nu_ref[0])
    def _():
        ys_ref[...] = jnp.zeros_like(ys_ref)


def _routed_experts(xs, blk_expert, n_used, w1, w3, w2):
    _, P, Dq = xs.shape
    blk = EXPERT_BLK
    n_blocks = P // blk
    D, F = w1.shape[-2:]

    b_ids = jnp.arange(n_blocks, dtype=jnp.int32)
    used = b_ids < n_used[0]
    prev = jnp.concatenate([jnp.full((1,), -1, jnp.int32), blk_expert[:-1]])
    first = (used & (blk_expert != prev)).astype(jnp.int32)
    slot = (jnp.cumsum(first) - 1) % 2
    start_at = jnp.where(first == 1, b_ids, n_blocks)
    later = jnp.concatenate([start_at[1:], jnp.full((1,), n_blocks, jnp.int32)])
    next_start = lax.cummin(later, axis=0, reverse=True)
    nxt = jnp.where(next_start < n_blocks,
                    blk_expert[jnp.minimum(next_start, n_blocks - 1)], -1).astype(jnp.int32)

    row_blk = lambda b, be, nu, *_: (0, jnp.minimum(b, nu[0] - 1), 0)
    grid_spec = pltpu.PrefetchScalarGridSpec(
        num_scalar_prefetch=5,
        grid=(n_blocks,),
        in_specs=[pl.BlockSpec((2, blk, Dq), row_blk),
                  pl.BlockSpec(memory_space=pl.ANY),
                  pl.BlockSpec(memory_space=pl.ANY),
                  pl.BlockSpec(memory_space=pl.ANY)],
        out_specs=pl.BlockSpec((2, blk, Dq), lambda b, *_: (0, b, 0)),
        scratch_shapes=[pltpu.VMEM((2, D, F), F32), pltpu.VMEM((2, D, F), F32),
                        pltpu.VMEM((2, F, D), F32), pltpu.SemaphoreType.DMA((2, 3))],
    )
    return pl.pallas_call(
        _expert_kernel,
        out_shape=jax.ShapeDtypeStruct((2, P, Dq), U32),
        grid_spec=grid_spec,
        compiler_params=_cparams("arbitrary"),
        name="routed_experts",
    )(blk_expert, n_used, first, slot.astype(jnp.int32), nxt, xs, w1, w3, w2)


def _sc_mesh():
    return plsc.VectorSubcoreMesh(core_axis_name="c", subcore_axis_name="s")


def _sc_dispatch(rows, slots_t, n_slots):
    T, W = rows.shape
    K = slots_t.shape[0]
    win = SC_WINDOW

    @pl.kernel(out_type=jax.ShapeDtypeStruct((n_slots, W), rows.dtype), mesh=_sc_mesh(),
               scratch_types=[])
    def dispatch(x_hbm, i_hbm, o_hbm):
        def body(x_vmem, i_vmem):
            for k in range(K):
                pltpu.sync_copy(x_vmem, o_hbm.at[i_vmem.at[k]])

        pltpu.emit_pipeline(
            body,
            grid=(T // win,),
            in_specs=[pl.BlockSpec((win, W), lambda i: (i, 0)),
                      pl.BlockSpec((K, win), lambda i: (0, i))],
            out_specs=[],
            core_axis_name=("c", "s"),
            dimension_semantics=(pltpu.PARALLEL,),
        )(x_hbm, i_hbm)

    return dispatch(rows, slots_t)


def _sc_gather(table, idx):
    W = table.shape[1]
    N = idx.shape[1]
    win = SC_WINDOW

    @pl.kernel(out_type=jax.ShapeDtypeStruct((N, W), table.dtype), mesh=_sc_mesh(),
               scratch_types=[])
    def gather(x_hbm, i_hbm, o_hbm):
        def body(i_vmem, o_vmem):
            pltpu.sync_copy(x_hbm.at[i_vmem.at[0]], o_vmem)

        pltpu.emit_pipeline(
            body,
            grid=(N // win,),
            in_specs=[pl.BlockSpec((1, win), lambda i: (0, i))],
            out_specs=[pl.BlockSpec((win, W), lambda i: (i, 0))],
            core_axis_name=("c", "s"),
            dimension_semantics=(pltpu.PARALLEL,),
        )(i_hbm, o_hbm)

    return gather(table, idx)


def _final_kernel(x1_ref, h2_ref, yg_ref, gate_ref, g2_ref, ws1_ref, ws3_ref, ws2_ref,
                  lng_ref, lnb_ref, o_ref):
    Dh = D_MODEL // 2
    gate = gate_ref[...]
    routed_lo = jnp.zeros((x1_ref.shape[0], Dh), F32)
    routed_hi = jnp.zeros((x1_ref.shape[0], Dh), F32)
    for k in range(TOP_K):
        lo, hi = _load_packed(yg_ref.at[k])
        routed_lo = routed_lo + lo * gate[:, k:k + 1]
        routed_hi = routed_hi + hi * gate[:, k:k + 1]
    h2 = jnp.concatenate(_load_packed(h2_ref), axis=-1).astype(BF16)
    a = jnp.dot(h2, ws1_ref[...], preferred_element_type=F32)
    b = jnp.dot(h2, ws3_ref[...], preferred_element_type=F32)
    shared = jnp.dot((_silu(a) * b).astype(BF16), ws2_ref[...], preferred_element_type=F32)
    y2 = jnp.concatenate([routed_lo, routed_hi], axis=-1) + shared
    o_ref[...] = _layer_norm(DN_ALPHA * x1_ref[...] + g2_ref[0] * y2, lng_ref[...], lnb_ref[...])


def _final(x1, h2, yg, gate, g2, ws1, ws3, ws2, ln2g, ln2b, seq):
    T, D = x1.shape
    tm = FINAL_TM
    tiles_per_seq = seq // tm
    F = ws1.shape[-1]
    tok = pl.BlockSpec((tm, D), lambda i: (i, 0))
    row = pl.BlockSpec((1, D), lambda i: (0, 0))
    return pl.pallas_call(
        _final_kernel,
        out_shape=jax.ShapeDtypeStruct((T, D), F32),
        grid=(T // tm,),
        in_specs=[tok, pl.BlockSpec((2, tm, D // 4), lambda i: (0, i, 0)),
                  pl.BlockSpec((TOP_K, 2, tm, D // 4), lambda i: (0, 0, i, 0)),
                  pl.BlockSpec((tm, TOP_K), lambda i: (i, 0)),
                  pl.BlockSpec((1, 1, D), lambda i: (i // tiles_per_seq, 0, 0)),
                  pl.BlockSpec((D, F), lambda i: (0, 0)),
                  pl.BlockSpec((D, F), lambda i: (0, 0)),
                  pl.BlockSpec((F, D), lambda i: (0, 0)),
                  row, row],
        out_specs=tok,
        compiler_params=_cparams("arbitrary"),
        name="moe_combine_final",
    )(x1, h2, yg, gate, g2, ws1, ws3, ws2, ln2g, ln2b)


def kernel(x, c, ada_w, ada_b, w_in, b_in, lambda_q1, lambda_k1, lambda_q2, lambda_k2, subln_g, w_attn_o, conv_w, conv_b, conv_ln_g, conv_ln_b, conv_pw_w, conv_pw_b, w_out, ln1_g, ln1_b, router_w, router_bias, w1, w3, w2, ws1, ws3, ws2, ln2_g, ln2_b):
    B, S, D = x.shape
    T = B * S
    l = 0
    x2 = x.reshape(T, D)
    row = lambda a: a.reshape(1, -1)

    mod = _ada_modulation(c, ada_w[l], ada_b[l])
    sh1, sc1, g1, sh2, sc2, g2 = [m.reshape(B, 1, D) for m in jnp.split(mod, 6, axis=-1)]

    qt, k, vt, u, ga, gb = _in_projection(x2, sc1, sh1, w_in[l].astype(BF16), row(b_in[l]), S)
    o = _diff_attention(qt, k, vt, row(lambda_q1[l]), row(lambda_k1[l]), row(lambda_q2[l]),
                        row(lambda_k2[l]), row(subln_g[l]), B, S)

    wr = router_w[l].T
    wr_hi = wr.astype(BF16)
    wr_lo = (wr - wr_hi.astype(F32)).astype(BF16)
    x1, h2, logits = _mixer_epilogue(
        x2, o, _causal_conv(u, conv_w[l], S), ga, gb, g1, sc2, sh2, w_attn_o[l].astype(BF16),
        row(conv_b[l]),
        row(conv_ln_g[l]), row(conv_ln_b[l]), conv_pw_w[l].astype(BF16), row(conv_pw_b[l]),
        w_out[l].astype(BF16), row(ln1_g[l]), row(ln1_b[l]), wr_hi, wr_lo, S)

    idx, gate, counts = _route(logits, router_bias[l])

    blk = EXPERT_BLK
    cnt = counts.reshape(N_EXPERTS).astype(jnp.int32)
    padded = (cnt + blk - 1) // blk * blk
    pad_end = jnp.cumsum(padded)
    pad_start = pad_end - padded
    P = T * TOP_K + N_EXPERTS * blk
    n_blocks = P // blk
    blk_first_row = jnp.arange(n_blocks, dtype=jnp.int32) * blk
    blk_expert = jnp.minimum(
        jnp.sum((pad_end[None, :] <= blk_first_row[:, None]).astype(jnp.int32), axis=1),
        N_EXPERTS - 1)
    n_used = (pad_end[-1:] // blk).astype(jnp.int32)

    pos = _assign_slots(idx, pad_start.astype(F32).reshape(N_EXPERTS, 1))

    Dq = D // 4
    half_base = (jnp.arange(2, dtype=jnp.int32) * P)[None, :, None]
    slots = pos[:, None, :] + half_base
    xs = _sc_dispatch(h2.reshape(2 * T, Dq), slots.reshape(TOP_K, 2 * T), 2 * P)
    ys = _routed_experts(xs.reshape(2, P, Dq), blk_expert, n_used, w1[l], w3[l], w2[l])
    yg = _sc_gather(ys.reshape(2 * P, Dq), slots.reshape(1, TOP_K * 2 * T))
    yg = yg.reshape(TOP_K, 2, T, Dq)

    out = _final(x1, h2, yg, gate.T, g2, ws1[l].astype(BF16), ws3[l].astype(BF16),
                 ws2[l].astype(BF16), row(ln2_g[l]), row(ln2_b[l]), S)
    return out.reshape(B, S, D)
```

```python
import math
from functools import partial

import jax
import jax.numpy as jnp
from jax import lax
from jax.experimental import pallas as pl
from jax.experimental.pallas import tpu as pltpu
from jax.experimental.pallas import tpu_sc as plsc

D_MODEL = 1024
N_HEADS = 8
HEAD_DIM = 64
V_DIM = 128
CONV_K = 31
N_EXPERTS = 256
TOP_K = 8
N_GROUPS = 8
GROUP_SIZE = N_EXPERTS // N_GROUPS
TOPK_GROUPS = 4
ROUTED_SCALE = 2.5
LN_EPS = 1e-5
RMS_EPS = 1e-5
DEPTH = 1
DN_ALPHA = (2.0 * DEPTH) ** 0.25
LAMBDA_INIT = 0.8 - 0.6 * math.exp(-0.3 * 0)
Q_SCALE = HEAD_DIM ** -0.5 * math.log2(math.e)

LANES = 128
SUBLANES = 8
BF16_ROWS = 16
VMEM_LIMIT_BYTES = 56 * 1024 * 1024

PROJ_TM = 512
ATTN_TQ = 512
ATTN_QG = 512
ATTN_KC = 512
ONES_ROWS = 16
MIX_TM = 512
MIX_SUB = 512
CONV_HALO = 32
CONV_ROWS = 64
CONV_TM = 512
CONV_SUB = 256
ROUTE_TM = 512
EXPERT_BLK = 512
FINAL_TM = 512
SC_WINDOW = 128

F32 = jnp.float32
BF16 = jnp.bfloat16
U32 = jnp.uint32
NEG_INF = float("-inf")


def _cparams(*sem):
    return pltpu.CompilerParams(dimension_semantics=sem, vmem_limit_bytes=VMEM_LIMIT_BYTES)


def _sigmoid(x):
    return 1.0 / (1.0 + jnp.exp(-x))


def _silu(x):
    return x * _sigmoid(x)


BF16_BITS = 16
HIGH_HALF_WORD = 0xFFFF0000


def _pack_halves(x):
    half = x.shape[-1] // 2
    bits = lambda v: lax.bitcast_convert_type(v.astype(BF16).astype(F32), U32)
    return (bits(x[:, :half]) >> BF16_BITS) | (bits(x[:, half:]) & jnp.uint32(HIGH_HALF_WORD))


def _unpack_halves(w):
    lo = lax.bitcast_convert_type(w << BF16_BITS, F32)
    hi = lax.bitcast_convert_type(w & jnp.uint32(HIGH_HALF_WORD), F32)
    return lo, hi


def _store_packed(ref, x):
    w = _pack_halves(x)
    q = w.shape[-1] // 2
    ref[0] = w[:, :q]
    ref[1] = w[:, q:]


def _load_packed(ref):
    return _unpack_halves(jnp.concatenate([ref[0], ref[1]], axis=-1))


def _layer_norm(r, g, b):
    mu = jnp.mean(r, axis=-1, keepdims=True)
    d = r - mu
    var = jnp.mean(d * d, axis=-1, keepdims=True)
    return d * lax.rsqrt(var + LN_EPS) * g + b


def _ada_kernel(c_ref, w_ref, b_ref, o_ref):
    a = _silu(c_ref[...])
    o_ref[...] = jnp.dot(a, w_ref[...], preferred_element_type=F32,
                         precision=lax.Precision.HIGHEST) + b_ref[...]


def _ada_modulation(c, w, b):
    B, D = c.shape
    n = w.shape[1] // D
    return pl.pallas_call(
        _ada_kernel,
        out_shape=jax.ShapeDtypeStruct((B, n * D), F32),
        grid=(n,),
        in_specs=[pl.BlockSpec((B, D), lambda j: (0, 0)),
                  pl.BlockSpec((D, D), lambda j: (0, j)),
                  pl.BlockSpec((1, D), lambda j: (0, j))],
        out_specs=pl.BlockSpec((B, D), lambda j: (0, j)),
        compiler_params=_cparams("arbitrary"),
        name="ada_modulation",
    )(c, w, b.reshape(1, n * D))


def _inproj_kernel(x_ref, sc_ref, sh_ref, w_ref, b_ref,
                   qt_ref, k_ref, vt_ref, u_ref, ga_ref, gb_ref):
    D = D_MODEL
    h = (x_ref[...] * (1.0 + sc_ref[0]) + sh_ref[0]).astype(BF16)

    def proj(j):
        return (jnp.dot(h, w_ref[:, j * D:(j + 1) * D], preferred_element_type=F32)
                + b_ref[:, j * D:(j + 1) * D])

    qt_ref[...] = (proj(0) * Q_SCALE).T.astype(BF16)
    k_ref[...] = proj(1).astype(BF16)
    vt_ref[...] = proj(2).T.astype(BF16)
    u_ref[...] = (proj(3) * _sigmoid(proj(4))).astype(BF16)
    ga_ref[...] = _sigmoid(proj(5)).astype(BF16)
    gb_ref[...] = _sigmoid(proj(6)).astype(BF16)


def _in_projection(x2, sc1, sh1, w_in, b_in, seq):
    T, D = x2.shape
    tm = PROJ_TM
    tiles_per_seq = seq // tm
    ncols = w_in.shape[1]
    tok = pl.BlockSpec((tm, D), lambda i: (i, 0))
    mod = pl.BlockSpec((1, 1, D), lambda i: (i // tiles_per_seq, 0, 0))
    tok_t = pl.BlockSpec((D, tm), lambda i: (0, i))
    out = jax.ShapeDtypeStruct((T, D), BF16)
    out_t = jax.ShapeDtypeStruct((D, T), BF16)
    return pl.pallas_call(
        _inproj_kernel,
        out_shape=[out_t, out, out_t, out, out, out],
        grid=(T // tm,),
        in_specs=[tok, mod, mod,
                  pl.BlockSpec((D, ncols), lambda i: (0, 0)),
                  pl.BlockSpec((1, ncols), lambda i: (0, 0))],
        out_specs=[tok_t, tok, tok_t, tok, tok, tok],
        compiler_params=_cparams("arbitrary"),
        name="in_projection",
    )(x2, sc1, sh1, w_in, b_in)


def _attn_tasks(nq):
    tasks = [(qi, j, int(j == qi)) for qi in range(nq) for j in range(qi + 1)]
    if len(tasks) % 2:
        raise ValueError("the two-slot task loop needs an even number of tile pairs")
    qis, js, lasts = zip(*(tasks + [(0, 0, 0)]))
    as_i32 = lambda v: jnp.asarray(v, jnp.int32)
    return as_i32(qis), as_i32(js), as_i32(lasts), len(tasks)


def _attn_kernel(tqi_ref, tj_ref, tlast_ref, lq1_ref, lk1_ref, lq2_ref, lk2_ref, g_ref,
                 qt_ref, k_ref, vt_ref, o_ref, q2_ref, vx_ref, *scratch, n_tasks):
    tq = ATTN_TQ
    qg = ATTN_QG
    if tq % qg:
        raise ValueError("a query group must not span the two softmax maps of a tile")
    ng = 2 * tq // qg
    groups = [scratch[i * ng:(i + 1) * ng] for i in range(7)]
    s_slots, x_slots = groups[0:2], groups[2:4]
    m_refs, l_refs, acc_refs = groups[4:]
    seq = k_ref.shape[0]

    vx_ref[:V_DIM, :] = vt_ref[...]
    vx_ref[V_DIM:, :] = jnp.ones((ONES_ROWS, seq), BF16)

    lam = (jnp.exp(jnp.sum(lq1_ref[...] * lk1_ref[...], axis=-1, keepdims=True))
           - jnp.exp(jnp.sum(lq2_ref[...] * lk2_ref[...], axis=-1, keepdims=True))
           + LAMBDA_INIT)

    for i in range(seq // tq):
        qt = qt_ref[:, i * tq:(i + 1) * tq]
        feat = lax.broadcasted_iota(jnp.int32, qt.shape, 0)
        zero = jnp.zeros_like(qt)
        q2_ref[:, 2 * tq * i:2 * tq * i + tq] = jnp.where(feat < HEAD_DIM, qt, zero)
        q2_ref[:, 2 * tq * i + tq:2 * tq * (i + 1)] = jnp.where(feat >= HEAD_DIM, qt, zero)

    def scores(qi, j, slot, g):
        start = pl.multiple_of(j * tq, tq)
        col = pl.multiple_of(qi * (2 * tq) + g * qg, qg)
        s = jnp.dot(k_ref[pl.ds(start, tq), :], q2_ref[:, pl.ds(col, qg)],
                    preferred_element_type=F32)
        s_slots[slot][g][...] = s
        x_slots[slot][g][...] = jnp.max(s, axis=0, keepdims=True)

    def update(j, slot, g, masked):
        start = pl.multiple_of(j * tq, tq)
        s_ref = s_slots[slot][g]

        def load(k0):
            s = s_ref[k0:k0 + ATTN_KC, :]
            if masked:
                key = lax.broadcasted_iota(jnp.int32, s.shape, 0) + k0
                qry = lax.broadcasted_iota(jnp.int32, s.shape, 1) + (g * qg) % tq
                s = jnp.where(key <= qry, s, NEG_INF)
            return s

        m_prev = m_refs[g][...]
        if masked:
            tile_max = jnp.max(load(0), axis=0, keepdims=True)
            for k0 in range(ATTN_KC, tq, ATTN_KC):
                tile_max = jnp.maximum(tile_max, jnp.max(load(k0), axis=0, keepdims=True))
        else:
            tile_max = x_slots[slot][g][...]
        m_new = jnp.maximum(m_prev, tile_max)
        alpha = jnp.exp2(m_prev - m_new)
        m_refs[g][...] = m_new
        pv = None
        for k0 in range(0, tq, ATTN_KC):
            p = jnp.exp2((load(k0) - m_new).astype(BF16))
            keys = pl.ds(pl.multiple_of(start + k0, ATTN_KC), ATTN_KC)
            part = jnp.dot(vx_ref[:, keys], p, preferred_element_type=F32)
            pv = part if pv is None else pv + part
        acc_refs[g][...] = alpha * acc_refs[g][...] + pv[:V_DIM, :]
        l_refs[g][...] = alpha * l_refs[g][...] + pv[V_DIM:V_DIM + 1, :]

    def reset():
        for g in range(ng):
            m_refs[g][...] = jnp.full(m_refs[g].shape, NEG_INF, F32)
            l_refs[g][...] = jnp.zeros(l_refs[g].shape, F32)
            acc_refs[g][...] = jnp.zeros(acc_refs[g].shape, F32)

    def finish(qi):
        o = (jnp.concatenate([r[...] for r in acc_refs], axis=-1)
             / jnp.concatenate([r[...] for r in l_refs], axis=-1))
        o = o[:, :tq] - lam * o[:, tq:]
        o = o * lax.rsqrt(jnp.mean(o * o, axis=0, keepdims=True) + RMS_EPS) * g_ref[...]
        row = pl.multiple_of(qi * tq, tq)
        o_ref[pl.ds(row, tq), :] = (o * (1.0 - LAMBDA_INIT)).T.astype(BF16)

    def task(t, slot):
        qi, j = tqi_ref[t], tj_ref[t]
        nqi, nj = tqi_ref[t + 1], tj_ref[t + 1]

        def run(masked):
            for g in range(ng):
                scores(nqi, nj, 1 - slot, g)
                update(j, slot, g, masked)

        @pl.when(tlast_ref[t] == 0)
        def _():
            run(False)

        @pl.when(tlast_ref[t] == 1)
        def _():
            run(True)
            finish(qi)
            reset()

    reset()
    for g in range(ng):
        scores(0, 0, 0, g)

    def pair(i, carry):
        task(2 * i, 0)
        task(2 * i + 1, 1)
        return carry

    lax.fori_loop(0, n_tasks // 2, pair, 0)


def _diff_attention(qt, k, vt, lq1, lk1, lq2, lk2, subln_g, batch, seq):
    T, D = k.shape
    tq = ATTN_TQ
    nq = seq // tq
    tqi, tj, tlast, n_tasks = _attn_tasks(nq)
    lam_spec = pl.BlockSpec((1, HEAD_DIM), lambda b, h, *_: (0, 0))
    rows_spec = pl.BlockSpec((seq, V_DIM), lambda b, h, *_: (b, h))
    cols_spec = pl.BlockSpec((V_DIM, seq), lambda b, h, *_: (h, b))
    qg = ATTN_QG
    per_group = lambda shape, dtype: [pltpu.VMEM(shape, dtype) for _ in range(2 * tq // qg)]
    grid_spec = pltpu.PrefetchScalarGridSpec(
        num_scalar_prefetch=3,
        grid=(batch, N_HEADS),
        in_specs=[lam_spec, lam_spec, lam_spec, lam_spec,
                  pl.BlockSpec((V_DIM, 1), lambda b, h, *_: (0, 0)),
                  cols_spec, rows_spec, cols_spec],
        out_specs=rows_spec,
        scratch_shapes=(
            [pltpu.VMEM((V_DIM, 2 * seq), BF16),
             pltpu.VMEM((V_DIM + ONES_ROWS, seq), BF16)]
            + per_group((tq, qg), F32)
            + per_group((tq, qg), F32)
            + per_group((1, qg), F32)
            + per_group((1, qg), F32)
            + per_group((1, qg), F32)
            + per_group((1, qg), F32)
            + per_group((V_DIM, qg), F32)),
    )
    return pl.pallas_call(
        partial(_attn_kernel, n_tasks=n_tasks),
        out_shape=jax.ShapeDtypeStruct((T, D), BF16),
        grid_spec=grid_spec,
        compiler_params=_cparams("arbitrary", "arbitrary"),
        name="diff_attention",
    )(tqi, tj, tlast, lq1, lk1, lq2, lk2, subln_g.reshape(V_DIM, 1), qt, k, vt)


def _conv_kernel(u_ref, up_ref, cw_ref, o_ref, ubuf_ref, ush_ref, ub_ref, *, tiles_per_seq):
    tm = CONV_TM
    first = (pl.program_id(0) % tiles_per_seq) == 0
    prev = up_ref[...].astype(F32)
    ubuf_ref[0:CONV_HALO, :] = jnp.where(first, jnp.zeros_like(prev), prev)
    ubuf_ref[CONV_HALO:, :] = u_ref[...].astype(F32)
    base = CONV_HALO - (CONV_K - 1)
    n_shift = CONV_SUB + CONV_HALO - SUBLANES
    for t0 in range(0, tm, CONV_SUB):
        for r in range(1, SUBLANES):
            ush_ref[r - 1] = ubuf_ref[t0 + r:t0 + r + n_shift, :]
        for q in range(BF16_ROWS):
            r, up = q % SUBLANES, SUBLANES * (q // SUBLANES)
            n = ub_ref.shape[2] if q == 0 else n_shift - SUBLANES
            for c in range(D_MODEL // LANES):
                cl = slice(c * LANES, (c + 1) * LANES)
                src = (ubuf_ref[t0 + up:t0 + up + n, cl] if r == 0
                       else ush_ref[r - 1, up:up + n, cl])
                ub_ref[q, c, 0:n, :] = src.astype(BF16)
        for c in range(D_MODEL // LANES):
            cl = slice(c * LANES, (c + 1) * LANES)
            for r0 in range(0, CONV_SUB, CONV_ROWS):
                acc = jnp.zeros((CONV_ROWS // BF16_ROWS, BF16_ROWS, LANES), F32)
                for j in range(CONV_K):
                    b, q = divmod(base + j, BF16_ROWS)
                    lo = BF16_ROWS * b + r0
                    win = ub_ref[q, c, lo:lo + CONV_ROWS, :].astype(F32)
                    w = cw_ref[j, c].astype(F32)
                    acc = acc + win.reshape(acc.shape) * w[None]
                o_ref[t0 + r0:t0 + r0 + CONV_ROWS, cl] = acc.reshape(CONV_ROWS, LANES).astype(BF16)


def _causal_conv(u, conv_w, seq):
    T, D = u.shape
    tm = CONV_TM
    tiles_per_seq = seq // tm
    halo_per_tile = tm // CONV_HALO
    taps = jnp.broadcast_to(conv_w.astype(BF16).reshape(CONV_K, D // LANES, 1, LANES),
                            (CONV_K, D // LANES, BF16_ROWS, LANES))
    tok = pl.BlockSpec((tm, D), lambda i: (i, 0))
    return pl.pallas_call(
        partial(_conv_kernel, tiles_per_seq=tiles_per_seq),
        out_shape=jax.ShapeDtypeStruct((T, D), BF16),
        grid=(T // tm,),
        in_specs=[tok,
                  pl.BlockSpec((CONV_HALO, D), lambda i: (jnp.maximum(i * halo_per_tile - 1, 0), 0)),
                  pl.BlockSpec((CONV_K, D // LANES, BF16_ROWS, LANES), lambda i: (0, 0, 0, 0))],
        out_specs=tok,
        scratch_shapes=[pltpu.VMEM((CONV_HALO + tm, D), F32),
                        pltpu.VMEM((SUBLANES - 1, CONV_SUB + CONV_HALO - SUBLANES, D), F32),
                        pltpu.VMEM((BF16_ROWS, D // LANES, CONV_SUB + CONV_HALO, LANES), BF16)],
        compiler_params=_cparams("arbitrary"),
        name="causal_conv",
    )(u, u, taps)


def _mixer_kernel(x_ref, o_ref, uc_ref, ga_ref, gb_ref, g1_ref, sc2_ref, sh2_ref,
                  wao_ref, cb_ref, clg_ref, clb_ref, wpw_ref, bpw_ref, wout_ref,
                  ln1g_ref, ln1b_ref, wrh_ref, wrl_ref,
                  x1_ref, h2_ref, lg_ref):
    tm = MIX_TM
    for t0 in range(0, tm, MIX_SUB):
        rows = slice(t0, t0 + MIX_SUB)
        conv = uc_ref[rows, :].astype(F32)
        yc = _silu(_layer_norm(conv + cb_ref[...], clg_ref[...], clb_ref[...]))
        y_conv = (jnp.dot(yc.astype(BF16), wpw_ref[...], preferred_element_type=F32)
                  + bpw_ref[...])

        y_attn = jnp.dot(o_ref[rows, :], wao_ref[...], preferred_element_type=F32)
        mix = ga_ref[rows, :].astype(F32) * y_attn + gb_ref[rows, :].astype(F32) * y_conv
        y = jnp.dot(mix.astype(BF16), wout_ref[...], preferred_element_type=F32)
        x1 = _layer_norm(DN_ALPHA * x_ref[rows, :] + g1_ref[0] * y, ln1g_ref[...], ln1b_ref[...])
        x1_ref[rows, :] = x1

        h2 = x1 * (1.0 + sc2_ref[0]) + sh2_ref[0]
        h2_hi = h2.astype(BF16)
        _store_packed(h2_ref.at[:, rows, :], h2)
        h2_lo = (h2 - h2_hi.astype(F32)).astype(BF16)
        nt = lambda w, a: lax.dot_general(w, a, (((1,), (1,)), ((), ())),
                                          preferred_element_type=F32)
        lg_ref[:, rows] = (nt(wrh_ref[...], h2_hi)
                           + (nt(wrl_ref[...], h2_hi) + nt(wrh_ref[...], h2_lo)))


def _mixer_epilogue(x2, o, uc, ga, gb, g1, sc2, sh2, wao, cb, clg, clb, wpw, bpw, wout,
                    ln1g, ln1b, wr_hi, wr_lo, seq):
    T, D = x2.shape
    tm = MIX_TM
    tiles_per_seq = seq // tm
    tok = pl.BlockSpec((tm, D), lambda i: (i, 0))
    mod = pl.BlockSpec((1, 1, D), lambda i: (i // tiles_per_seq, 0, 0))
    row = pl.BlockSpec((1, D), lambda i: (0, 0))
    mat = pl.BlockSpec((D, D), lambda i: (0, 0), pipeline_mode=pl.Buffered(1))
    rw = pl.BlockSpec((N_EXPERTS, D), lambda i: (0, 0), pipeline_mode=pl.Buffered(1))
    return pl.pallas_call(
        _mixer_kernel,
        out_shape=[jax.ShapeDtypeStruct((T, D), F32),
                   jax.ShapeDtypeStruct((2, T, D // 4), U32),
                   jax.ShapeDtypeStruct((N_EXPERTS, T), F32)],
        grid=(T // tm,),
        in_specs=[tok, tok, tok, tok, tok, mod, mod, mod,
                  mat, row, row, row, mat, row, mat,
                  row, row, rw, rw],
        out_specs=[tok, pl.BlockSpec((2, tm, D // 4), lambda i: (0, i, 0)),
                   pl.BlockSpec((N_EXPERTS, tm), lambda i: (0, i))],
        compiler_params=_cparams("arbitrary"),
        name="mixer_epilogue",
    )(x2, o, uc, ga, gb, g1, sc2, sh2, wao, cb, clg, clb, wpw, bpw, wout,
      ln1g, ln1b, wr_hi, wr_lo)


def _first_argmax(v, eid):
    m = jnp.max(v, axis=0, keepdims=True)
    i = jnp.min(jnp.where(v == m, eid, float(N_EXPERTS)), axis=0, keepdims=True)
    return m, i


def _route_kernel(lg_ref, rb_ref, idx_ref, gate_ref, cnt_ref):
    tm = ROUTE_TM
    scores = _sigmoid(lg_ref[...])
    choice = scores + rb_ref[...]
    eid = lax.broadcasted_iota(jnp.int32, (N_EXPERTS, tm), 0).astype(F32)
    rows = lambda a, g: a[g * GROUP_SIZE:(g + 1) * GROUP_SIZE, :]

    gs = []
    for g in range(N_GROUPS):
        cg = rows(choice, g)
        eg = (lax.broadcasted_iota(jnp.int32, cg.shape, 0) + g * GROUP_SIZE).astype(F32)
        m1, i1 = _first_argmax(cg, eg)
        m2 = jnp.max(jnp.where(eg == i1, NEG_INF, cg), axis=0, keepdims=True)
        gs.append(jnp.broadcast_to(m1 + m2, (SUBLANES, tm)))

    kept = []
    for g in range(N_GROUPS):
        ahead = jnp.zeros((SUBLANES, tm), F32)
        for o in range(N_GROUPS):
            if o == g:
                continue
            beats = (gs[o] >= gs[g]) if o < g else (gs[o] > gs[g])
            ahead = ahead + jnp.where(beats, 1.0, 0.0)
        cap = jnp.where(ahead < TOPK_GROUPS, float("inf"), NEG_INF)
        for r in range(g * GROUP_SIZE, (g + 1) * GROUP_SIZE, SUBLANES):
            kept.append(jnp.minimum(choice[r:r + SUBLANES, :], cap))
    masked = jnp.concatenate(kept, axis=0)

    sel = jnp.zeros((N_EXPERTS, tm), F32)
    total = jnp.zeros((1, tm), F32)
    picked = []
    for k in range(TOP_K):
        _, i = _first_argmax(masked, eid)
        hit = eid == i
        sc = jnp.sum(jnp.where(hit, scores, 0.0), axis=0, keepdims=True)
        masked = jnp.where(hit, NEG_INF, masked)
        sel = sel + hit.astype(F32)
        total = total + sc
        idx_ref[k:k + 1, :] = i.astype(jnp.int32)
        picked.append(sc)
    for k in range(TOP_K):
        gate_ref[k:k + 1, :] = picked[k] / total * ROUTED_SCALE

    @pl.when(pl.program_id(0) == 0)
    def _():
        cnt_ref[...] = jnp.zeros_like(cnt_ref)
    cnt_ref[...] += jnp.sum(sel, axis=1, keepdims=True)


def _route(logits_t, router_bias):
    T = logits_t.shape[1]
    tm = ROUTE_TM
    return pl.pallas_call(
        _route_kernel,
        out_shape=[jax.ShapeDtypeStruct((TOP_K, T), jnp.int32),
                   jax.ShapeDtypeStruct((TOP_K, T), F32),
                   jax.ShapeDtypeStruct((N_EXPERTS, 1), F32)],
        grid=(T // tm,),
        in_specs=[pl.BlockSpec((N_EXPERTS, tm), lambda i: (0, i)),
                  pl.BlockSpec((N_EXPERTS, 1), lambda i: (0, 0))],
        out_specs=[pl.BlockSpec((TOP_K, tm), lambda i: (0, i)),
                   pl.BlockSpec((TOP_K, tm), lambda i: (0, i)),
                   pl.BlockSpec((N_EXPERTS, 1), lambda i: (0, 0))],
        compiler_params=_cparams("arbitrary"),
        name="route_topk",
    )(logits_t, router_bias.reshape(N_EXPERTS, 1))


def _slot_kernel(idx_ref, start_ref, pos_ref, carry_ref):
    tm = ROUTE_TM

    @pl.when(pl.program_id(0) == 0)
    def _():
        carry_ref[...] = jnp.zeros_like(carry_ref)

    idx = idx_ref[...]
    eid = lax.broadcasted_iota(jnp.int32, (N_EXPERTS, tm), 0)
    hits = [eid == idx[k:k + 1, :] for k in range(TOP_K)]
    sel = jnp.zeros((N_EXPERTS, tm), F32)
    for h in hits:
        sel = sel + h.astype(F32)
    r = lax.broadcasted_iota(jnp.int32, (tm, tm), 0)
    c = lax.broadcasted_iota(jnp.int32, (tm, tm), 1)
    earlier = (r < c).astype(BF16)
    rank = jnp.dot(sel.astype(BF16), earlier, preferred_element_type=F32)
    slot = rank + carry_ref[...] + start_ref[...]
    for k, h in enumerate(hits):
        pos_ref[k:k + 1, :] = jnp.sum(jnp.where(h, slot, 0.0), axis=0,
                                      keepdims=True).astype(jnp.int32)
    carry_ref[...] += jnp.sum(sel, axis=1, keepdims=True)


def _assign_slots(idx_t, pad_start):
    T = idx_t.shape[1]
    tm = ROUTE_TM
    return pl.pallas_call(
        _slot_kernel,
        out_shape=jax.ShapeDtypeStruct((TOP_K, T), jnp.int32),
        grid=(T // tm,),
        in_specs=[pl.BlockSpec((TOP_K, tm), lambda i: (0, i)),
                  pl.BlockSpec((N_EXPERTS, 1), lambda i: (0, 0))],
        out_specs=pl.BlockSpec((TOP_K, tm), lambda i: (0, i)),
        scratch_shapes=[pltpu.VMEM((N_EXPERTS, 1), F32)],
        compiler_params=_cparams("arbitrary"),
        name="assign_slots",
    )(idx_t, pad_start)


def _expert_kernel(be_ref, nu_ref, first_ref, slot_ref, nxt_ref, xs_ref, w1_hbm, w3_hbm, w2_hbm,
                   ys_ref, w1_buf, w3_buf, w2_buf, sem):
    b = pl.program_id(0)

    def weight_copies(expert, slot):
        pairs = ((w1_hbm, w1_buf), (w3_hbm, w3_buf), (w2_hbm, w2_buf))
        return [pltpu.make_async_copy(hbm.at[expert], buf.at[slot], sem.at[slot, i])
                for i, (hbm, buf) in enumerate(pairs)]

    @pl.when(b < nu_ref[0])
    def _():
        slot = slot_ref[b]

        @pl.when(b == 0)
        def _():
            for c in weight_copies(be_ref[0], 0):
                c.start()

        @pl.when(first_ref[b] == 1)
        def _():
            for c in weight_copies(be_ref[b], slot):
                c.wait()

            @pl.when(nxt_ref[b] >= 0)
            def _():
                for c in weight_copies(nxt_ref[b], 1 - slot):
                    c.start()

        lo, hi = _load_packed(xs_ref)
        half = lo.shape[-1]
        mm = lambda w: (jnp.dot(lo, w[slot, :half, :], preferred_element_type=F32)
                        + jnp.dot(hi, w[slot, half:, :], preferred_element_type=F32))
        hb = _silu(mm(w1_buf)) * mm(w3_buf)
        _store_packed(ys_ref, jnp.dot(hb, w2_buf[slot], preferred_element_type=F32))

    @pl.when(b >= nu_ref[0])
    def _():
        ys_ref[...] = jnp.zeros_like(ys_ref)


def _routed_experts(xs, blk_expert, n_used, w1, w3, w2):
    _, P, Dq = xs.shape
    blk = EXPERT_BLK
    n_blocks = P // blk
    D, F = w1.shape[-2:]

    b_ids = jnp.arange(n_blocks, dtype=jnp.int32)
    used = b_ids < n_used[0]
    prev = jnp.concatenate([jnp.full((1,), -1, jnp.int32), blk_expert[:-1]])
    first = (used & (blk_expert != prev)).astype(jnp.int32)
    slot = (jnp.cumsum(first) - 1) % 2
    start_at = jnp.where(first == 1, b_ids, n_blocks)
    later = jnp.concatenate([start_at[1:], jnp.full((1,), n_blocks, jnp.int32)])
    next_start = lax.cummin(later, axis=0, reverse=True)
    nxt = jnp.where(next_start < n_blocks,
                    blk_expert[jnp.minimum(next_start, n_blocks - 1)], -1).astype(jnp.int32)

    row_blk = lambda b, be, nu, *_: (0, jnp.minimum(b, nu[0] - 1), 0)
    grid_spec = pltpu.PrefetchScalarGridSpec(
        num_scalar_prefetch=5,
        grid=(n_blocks,),
        in_specs=[pl.BlockSpec((2, blk, Dq), row_blk),
                  pl.BlockSpec(memory_space=pl.ANY),
                  pl.BlockSpec(memory_space=pl.ANY),
                  pl.BlockSpec(memory_space=pl.ANY)],
        out_specs=pl.BlockSpec((2, blk, Dq), lambda b, *_: (0, b, 0)),
        scratch_shapes=[pltpu.VMEM((2, D, F), F32), pltpu.VMEM((2, D, F), F32),
                        pltpu.VMEM((2, F, D), F32), pltpu.SemaphoreType.DMA((2, 3))],
    )
    return pl.pallas_call(
        _expert_kernel,
        out_shape=jax.ShapeDtypeStruct((2, P, Dq), U32),
        grid_spec=grid_spec,
        compiler_params=_cparams("arbitrary"),
        name="routed_experts",
    )(blk_expert, n_used, first, slot.astype(jnp.int32), nxt, xs, w1, w3, w2)


def _sc_mesh():
    return plsc.VectorSubcoreMesh(core_axis_name="c", subcore_axis_name="s")


def _sc_dispatch(rows, slots_t, n_slots):
    T, W = rows.shape
    K = slots_t.shape[0]
    win = SC_WINDOW

    @pl.kernel(out_type=jax.ShapeDtypeStruct((n_slots, W), rows.dtype), mesh=_sc_mesh(),
               scratch_types=[])
    def dispatch(x_hbm, i_hbm, o_hbm):
        def body(x_vmem, i_vmem):
            for k in range(K):
                pltpu.sync_copy(x_vmem, o_hbm.at[i_vmem.at[k]])

        pltpu.emit_pipeline(
            body,
            grid=(T // win,),
            in_specs=[pl.BlockSpec((win, W), lambda i: (i, 0)),
                      pl.BlockSpec((K, win), lambda i: (0, i))],
            out_specs=[],
            core_axis_name=("c", "s"),
            dimension_semantics=(pltpu.PARALLEL,),
        )(x_hbm, i_hbm)

    return dispatch(rows, slots_t)


def _sc_gather(table, idx):
    W = table.shape[1]
    N = idx.shape[1]
    win = SC_WINDOW

    @pl.kernel(out_type=jax.ShapeDtypeStruct((N, W), table.dtype), mesh=_sc_mesh(),
               scratch_types=[])
    def gather(x_hbm, i_hbm, o_hbm):
        def body(i_vmem, o_vmem):
            pltpu.sync_copy(x_hbm.at[i_vmem.at[0]], o_vmem)

        pltpu.emit_pipeline(
            body,
            grid=(N // win,),
            in_specs=[pl.BlockSpec((1, win), lambda i: (0, i))],
            out_specs=[pl.BlockSpec((win, W), lambda i: (i, 0))],
            core_axis_name=("c", "s"),
            dimension_semantics=(pltpu.PARALLEL,),
        )(i_hbm, o_hbm)

    return gather(table, idx)


def _final_kernel(x1_ref, h2_ref, yg_ref, gate_ref, g2_ref, ws1_ref, ws3_ref, ws2_ref,
                  lng_ref, lnb_ref, o_ref):
    Dh = D_MODEL // 2
    gate = gate_ref[...]
    routed_lo = jnp.zeros((x1_ref.shape[0], Dh), F32)
    routed_hi = jnp.zeros((x1_ref.shape[0], Dh), F32)
    for k in range(TOP_K):
        lo, hi = _load_packed(yg_ref.at[k])
        routed_lo = routed_lo + lo * gate[:, k:k + 1]
        routed_hi = routed_hi + hi * gate[:, k:k + 1]
    h2 = jnp.concatenate(_load_packed(h2_ref), axis=-1).astype(BF16)
    a = jnp.dot(h2, ws1_ref[...], preferred_element_type=F32)
    b = jnp.dot(h2, ws3_ref[...], preferred_element_type=F32)
    shared = jnp.dot((_silu(a) * b).astype(BF16), ws2_ref[...], preferred_element_type=F32)
    y2 = jnp.concatenate([routed_lo, routed_hi], axis=-1) + shared
    o_ref[...] = _layer_norm(DN_ALPHA * x1_ref[...] + g2_ref[0] * y2, lng_ref[...], lnb_ref[...])


def _final(x1, h2, yg, gate, g2, ws1, ws3, ws2, ln2g, ln2b, seq):
    T, D = x1.shape
    tm = FINAL_TM
    tiles_per_seq = seq // tm
    F = ws1.shape[-1]
    tok = pl.BlockSpec((tm, D), lambda i: (i, 0))
    row = pl.BlockSpec((1, D), lambda i: (0, 0))
    return pl.pallas_call(
        _final_kernel,
        out_shape=jax.ShapeDtypeStruct((T, D), F32),
        grid=(T // tm,),
        in_specs=[tok, pl.BlockSpec((2, tm, D // 4), lambda i: (0, i, 0)),
                  pl.BlockSpec((TOP_K, 2, tm, D // 4), lambda i: (0, 0, i, 0)),
                  pl.BlockSpec((tm, TOP_K), lambda i: (i, 0)),
                  pl.BlockSpec((1, 1, D), lambda i: (i // tiles_per_seq, 0, 0)),
                  pl.BlockSpec((D, F), lambda i: (0, 0)),
                  pl.BlockSpec((D, F), lambda i: (0, 0)),
                  pl.BlockSpec((F, D), lambda i: (0, 0)),
                  row, row],
        out_specs=tok,
        compiler_params=_cparams("arbitrary"),
        name="moe_combine_final",
    )(x1, h2, yg, gate, g2, ws1, ws3, ws2, ln2g, ln2b)


def kernel(x, c, ada_w, ada_b, w_in, b_in, lambda_q1, lambda_k1, lambda_q2, lambda_k2, subln_g, w_attn_o, conv_w, conv_b, conv_ln_g, conv_ln_b, conv_pw_w, conv_pw_b, w_out, ln1_g, ln1_b, router_w, router_bias, w1, w3, w2, ws1, ws3, ws2, ln2_g, ln2_b):
    B, S, D = x.shape
    T = B * S
    l = 0
    x2 = x.reshape(T, D)
    row = lambda a: a.reshape(1, -1)

    mod = _ada_modulation(c, ada_w[l], ada_b[l])
    sh1, sc1, g1, sh2, sc2, g2 = [m.reshape(B, 1, D) for m in jnp.split(mod, 6, axis=-1)]

    qt, k, vt, u, ga, gb = _in_projection(x2, sc1, sh1, w_in[l].astype(BF16), row(b_in[l]), S)
    o = _diff_attention(qt, k, vt, row(lambda_q1[l]), row(lambda_k1[l]), row(lambda_q2[l]),
                        row(lambda_k2[l]), row(subln_g[l]), B, S)

    wr = router_w[l].T
    wr_hi = wr.astype(BF16)
    wr_lo = (wr - wr_hi.astype(F32)).astype(BF16)
    x1, h2, logits = _mixer_epilogue(
        x2, o, _causal_conv(u, conv_w[l], S), ga, gb, g1, sc2, sh2, w_attn_o[l].astype(BF16),
        row(conv_b[l]),
        row(conv_ln_g[l]), row(conv_ln_b[l]), conv_pw_w[l].astype(BF16), row(conv_pw_b[l]),
        w_out[l].astype(BF16), row(ln1_g[l]), row(ln1_b[l]), wr_hi, wr_lo, S)

    idx, gate, counts = _route(logits, router_bias[l])

    blk = EXPERT_BLK
    cnt = counts.reshape(N_EXPERTS).astype(jnp.int32)
    padded = (cnt + blk - 1) // blk * blk
    pad_end = jnp.cumsum(padded)
    pad_start = pad_end - padded
    P = T * TOP_K + N_EXPERTS * blk
    n_blocks = P // blk
    blk_first_row = jnp.arange(n_blocks, dtype=jnp.int32) * blk
    blk_expert = jnp.minimum(
        jnp.sum((pad_end[None, :] <= blk_first_row[:, None]).astype(jnp.int32), axis=1),
        N_EXPERTS - 1)
    n_used = (pad_end[-1:] // blk).astype(jnp.int32)

    pos = _assign_slots(idx, pad_start.astype(F32).reshape(N_EXPERTS, 1))

    Dq = D // 4
    half_base = (jnp.arange(2, dtype=jnp.int32) * P)[None, :, None]
    slots = pos[:, None, :] + half_base
    xs = _sc_dispatch(h2.reshape(2 * T, Dq), slots.reshape(TOP_K, 2 * T), 2 * P)
    ys = _routed_experts(xs.reshape(2, P, Dq), blk_expert, n_used, w1[l], w3[l], w2[l])
    yg = _sc_gather(ys.reshape(2 * P, Dq), slots.reshape(1, TOP_K * 2 * T))
    yg = yg.reshape(TOP_K, 2, T, Dq)

    out = _final(x1, h2, yg, gate.T, g2, ws1[l].astype(BF16), ws3[l].astype(BF16),
                 ws2[l].astype(BF16), row(ln2_g[l]), row(ln2_b[l]), S)
    return out.reshape(B, S, D)
```

```python
import math
from functools import partial

import jax
import jax.numpy as jnp
from jax import lax
from jax.experimental import pallas as pl
from jax.experimental.pallas import tpu as pltpu
from jax.experimental.pallas import tpu_sc as plsc

D_MODEL = 1024
N_HEADS = 8
HEAD_DIM = 64
V_DIM = 128
CONV_K = 31
N_EXPERTS = 256
TOP_K = 8
N_GROUPS = 8
GROUP_SIZE = N_EXPERTS // N_GROUPS
TOPK_GROUPS = 4
ROUTED_SCALE = 2.5
LN_EPS = 1e-5
RMS_EPS = 1e-5
DEPTH = 1
DN_ALPHA = (2.0 * DEPTH) ** 0.25
LAMBDA_INIT = 0.8 - 0.6 * math.exp(-0.3 * 0)
Q_SCALE = HEAD_DIM ** -0.5 * math.log2(math.e)

LANES = 128
SUBLANES = 8
BF16_ROWS = 16
VMEM_LIMIT_BYTES = 56 * 1024 * 1024

PROJ_TM = 512
ATTN_TQ = 512
ATTN_QG = 512
ATTN_KC = 256
ONES_ROWS = 16
MIX_TM = 512
MIX_SUB = 512
CONV_HALO = 32
CONV_ROWS = 64
CONV_TM = 512
CONV_SUB = 256
ROUTE_TM = 512
EXPERT_BLK = 512
WEIGHT_DMA_PRIORITY = 1
FINAL_TM = 512
SC_WINDOW = 128

F32 = jnp.float32
BF16 = jnp.bfloat16
U32 = jnp.uint32
NEG_INF = float("-inf")


def _cparams(*sem):
    return pltpu.CompilerParams(dimension_semantics=sem, vmem_limit_bytes=VMEM_LIMIT_BYTES)


def _sigmoid(x):
    return 1.0 / (1.0 + jnp.exp(-x))


def _silu(x):
    return x * _sigmoid(x)


BF16_BITS = 16
HIGH_HALF_WORD = 0xFFFF0000


def _pack_halves(x):
    half = x.shape[-1] // 2
    bits = lambda v: lax.bitcast_convert_type(v.astype(BF16).astype(F32), U32)
    return (bits(x[:, :half]) >> BF16_BITS) | (bits(x[:, half:]) & jnp.uint32(HIGH_HALF_WORD))


def _unpack_halves(w):
    lo = lax.bitcast_convert_type(w << BF16_BITS, F32)
    hi = lax.bitcast_convert_type(w & jnp.uint32(HIGH_HALF_WORD), F32)
    return lo, hi


def _store_packed(ref, x):
    w = _pack_halves(x)
    q = w.shape[-1] // 2
    ref[0] = w[:, :q]
    ref[1] = w[:, q:]


def _load_packed(ref):
    return _unpack_halves(jnp.concatenate([ref[0], ref[1]], axis=-1))


def _layer_norm(r, g, b):
    mu = jnp.mean(r, axis=-1, keepdims=True)
    d = r - mu
    var = jnp.mean(d * d, axis=-1, keepdims=True)
    return d * lax.rsqrt(var + LN_EPS) * g + b


def _ada_kernel(c_ref, w_ref, b_ref, o_ref):
    a = _silu(c_ref[...])
    o_ref[...] = jnp.dot(a, w_ref[...], preferred_element_type=F32,
                         precision=lax.Precision.HIGHEST) + b_ref[...]


def _ada_modulation(c, w, b):
    B, D = c.shape
    n = w.shape[1] // D
    return pl.pallas_call(
        _ada_kernel,
        out_shape=jax.ShapeDtypeStruct((B, n * D), F32),
        grid=(n,),
        in_specs=[pl.BlockSpec((B, D), lambda j: (0, 0)),
                  pl.BlockSpec((D, D), lambda j: (0, j)),
                  pl.BlockSpec((1, D), lambda j: (0, j))],
        out_specs=pl.BlockSpec((B, D), lambda j: (0, j)),
        compiler_params=_cparams("arbitrary"),
        name="ada_modulation",
    )(c, w, b.reshape(1, n * D))


def _inproj_kernel(x_ref, sc_ref, sh_ref, w_ref, b_ref,
                   qt_ref, k_ref, vt_ref, u_ref, ga_ref, gb_ref):
    D = D_MODEL
    h = (x_ref[...] * (1.0 + sc_ref[0]) + sh_ref[0]).astype(BF16)

    def proj(j):
        return (jnp.dot(h, w_ref[:, j * D:(j + 1) * D], preferred_element_type=F32)
                + b_ref[:, j * D:(j + 1) * D])

    qt_ref[...] = (proj(0) * Q_SCALE).T.astype(BF16)
    k_ref[...] = proj(1).astype(BF16)
    vt_ref[...] = proj(2).T.astype(BF16)
    u_ref[...] = (proj(3) * _sigmoid(proj(4))).astype(BF16)
    ga_ref[...] = _sigmoid(proj(5)).astype(BF16)
    gb_ref[...] = _sigmoid(proj(6)).astype(BF16)


def _in_projection(x2, sc1, sh1, w_in, b_in, seq):
    T, D = x2.shape
    tm = PROJ_TM
    tiles_per_seq = seq // tm
    ncols = w_in.shape[1]
    tok = pl.BlockSpec((tm, D), lambda i: (i, 0))
    mod = pl.BlockSpec((1, 1, D), lambda i: (i // tiles_per_seq, 0, 0))
    tok_t = pl.BlockSpec((D, tm), lambda i: (0, i))
    out = jax.ShapeDtypeStruct((T, D), BF16)
    out_t = jax.ShapeDtypeStruct((D, T), BF16)
    return pl.pallas_call(
        _inproj_kernel,
        out_shape=[out_t, out, out_t, out, out, out],
        grid=(T // tm,),
        in_specs=[tok, mod, mod,
                  pl.BlockSpec((D, ncols), lambda i: (0, 0)),
                  pl.BlockSpec((1, ncols), lambda i: (0, 0))],
        out_specs=[tok_t, tok, tok_t, tok, tok, tok],
        compiler_params=_cparams("arbitrary"),
        name="in_projection",
    )(x2, sc1, sh1, w_in, b_in)


def _attn_tasks(nq):
    tasks = [(qi, j, int(j == qi)) for qi in range(nq) for j in range(qi + 1)]
    if len(tasks) % 2:
        raise ValueError("the two-slot task loop needs an even number of tile pairs")
    qis, js, lasts = zip(*(tasks + [(0, 0, 0)]))
    as_i32 = lambda v: jnp.asarray(v, jnp.int32)
    return as_i32(qis), as_i32(js), as_i32(lasts), len(tasks)


def _attn_kernel(tqi_ref, tj_ref, tlast_ref, lq1_ref, lk1_ref, lq2_ref, lk2_ref, g_ref,
                 qt_ref, k_ref, vt_ref, o_ref, q2_ref, vx_ref, *scratch, n_tasks):
    tq = ATTN_TQ
    qg = ATTN_QG
    if tq % qg:
        raise ValueError("a query group must not span the two softmax maps of a tile")
    ng = 2 * tq // qg
    groups = [scratch[i * ng:(i + 1) * ng] for i in range(7)]
    s_slots, x_slots = groups[0:2], groups[2:4]
    m_refs, l_refs, acc_refs = groups[4:]
    seq = k_ref.shape[0]

    vx_ref[:V_DIM, :] = vt_ref[...]
    vx_ref[V_DIM:, :] = jnp.ones((ONES_ROWS, seq), BF16)

    lam = (jnp.exp(jnp.sum(lq1_ref[...] * lk1_ref[...], axis=-1, keepdims=True))
           - jnp.exp(jnp.sum(lq2_ref[...] * lk2_ref[...], axis=-1, keepdims=True))
           + LAMBDA_INIT)

    for i in range(seq // tq):
        qt = qt_ref[:, i * tq:(i + 1) * tq]
        feat = lax.broadcasted_iota(jnp.int32, qt.shape, 0)
        zero = jnp.zeros_like(qt)
        q2_ref[:, 2 * tq * i:2 * tq * i + tq] = jnp.where(feat < HEAD_DIM, qt, zero)
        q2_ref[:, 2 * tq * i + tq:2 * tq * (i + 1)] = jnp.where(feat >= HEAD_DIM, qt, zero)

    def scores(qi, j, slot, g):
        start = pl.multiple_of(j * tq, tq)
        col = pl.multiple_of(qi * (2 * tq) + g * qg, qg)
        s = jnp.dot(k_ref[pl.ds(start, tq), :], q2_ref[:, pl.ds(col, qg)],
                    preferred_element_type=F32)
        s_slots[slot][g][...] = s
        x_slots[slot][g][...] = jnp.max(s, axis=0, keepdims=True)

    def update(j, slot, g, masked):
        start = pl.multiple_of(j * tq, tq)
        s_ref = s_slots[slot][g]

        def load(k0):
            s = s_ref[k0:k0 + ATTN_KC, :]
            if masked:
                key = lax.broadcasted_iota(jnp.int32, s.shape, 0) + k0
                qry = lax.broadcasted_iota(jnp.int32, s.shape, 1) + (g * qg) % tq
                s = jnp.where(key <= qry, s, NEG_INF)
            return s

        m_prev = m_refs[g][...]
        if masked:
            tile_max = jnp.max(load(0), axis=0, keepdims=True)
            for k0 in range(ATTN_KC, tq, ATTN_KC):
                tile_max = jnp.maximum(tile_max, jnp.max(load(k0), axis=0, keepdims=True))
        else:
            tile_max = x_slots[slot][g][...]
        m_new = jnp.maximum(m_prev, tile_max)
        alpha = jnp.exp2(m_prev - m_new)
        m_refs[g][...] = m_new
        pv = None
        for k0 in range(0, tq, ATTN_KC):
            p = jnp.exp2((load(k0) - m_new).astype(BF16))
            keys = pl.ds(pl.multiple_of(start + k0, ATTN_KC), ATTN_KC)
            part = jnp.dot(vx_ref[:, keys], p, preferred_element_type=F32)
            pv = part if pv is None else pv + part
        acc_refs[g][...] = alpha * acc_refs[g][...] + pv[:V_DIM, :]
        l_refs[g][...] = alpha * l_refs[g][...] + pv[V_DIM:V_DIM + 1, :]

    def reset():
        for g in range(ng):
            m_refs[g][...] = jnp.full(m_refs[g].shape, NEG_INF, F32)
            l_refs[g][...] = jnp.zeros(l_refs[g].shape, F32)
            acc_refs[g][...] = jnp.zeros(acc_refs[g].shape, F32)

    def finish(qi):
        o = (jnp.concatenate([r[...] for r in acc_refs], axis=-1)
             / jnp.concatenate([r[...] for r in l_refs], axis=-1))
        o = o[:, :tq] - lam * o[:, tq:]
        o = o * lax.rsqrt(jnp.mean(o * o, axis=0, keepdims=True) + RMS_EPS) * g_ref[...]
        row = pl.multiple_of(qi * tq, tq)
        o_ref[pl.ds(row, tq), :] = (o * (1.0 - LAMBDA_INIT)).T.astype(BF16)

    def task(t, slot):
        qi, j = tqi_ref[t], tj_ref[t]
        nqi, nj = tqi_ref[t + 1], tj_ref[t + 1]

        def run(masked):
            for g in range(ng):
                scores(nqi, nj, 1 - slot, g)
                update(j, slot, g, masked)

        @pl.when(tlast_ref[t] == 0)
        def _():
            run(False)

        @pl.when(tlast_ref[t] == 1)
        def _():
            run(True)
            finish(qi)
            reset()

    reset()
    for g in range(ng):
        scores(0, 0, 0, g)

    def pair(i, carry):
        task(2 * i, 0)
        task(2 * i + 1, 1)
        return carry

    lax.fori_loop(0, n_tasks // 2, pair, 0)


def _diff_attention(qt, k, vt, lq1, lk1, lq2, lk2, subln_g, batch, seq):
    T, D = k.shape
    tq = ATTN_TQ
    nq = seq // tq
    tqi, tj, tlast, n_tasks = _attn_tasks(nq)
    lam_spec = pl.BlockSpec((1, HEAD_DIM), lambda b, h, *_: (0, 0))
    rows_spec = pl.BlockSpec((seq, V_DIM), lambda b, h, *_: (b, h))
    cols_spec = pl.BlockSpec((V_DIM, seq), lambda b, h, *_: (h, b))
    qg = ATTN_QG
    per_group = lambda shape, dtype: [pltpu.VMEM(shape, dtype) for _ in range(2 * tq // qg)]
    grid_spec = pltpu.PrefetchScalarGridSpec(
        num_scalar_prefetch=3,
        grid=(batch, N_HEADS),
        in_specs=[lam_spec, lam_spec, lam_spec, lam_spec,
                  pl.BlockSpec((V_DIM, 1), lambda b, h, *_: (0, 0)),
                  cols_spec, rows_spec, cols_spec],
        out_specs=rows_spec,
        scratch_shapes=(
            [pltpu.VMEM((V_DIM, 2 * seq), BF16),
             pltpu.VMEM((V_DIM + ONES_ROWS, seq), BF16)]
            + per_group((tq, qg), F32)
            + per_group((tq, qg), F32)
            + per_group((1, qg), F32)
            + per_group((1, qg), F32)
            + per_group((1, qg), F32)
            + per_group((1, qg), F32)
            + per_group((V_DIM, qg), F32)),
    )
    return pl.pallas_call(
        partial(_attn_kernel, n_tasks=n_tasks),
        out_shape=jax.ShapeDtypeStruct((T, D), BF16),
        grid_spec=grid_spec,
        compiler_params=_cparams("arbitrary", "arbitrary"),
        name="diff_attention",
    )(tqi, tj, tlast, lq1, lk1, lq2, lk2, subln_g.reshape(V_DIM, 1), qt, k, vt)


def _conv_kernel(u_ref, up_ref, cw_ref, o_ref, ubuf_ref, ush_ref, ub_ref, *, tiles_per_seq):
    tm = CONV_TM
    first = (pl.program_id(0) % tiles_per_seq) == 0
    prev = up_ref[...].astype(F32)
    ubuf_ref[0:CONV_HALO, :] = jnp.where(first, jnp.zeros_like(prev), prev)
    ubuf_ref[CONV_HALO:, :] = u_ref[...].astype(F32)
    base = CONV_HALO - (CONV_K - 1)
    n_shift = CONV_SUB + CONV_HALO - SUBLANES
    for t0 in range(0, tm, CONV_SUB):
        for r in range(1, SUBLANES):
            ush_ref[r - 1] = ubuf_ref[t0 + r:t0 + r + n_shift, :]
        for q in range(BF16_ROWS):
            r, up = q % SUBLANES, SUBLANES * (q // SUBLANES)
            n = ub_ref.shape[2] if q == 0 else n_shift - SUBLANES
            for c in range(D_MODEL // LANES):
                cl = slice(c * LANES, (c + 1) * LANES)
                src = (ubuf_ref[t0 + up:t0 + up + n, cl] if r == 0
                       else ush_ref[r - 1, up:up + n, cl])
                ub_ref[q, c, 0:n, :] = src.astype(BF16)
        for c in range(D_MODEL // LANES):
            cl = slice(c * LANES, (c + 1) * LANES)
            for r0 in range(0, CONV_SUB, CONV_ROWS):
                acc = jnp.zeros((CONV_ROWS // BF16_ROWS, BF16_ROWS, LANES), F32)
                for j in range(CONV_K):
                    b, q = divmod(base + j, BF16_ROWS)
                    lo = BF16_ROWS * b + r0
                    win = ub_ref[q, c, lo:lo + CONV_ROWS, :].astype(F32)
                    w = cw_ref[j, c].astype(F32)
                    acc = acc + win.reshape(acc.shape) * w[None]
                o_ref[t0 + r0:t0 + r0 + CONV_ROWS, cl] = acc.reshape(CONV_ROWS, LANES).astype(BF16)


def _causal_conv(u, conv_w, seq):
    T, D = u.shape
    tm = CONV_TM
    tiles_per_seq = seq // tm
    halo_per_tile = tm // CONV_HALO
    taps = jnp.broadcast_to(conv_w.astype(BF16).reshape(CONV_K, D // LANES, 1, LANES),
                            (CONV_K, D // LANES, BF16_ROWS, LANES))
    tok = pl.BlockSpec((tm, D), lambda i: (i, 0))
    return pl.pallas_call(
        partial(_conv_kernel, tiles_per_seq=tiles_per_seq),
        out_shape=jax.ShapeDtypeStruct((T, D), BF16),
        grid=(T // tm,),
        in_specs=[tok,
                  pl.BlockSpec((CONV_HALO, D), lambda i: (jnp.maximum(i * halo_per_tile - 1, 0), 0)),
                  pl.BlockSpec((CONV_K, D // LANES, BF16_ROWS, LANES), lambda i: (0, 0, 0, 0))],
        out_specs=tok,
        scratch_shapes=[pltpu.VMEM((CONV_HALO + tm, D), F32),
                        pltpu.VMEM((SUBLANES - 1, CONV_SUB + CONV_HALO - SUBLANES, D), F32),
                        pltpu.VMEM((BF16_ROWS, D // LANES, CONV_SUB + CONV_HALO, LANES), BF16)],
        compiler_params=_cparams("arbitrary"),
        name="causal_conv",
    )(u, u, taps)


def _mixer_kernel(x_ref, o_ref, uc_ref, ga_ref, gb_ref, g1_ref, sc2_ref, sh2_ref,
                  wao_ref, cb_ref, clg_ref, clb_ref, wpw_ref, bpw_ref, wout_ref,
                  ln1g_ref, ln1b_ref, wrh_ref, wrl_ref,
                  x1_ref, h2_ref, lg_ref):
    tm = MIX_TM
    for t0 in range(0, tm, MIX_SUB):
        rows = slice(t0, t0 + MIX_SUB)
        conv = uc_ref[rows, :].astype(F32)
        yc = _silu(_layer_norm(conv + cb_ref[...], clg_ref[...], clb_ref[...]))
        y_conv = (jnp.dot(yc.astype(BF16), wpw_ref[...], preferred_element_type=F32)
                  + bpw_ref[...])

        y_attn = jnp.dot(o_ref[rows, :], wao_ref[...], preferred_element_type=F32)
        mix = ga_ref[rows, :].astype(F32) * y_attn + gb_ref[rows, :].astype(F32) * y_conv
        y = jnp.dot(mix.astype(BF16), wout_ref[...], preferred_element_type=F32)
        x1 = _layer_norm(DN_ALPHA * x_ref[rows, :] + g1_ref[0] * y, ln1g_ref[...], ln1b_ref[...])
        x1_ref[rows, :] = x1

        h2 = x1 * (1.0 + sc2_ref[0]) + sh2_ref[0]
        h2_hi = h2.astype(BF16)
        _store_packed(h2_ref.at[:, rows, :], h2)
        h2_lo = (h2 - h2_hi.astype(F32)).astype(BF16)
        nt = lambda w, a: lax.dot_general(w, a, (((1,), (1,)), ((), ())),
                                          preferred_element_type=F32)
        lg_ref[:, rows] = (nt(wrh_ref[...], h2_hi)
                           + (nt(wrl_ref[...], h2_hi) + nt(wrh_ref[...], h2_lo)))


def _mixer_epilogue(x2, o, uc, ga, gb, g1, sc2, sh2, wao, cb, clg, clb, wpw, bpw, wout,
                    ln1g, ln1b, wr_hi, wr_lo, seq):
    T, D = x2.shape
    tm = MIX_TM
    tiles_per_seq = seq // tm
    tok = pl.BlockSpec((tm, D), lambda i: (i, 0))
    mod = pl.BlockSpec((1, 1, D), lambda i: (i // tiles_per_seq, 0, 0))
    row = pl.BlockSpec((1, D), lambda i: (0, 0))
    mat = pl.BlockSpec((D, D), lambda i: (0, 0), pipeline_mode=pl.Buffered(1))
    rw = pl.BlockSpec((N_EXPERTS, D), lambda i: (0, 0), pipeline_mode=pl.Buffered(1))
    return pl.pallas_call(
        _mixer_kernel,
        out_shape=[jax.ShapeDtypeStruct((T, D), F32),
                   jax.ShapeDtypeStruct((2, T, D // 4), U32),
                   jax.ShapeDtypeStruct((N_EXPERTS, T), F32)],
        grid=(T // tm,),
        in_specs=[tok, tok, tok, tok, tok, mod, mod, mod,
                  mat, row, row, row, mat, row, mat,
                  row, row, rw, rw],
        out_specs=[tok, pl.BlockSpec((2, tm, D // 4), lambda i: (0, i, 0)),
                   pl.BlockSpec((N_EXPERTS, tm), lambda i: (0, i))],
        compiler_params=_cparams("arbitrary"),
        name="mixer_epilogue",
    )(x2, o, uc, ga, gb, g1, sc2, sh2, wao, cb, clg, clb, wpw, bpw, wout,
      ln1g, ln1b, wr_hi, wr_lo)


def _first_argmax(v, eid):
    m = jnp.max(v, axis=0, keepdims=True)
    i = jnp.min(jnp.where(v == m, eid, float(N_EXPERTS)), axis=0, keepdims=True)
    return m, i


def _route_kernel(lg_ref, rb_ref, idx_ref, gate_ref, cnt_ref):
    tm = ROUTE_TM
    scores = _sigmoid(lg_ref[...])
    choice = scores + rb_ref[...]
    eid = lax.broadcasted_iota(jnp.int32, (N_EXPERTS, tm), 0).astype(F32)
    rows = lambda a, g: a[g * GROUP_SIZE:(g + 1) * GROUP_SIZE, :]

    gs = []
    for g in range(N_GROUPS):
        cg = rows(choice, g)
        eg = (lax.broadcasted_iota(jnp.int32, cg.shape, 0) + g * GROUP_SIZE).astype(F32)
        m1, i1 = _first_argmax(cg, eg)
        m2 = jnp.max(jnp.where(eg == i1, NEG_INF, cg), axis=0, keepdims=True)
        gs.append(jnp.broadcast_to(m1 + m2, (SUBLANES, tm)))

    kept = []
    for g in range(N_GROUPS):
        ahead = jnp.zeros((SUBLANES, tm), F32)
        for o in range(N_GROUPS):
            if o == g:
                continue
            beats = (gs[o] >= gs[g]) if o < g else (gs[o] > gs[g])
            ahead = ahead + jnp.where(beats, 1.0, 0.0)
        cap = jnp.where(ahead < TOPK_GROUPS, float("inf"), NEG_INF)
        for r in range(g * GROUP_SIZE, (g + 1) * GROUP_SIZE, SUBLANES):
            kept.append(jnp.minimum(choice[r:r + SUBLANES, :], cap))
    masked = jnp.concatenate(kept, axis=0)

    sel = jnp.zeros((N_EXPERTS, tm), F32)
    total = jnp.zeros((1, tm), F32)
    picked = []
    for k in range(TOP_K):
        _, i = _first_argmax(masked, eid)
        hit = eid == i
        sc = jnp.sum(jnp.where(hit, scores, 0.0), axis=0, keepdims=True)
        masked = jnp.where(hit, NEG_INF, masked)
        sel = sel + hit.astype(F32)
        total = total + sc
        idx_ref[k:k + 1, :] = i.astype(jnp.int32)
        picked.append(sc)
    for k in range(TOP_K):
        gate_ref[k:k + 1, :] = picked[k] / total * ROUTED_SCALE

    @pl.when(pl.program_id(0) == 0)
    def _():
        cnt_ref[...] = jnp.zeros_like(cnt_ref)
    cnt_ref[...] += jnp.sum(sel, axis=1, keepdims=True)


def _route(logits_t, router_bias):
    T = logits_t.shape[1]
    tm = ROUTE_TM
    return pl.pallas_call(
        _route_kernel,
        out_shape=[jax.ShapeDtypeStruct((TOP_K, T), jnp.int32),
                   jax.ShapeDtypeStruct((TOP_K, T), F32),
                   jax.ShapeDtypeStruct((N_EXPERTS, 1), F32)],
        grid=(T // tm,),
        in_specs=[pl.BlockSpec((N_EXPERTS, tm), lambda i: (0, i)),
                  pl.BlockSpec((N_EXPERTS, 1), lambda i: (0, 0))],
        out_specs=[pl.BlockSpec((TOP_K, tm), lambda i: (0, i)),
                   pl.BlockSpec((TOP_K, tm), lambda i: (0, i)),
                   pl.BlockSpec((N_EXPERTS, 1), lambda i: (0, 0))],
        compiler_params=_cparams("arbitrary"),
        name="route_topk",
    )(logits_t, router_bias.reshape(N_EXPERTS, 1))


def _slot_kernel(idx_ref, start_ref, pos_ref, carry_ref):
    tm = ROUTE_TM

    @pl.when(pl.program_id(0) == 0)
    def _():
        carry_ref[...] = jnp.zeros_like(carry_ref)

    idx = idx_ref[...]
    eid = lax.broadcasted_iota(jnp.int32, (N_EXPERTS, tm), 0)
    hits = [eid == idx[k:k + 1, :] for k in range(TOP_K)]
    sel = jnp.zeros((N_EXPERTS, tm), F32)
    for h in hits:
        sel = sel + h.astype(F32)
    r = lax.broadcasted_iota(jnp.int32, (tm, tm), 0)
    c = lax.broadcasted_iota(jnp.int32, (tm, tm), 1)
    earlier = (r < c).astype(BF16)
    rank = jnp.dot(sel.astype(BF16), earlier, preferred_element_type=F32)
    slot = rank + carry_ref[...] + start_ref[...]
    for k, h in enumerate(hits):
        pos_ref[k:k + 1, :] = jnp.sum(jnp.where(h, slot, 0.0), axis=0,
                                      keepdims=True).astype(jnp.int32)
    carry_ref[...] += jnp.sum(sel, axis=1, keepdims=True)


def _assign_slots(idx_t, pad_start):
    T = idx_t.shape[1]
    tm = ROUTE_TM
    return pl.pallas_call(
        _slot_kernel,
        out_shape=jax.ShapeDtypeStruct((TOP_K, T), jnp.int32),
        grid=(T // tm,),
        in_specs=[pl.BlockSpec((TOP_K, tm), lambda i: (0, i)),
                  pl.BlockSpec((N_EXPERTS, 1), lambda i: (0, 0))],
        out_specs=pl.BlockSpec((TOP_K, tm), lambda i: (0, i)),
        scratch_shapes=[pltpu.VMEM((N_EXPERTS, 1), F32)],
        compiler_params=_cparams("arbitrary"),
        name="assign_slots",
    )(idx_t, pad_start)


def _expert_kernel(be_ref, nu_ref, first_ref, slot_ref, nxt_ref, xs_ref, w1_hbm, w3_hbm, w2_hbm,
                   ys_ref, w1_buf, w3_buf, w2_buf, sem):
    b = pl.program_id(0)

    def weight_copies(expert, slot):
        pairs = ((w1_hbm, w1_buf), (w3_hbm, w3_buf), (w2_hbm, w2_buf))
        return [pltpu.make_async_copy(hbm.at[expert], buf.at[slot], sem.at[slot, i])
                for i, (hbm, buf) in enumerate(pairs)]

    @pl.when(b < nu_ref[0])
    def _():
        slot = slot_ref[b]

        @pl.when(b == 0)
        def _():
            for c in weight_copies(be_ref[0], 0):
                c.start(priority=WEIGHT_DMA_PRIORITY)

        @pl.when(first_ref[b] == 1)
        def _():
            for c in weight_copies(be_ref[b], slot):
                c.wait()

            @pl.when(nxt_ref[b] >= 0)
            def _():
                for c in weight_copies(nxt_ref[b], 1 - slot):
                    c.start(priority=WEIGHT_DMA_PRIORITY)

        lo, hi = _load_packed(xs_ref)
        half = lo.shape[-1]
        mm = lambda w: (jnp.dot(lo, w[slot, :half, :], preferred_element_type=F32)
                        + jnp.dot(hi, w[slot, half:, :], preferred_element_type=F32))
        hb = _silu(mm(w1_buf)) * mm(w3_buf)
        _store_packed(ys_ref, jnp.dot(hb, w2_buf[slot], preferred_element_type=F32))

    @pl.when(b >= nu_ref[0])
    def _():
        ys_ref[...] = jnp.zeros_like(ys_ref)


def _routed_experts(xs, blk_expert, n_used, w1, w3, w2):
    _, P, Dq = xs.shape
    blk = EXPERT_BLK
    n_blocks = P // blk
    D, F = w1.shape[-2:]

    b_ids = jnp.arange(n_blocks, dtype=jnp.int32)
    used = b_ids < n_used[0]
    prev = jnp.concatenate([jnp.full((1,), -1, jnp.int32), blk_expert[:-1]])
    first = (used & (blk_expert != prev)).astype(jnp.int32)
    slot = (jnp.cumsum(first) - 1) % 2
    start_at = jnp.where(first == 1, b_ids, n_blocks)
    later = jnp.concatenate([start_at[1:], jnp.full((1,), n_blocks, jnp.int32)])
    next_start = lax.cummin(later, axis=0, reverse=True)
    nxt = jnp.where(next_start < n_blocks,
                    blk_expert[jnp.minimum(next_start, n_blocks - 1)], -1).astype(jnp.int32)

    row_blk = lambda b, be, nu, *_: (0, jnp.minimum(b, nu[0] - 1), 0)
    grid_spec = pltpu.PrefetchScalarGridSpec(
        num_scalar_prefetch=5,
        grid=(n_blocks,),
        in_specs=[pl.BlockSpec((2, blk, Dq), row_blk),
                  pl.BlockSpec(memory_space=pl.ANY),
                  pl.BlockSpec(memory_space=pl.ANY),
                  pl.BlockSpec(memory_space=pl.ANY)],
        out_specs=pl.BlockSpec((2, blk, Dq), lambda b, *_: (0, b, 0)),
        scratch_shapes=[pltpu.VMEM((2, D, F), F32), pltpu.VMEM((2, D, F), F32),
                        pltpu.VMEM((2, F, D), F32), pltpu.SemaphoreType.DMA((2, 3))],
    )
    return pl.pallas_call(
        _expert_kernel,
        out_shape=jax.ShapeDtypeStruct((2, P, Dq), U32),
        grid_spec=grid_spec,
        compiler_params=_cparams("arbitrary"),
        name="routed_experts",
    )(blk_expert, n_used, first, slot.astype(jnp.int32), nxt, xs, w1, w3, w2)


def _sc_mesh():
    return plsc.VectorSubcoreMesh(core_axis_name="c", subcore_axis_name="s")


def _sc_dispatch(rows, slots_t, n_slots):
    T, W = rows.shape
    K = slots_t.shape[0]
    win = SC_WINDOW

    @pl.kernel(out_type=jax.ShapeDtypeStruct((n_slots, W), rows.dtype), mesh=_sc_mesh(),
               scratch_types=[])
    def dispatch(x_hbm, i_hbm, o_hbm):
        def body(x_vmem, i_vmem):
            for k in range(K):
                pltpu.sync_copy(x_vmem, o_hbm.at[i_vmem.at[k]])

        pltpu.emit_pipeline(
            body,
            grid=(T // win,),
            in_specs=[pl.BlockSpec((win, W), lambda i: (i, 0)),
                      pl.BlockSpec((K, win), lambda i: (0, i))],
            out_specs=[],
            core_axis_name=("c", "s"),
            dimension_semantics=(pltpu.PARALLEL,),
        )(x_hbm, i_hbm)

    return dispatch(rows, slots_t)


def _sc_gather(table, idx):
    W = table.shape[1]
    N = idx.shape[1]
    win = SC_WINDOW

    @pl.kernel(out_type=jax.ShapeDtypeStruct((N, W), table.dtype), mesh=_sc_mesh(),
               scratch_types=[])
    def gather(x_hbm, i_hbm, o_hbm):
        def body(i_vmem, o_vmem):
            pltpu.sync_copy(x_hbm.at[i_vmem.at[0]], o_vmem)

        pltpu.emit_pipeline(
            body,
            grid=(N // win,),
            in_specs=[pl.BlockSpec((1, win), lambda i: (0, i))],
            out_specs=[pl.BlockSpec((win, W), lambda i: (i, 0))],
            core_axis_name=("c", "s"),
            dimension_semantics=(pltpu.PARALLEL,),
        )(i_hbm, o_hbm)

    return gather(table, idx)


def _final_kernel(x1_ref, h2_ref, yg_ref, gate_ref, g2_ref, ws1_ref, ws3_ref, ws2_ref,
                  lng_ref, lnb_ref, o_ref):
    Dh = D_MODEL // 2
    gate = gate_ref[...]
    routed_lo = jnp.zeros((x1_ref.shape[0], Dh), F32)
    routed_hi = jnp.zeros((x1_ref.shape[0], Dh), F32)
    for k in range(TOP_K):
        lo, hi = _load_packed(yg_ref.at[k])
        routed_lo = routed_lo + lo * gate[:, k:k + 1]
        routed_hi = routed_hi + hi * gate[:, k:k + 1]
    h2 = jnp.concatenate(_load_packed(h2_ref), axis=-1).astype(BF16)
    a = jnp.dot(h2, ws1_ref[...], preferred_element_type=F32)
    b = jnp.dot(h2, ws3_ref[...], preferred_element_type=F32)
    shared = jnp.dot((_silu(a) * b).astype(BF16), ws2_ref[...], preferred_element_type=F32)
    y2 = jnp.concatenate([routed_lo, routed_hi], axis=-1) + shared
    o_ref[...] = _layer_norm(DN_ALPHA * x1_ref[...] + g2_ref[0] * y2, lng_ref[...], lnb_ref[...])


def _final(x1, h2, yg, gate, g2, ws1, ws3, ws2, ln2g, ln2b, seq):
    T, D = x1.shape
    tm = FINAL_TM
    tiles_per_seq = seq // tm
    F = ws1.shape[-1]
    tok = pl.BlockSpec((tm, D), lambda i: (i, 0))
    row = pl.BlockSpec((1, D), lambda i: (0, 0))
    return pl.pallas_call(
        _final_kernel,
        out_shape=jax.ShapeDtypeStruct((T, D), F32),
        grid=(T // tm,),
        in_specs=[tok, pl.BlockSpec((2, tm, D // 4), lambda i: (0, i, 0)),
                  pl.BlockSpec((TOP_K, 2, tm, D // 4), lambda i: (0, 0, i, 0)),
                  pl.BlockSpec((tm, TOP_K), lambda i: (i, 0)),
                  pl.BlockSpec((1, 1, D), lambda i: (i // tiles_per_seq, 0, 0)),
                  pl.BlockSpec((D, F), lambda i: (0, 0)),
                  pl.BlockSpec((D, F), lambda i: (0, 0)),
                  pl.BlockSpec((F, D), lambda i: (0, 0)),
                  row, row],
        out_specs=tok,
        compiler_params=_cparams("arbitrary"),
        name="moe_combine_final",
    )(x1, h2, yg, gate, g2, ws1, ws3, ws2, ln2g, ln2b)


def kernel(x, c, ada_w, ada_b, w_in, b_in, lambda_q1, lambda_k1, lambda_q2, lambda_k2, subln_g, w_attn_o, conv_w, conv_b, conv_ln_g, conv_ln_b, conv_pw_w, conv_pw_b, w_out, ln1_g, ln1_b, router_w, router_bias, w1, w3, w2, ws1, ws3, ws2, ln2_g, ln2_b):
    B, S, D = x.shape
    T = B * S
    l = 0
    x2 = x.reshape(T, D)
    row = lambda a: a.reshape(1, -1)

    mod = _ada_modulation(c, ada_w[l], ada_b[l])
    sh1, sc1, g1, sh2, sc2, g2 = [m.reshape(B, 1, D) for m in jnp.split(mod, 6, axis=-1)]

    qt, k, vt, u, ga, gb = _in_projection(x2, sc1, sh1, w_in[l].astype(BF16), row(b_in[l]), S)
    o = _diff_attention(qt, k, vt, row(lambda_q1[l]), row(lambda_k1[l]), row(lambda_q2[l]),
                        row(lambda_k2[l]), row(subln_g[l]), B, S)

    wr = router_w[l].T
    wr_hi = wr.astype(BF16)
    wr_lo = (wr - wr_hi.astype(F32)).astype(BF16)
    x1, h2, logits = _mixer_epilogue(
        x2, o, _causal_conv(u, conv_w[l], S), ga, gb, g1, sc2, sh2, w_attn_o[l].astype(BF16),
        row(conv_b[l]),
        row(conv_ln_g[l]), row(conv_ln_b[l]), conv_pw_w[l].astype(BF16), row(conv_pw_b[l]),
        w_out[l].astype(BF16), row(ln1_g[l]), row(ln1_b[l]), wr_hi, wr_lo, S)

    idx, gate, counts = _route(logits, router_bias[l])

    blk = EXPERT_BLK
    cnt = counts.reshape(N_EXPERTS).astype(jnp.int32)
    padded = (cnt + blk - 1) // blk * blk
    pad_end = jnp.cumsum(padded)
    pad_start = pad_end - padded
    P = T * TOP_K + N_EXPERTS * blk
    n_blocks = P // blk
    blk_first_row = jnp.arange(n_blocks, dtype=jnp.int32) * blk
    blk_expert = jnp.minimum(
        jnp.sum((pad_end[None, :] <= blk_first_row[:, None]).astype(jnp.int32), axis=1),
        N_EXPERTS - 1)
    n_used = (pad_end[-1:] // blk).astype(jnp.int32)

    pos = _assign_slots(idx, pad_start.astype(F32).reshape(N_EXPERTS, 1))

    Dq = D // 4
    half_base = (jnp.arange(2, dtype=jnp.int32) * P)[None, :, None]
    slots = pos[:, None, :] + half_base
    xs = _sc_dispatch(h2.reshape(2 * T, Dq), slots.reshape(TOP_K, 2 * T), 2 * P)
    ys = _routed_experts(xs.reshape(2, P, Dq), blk_expert, n_used, w1[l], w3[l], w2[l])
    yg = _sc_gather(ys.reshape(2 * P, Dq), slots.reshape(1, TOP_K * 2 * T))
    yg = yg.reshape(TOP_K, 2, T, Dq)

    out = _final(x1, h2, yg, gate.T, g2, ws1[l].astype(BF16), ws3[l].astype(BF16),
                 ws2[l].astype(BF16), row(ln2_g[l]), row(ln2_b[l]), S)
    return out.reshape(B, S, D)
```
